```python
import math
import jax, jax.numpy as jnp
from jax import lax
import numpy as np

D_MODEL = 4096
BATCH = 8
SEQ = 4096
DEPTH = 1

N_META = 16
D_FF = 11008
D_RGLRU = D_MODEL // 2
RG_HEADS = 8
RG_HEAD_DIM = D_RGLRU // RG_HEADS
CONV_WIDTH = 4
RG_C = 8.0
D_S5 = D_MODEL - D_RGLRU
S5_GROUP = 16
S5_GROUPS = D_S5 // S5_GROUP
S5_STATE = 64
DT_MIN = 0.001
DT_MAX = 0.1
D_MIX = D_RGLRU + D_S5
D_IN_PROJ = 2 * D_RGLRU + D_S5
EPS = 1e-6

kernel_name = "hymba_rglru_s5_macaron_layer"


def rmsnorm(x, g):
    xf = x.astype(jnp.float32)
    y = xf * lax.rsqrt(jnp.mean(xf * xf, axis=-1, keepdims=True) + EPS)
    return (y * g.astype(jnp.float32)).astype(x.dtype)


def swiglu(h, w_gate, w_up, w_down):
    return (jax.nn.silu(h @ w_gate) * (h @ w_up)) @ w_down


def rg_lru_mixer(u, gate, conv_w, conv_b, w_a, b_a, w_x, b_x, lam):
    bsz, t_len, _ = u.shape
    up = jnp.pad(u, ((0, 0), (CONV_WIDTH - 1, 0), (0, 0)))
    xc = conv_b + sum(up[:, k:k + t_len] * conv_w[k] for k in range(CONV_WIDTH))
    xh = xc.reshape(bsz, t_len, RG_HEADS, RG_HEAD_DIM)
    r = jax.nn.sigmoid(jnp.einsum('bthi,hij->bthj', xh, w_a) + b_a).reshape(bsz, t_len, D_RGLRU)
    i = jax.nn.sigmoid(jnp.einsum('bthi,hij->bthj', xh, w_x) + b_x).reshape(bsz, t_len, D_RGLRU)
    log_a = -RG_C * r.astype(jnp.float32) * jax.nn.softplus(-lam.astype(jnp.float32))
    a = jnp.exp(log_a)
    mult = jnp.sqrt(-jnp.expm1(2.0 * log_a))
    bx = mult * i.astype(jnp.float32) * xc.astype(jnp.float32)

    def step(h, ab):
        a_t, b_t = ab
        h = a_t * h + b_t
        return h, h

    _, hs = lax.scan(step, jnp.zeros((bsz, D_RGLRU), jnp.float32),
                     (jnp.swapaxes(a, 0, 1), jnp.swapaxes(bx, 0, 1)))
    h = jnp.swapaxes(hs, 0, 1)
    return (h * jax.nn.gelu(gate.astype(jnp.float32))).astype(u.dtype)


def s5_mixer(u, lam_re, lam_im, log_dt, b_re, b_im, c_re, c_im, d, glu_w, glu_b):
    bsz, t_len, _ = u.shape
    f32 = jnp.float32
    dt = jnp.exp(log_dt.astype(f32))[:, None]
    lam = lax.complex(lam_re.astype(f32), lam_im.astype(f32))
    lam_bar = jnp.exp(lam * dt)
    b = lax.complex(b_re.astype(f32), b_im.astype(f32))
    b_bar = ((lam_bar - 1.0) / lam)[..., None] * b
    ug = u.astype(f32).reshape(bsz, t_len, S5_GROUPS, S5_GROUP)
    bu = jnp.einsum('btgc,gnc->btgn', ug, b_bar)
    a_elems = jnp.broadcast_to(lam_bar[None, None], (1, t_len, S5_GROUPS, S5_STATE))

    def combine(e_i, e_j):
        a_i, b_i = e_i
        a_j, b_j = e_j
        return (a_j * a_i, a_j * b_i + b_j)

    _, states = lax.associative_scan(combine, (a_elems, bu), axis=1)
    c = lax.complex(c_re.astype(f32), c_im.astype(f32))
    y = jnp.einsum('btgn,gcn->btgc', states, c).real
    y = y + d.astype(f32).reshape(S5_GROUPS, S5_GROUP) * ug
    y = y.reshape(bsz, t_len, D_S5).astype(u.dtype)
    z = jax.nn.gelu(y)
    return z * jax.nn.sigmoid(z @ glu_w + glu_b)


def _fwd_setup_inputs(seed: int = 0) -> dict:
    key = jax.random.key(seed)
    ks = jax.random.split(key, 40)
    f32 = jnp.float32
    nrm = lambda k, shape, s: jax.random.normal(k, shape, f32) * s
    L = DEPTH
    u = jax.random.uniform(ks[30], (L, D_RGLRU), f32, 0.9, 0.999)
    rg_lambda = jnp.log(u ** (1.0 / RG_C)) - jnp.log1p(-(u ** (1.0 / RG_C)))
    n_idx = jnp.arange(S5_STATE, dtype=f32)
    s5_lambda_re = -0.5 + nrm(ks[31], (L, S5_GROUPS, S5_STATE), 0.01)
    s5_lambda_im = math.pi * n_idx + nrm(ks[32], (L, S5_GROUPS, S5_STATE), 0.01)
    s5_log_dt = jax.random.uniform(ks[33], (L, S5_GROUPS), f32, math.log(DT_MIN), math.log(DT_MAX))
    return {
        "x": nrm(ks[0], (BATCH, SEQ, D_MODEL), 1.0),
        "meta_tokens": nrm(ks[1], (N_META, D_MODEL), 1.0),
        "ffn1_norm": 1.0 + nrm(ks[2], (L, D_MODEL), 0.02),
        "ffn1_w_gate": nrm(ks[3], (L, D_MODEL, D_FF), D_MODEL ** -0.5),
        "ffn1_w_up": nrm(ks[4], (L, D_MODEL, D_FF), D_MODEL ** -0.5),
        "ffn1_w_down": nrm(ks[5], (L, D_FF, D_MODEL), D_FF ** -0.5),
        "mix_norm": 1.0 + nrm(ks[6], (L, D_MODEL), 0.02),
        "w_in": nrm(ks[7], (L, D_MODEL, D_IN_PROJ), D_MODEL ** -0.5),
        "rg_conv_w": nrm(ks[8], (L, CONV_WIDTH, D_RGLRU), CONV_WIDTH ** -0.5),
        "rg_conv_b": nrm(ks[9], (L, D_RGLRU), 0.01),
        "rg_w_a": nrm(ks[10], (L, RG_HEADS, RG_HEAD_DIM, RG_HEAD_DIM), RG_HEAD_DIM ** -0.5),
        "rg_b_a": nrm(ks[11], (L, RG_HEADS, RG_HEAD_DIM), 0.01),
        "rg_w_x": nrm(ks[12], (L, RG_HEADS, RG_HEAD_DIM, RG_HEAD_DIM), RG_HEAD_DIM ** -0.5),
        "rg_b_x": nrm(ks[13], (L, RG_HEADS, RG_HEAD_DIM), 0.01),
        "rg_lambda": rg_lambda,
        "s5_lambda_re": s5_lambda_re,
        "s5_lambda_im": s5_lambda_im,
        "s5_log_dt": s5_log_dt,
        "s5_b_re": nrm(ks[14], (L, S5_GROUPS, S5_STATE, S5_GROUP), (2.0 * S5_GROUP) ** -0.5),
        "s5_b_im": nrm(ks[15], (L, S5_GROUPS, S5_STATE, S5_GROUP), (2.0 * S5_GROUP) ** -0.5),
        "s5_c_re": nrm(ks[16], (L, S5_GROUPS, S5_GROUP, S5_STATE), 1.0),
        "s5_c_im": nrm(ks[17], (L, S5_GROUPS, S5_GROUP, S5_STATE), 1.0),
        "s5_d": nrm(ks[18], (L, D_S5), 0.5),
        "s5_glu_w": nrm(ks[19], (L, D_S5, D_S5), D_S5 ** -0.5),
        "s5_glu_b": nrm(ks[20], (L, D_S5), 0.01),
        "rg_out_norm": 1.0 + nrm(ks[21], (L, D_RGLRU), 0.02),
        "s5_out_norm": 1.0 + nrm(ks[22], (L, D_S5), 0.02),
        "w_out": nrm(ks[23], (L, D_MIX, D_MODEL), D_MIX ** -0.5),
        "ffn2_norm": 1.0 + nrm(ks[24], (L, D_MODEL), 0.02),
        "ffn2_w_gate": nrm(ks[25], (L, D_MODEL, D_FF), D_MODEL ** -0.5),
        "ffn2_w_up": nrm(ks[26], (L, D_MODEL, D_FF), D_MODEL ** -0.5),
        "ffn2_w_down": nrm(ks[27], (L, D_FF, D_MODEL), D_FF ** -0.5),
        "final_norm": 1.0 + nrm(ks[28], (D_MODEL,), 0.02),
    }


def _fwd_reference(x, meta_tokens, ffn1_norm, ffn1_w_gate, ffn1_w_up, ffn1_w_down, mix_norm, w_in,
              rg_conv_w, rg_conv_b, rg_w_a, rg_b_a, rg_w_x, rg_b_x, rg_lambda,
              s5_lambda_re, s5_lambda_im, s5_log_dt, s5_b_re, s5_b_im, s5_c_re, s5_c_im, s5_d,
              s5_glu_w, s5_glu_b, rg_out_norm, s5_out_norm, w_out,
              ffn2_norm, ffn2_w_gate, ffn2_w_up, ffn2_w_down, final_norm):
    bsz = x.shape[0]
    meta = jnp.broadcast_to(meta_tokens[None].astype(x.dtype), (bsz, N_META, D_MODEL))
    h = jnp.concatenate([meta, x], axis=1)
    for l in range(DEPTH):
        h = h + 0.5 * swiglu(rmsnorm(h, ffn1_norm[l]), ffn1_w_gate[l], ffn1_w_up[l], ffn1_w_down[l])
        proj = rmsnorm(h, mix_norm[l]) @ w_in[l]
        u_rg, g_rg, u_s5 = jnp.split(proj, [D_RGLRU, 2 * D_RGLRU], axis=-1)
        y_rg = rg_lru_mixer(u_rg, g_rg, rg_conv_w[l], rg_conv_b[l], rg_w_a[l], rg_b_a[l],
                            rg_w_x[l], rg_b_x[l], rg_lambda[l])
        y_s5 = s5_mixer(u_s5, s5_lambda_re[l], s5_lambda_im[l], s5_log_dt[l], s5_b_re[l], s5_b_im[l],
                        s5_c_re[l], s5_c_im[l], s5_d[l], s5_glu_w[l], s5_glu_b[l])
        y = jnp.concatenate([rmsnorm(y_rg, rg_out_norm[l]), rmsnorm(y_s5, s5_out_norm[l])], axis=-1)
        h = h + y @ w_out[l]
        h = h + 0.5 * swiglu(rmsnorm(h, ffn2_norm[l]), ffn2_w_gate[l], ffn2_w_up[l], ffn2_w_down[l])
    out = rmsnorm(h, final_norm)
    return out[:, N_META:]


import jax as _jax
import jax.numpy as _jnp

TWIN_FORMAT = 'train_step'
FWD_PARAMS = ['x', 'meta_tokens', 'ffn1_norm', 'ffn1_w_gate', 'ffn1_w_up', 'ffn1_w_down', 'mix_norm', 'w_in', 'rg_conv_w', 'rg_conv_b', 'rg_w_a', 'rg_b_a', 'rg_w_x', 'rg_b_x', 'rg_lambda', 's5_lambda_re', 's5_lambda_im', 's5_log_dt', 's5_b_re', 's5_b_im', 's5_c_re', 's5_c_im', 's5_d', 's5_glu_w', 's5_glu_b', 'rg_out_norm', 's5_out_norm', 'w_out', 'ffn2_norm', 'ffn2_w_gate', 'ffn2_w_up', 'ffn2_w_down', 'final_norm']
TWIN_WEIGHTS = ['meta_tokens', 'ffn1_norm', 'ffn1_w_gate', 'ffn1_w_up', 'ffn1_w_down', 'mix_norm', 'w_in', 'rg_conv_w', 'rg_conv_b', 'rg_w_a', 'rg_b_a', 'rg_w_x', 'rg_b_x', 'rg_lambda', 's5_lambda_re', 's5_lambda_im', 's5_log_dt', 's5_b_re', 's5_b_im', 's5_c_re', 's5_c_im', 's5_d', 's5_glu_w', 's5_glu_b', 'rg_out_norm', 's5_out_norm', 'w_out', 'ffn2_norm', 'ffn2_w_gate', 'ffn2_w_up', 'ffn2_w_down', 'final_norm']
TWIN_DIFF_INPUT = 'x'
TWIN_INPUTS = ['x', 'meta_tokens', 'ffn1_norm', 'ffn1_w_gate', 'ffn1_w_up', 'ffn1_w_down', 'mix_norm', 'w_in', 'rg_conv_w', 'rg_conv_b', 'rg_w_a', 'rg_b_a', 'rg_w_x', 'rg_b_x', 'rg_lambda', 's5_lambda_re', 's5_lambda_im', 's5_log_dt', 's5_b_re', 's5_b_im', 's5_c_re', 's5_c_im', 's5_d', 's5_glu_w', 's5_glu_b', 'rg_out_norm', 's5_out_norm', 'w_out', 'ffn2_norm', 'ffn2_w_gate', 'ffn2_w_up', 'ffn2_w_down', 'final_norm', 'loss_target', 'm_meta_tokens', 'm_ffn1_norm', 'm_ffn1_w_gate', 'm_ffn1_w_up', 'm_ffn1_w_down', 'm_mix_norm', 'm_w_in', 'm_rg_conv_w', 'm_rg_conv_b', 'm_rg_w_a', 'm_rg_b_a', 'm_rg_w_x', 'm_rg_b_x', 'm_rg_lambda', 'm_s5_lambda_re', 'm_s5_lambda_im', 'm_s5_log_dt', 'm_s5_b_re', 'm_s5_b_im', 'm_s5_c_re', 'm_s5_c_im', 'm_s5_d', 'm_s5_glu_w', 'm_s5_glu_b', 'm_rg_out_norm', 'm_s5_out_norm', 'm_w_out', 'm_ffn2_norm', 'm_ffn2_w_gate', 'm_ffn2_w_up', 'm_ffn2_w_down', 'm_final_norm', 'v_meta_tokens', 'v_ffn1_norm', 'v_ffn1_w_gate', 'v_ffn1_w_up', 'v_ffn1_w_down', 'v_mix_norm', 'v_w_in', 'v_rg_conv_w', 'v_rg_conv_b', 'v_rg_w_a', 'v_rg_b_a', 'v_rg_w_x', 'v_rg_b_x', 'v_rg_lambda', 'v_s5_lambda_re', 'v_s5_lambda_im', 'v_s5_log_dt', 'v_s5_b_re', 'v_s5_b_im', 'v_s5_c_re', 'v_s5_c_im', 'v_s5_d', 'v_s5_glu_w', 'v_s5_glu_b', 'v_rg_out_norm', 'v_s5_out_norm', 'v_w_out', 'v_ffn2_norm', 'v_ffn2_w_gate', 'v_ffn2_w_up', 'v_ffn2_w_down', 'v_final_norm']
TWIN_OUTPUTS = ['loss', 'grad_x', 'grad_meta_tokens', 'grad_ffn1_norm', 'grad_ffn1_w_gate', 'grad_ffn1_w_up', 'grad_ffn1_w_down', 'grad_mix_norm', 'grad_w_in', 'grad_rg_conv_w', 'grad_rg_conv_b', 'grad_rg_w_a', 'grad_rg_b_a', 'grad_rg_w_x', 'grad_rg_b_x', 'grad_rg_lambda', 'grad_s5_lambda_re', 'grad_s5_lambda_im', 'grad_s5_log_dt', 'grad_s5_b_re', 'grad_s5_b_im', 'grad_s5_c_re', 'grad_s5_c_im', 'grad_s5_d', 'grad_s5_glu_w', 'grad_s5_glu_b', 'grad_rg_out_norm', 'grad_s5_out_norm', 'grad_w_out', 'grad_ffn2_norm', 'grad_ffn2_w_gate', 'grad_ffn2_w_up', 'grad_ffn2_w_down', 'grad_final_norm', 'delta_meta_tokens', 'delta_ffn1_norm', 'delta_ffn1_w_gate', 'delta_ffn1_w_up', 'delta_ffn1_w_down', 'delta_mix_norm', 'delta_w_in', 'delta_rg_conv_w', 'delta_rg_conv_b', 'delta_rg_w_a', 'delta_rg_b_a', 'delta_rg_w_x', 'delta_rg_b_x', 'delta_rg_lambda', 'delta_s5_lambda_re', 'delta_s5_lambda_im', 'delta_s5_log_dt', 'delta_s5_b_re', 'delta_s5_b_im', 'delta_s5_c_re', 'delta_s5_c_im', 'delta_s5_d', 'delta_s5_glu_w', 'delta_s5_glu_b', 'delta_rg_out_norm', 'delta_s5_out_norm', 'delta_w_out', 'delta_ffn2_norm', 'delta_ffn2_w_gate', 'delta_ffn2_w_up', 'delta_ffn2_w_down', 'delta_final_norm', 'new_m_meta_tokens', 'new_m_ffn1_norm', 'new_m_ffn1_w_gate', 'new_m_ffn1_w_up', 'new_m_ffn1_w_down', 'new_m_mix_norm', 'new_m_w_in', 'new_m_rg_conv_w', 'new_m_rg_conv_b', 'new_m_rg_w_a', 'new_m_rg_b_a', 'new_m_rg_w_x', 'new_m_rg_b_x', 'new_m_rg_lambda', 'new_m_s5_lambda_re', 'new_m_s5_lambda_im', 'new_m_s5_log_dt', 'new_m_s5_b_re', 'new_m_s5_b_im', 'new_m_s5_c_re', 'new_m_s5_c_im', 'new_m_s5_d', 'new_m_s5_glu_w', 'new_m_s5_glu_b', 'new_m_rg_out_norm', 'new_m_s5_out_norm', 'new_m_w_out', 'new_m_ffn2_norm', 'new_m_ffn2_w_gate', 'new_m_ffn2_w_up', 'new_m_ffn2_w_down', 'new_m_final_norm', 'new_v_meta_tokens', 'new_v_ffn1_norm', 'new_v_ffn1_w_gate', 'new_v_ffn1_w_up', 'new_v_ffn1_w_down', 'new_v_mix_norm', 'new_v_w_in', 'new_v_rg_conv_w', 'new_v_rg_conv_b', 'new_v_rg_w_a', 'new_v_rg_b_a', 'new_v_rg_w_x', 'new_v_rg_b_x', 'new_v_rg_lambda', 'new_v_s5_lambda_re', 'new_v_s5_lambda_im', 'new_v_s5_log_dt', 'new_v_s5_b_re', 'new_v_s5_b_im', 'new_v_s5_c_re', 'new_v_s5_c_im', 'new_v_s5_d', 'new_v_s5_glu_w', 'new_v_s5_glu_b', 'new_v_rg_out_norm', 'new_v_s5_out_norm', 'new_v_w_out', 'new_v_ffn2_norm', 'new_v_ffn2_w_gate', 'new_v_ffn2_w_up', 'new_v_ffn2_w_down', 'new_v_final_norm']
TWIN_LEAF_KINDS = {'loss': 'loss', 'grad_x': 'grad_x', 'grad_meta_tokens': 'grad_w', 'grad_ffn1_norm': 'grad_w', 'grad_ffn1_w_gate': 'grad_w', 'grad_ffn1_w_up': 'grad_w', 'grad_ffn1_w_down': 'grad_w', 'grad_mix_norm': 'grad_w', 'grad_w_in': 'grad_w', 'grad_rg_conv_w': 'grad_w', 'grad_rg_conv_b': 'grad_w', 'grad_rg_w_a': 'grad_w', 'grad_rg_b_a': 'grad_w', 'grad_rg_w_x': 'grad_w', 'grad_rg_b_x': 'grad_w', 'grad_rg_lambda': 'grad_w', 'grad_s5_lambda_re': 'grad_w', 'grad_s5_lambda_im': 'grad_w', 'grad_s5_log_dt': 'grad_w', 'grad_s5_b_re': 'grad_w', 'grad_s5_b_im': 'grad_w', 'grad_s5_c_re': 'grad_w', 'grad_s5_c_im': 'grad_w', 'grad_s5_d': 'grad_w', 'grad_s5_glu_w': 'grad_w', 'grad_s5_glu_b': 'grad_w', 'grad_rg_out_norm': 'grad_w', 'grad_s5_out_norm': 'grad_w', 'grad_w_out': 'grad_w', 'grad_ffn2_norm': 'grad_w', 'grad_ffn2_w_gate': 'grad_w', 'grad_ffn2_w_up': 'grad_w', 'grad_ffn2_w_down': 'grad_w', 'grad_final_norm': 'grad_w', 'delta_meta_tokens': 'delta_w', 'delta_ffn1_norm': 'delta_w', 'delta_ffn1_w_gate': 'delta_w', 'delta_ffn1_w_up': 'delta_w', 'delta_ffn1_w_down': 'delta_w', 'delta_mix_norm': 'delta_w', 'delta_w_in': 'delta_w', 'delta_rg_conv_w': 'delta_w', 'delta_rg_conv_b': 'delta_w', 'delta_rg_w_a': 'delta_w', 'delta_rg_b_a': 'delta_w', 'delta_rg_w_x': 'delta_w', 'delta_rg_b_x': 'delta_w', 'delta_rg_lambda': 'delta_w', 'delta_s5_lambda_re': 'delta_w', 'delta_s5_lambda_im': 'delta_w', 'delta_s5_log_dt': 'delta_w', 'delta_s5_b_re': 'delta_w', 'delta_s5_b_im': 'delta_w', 'delta_s5_c_re': 'delta_w', 'delta_s5_c_im': 'delta_w', 'delta_s5_d': 'delta_w', 'delta_s5_glu_w': 'delta_w', 'delta_s5_glu_b': 'delta_w', 'delta_rg_out_norm': 'delta_w', 'delta_s5_out_norm': 'delta_w', 'delta_w_out': 'delta_w', 'delta_ffn2_norm': 'delta_w', 'delta_ffn2_w_gate': 'delta_w', 'delta_ffn2_w_up': 'delta_w', 'delta_ffn2_w_down': 'delta_w', 'delta_final_norm': 'delta_w', 'new_m_meta_tokens': 'new_m', 'new_m_ffn1_norm': 'new_m', 'new_m_ffn1_w_gate': 'new_m', 'new_m_ffn1_w_up': 'new_m', 'new_m_ffn1_w_down': 'new_m', 'new_m_mix_norm': 'new_m', 'new_m_w_in': 'new_m', 'new_m_rg_conv_w': 'new_m', 'new_m_rg_conv_b': 'new_m', 'new_m_rg_w_a': 'new_m', 'new_m_rg_b_a': 'new_m', 'new_m_rg_w_x': 'new_m', 'new_m_rg_b_x': 'new_m', 'new_m_rg_lambda': 'new_m', 'new_m_s5_lambda_re': 'new_m', 'new_m_s5_lambda_im': 'new_m', 'new_m_s5_log_dt': 'new_m', 'new_m_s5_b_re': 'new_m', 'new_m_s5_b_im': 'new_m', 'new_m_s5_c_re': 'new_m', 'new_m_s5_c_im': 'new_m', 'new_m_s5_d': 'new_m', 'new_m_s5_glu_w': 'new_m', 'new_m_s5_glu_b': 'new_m', 'new_m_rg_out_norm': 'new_m', 'new_m_s5_out_norm': 'new_m', 'new_m_w_out': 'new_m', 'new_m_ffn2_norm': 'new_m', 'new_m_ffn2_w_gate': 'new_m', 'new_m_ffn2_w_up': 'new_m', 'new_m_ffn2_w_down': 'new_m', 'new_m_final_norm': 'new_m', 'new_v_meta_tokens': 'new_v', 'new_v_ffn1_norm': 'new_v', 'new_v_ffn1_w_gate': 'new_v', 'new_v_ffn1_w_up': 'new_v', 'new_v_ffn1_w_down': 'new_v', 'new_v_mix_norm': 'new_v', 'new_v_w_in': 'new_v', 'new_v_rg_conv_w': 'new_v', 'new_v_rg_conv_b': 'new_v', 'new_v_rg_w_a': 'new_v', 'new_v_rg_b_a': 'new_v', 'new_v_rg_w_x': 'new_v', 'new_v_rg_b_x': 'new_v', 'new_v_rg_lambda': 'new_v', 'new_v_s5_lambda_re': 'new_v', 'new_v_s5_lambda_im': 'new_v', 'new_v_s5_log_dt': 'new_v', 'new_v_s5_b_re': 'new_v', 'new_v_s5_b_im': 'new_v', 'new_v_s5_c_re': 'new_v', 'new_v_s5_c_im': 'new_v', 'new_v_s5_d': 'new_v', 'new_v_s5_glu_w': 'new_v', 'new_v_s5_glu_b': 'new_v', 'new_v_rg_out_norm': 'new_v', 'new_v_s5_out_norm': 'new_v', 'new_v_w_out': 'new_v', 'new_v_ffn2_norm': 'new_v', 'new_v_ffn2_w_gate': 'new_v', 'new_v_ffn2_w_up': 'new_v', 'new_v_ffn2_w_down': 'new_v', 'new_v_final_norm': 'new_v'}


def _forward(args):
    return _fwd_reference(*[args[k] for k in FWD_PARAMS])


def _output_shape():
    out = _jax.eval_shape(lambda: _forward(_fwd_setup_inputs(0)))
    return out.shape, out.dtype

N_MICROBATCH = 1
ADAM_LR = 0.001
ADAM_B1 = 0.9
ADAM_B2 = 0.999
ADAM_EPS = 1e-08
ADAM_WD = 0.01
ADAM_STEP = 10
PER_EXAMPLE_BATCH_AXIS = {'x': 0, 'loss_target': 0}
SHARED_INPUTS = []
_WEIGHT_DTYPES = {'meta_tokens': _jnp.float32, 'ffn1_norm': _jnp.float32, 'ffn1_w_gate': _jnp.float32, 'ffn1_w_up': _jnp.float32, 'ffn1_w_down': _jnp.float32, 'mix_norm': _jnp.float32, 'w_in': _jnp.float32, 'rg_conv_w': _jnp.float32, 'rg_conv_b': _jnp.float32, 'rg_w_a': _jnp.float32, 'rg_b_a': _jnp.float32, 'rg_w_x': _jnp.float32, 'rg_b_x': _jnp.float32, 'rg_lambda': _jnp.float32, 's5_lambda_re': _jnp.float32, 's5_lambda_im': _jnp.float32, 's5_log_dt': _jnp.float32, 's5_b_re': _jnp.float32, 's5_b_im': _jnp.float32, 's5_c_re': _jnp.float32, 's5_c_im': _jnp.float32, 's5_d': _jnp.float32, 's5_glu_w': _jnp.float32, 's5_glu_b': _jnp.float32, 'rg_out_norm': _jnp.float32, 's5_out_norm': _jnp.float32, 'w_out': _jnp.float32, 'ffn2_norm': _jnp.float32, 'ffn2_w_gate': _jnp.float32, 'ffn2_w_up': _jnp.float32, 'ffn2_w_down': _jnp.float32, 'final_norm': _jnp.float32}
MOMENT_SCALE = {'meta_tokens': 2.002019e-03, 'ffn1_norm': 2.191802e-02, 'ffn1_w_gate': 9.398701e-03, 'ffn1_w_up': 9.090815e-03, 'ffn1_w_down': 1.489432e-02, 'mix_norm': 4.074807e-02, 'w_in': 3.309112e-02, 'rg_conv_w': 3.534421e-02, 'rg_conv_b': 4.141914e-01, 'rg_w_a': 9.044195e-03, 'rg_b_a': 8.190603e-03, 'rg_w_x': 1.620462e-02, 'rg_b_x': 1.220466e-02, 'rg_lambda': 1.719414e-02, 's5_lambda_re': 2.225295e-02, 's5_lambda_im': 2.328324e-02, 's5_log_dt': 1.876391e+01, 's5_b_re': 1.411414e-02, 's5_b_im': 1.440434e-02, 's5_c_re': 2.450888e-03, 's5_c_im': 2.480462e-03, 's5_d': 4.200115e-02, 's5_glu_w': 8.359157e-03, 's5_glu_b': 1.465033e-02, 'rg_out_norm': 3.305504e-02, 's5_out_norm': 3.241788e-02, 'w_out': 3.231611e-02, 'ffn2_norm': 1.287097e-02, 'ffn2_w_gate': 5.636647e-03, 'ffn2_w_up': 5.464296e-03, 'ffn2_w_down': 8.960406e-03, 'final_norm': 8.006904e+00}


def _to_microbatches(a, axis):
    t = _jnp.moveaxis(a, axis, 0)
    t = t.reshape((N_MICROBATCH, t.shape[0] // N_MICROBATCH) + t.shape[1:])
    return _jnp.moveaxis(t, 1, axis + 1)


def setup_inputs(seed: int = 0) -> dict:
    inp = _fwd_setup_inputs(seed)
    key = _jax.random.fold_in(_jax.random.key(seed), 7919)
    shape, _ = _output_shape()
    out = dict(inp)
    out["loss_target"] = _jax.random.normal(_jax.random.fold_in(key, 0), shape, _jnp.float32)
    for i, name in enumerate(TWIN_WEIGHTS):
        w = inp[name].astype(_jnp.float32)
        if MOMENT_SCALE is None:
            s = _jnp.sqrt(_jnp.mean(_jnp.square(w)) + 1e-30)
        else:
            s = MOMENT_SCALE[name]
        km, kv = _jax.random.split(_jax.random.fold_in(key, i + 1))
        out[name] = w
        out["m_" + name] = s * _jax.random.normal(km, w.shape, _jnp.float32)
        out["v_" + name] = (s * s) * _jax.random.uniform(kv, w.shape, _jnp.float32, 0.5, 1.5)
    if N_MICROBATCH > 1:
        for name, axis in PER_EXAMPLE_BATCH_AXIS.items():
            out[name] = _to_microbatches(out[name], axis)
    return {'x': out['x'], 'meta_tokens': out['meta_tokens'], 'ffn1_norm': out['ffn1_norm'], 'ffn1_w_gate': out['ffn1_w_gate'], 'ffn1_w_up': out['ffn1_w_up'], 'ffn1_w_down': out['ffn1_w_down'], 'mix_norm': out['mix_norm'], 'w_in': out['w_in'], 'rg_conv_w': out['rg_conv_w'], 'rg_conv_b': out['rg_conv_b'], 'rg_w_a': out['rg_w_a'], 'rg_b_a': out['rg_b_a'], 'rg_w_x': out['rg_w_x'], 'rg_b_x': out['rg_b_x'], 'rg_lambda': out['rg_lambda'], 's5_lambda_re': out['s5_lambda_re'], 's5_lambda_im': out['s5_lambda_im'], 's5_log_dt': out['s5_log_dt'], 's5_b_re': out['s5_b_re'], 's5_b_im': out['s5_b_im'], 's5_c_re': out['s5_c_re'], 's5_c_im': out['s5_c_im'], 's5_d': out['s5_d'], 's5_glu_w': out['s5_glu_w'], 's5_glu_b': out['s5_glu_b'], 'rg_out_norm': out['rg_out_norm'], 's5_out_norm': out['s5_out_norm'], 'w_out': out['w_out'], 'ffn2_norm': out['ffn2_norm'], 'ffn2_w_gate': out['ffn2_w_gate'], 'ffn2_w_up': out['ffn2_w_up'], 'ffn2_w_down': out['ffn2_w_down'], 'final_norm': out['final_norm'], 'loss_target': out['loss_target'], 'm_meta_tokens': out['m_meta_tokens'], 'm_ffn1_norm': out['m_ffn1_norm'], 'm_ffn1_w_gate': out['m_ffn1_w_gate'], 'm_ffn1_w_up': out['m_ffn1_w_up'], 'm_ffn1_w_down': out['m_ffn1_w_down'], 'm_mix_norm': out['m_mix_norm'], 'm_w_in': out['m_w_in'], 'm_rg_conv_w': out['m_rg_conv_w'], 'm_rg_conv_b': out['m_rg_conv_b'], 'm_rg_w_a': out['m_rg_w_a'], 'm_rg_b_a': out['m_rg_b_a'], 'm_rg_w_x': out['m_rg_w_x'], 'm_rg_b_x': out['m_rg_b_x'], 'm_rg_lambda': out['m_rg_lambda'], 'm_s5_lambda_re': out['m_s5_lambda_re'], 'm_s5_lambda_im': out['m_s5_lambda_im'], 'm_s5_log_dt': out['m_s5_log_dt'], 'm_s5_b_re': out['m_s5_b_re'], 'm_s5_b_im': out['m_s5_b_im'], 'm_s5_c_re': out['m_s5_c_re'], 'm_s5_c_im': out['m_s5_c_im'], 'm_s5_d': out['m_s5_d'], 'm_s5_glu_w': out['m_s5_glu_w'], 'm_s5_glu_b': out['m_s5_glu_b'], 'm_rg_out_norm': out['m_rg_out_norm'], 'm_s5_out_norm': out['m_s5_out_norm'], 'm_w_out': out['m_w_out'], 'm_ffn2_norm': out['m_ffn2_norm'], 'm_ffn2_w_gate': out['m_ffn2_w_gate'], 'm_ffn2_w_up': out['m_ffn2_w_up'], 'm_ffn2_w_down': out['m_ffn2_w_down'], 'm_final_norm': out['m_final_norm'], 'v_meta_tokens': out['v_meta_tokens'], 'v_ffn1_norm': out['v_ffn1_norm'], 'v_ffn1_w_gate': out['v_ffn1_w_gate'], 'v_ffn1_w_up': out['v_ffn1_w_up'], 'v_ffn1_w_down': out['v_ffn1_w_down'], 'v_mix_norm': out['v_mix_norm'], 'v_w_in': out['v_w_in'], 'v_rg_conv_w': out['v_rg_conv_w'], 'v_rg_conv_b': out['v_rg_conv_b'], 'v_rg_w_a': out['v_rg_w_a'], 'v_rg_b_a': out['v_rg_b_a'], 'v_rg_w_x': out['v_rg_w_x'], 'v_rg_b_x': out['v_rg_b_x'], 'v_rg_lambda': out['v_rg_lambda'], 'v_s5_lambda_re': out['v_s5_lambda_re'], 'v_s5_lambda_im': out['v_s5_lambda_im'], 'v_s5_log_dt': out['v_s5_log_dt'], 'v_s5_b_re': out['v_s5_b_re'], 'v_s5_b_im': out['v_s5_b_im'], 'v_s5_c_re': out['v_s5_c_re'], 'v_s5_c_im': out['v_s5_c_im'], 'v_s5_d': out['v_s5_d'], 'v_s5_glu_w': out['v_s5_glu_w'], 'v_s5_glu_b': out['v_s5_glu_b'], 'v_rg_out_norm': out['v_rg_out_norm'], 'v_s5_out_norm': out['v_s5_out_norm'], 'v_w_out': out['v_w_out'], 'v_ffn2_norm': out['v_ffn2_norm'], 'v_ffn2_w_gate': out['v_ffn2_w_gate'], 'v_ffn2_w_up': out['v_ffn2_w_up'], 'v_ffn2_w_down': out['v_ffn2_w_down'], 'v_final_norm': out['v_final_norm']}


def _loss(weights, diff, rest, loss_target):
    with _jax.named_scope("forward"):
        args = {**rest, TWIN_DIFF_INPUT: diff, **{k: w.astype(_WEIGHT_DTYPES[k]) for k, w in weights.items()}}
        y = _forward(args)
    with _jax.named_scope("loss_head"):
        err = _jnp.square(y.astype(_jnp.float32) - loss_target)
        return 0.5 * _jnp.sum(_jnp.mean(err, axis=-1)) if err.ndim else 0.5 * err


def _adamw(w, g, m, v):
    m = ADAM_B1 * m + (1.0 - ADAM_B1) * g
    v = ADAM_B2 * v + (1.0 - ADAM_B2) * _jnp.square(g)
    m_hat = m / (1.0 - ADAM_B1 ** ADAM_STEP)
    v_hat = v / (1.0 - ADAM_B2 ** ADAM_STEP)
    delta = -ADAM_LR * (m_hat / (_jnp.sqrt(v_hat) + ADAM_EPS) + ADAM_WD * w)
    return delta, m, v


def reference(x, meta_tokens, ffn1_norm, ffn1_w_gate, ffn1_w_up, ffn1_w_down, mix_norm, w_in, rg_conv_w, rg_conv_b, rg_w_a, rg_b_a, rg_w_x, rg_b_x, rg_lambda, s5_lambda_re, s5_lambda_im, s5_log_dt, s5_b_re, s5_b_im, s5_c_re, s5_c_im, s5_d, s5_glu_w, s5_glu_b, rg_out_norm, s5_out_norm, w_out, ffn2_norm, ffn2_w_gate, ffn2_w_up, ffn2_w_down, final_norm, loss_target, m_meta_tokens, m_ffn1_norm, m_ffn1_w_gate, m_ffn1_w_up, m_ffn1_w_down, m_mix_norm, m_w_in, m_rg_conv_w, m_rg_conv_b, m_rg_w_a, m_rg_b_a, m_rg_w_x, m_rg_b_x, m_rg_lambda, m_s5_lambda_re, m_s5_lambda_im, m_s5_log_dt, m_s5_b_re, m_s5_b_im, m_s5_c_re, m_s5_c_im, m_s5_d, m_s5_glu_w, m_s5_glu_b, m_rg_out_norm, m_s5_out_norm, m_w_out, m_ffn2_norm, m_ffn2_w_gate, m_ffn2_w_up, m_ffn2_w_down, m_final_norm, v_meta_tokens, v_ffn1_norm, v_ffn1_w_gate, v_ffn1_w_up, v_ffn1_w_down, v_mix_norm, v_w_in, v_rg_conv_w, v_rg_conv_b, v_rg_w_a, v_rg_b_a, v_rg_w_x, v_rg_b_x, v_rg_lambda, v_s5_lambda_re, v_s5_lambda_im, v_s5_log_dt, v_s5_b_re, v_s5_b_im, v_s5_c_re, v_s5_c_im, v_s5_d, v_s5_glu_w, v_s5_glu_b, v_rg_out_norm, v_s5_out_norm, v_w_out, v_ffn2_norm, v_ffn2_w_gate, v_ffn2_w_up, v_ffn2_w_down, v_final_norm):
    given = dict(x=x, meta_tokens=meta_tokens, ffn1_norm=ffn1_norm, ffn1_w_gate=ffn1_w_gate, ffn1_w_up=ffn1_w_up, ffn1_w_down=ffn1_w_down, mix_norm=mix_norm, w_in=w_in, rg_conv_w=rg_conv_w, rg_conv_b=rg_conv_b, rg_w_a=rg_w_a, rg_b_a=rg_b_a, rg_w_x=rg_w_x, rg_b_x=rg_b_x, rg_lambda=rg_lambda, s5_lambda_re=s5_lambda_re, s5_lambda_im=s5_lambda_im, s5_log_dt=s5_log_dt, s5_b_re=s5_b_re, s5_b_im=s5_b_im, s5_c_re=s5_c_re, s5_c_im=s5_c_im, s5_d=s5_d, s5_glu_w=s5_glu_w, s5_glu_b=s5_glu_b, rg_out_norm=rg_out_norm, s5_out_norm=s5_out_norm, w_out=w_out, ffn2_norm=ffn2_norm, ffn2_w_gate=ffn2_w_gate, ffn2_w_up=ffn2_w_up, ffn2_w_down=ffn2_w_down, final_norm=final_norm, loss_target=loss_target, m_meta_tokens=m_meta_tokens, m_ffn1_norm=m_ffn1_norm, m_ffn1_w_gate=m_ffn1_w_gate, m_ffn1_w_up=m_ffn1_w_up, m_ffn1_w_down=m_ffn1_w_down, m_mix_norm=m_mix_norm, m_w_in=m_w_in, m_rg_conv_w=m_rg_conv_w, m_rg_conv_b=m_rg_conv_b, m_rg_w_a=m_rg_w_a, m_rg_b_a=m_rg_b_a, m_rg_w_x=m_rg_w_x, m_rg_b_x=m_rg_b_x, m_rg_lambda=m_rg_lambda, m_s5_lambda_re=m_s5_lambda_re, m_s5_lambda_im=m_s5_lambda_im, m_s5_log_dt=m_s5_log_dt, m_s5_b_re=m_s5_b_re, m_s5_b_im=m_s5_b_im, m_s5_c_re=m_s5_c_re, m_s5_c_im=m_s5_c_im, m_s5_d=m_s5_d, m_s5_glu_w=m_s5_glu_w, m_s5_glu_b=m_s5_glu_b, m_rg_out_norm=m_rg_out_norm, m_s5_out_norm=m_s5_out_norm, m_w_out=m_w_out, m_ffn2_norm=m_ffn2_norm, m_ffn2_w_gate=m_ffn2_w_gate, m_ffn2_w_up=m_ffn2_w_up, m_ffn2_w_down=m_ffn2_w_down, m_final_norm=m_final_norm, v_meta_tokens=v_meta_tokens, v_ffn1_norm=v_ffn1_norm, v_ffn1_w_gate=v_ffn1_w_gate, v_ffn1_w_up=v_ffn1_w_up, v_ffn1_w_down=v_ffn1_w_down, v_mix_norm=v_mix_norm, v_w_in=v_w_in, v_rg_conv_w=v_rg_conv_w, v_rg_conv_b=v_rg_conv_b, v_rg_w_a=v_rg_w_a, v_rg_b_a=v_rg_b_a, v_rg_w_x=v_rg_w_x, v_rg_b_x=v_rg_b_x, v_rg_lambda=v_rg_lambda, v_s5_lambda_re=v_s5_lambda_re, v_s5_lambda_im=v_s5_lambda_im, v_s5_log_dt=v_s5_log_dt, v_s5_b_re=v_s5_b_re, v_s5_b_im=v_s5_b_im, v_s5_c_re=v_s5_c_re, v_s5_c_im=v_s5_c_im, v_s5_d=v_s5_d, v_s5_glu_w=v_s5_glu_w, v_s5_glu_b=v_s5_glu_b, v_rg_out_norm=v_rg_out_norm, v_s5_out_norm=v_s5_out_norm, v_w_out=v_w_out, v_ffn2_norm=v_ffn2_norm, v_ffn2_w_gate=v_ffn2_w_gate, v_ffn2_w_up=v_ffn2_w_up, v_ffn2_w_down=v_ffn2_w_down, v_final_norm=v_final_norm)
    weights = {n: given[n] for n in TWIN_WEIGHTS}
    shared = {n: given[n] for n in SHARED_INPUTS}
    per_example = {n: given[n] for n in ['x']}
    grad_fn = _jax.value_and_grad(_loss, argnums=(0, 1))

    def one_microbatch(ex, loss_target):
        ex = dict(ex)
        diff = ex.pop(TWIN_DIFF_INPUT)
        return grad_fn(weights, diff, {**shared, **ex}, loss_target)

    if N_MICROBATCH == 1:
        loss, (grad_w, grad_x) = one_microbatch(per_example, given["loss_target"])
    else:
        def body(carry, xs):
            loss_sum, grad_sum = carry
            l_k, (gw_k, gx_k) = one_microbatch(xs[0], xs[1])
            with _jax.named_scope("update"):
                return (loss_sum + l_k, _jax.tree.map(_jnp.add, grad_sum, gw_k)), gx_k

        init = (_jnp.zeros((), _jnp.float32), _jax.tree.map(_jnp.zeros_like, weights))
        (loss, grad_w), grad_x = _jax.lax.scan(body, init, (per_example, given["loss_target"]))
    with _jax.named_scope("update"):
        delta_w, new_m, new_v = {}, {}, {}
        for n in TWIN_WEIGHTS:
            delta_w[n], new_m[n], new_v[n] = _adamw(weights[n], grad_w[n], given["m_" + n], given["v_" + n])
    return (loss, grad_x, *[grad_w[n] for n in TWIN_WEIGHTS], *[delta_w[n] for n in TWIN_WEIGHTS],
            *[new_m[n] for n in TWIN_WEIGHTS], *[new_v[n] for n in TWIN_WEIGHTS])
```

```python
import functools
import math

import jax
import jax.numpy as jnp
from jax import lax
from jax.experimental import pallas as pl
from jax.experimental.pallas import tpu as pltpu

F32, BF16 = jnp.float32, jnp.bfloat16
SDS = jax.ShapeDtypeStruct
MESH_ID = pl.DeviceIdType.MESH
N_DEV = 8
N_CHIP = 4
LANES = 128
SUBLANES = 8
VMEM_LIMIT = 56 * 1024 * 1024

EPS = 1e-6
RG_C = 8.0
N_META = 16
CONV_WIDTH = 4
S5_GROUP = 16
S5_STATE = 64
GROUPS_PER_BLOCK = LANES // S5_GROUP
ADAM_LR, ADAM_B1, ADAM_B2, ADAM_EPS, ADAM_WD, ADAM_STEP = 0.001, 0.9, 0.999, 1e-08, 0.01, 10

NN = (((1,), (0,)), ((), ()))
NT = (((1,), (1,)), ((), ()))
TN = (((0,), (0,)), ((), ()))


def _pick(n, target, mult=16):
    if n <= target:
        return n
    best = None
    for d in range(mult, target + 1, mult):
        if n % d == 0:
            best = d
    assert best is not None, (n, target, mult)
    return best


def _row_tile(nrows, ncols, itembytes=4, budget=2 * 1024 * 1024):
    return _pick(nrows, max(16, budget // (ncols * itembytes)))


def _params(sem):
    return pltpu.CompilerParams(dimension_semantics=sem, vmem_limit_bytes=VMEM_LIMIT)


def _rowcall(name, fn, nrows, tm, rows, fulls, row_outs, acc_outs=()):
    n_in = len(rows) + len(fulls)
    in_specs = []
    for arr, lead, cb, C in rows:
        if arr.ndim == 3:
            in_specs.append(pl.BlockSpec((None, tm, C), lambda i, lead=lead, cb=cb: (lead, i, cb)))
        else:
            in_specs.append(pl.BlockSpec((tm, C), lambda i, cb=cb: (i, cb)))
    for f in fulls:
        in_specs.append(pl.BlockSpec(f.shape, lambda i, nd=f.ndim: (0,) * nd))
    out_specs = [pl.BlockSpec((tm, C), lambda i: (i, 0)) for C, _ in row_outs]
    out_shape = [SDS((nrows, C), dt) for C, dt in row_outs]
    for shp in acc_outs:
        out_specs.append(pl.BlockSpec(shp, lambda i, nd=len(shp): (0,) * nd))
        out_shape.append(SDS(shp, F32))
    n_row_out = len(row_outs)

    def body(*refs):
        i = pl.program_id(0)
        res = fn(i, *[r[...] for r in refs[:n_in]])
        outs = refs[n_in:]
        for k in range(n_row_out):
            outs[k][...] = res[k].astype(outs[k].dtype)
        if acc_outs:
            @pl.when(i == 0)
            def _():
                for o in outs[n_row_out:]:
                    o[...] = jnp.zeros_like(o)
            for k in range(n_row_out, len(outs)):
                outs[k][...] += res[k].astype(F32)

    return pl.pallas_call(
        body, grid=(nrows // tm,), in_specs=in_specs, out_specs=out_specs, out_shape=out_shape,
        name=name, compiler_params=_params(("arbitrary",)))(*[r[0] for r in rows], *fulls)


def _mm(name, terms, dims, grid, nk, out_shape, out_spec, acc_shape, scale=1.0, res=None):
    n_t = len(terms)
    kax = len(grid) - 1

    def body(*refs):
        k = pl.program_id(kax)
        o_ref, acc = refs[-2], refs[-1]

        @pl.when(k == 0)
        def _():
            acc[...] = jnp.zeros_like(acc)

        for t in range(n_t):
            acc[...] += lax.dot_general(refs[2 * t][...].astype(BF16), refs[2 * t + 1][...].astype(BF16), dims,
                                        preferred_element_type=F32)

        @pl.when(k == nk - 1)
        def _():
            r = acc[...] * scale
            if res is not None:
                r = r + refs[2 * n_t][...].astype(F32)
            o_ref[...] = r.astype(o_ref.dtype)

    ops, specs = [], []
    for a, a_spec, b, b_spec in terms:
        ops += [a, b]
        specs += [a_spec, b_spec]
    if res is not None:
        ops.append(res[0])
        specs.append(res[1])
    sem = ("arbitrary",) * len(grid)
    return pl.pallas_call(
        body, grid=grid, in_specs=specs, out_specs=out_spec, out_shape=out_shape,
        scratch_shapes=[pltpu.VMEM(acc_shape, F32)], name=name, compiler_params=_params(sem))(*ops)


def _mm_bcast(name, a, w, dims, std_out, out_dtype, tm_target=384):
    M, K = a.shape
    J = w.shape[0]
    Nb = w.shape[2] if dims == NN else w.shape[1]
    tm = _pick(M, tm_target)
    a_spec = pl.BlockSpec((tm, K), lambda j, i, k: (i, 0))
    w_spec = pl.BlockSpec((None,) + w.shape[1:], lambda j, i, k: (j, 0, 0))
    if std_out:
        out_shape, out_spec = SDS((M, J * Nb), out_dtype), pl.BlockSpec((tm, Nb), lambda j, i, k: (i, j))
    else:
        out_shape, out_spec = SDS((J, M, Nb), out_dtype), pl.BlockSpec((None, tm, Nb), lambda j, i, k: (j, i, 0))
    return _mm(name, [(a, a_spec, w, w_spec)], dims, (J, M // tm, 1), 1, out_shape, out_spec, (tm, Nb))


def _a_blk(a, tm, Kb, off=0):
    if a.ndim == 3:
        return pl.BlockSpec((None, tm, Kb), lambda i, n, j: (j, i, 0))
    return pl.BlockSpec((tm, Kb), lambda i, n, j, off=off: (i, j + off))


def _mm_red(name, pairs, dims, out_dtype, scale=1.0, res=None, tm_target=704, tn_target=1024):
    a0, w0 = pairs[0][0], pairs[0][1]
    J = w0.shape[0]
    M = a0.shape[-2]
    N = w0.shape[2] if dims == NN else w0.shape[1]
    Kb = w0.shape[1] if dims == NN else w0.shape[2]
    tm, tn = _pick(M, tm_target), _pick(N, tn_target, LANES)
    terms = []
    for a, w, *rest in pairs:
        off = rest[0] if rest else 0
        if dims == NN:
            w_spec = pl.BlockSpec((None, Kb, tn), lambda i, n, j: (j, 0, n))
        else:
            w_spec = pl.BlockSpec((None, tn, Kb), lambda i, n, j: (j, n, 0))
        terms.append((a, _a_blk(a, tm, Kb, off), w, w_spec))
    out_spec = pl.BlockSpec((tm, tn), lambda i, n, j: (i, n))
    r = None if res is None else (res, out_spec)
    return _mm(name, terms, dims, (M // tm, N // tn, J), J, SDS((M, N), out_dtype), out_spec, (tm, tn), scale, r)


def _mm_tn(name, a, b, J, Ka, Nb, out_dtype, a_off=0, b_off=0, scale=1.0, tw_target=1024, tk_target=384):
    M = a.shape[-2]
    tk = _pick(M, tk_target)
    tka = _pick(Ka, tw_target, LANES) if Ka % LANES == 0 else Ka
    tnb = _pick(Nb, tw_target, LANES) if Nb % LANES == 0 else Nb

    def spec(arr, width, tw, is_a, off):
        nb = width // tw

        def wi(m, n):
            return m if is_a else n
        if arr.ndim == 3:
            return pl.BlockSpec((None, tk, tw), lambda j, m, n, k: (j, k, wi(m, n)))
        if arr.shape[1] == width:
            return pl.BlockSpec((tk, tw), lambda j, m, n, k: (k, wi(m, n)))
        return pl.BlockSpec((tk, tw), lambda j, m, n, k: (k, (j + off) * nb + wi(m, n)))

    a_spec = spec(a, Ka, tka, True, a_off)
    b_spec = spec(b, Nb, tnb, False, b_off)
    out_spec = pl.BlockSpec((None, tka, tnb), lambda j, m, n, k: (j, m, n))
    return _mm(name, [(a, a_spec, b, b_spec)], TN, (J, Ka // tka, Nb // tnb, M // tk), M // tk,
               SDS((J, Ka, Nb), out_dtype), out_spec, (tka, tnb), scale)


def _mm_bdiag(name, pairs, dims, Kb, Nb, out_dtype, res=None, tm_target=704):
    a0, w0 = pairs[0][0], pairs[0][1]
    M, J = a0.shape[0], w0.shape[0]
    tm = _pick(M, tm_target)
    terms = []
    for a, w, off in pairs:
        a_spec = pl.BlockSpec((tm, Kb), lambda j, i, k, off=off: (i, j + off))
        w_spec = pl.BlockSpec((None,) + w.shape[1:], lambda j, i, k: (j, 0, 0))
        terms.append((a, a_spec, w, w_spec))
    out_spec = pl.BlockSpec((tm, Nb), lambda j, i, k: (i, j))
    r = None if res is None else (res[0], pl.BlockSpec((tm, Nb), lambda j, i, k, off=res[1]: (i, j + off)))
    return _mm(name, terms, dims, (J, M // tm, 1), 1, SDS((M, J * Nb), out_dtype), out_spec, (tm, Nb), 1.0, r)


def _rmsnorm(x, g):
    x = x.astype(F32)
    return x * lax.rsqrt(jnp.mean(x * x, axis=-1, keepdims=True) + EPS) * g


def _swiglu_act(gate, up):
    gate, up = gate.astype(F32), up.astype(F32)
    return gate * jax.nn.sigmoid(gate) * up


def _neg_expm1(x):
    series = -x * (1.0 + x * (1.0 / 2 + x * (1.0 / 6 + x * (1.0 / 24 + x * (1.0 / 120 + x * (1.0 / 720))))))
    return jnp.where(x > -0.25, series, 1.0 - jnp.exp(x))


def _rg_gates(xc, wa, ba, wx, bx, lam):
    heads, hd = wa.shape[0], wa.shape[1]
    xb = xc.astype(BF16)
    rs, ig = [], []
    for h in range(heads):
        xh = xb[:, h * hd:(h + 1) * hd]
        rs.append(jnp.dot(xh, wa[h].astype(BF16), preferred_element_type=F32))
        ig.append(jnp.dot(xh, wx[h].astype(BF16), preferred_element_type=F32))
    r = jax.nn.sigmoid(jnp.concatenate(rs, axis=1) + ba)
    ii = jax.nn.sigmoid(jnp.concatenate(ig, axis=1) + bx)
    log_a = -RG_C * r * jax.nn.softplus(-lam)
    a = jnp.exp(log_a)
    mult = jnp.sqrt(_neg_expm1(2.0 * log_a))
    return a, mult * ii * xc


def _rg_out(h, g, gain):
    return _rmsnorm(h * jax.nn.gelu(g), gain)


def _s5_pre(y, u, d):
    return jax.nn.gelu(y + d * u)


def _s5_out(z, zz, glu_b, gain):
    return _rmsnorm(z * jax.nn.sigmoid(zz + glu_b), gain)


def _adamw(w, g, m, v):
    m = ADAM_B1 * m + (1.0 - ADAM_B1) * g
    v = ADAM_B2 * v + (1.0 - ADAM_B2) * jnp.square(g)
    m_hat = m / (1.0 - ADAM_B1 ** ADAM_STEP)
    v_hat = v / (1.0 - ADAM_B2 ** ADAM_STEP)
    delta = -ADAM_LR * (m_hat / (jnp.sqrt(v_hat) + ADAM_EPS) + ADAM_WD * w)
    return delta, m, v


def _conv_fwd(proj, conv_w, conv_b, R, tm):
    T = proj.shape[0]
    tpb = tm // SUBLANES

    def body(cur_ref, prev_ref, w_ref, b_ref, o_ref):
        i = pl.program_id(0)
        cur = cur_ref[...]
        prev = jnp.where(i > 0, prev_ref[...], 0.0)
        ext = jnp.concatenate([prev, cur], axis=0)
        acc = b_ref[...] + w_ref[CONV_WIDTH - 1:CONV_WIDTH, :] * cur
        for s in range(1, CONV_WIDTH):
            acc = acc + w_ref[CONV_WIDTH - 1 - s:CONV_WIDTH - s, :] * pltpu.roll(ext, s, 0)[SUBLANES:, :]
        o_ref[...] = acc

    return pl.pallas_call(
        body, grid=(T // tm,),
        in_specs=[pl.BlockSpec((tm, R), lambda i: (i, 0)),
                  pl.BlockSpec((SUBLANES, R), lambda i: (jnp.maximum(i * tpb - 1, 0), 0)),
                  pl.BlockSpec((CONV_WIDTH, R), lambda i: (0, 0)), pl.BlockSpec((1, R), lambda i: (0, 0))],
        out_specs=pl.BlockSpec((tm, R), lambda i: (i, 0)), out_shape=SDS((T, R), F32),
        name="rg_conv_fwd", compiler_params=_params(("arbitrary",)))(proj, proj, conv_w, conv_b)


def _conv_bwd(dxc, proj, conv_w, R, tm):
    T = dxc.shape[0]
    tpb = tm // SUBLANES
    nt = T // tm
    n_ext = tm + SUBLANES

    def body(d_ref, dnext_ref, u_ref, uprev_ref, w_ref, du_ref, dw_ref, db_ref):
        i = pl.program_id(0)
        d = d_ref[...]
        dnext = jnp.where(i < nt - 1, dnext_ref[...], 0.0)
        dext = jnp.concatenate([d, dnext], axis=0)
        u = u_ref[...]
        uprev = jnp.where(i > 0, uprev_ref[...], 0.0)
        uext = jnp.concatenate([uprev, u], axis=0)
        du = w_ref[CONV_WIDTH - 1:CONV_WIDTH, :] * d
        dws = [jnp.sum(d * u, axis=0, keepdims=True)]
        for s in range(1, CONV_WIDTH):
            du = du + w_ref[CONV_WIDTH - 1 - s:CONV_WIDTH - s, :] * pltpu.roll(dext, n_ext - s, 0)[:tm, :]
            dws.append(jnp.sum(d * pltpu.roll(uext, s, 0)[SUBLANES:, :], axis=0, keepdims=True))
        du_ref[...] = du

        @pl.when(i == 0)
        def _():
            dw_ref[...] = jnp.zeros_like(dw_ref)
            db_ref[...] = jnp.zeros_like(db_ref)

        dw_ref[...] += jnp.concatenate(dws[::-1], axis=0)
        db_ref[...] += jnp.sum(d, axis=0, keepdims=True)

    row = pl.BlockSpec((tm, R), lambda i: (i, 0))
    return pl.pallas_call(
        body, grid=(nt,),
        in_specs=[row, pl.BlockSpec((SUBLANES, R), lambda i: (jnp.minimum((i + 1) * tpb, T // SUBLANES - 1), 0)),
                  row, pl.BlockSpec((SUBLANES, R), lambda i: (jnp.maximum(i * tpb - 1, 0), 0)),
                  pl.BlockSpec((CONV_WIDTH, R), lambda i: (0, 0))],
        out_specs=[row, pl.BlockSpec((CONV_WIDTH, R), lambda i: (0, 0)), pl.BlockSpec((1, R), lambda i: (0, 0))],
        out_shape=[SDS((T, R), F32), SDS((CONV_WIDTH, R), F32), SDS((1, R), F32)],
        name="rg_conv_bwd", compiler_params=_params(("arbitrary",)))(dxc, dxc, proj, proj, conv_w)


def _scan_tiles(tb):
    return tb // SUBLANES


def _rg_scan_fwd(a, b, tb, cb):
    T, W = a.shape

    def body(a_ref, b_ref, h_ref, p_ref, carry):
        @pl.when(pl.program_id(1) == 0)
        def _():
            carry[...] = jnp.zeros_like(carry)

        def tile(t, h):
            ds = pl.ds(pl.multiple_of(t * SUBLANES, SUBLANES), SUBLANES)
            a8, b8 = a_ref[ds, :], b_ref[ds, :]
            hs, ps = [], []
            for j in range(SUBLANES):
                ps.append(h)
                h = a8[j:j + 1, :] * h + b8[j:j + 1, :]
                hs.append(h)
            h_ref[ds, :] = jnp.concatenate(hs, axis=0)
            p_ref[ds, :] = jnp.concatenate(ps, axis=0)
            return h

        carry[0:1, :] = lax.fori_loop(0, _scan_tiles(tb), tile, carry[0:1, :])

    blk = pl.BlockSpec((tb, cb), lambda c, i: (i, c))
    return pl.pallas_call(
        body, grid=(W // cb, T // tb), in_specs=[blk, blk], out_specs=[blk, blk],
        out_shape=[SDS((T, W), F32)] * 2, scratch_shapes=[pltpu.VMEM((SUBLANES, cb), F32)],
        name="rg_scan_fwd", compiler_params=_params(("arbitrary", "arbitrary")))(a, b)


def _rg_scan_bwd(dh, a, hprev, tb, cb):
    T, W = a.shape
    nt = T // tb

    def body(g_ref, a_ref, p_ref, db_ref, da_ref, carry):
        @pl.when(pl.program_id(1) == 0)
        def _():
            carry[...] = jnp.zeros_like(carry)

        def tile(tt, c):
            t = _scan_tiles(tb) - 1 - tt
            ds = pl.ds(pl.multiple_of(t * SUBLANES, SUBLANES), SUBLANES)
            g8, a8 = g_ref[ds, :], a_ref[ds, :]
            adjs = [None] * SUBLANES
            for j in range(SUBLANES - 1, -1, -1):
                adj = g8[j:j + 1, :] + c
                adjs[j] = adj
                c = a8[j:j + 1, :] * adj
            adj8 = jnp.concatenate(adjs, axis=0)
            db_ref[ds, :] = adj8
            da_ref[ds, :] = adj8 * p_ref[ds, :]
            return c

        carry[0:1, :] = lax.fori_loop(0, _scan_tiles(tb), tile, carry[0:1, :])

    blk = pl.BlockSpec((tb, cb), lambda c, i: (nt - 1 - i, c))
    return pl.pallas_call(
        body, grid=(W // cb, nt), in_specs=[blk, blk, blk], out_specs=[blk, blk],
        out_shape=[SDS((T, W), F32)] * 2, scratch_shapes=[pltpu.VMEM((SUBLANES, cb), F32)],
        name="rg_scan_bwd", compiler_params=_params(("arbitrary", "arbitrary")))(dh, a, hprev)


def _s5_scan_fwd(bur, bui, lr, li, tb, cb):
    T, W = bur.shape

    def body(br_ref, bi_ref, lr_ref, li_ref, xr_ref, xi_ref, pr_ref, pi_ref, cr, ci):
        @pl.when(pl.program_id(1) == 0)
        def _():
            cr[...] = jnp.zeros_like(cr)
            ci[...] = jnp.zeros_like(ci)

        lrv, liv = lr_ref[...], li_ref[...]

        def tile(t, carry):
            xr, xi = carry
            ds = pl.ds(pl.multiple_of(t * SUBLANES, SUBLANES), SUBLANES)
            br8, bi8 = br_ref[ds, :], bi_ref[ds, :]
            rr, ri, qr, qi = [], [], [], []
            for j in range(SUBLANES):
                qr.append(xr)
                qi.append(xi)
                xr, xi = (lrv * xr - liv * xi + br8[j:j + 1, :], lrv * xi + liv * xr + bi8[j:j + 1, :])
                rr.append(xr)
                ri.append(xi)
            xr_ref[ds, :] = jnp.concatenate(rr, axis=0)
            xi_ref[ds, :] = jnp.concatenate(ri, axis=0)
            pr_ref[ds, :] = jnp.concatenate(qr, axis=0)
            pi_ref[ds, :] = jnp.concatenate(qi, axis=0)
            return xr, xi

        xr, xi = lax.fori_loop(0, _scan_tiles(tb), tile, (cr[0:1, :], ci[0:1, :]))
        cr[0:1, :] = xr
        ci[0:1, :] = xi

    blk = pl.BlockSpec((tb, cb), lambda c, i: (i, c))
    par = pl.BlockSpec((1, cb), lambda c, i: (0, c))
    return pl.pallas_call(
        body, grid=(W // cb, T // tb), in_specs=[blk, blk, par, par], out_specs=[blk] * 4,
        out_shape=[SDS((T, W), F32)] * 4,
        scratch_shapes=[pltpu.VMEM((SUBLANES, cb), F32), pltpu.VMEM((SUBLANES, cb), F32)],
        name="s5_scan_fwd", compiler_params=_params(("arbitrary", "arbitrary")))(bur, bui, lr, li)


def _s5_scan_bwd(gr, gi, pr, pi, lr, li, tb, cb):
    T, W = gr.shape
    nt = T // tb

    def body(gr_ref, gi_ref, pr_ref, pi_ref, lr_ref, li_ref, ar_ref, ai_ref, dlr_ref, dli_ref, cr, ci):
        @pl.when(pl.program_id(1) == 0)
        def _():
            cr[...] = jnp.zeros_like(cr)
            ci[...] = jnp.zeros_like(ci)
            dlr_ref[...] = jnp.zeros_like(dlr_ref)
            dli_ref[...] = jnp.zeros_like(dli_ref)

        lrv, liv = lr_ref[...], li_ref[...]

        def tile(tt, carry):
            ar, ai = carry
            t = _scan_tiles(tb) - 1 - tt
            ds = pl.ds(pl.multiple_of(t * SUBLANES, SUBLANES), SUBLANES)
            gr8, gi8 = gr_ref[ds, :], gi_ref[ds, :]
            rr, ri = [None] * SUBLANES, [None] * SUBLANES
            for j in range(SUBLANES - 1, -1, -1):
                ar, ai = (gr8[j:j + 1, :] + lrv * ar + liv * ai, gi8[j:j + 1, :] - liv * ar + lrv * ai)
                rr[j], ri[j] = ar, ai
            ar8, ai8 = jnp.concatenate(rr, axis=0), jnp.concatenate(ri, axis=0)
            ar_ref[ds, :] = ar8
            ai_ref[ds, :] = ai8
            pr8, pi8 = pr_ref[ds, :], pi_ref[ds, :]
            dlr_ref[...] += ar8 * pr8 + ai8 * pi8
            dli_ref[...] += ai8 * pr8 - ar8 * pi8
            return ar, ai

        ar, ai = lax.fori_loop(0, _scan_tiles(tb), tile, (cr[0:1, :], ci[0:1, :]))
        cr[0:1, :] = ar
        ci[0:1, :] = ai

    blk = pl.BlockSpec((tb, cb), lambda c, i: (nt - 1 - i, c))
    par = pl.BlockSpec((1, cb), lambda c, i: (0, c))
    acc = pl.BlockSpec((SUBLANES, cb), lambda c, i: (0, c))
    return pl.pallas_call(
        body, grid=(W // cb, nt), in_specs=[blk] * 4 + [par, par], out_specs=[blk, blk, acc, acc],
        out_shape=[SDS((T, W), F32)] * 2 + [SDS((SUBLANES, W), F32)] * 2,
        scratch_shapes=[pltpu.VMEM((SUBLANES, cb), F32), pltpu.VMEM((SUBLANES, cb), F32)],
        name="s5_scan_bwd", compiler_params=_params(("arbitrary", "arbitrary")))(gr, gi, pr, pi, lr, li)


_ANY = pl.BlockSpec(memory_space=pl.ANY)


def _all_gather(name, shard):
    def body(x_ref, out_ref, send_sems, recv_sems, local_sem):
        x, y, c = lax.axis_index("x"), lax.axis_index("y"), lax.axis_index("c")
        me, sibling = (x, y, c), (x, y, 1 - c)
        chips = [(1 - x, y), (x, 1 - y), (1 - x, 1 - y)]

        def slot(px, py, pc):
            return out_ref.at[4 * px + 2 * py + pc]

        def copy(k, block, to, src=None):
            return pltpu.make_async_remote_copy(
                src_ref=slot(*block) if src is None else src, dst_ref=slot(*block),
                send_sem=send_sems.at[k], recv_sem=recv_sems.at[k], device_id=to, device_id_type=MESH_ID)

        mine = pltpu.make_async_copy(x_ref, slot(*me), local_sem)
        mine.start()
        first = [copy(0, me, sibling, src=x_ref)]
        first += [copy(1 + j, me, (*chip, c), src=x_ref) for j, chip in enumerate(chips)]
        for cp in first:
            cp.start()
        passed = [copy(4 + j, (*chip, c), sibling) for j, chip in enumerate(chips)]
        for j, chip in enumerate(chips):
            copy(1 + j, (*chip, c), me).wait_recv()
            passed[j].start()
        copy(0, sibling, me).wait_recv()
        for j, chip in enumerate(chips):
            copy(4 + j, (*chip, 1 - c), me).wait_recv()
        for cp in first + passed:
            cp.wait_send()
        mine.wait()

    return pl.pallas_call(
        body, out_shape=SDS((N_DEV,) + shard.shape, shard.dtype), in_specs=[_ANY], out_specs=_ANY,
        scratch_shapes=[pltpu.SemaphoreType.DMA((7,)), pltpu.SemaphoreType.DMA((7,)), pltpu.SemaphoreType.DMA],
        name=name)(shard)


def _rs_d2d(name, g):
    def body(g_ref, r_ref, send_sems, recv_sems):
        x, y, c = lax.axis_index("x"), lax.axis_index("y"), lax.axis_index("c")
        copies = [pltpu.make_async_remote_copy(
            src_ref=g_ref.at[2 * k + (1 - c)], dst_ref=r_ref.at[k], send_sem=send_sems.at[k], recv_sem=recv_sems.at[k],
            device_id=(x, y, 1 - c), device_id_type=MESH_ID) for k in range(N_CHIP)]
        for cp in copies:
            cp.start()
        for cp in copies:
            cp.wait_recv()
        for cp in copies:
            cp.wait_send()

    return pl.pallas_call(
        body, out_shape=SDS((N_CHIP,) + g.shape[1:], g.dtype), in_specs=[_ANY], out_specs=_ANY,
        scratch_shapes=[pltpu.SemaphoreType.DMA((N_CHIP,)), pltpu.SemaphoreType.DMA((N_CHIP,))], name=name)(g)


def _pair_add(name, g, r1, c_idx):
    _, R, C = g.shape
    tr = _row_tile(R, C, 2)

    def body(c_ref, g_ref, r_ref, o_ref):
        o_ref[...] = (g_ref[...].astype(F32) + r_ref[...].astype(F32)).astype(o_ref.dtype)

    grid_spec = pltpu.PrefetchScalarGridSpec(
        num_scalar_prefetch=1, grid=(N_CHIP, R // tr),
        in_specs=[pl.BlockSpec((None, tr, C), lambda k, i, c_ref: (2 * k + c_ref[0], i, 0)),
                  pl.BlockSpec((None, tr, C), lambda k, i, c_ref: (k, i, 0))],
        out_specs=pl.BlockSpec((None, tr, C), lambda k, i, c_ref: (k, i, 0)))
    return pl.pallas_call(body, grid_spec=grid_spec, out_shape=SDS((N_CHIP, R, C), g.dtype), name=name,
                          compiler_params=_params(("arbitrary", "arbitrary")))(c_idx, g, r1)


def _rs_ici(name, p):
    def body(p_ref, r_ref, send_sems, recv_sems, local_sem):
        x, y, c = lax.axis_index("x"), lax.axis_index("y"), lax.axis_index("c")
        myk = 2 * x + y
        chips = [(1 - x, y), (x, 1 - y), (1 - x, 1 - y)]
        mine = pltpu.make_async_copy(p_ref.at[myk], r_ref.at[myk], local_sem)
        mine.start()
        sends = [pltpu.make_async_remote_copy(
            src_ref=p_ref.at[2 * px + py], dst_ref=r_ref.at[myk], send_sem=send_sems.at[j], recv_sem=recv_sems.at[j],
            device_id=(px, py, c), device_id_type=MESH_ID) for j, (px, py) in enumerate(chips)]
        for cp in sends:
            cp.start()
        for j, (px, py) in enumerate(chips):
            pltpu.make_async_remote_copy(
                src_ref=p_ref.at[myk], dst_ref=r_ref.at[2 * px + py], send_sem=send_sems.at[j], recv_sem=recv_sems.at[j],
                device_id=(px, py, c), device_id_type=MESH_ID).wait_recv()
        for cp in sends:
            cp.wait_send()
        mine.wait()

    return pl.pallas_call(
        body, out_shape=SDS(p.shape, p.dtype), in_specs=[_ANY], out_specs=_ANY,
        scratch_shapes=[pltpu.SemaphoreType.DMA((3,)), pltpu.SemaphoreType.DMA((3,)), pltpu.SemaphoreType.DMA],
        name=name)(p)


PACK_ROWS = 64


def _pack(arrs):
    flat = jnp.concatenate([a.reshape(-1).astype(F32) for a in arrs])
    n = flat.shape[0]
    unit = PACK_ROWS * LANES
    padded = -(-n // unit) * unit
    return jnp.pad(flat, (0, padded - n)).reshape(padded // LANES, LANES)


def _unpack(buf, shapes):
    flat = buf.reshape(-1)
    outs, off = [], 0
    for shp in shapes:
        n = math.prod(shp)
        outs.append(flat[off:off + n].reshape(shp))
        off += n
    return outs


def _s5_discretise(lre, lim, log_dt, bre, bim):
    dt = jnp.exp(log_dt)[:, None]
    e_m1 = jnp.expm1(lre * dt)
    th = lim * dt
    lr = (e_m1 + 1.0) * jnp.cos(th)
    li = (e_m1 + 1.0) * jnp.sin(th)
    lr_m1 = e_m1 * jnp.cos(th) - 2.0 * jnp.square(jnp.sin(0.5 * th))
    den = lre * lre + lim * lim
    cr = (lr_m1 * lre + li * lim) / den
    ci = (li * lre - lr_m1 * lim) / den
    bbr = cr[..., None] * bre - ci[..., None] * bim
    bbi = cr[..., None] * bim + ci[..., None] * bre
    return lr, li, bbr, bbi


def _expand_diag(m, rows_first):
    G, A, B = m.shape
    q = G // GROUPS_PER_BLOCK
    eye = jnp.eye(GROUPS_PER_BLOCK, dtype=m.dtype)
    m5 = m.reshape(q, GROUPS_PER_BLOCK, A, 1, B) * eye[None, :, None, :, None]
    return m5.reshape(q, GROUPS_PER_BLOCK * A, GROUPS_PER_BLOCK * B)


def _extract_diag(m, A, B):
    q = m.shape[0]
    m5 = m.reshape(q, GROUPS_PER_BLOCK, A, GROUPS_PER_BLOCK, B)
    d = jnp.stack([m5[:, g, :, g, :] for g in range(GROUPS_PER_BLOCK)], axis=1)
    return d.reshape(q * GROUPS_PER_BLOCK, A, B)


def _ffn_fwd(tag, h, gain, wg, wu, wd, Tp):
    D = h.shape[1]
    J, _, Fb = wg.shape
    tmn = _row_tile(Tp, D)
    (n,) = _rowcall(f"{tag}_norm", lambda i, x, g: (_rmsnorm(x, g),), Tp, tmn, [(h, 0, 0, D)], [gain], [(D, BF16)])
    gate = _mm_bcast(f"{tag}_gate", n, wg, NN, False, BF16)
    up = _mm_bcast(f"{tag}_up", n, wu, NN, False, BF16)
    rows = J * Tp
    tma = _row_tile(rows, Fb, 2, 1024 * 1024)
    g2, u2 = gate.reshape(rows, Fb), up.reshape(rows, Fb)
    (act,) = _rowcall(f"{tag}_act", lambda i, g, u: (_swiglu_act(g, u),), rows, tma,
                      [(g2, 0, 0, Fb), (u2, 0, 0, Fb)], [], [(Fb, BF16)])
    act = act.reshape(J, Tp, Fb)
    h_out = _mm_red(f"{tag}_down", [(act, wd)], NN, F32, scale=0.5, res=h)
    return h_out, (n, gate, up, act)


def _ffn_bwd(tag, dh, h, gain, wg, wu, wd, saved, Tp):
    n, gate, up, act = saved
    D = h.shape[1]
    J, _, Fb = wg.shape
    dact = _mm_bcast(f"{tag}_dact", dh, wd, NT, False, BF16)
    dwd = _mm_tn(f"{tag}_dwd", act, dh, J, Fb, D, BF16, scale=0.5)
    rows = J * Tp
    tma = _row_tile(rows, Fb, 2, 1024 * 1024)

    def act_bwd(i, g, u, d):
        _, vjp = jax.vjp(_swiglu_act, g, u)
        return vjp(0.5 * d.astype(F32))

    dgate, dup = _rowcall(f"{tag}_act_bwd", act_bwd, rows, tma,
                          [(gate.reshape(rows, Fb), 0, 0, Fb), (up.reshape(rows, Fb), 0, 0, Fb),
                           (dact.reshape(rows, Fb), 0, 0, Fb)], [], [(Fb, BF16), (Fb, BF16)])
    dgate, dup = dgate.reshape(J, Tp, Fb), dup.reshape(J, Tp, Fb)
    dn = _mm_red(f"{tag}_dn", [(dgate, wg), (dup, wu)], NT, F32)
    dwg = _mm_tn(f"{tag}_dwg", n, dgate, J, D, Fb, BF16)
    dwu = _mm_tn(f"{tag}_dwu", n, dup, J, D, Fb, BF16)
    dh_in, dgain = _norm_bwd(f"{tag}_norm_bwd", h, gain, dn, dh, Tp)
    return dh_in, dgain, dwg, dwu, dwd


def _norm_bwd(name, h, gain, dn, dres, Tp):
    D = h.shape[1]

    def fn(i, x, d, r, g):
        _, vjp = jax.vjp(_rmsnorm, x, g)
        dx, dg = vjp(d.astype(F32))
        return r + dx, dg

    return _rowcall(name, fn, Tp, _row_tile(Tp, D), [(h, 0, 0, D), (dn, 0, 0, D), (dres, 0, 0, D)], [gain],
                    [(D, F32)], [(1, D)])


def kernel(x, meta_tokens, ffn1_norm, ffn1_w_gate, ffn1_w_up, ffn1_w_down, mix_norm, w_in, rg_conv_w, rg_conv_b, rg_w_a, rg_b_a, rg_w_x, rg_b_x, rg_lambda, s5_lambda_re, s5_lambda_im, s5_log_dt, s5_b_re, s5_b_im, s5_c_re, s5_c_im, s5_d, s5_glu_w, s5_glu_b, rg_out_norm, s5_out_norm, w_out, ffn2_norm, ffn2_w_gate, ffn2_w_up, ffn2_w_down, final_norm, loss_target, m_meta_tokens, m_ffn1_norm, m_ffn1_w_gate, m_ffn1_w_up, m_ffn1_w_down, m_mix_norm, m_w_in, m_rg_conv_w, m_rg_conv_b, m_rg_w_a, m_rg_b_a, m_rg_w_x, m_rg_b_x, m_rg_lambda, m_s5_lambda_re, m_s5_lambda_im, m_s5_log_dt, m_s5_b_re, m_s5_b_im, m_s5_c_re, m_s5_c_im, m_s5_d, m_s5_glu_w, m_s5_glu_b, m_rg_out_norm, m_s5_out_norm, m_w_out, m_ffn2_norm, m_ffn2_w_gate, m_ffn2_w_up, m_ffn2_w_down, m_final_norm, v_meta_tokens, v_ffn1_norm, v_ffn1_w_gate, v_ffn1_w_up, v_ffn1_w_down, v_mix_norm, v_w_in, v_rg_conv_w, v_rg_conv_b, v_rg_w_a, v_rg_b_a, v_rg_w_x, v_rg_b_x, v_rg_lambda, v_s5_lambda_re, v_s5_lambda_im, v_s5_log_dt, v_s5_b_re, v_s5_b_im, v_s5_c_re, v_s5_c_im, v_s5_d, v_s5_glu_w, v_s5_glu_b, v_rg_out_norm, v_s5_out_norm, v_w_out, v_ffn2_norm, v_ffn2_w_gate, v_ffn2_w_up, v_ffn2_w_down, v_final_norm):
    weights = dict(
        meta_tokens=meta_tokens, ffn1_norm=ffn1_norm, ffn1_w_gate=ffn1_w_gate, ffn1_w_up=ffn1_w_up, ffn1_w_down=ffn1_w_down,
        mix_norm=mix_norm, w_in=w_in, rg_conv_w=rg_conv_w, rg_conv_b=rg_conv_b, rg_w_a=rg_w_a, rg_b_a=rg_b_a, rg_w_x=rg_w_x,
        rg_b_x=rg_b_x, rg_lambda=rg_lambda, s5_lambda_re=s5_lambda_re, s5_lambda_im=s5_lambda_im, s5_log_dt=s5_log_dt,
        s5_b_re=s5_b_re, s5_b_im=s5_b_im, s5_c_re=s5_c_re, s5_c_im=s5_c_im, s5_d=s5_d, s5_glu_w=s5_glu_w, s5_glu_b=s5_glu_b,
        rg_out_norm=rg_out_norm, s5_out_norm=s5_out_norm, w_out=w_out, ffn2_norm=ffn2_norm, ffn2_w_gate=ffn2_w_gate,
        ffn2_w_up=ffn2_w_up, ffn2_w_down=ffn2_w_down, final_norm=final_norm)
    moments_m = dict(
        meta_tokens=m_meta_tokens, ffn1_norm=m_ffn1_norm, ffn1_w_gate=m_ffn1_w_gate, ffn1_w_up=m_ffn1_w_up,
        ffn1_w_down=m_ffn1_w_down, mix_norm=m_mix_norm, w_in=m_w_in, rg_conv_w=m_rg_conv_w, rg_conv_b=m_rg_conv_b,
        rg_w_a=m_rg_w_a, rg_b_a=m_rg_b_a, rg_w_x=m_rg_w_x, rg_b_x=m_rg_b_x, rg_lambda=m_rg_lambda,
        s5_lambda_re=m_s5_lambda_re, s5_lambda_im=m_s5_lambda_im, s5_log_dt=m_s5_log_dt, s5_b_re=m_s5_b_re,
        s5_b_im=m_s5_b_im, s5_c_re=m_s5_c_re, s5_c_im=m_s5_c_im, s5_d=m_s5_d, s5_glu_w=m_s5_glu_w, s5_glu_b=m_s5_glu_b,
        rg_out_norm=m_rg_out_norm, s5_out_norm=m_s5_out_norm, w_out=m_w_out, ffn2_norm=m_ffn2_norm,
        ffn2_w_gate=m_ffn2_w_gate, ffn2_w_up=m_ffn2_w_up, ffn2_w_down=m_ffn2_w_down, final_norm=m_final_norm)
    moments_v = dict(
        meta_tokens=v_meta_tokens, ffn1_norm=v_ffn1_norm, ffn1_w_gate=v_ffn1_w_gate, ffn1_w_up=v_ffn1_w_up,
        ffn1_w_down=v_ffn1_w_down, mix_norm=v_mix_norm, w_in=v_w_in, rg_conv_w=v_rg_conv_w, rg_conv_b=v_rg_conv_b,
        rg_w_a=v_rg_w_a, rg_b_a=v_rg_b_a, rg_w_x=v_rg_w_x, rg_b_x=v_rg_b_x, rg_lambda=v_rg_lambda,
        s5_lambda_re=v_s5_lambda_re, s5_lambda_im=v_s5_lambda_im, s5_log_dt=v_s5_log_dt, s5_b_re=v_s5_b_re,
        s5_b_im=v_s5_b_im, s5_c_re=v_s5_c_re, s5_c_im=v_s5_c_im, s5_d=v_s5_d, s5_glu_w=v_s5_glu_w, s5_glu_b=v_s5_glu_b,
        rg_out_norm=v_rg_out_norm, s5_out_norm=v_s5_out_norm, w_out=v_w_out, ffn2_norm=v_ffn2_norm,
        ffn2_w_gate=v_ffn2_w_gate, ffn2_w_up=v_ffn2_w_up, ffn2_w_down=v_ffn2_w_down, final_norm=v_final_norm)
    order = list(weights)

    seq, D = x.shape[1], x.shape[2]
    R = rg_conv_b.shape[1]
    S = s5_d.shape[1]
    G, N, C = s5_b_re.shape[1:]
    heads, hd = rg_w_a.shape[1], rg_w_a.shape[3]
    Q = G // GROUPS_PER_BLOCK
    W = G * N
    NB = GROUPS_PER_BLOCK * N
    T = N_META + seq
    Tp = -(-T // LANES) * LANES
    me = 4 * lax.axis_index("x") + 2 * lax.axis_index("y") + lax.axis_index("c")
    c_idx = lax.axis_index("c").astype(jnp.int32).reshape(1)

    big = ["ffn1_w_gate", "ffn1_w_up", "ffn1_w_down", "w_in", "s5_glu_w", "w_out", "ffn2_w_gate", "ffn2_w_up", "ffn2_w_down"]
    full = {k: _all_gather(f"ag_{k}", weights[k][0].astype(BF16)) for k in big}
    sharded_small = ["meta_tokens", "rg_conv_w", "rg_w_a", "rg_b_a", "rg_w_x", "rg_b_x"]
    sm = _all_gather("ag_small", _pack([weights[k] for k in sharded_small]))
    sm = [jnp.stack(p) for p in zip(*[_unpack(sm[d], [weights[k].shape for k in sharded_small]) for d in range(N_DEV)])]
    meta_full = jnp.moveaxis(sm[0], 0, 1).reshape(N_META, D)
    conv_w_full = jnp.moveaxis(sm[1][:, 0], 0, 1).reshape(CONV_WIDTH, R)
    wa_full = jnp.moveaxis(sm[2][:, 0], 0, 1).reshape(heads, hd, hd)
    ba_full = jnp.moveaxis(sm[3][:, 0], 0, 1).reshape(1, R)
    wx_full = jnp.moveaxis(sm[4][:, 0], 0, 1).reshape(heads, hd, hd)
    bx_full = jnp.moveaxis(sm[5][:, 0], 0, 1).reshape(1, R)

    lam_fn = functools.partial(_s5_discretise)
    (lr, li, bbr, bbi), disc_vjp = jax.vjp(lam_fn, s5_lambda_re[0], s5_lambda_im[0], s5_log_dt[0], s5_b_re[0], s5_b_im[0])
    lr_row, li_row = lr.reshape(1, W), li.reshape(1, W)
    wb_r = _expand_diag(jnp.swapaxes(bbr, 1, 2), True)
    wb_i = _expand_diag(jnp.swapaxes(bbi, 1, 2), True)
    wc_r = _expand_diag(jnp.swapaxes(s5_c_re[0], 1, 2), True)
    wc_i = _expand_diag(-jnp.swapaxes(s5_c_im[0], 1, 2), True)

    h0 = jnp.concatenate([meta_full, x[0], jnp.zeros((Tp - T, D), F32)], axis=0)
    tgt = jnp.concatenate([jnp.zeros((N_META, D), F32), loss_target[0], jnp.zeros((Tp - T, D), F32)], axis=0)

    h1, ffn1_saved = _ffn_fwd("ffn1", h0, ffn1_norm, full["ffn1_w_gate"], full["ffn1_w_up"], full["ffn1_w_down"], Tp)

    tmd = _row_tile(Tp, D)
    (n2,) = _rowcall("mix_norm", lambda i, a, g: (_rmsnorm(a, g),), Tp, tmd, [(h1, 0, 0, D)], [mix_norm], [(D, BF16)])
    proj = _mm_bcast("w_in", n2, full["w_in"], NN, True, F32)

    tmr = _row_tile(Tp, R)
    tb = _pick(Tp, 256, SUBLANES)
    xc = _conv_fwd(proj, conv_w_full, rg_conv_b, R, tmr)
    a_t, b_t = _rowcall("rg_gates", lambda i, *a: _rg_gates(*a), Tp, tmr, [(xc, 0, 0, R)],
                        [wa_full, ba_full, wx_full, bx_full, rg_lambda], [(R, F32), (R, F32)])
    h_rg, hprev = _rg_scan_fwd(a_t, b_t, tb, _pick(R, 512, LANES))
    (yn_rg,) = _rowcall("rg_out", lambda i, *a: (_rg_out(*a),), Tp, tmr, [(h_rg, 0, 0, R), (proj, 0, 1, R)],
                        [rg_out_norm], [(R, BF16)])

    u_off = 2 * R // LANES
    bur = _mm_bdiag("s5_bu_re", [(proj, wb_r, u_off)], NN, LANES, NB, F32)
    bui = _mm_bdiag("s5_bu_im", [(proj, wb_i, u_off)], NN, LANES, NB, F32)
    cbs = _pick(W, 512, LANES)
    xr, xi, xpr, xpi = _s5_scan_fwd(bur, bui, lr_row, li_row, tb, cbs)
    y_s5 = _mm_bdiag("s5_y", [(xr, wc_r, 0), (xi, wc_i, 0)], NN, NB, LANES, F32)
    tms = _row_tile(Tp, S)
    s_col = 2 * R // S
    (z,) = _rowcall("s5_pre", lambda i, *a: (_s5_pre(*a),), Tp, tms, [(y_s5, 0, 0, S), (proj, 0, s_col, S)], [s5_d],
                    [(S, F32)])
    gw = full["s5_glu_w"]
    zz = _mm_red("s5_glu", [(z, gw)], NN, F32)
    (yn_s5,) = _rowcall("s5_out", lambda i, *a: (_s5_out(*a),), Tp, tms, [(z, 0, 0, S), (zz, 0, 0, S)],
                        [s5_glu_b, s5_out_norm], [(S, BF16)])
    yn = jnp.concatenate([yn_rg, yn_s5], axis=1)
    h2 = _mm_red("w_out", [(yn, full["w_out"])], NN, F32, res=h1)

    h3, ffn2_saved = _ffn_fwd("ffn2", h2, ffn2_norm, full["ffn2_w_gate"], full["ffn2_w_up"], full["ffn2_w_down"], Tp)

    def final(i, hh, tt, g):
        out, vjp = jax.vjp(_rmsnorm, hh, g)
        row = i * tmd + lax.broadcasted_iota(jnp.int32, (tmd, 1), 0)
        valid = jnp.logical_and(row >= N_META, row < T)
        err = jnp.where(valid, out - tt, 0.0)
        part = 0.5 * jnp.sum(jnp.mean(err * err, axis=-1, keepdims=True))
        dx, dg = vjp(err * (1.0 / D))
        return dx, dg, jnp.full((SUBLANES, LANES), part, F32)

    dh3, d_final_norm, loss_part = _rowcall("final", final, Tp, tmd, [(h3, 0, 0, D), (tgt, 0, 0, D)],
                                            [final_norm.reshape(1, D)], [(D, F32)], [(1, D), (SUBLANES, LANES)])
    loss = lax.psum(loss_part[0, 0], ("x", "y", "c"))

    grads = {}
    dh2, grads["ffn2_norm"], g_g2, g_u2, g_d2 = _ffn_bwd(
        "ffn2", dh3, h2, ffn2_norm, full["ffn2_w_gate"], full["ffn2_w_up"], full["ffn2_w_down"], ffn2_saved, Tp)

    Kb_out = full["w_out"].shape[1]
    dyn = _mm_bcast("w_out_dx", dh2, full["w_out"], NT, True, F32)
    g_wout = _mm_tn("w_out_dw", yn, dh2, N_DEV, Kb_out, D, BF16)

    def s5_out_bwd(i, zv, zzv, d, gb, gn):
        _, vjp = jax.vjp(_s5_out, zv, zzv, gb, gn)
        return vjp(d)

    dz_a, dzz, grads["s5_glu_b"], grads["s5_out_norm"] = _rowcall(
        "s5_out_bwd", s5_out_bwd, Tp, tms, [(z, 0, 0, S), (zz, 0, 0, S), (dyn, 0, R // S, S)],
        [s5_glu_b, s5_out_norm], [(S, F32), (S, BF16)], [(1, S), (1, S)])
    Kb_glu = gw.shape[1]
    dz = _mm_bcast("s5_glu_dx", dzz, gw, NT, True, F32)
    g_glu = _mm_tn("s5_glu_dw", z, dzz, N_DEV, Kb_glu, S, BF16)

    def s5_pre_bwd(i, yv, uv, d1, d2, dd):
        _, vjp = jax.vjp(_s5_pre, yv, uv, dd)
        return vjp(d1 + d2)

    dy, du_a, grads["s5_d"] = _rowcall(
        "s5_pre_bwd", s5_pre_bwd, Tp, tms, [(y_s5, 0, 0, S), (proj, 0, s_col, S), (dz_a, 0, 0, S), (dz, 0, 0, S)],
        [s5_d], [(S, F32), (S, F32)], [(1, S)])
    gxr = _mm_bdiag("s5_dx_re", [(dy, wc_r, 0)], NT, LANES, NB, F32)
    gxi = _mm_bdiag("s5_dx_im", [(dy, wc_i, 0)], NT, LANES, NB, F32)
    d_wc_r = _mm_tn("s5_dc_re", xr, dy, Q, NB, LANES, F32)
    d_wc_i = _mm_tn("s5_dc_im", xi, dy, Q, NB, LANES, F32)
    ar, ai, dlr8, dli8 = _s5_scan_bwd(gxr, gxi, xpr, xpi, lr_row, li_row, tb, cbs)
    du_s5 = _mm_bdiag("s5_du", [(ar, wb_r, 0), (ai, wb_i, 0)], NT, NB, LANES, BF16, res=(du_a, 0))
    d_wb_r = _mm_tn("s5_db_re", proj, ar, Q, LANES, NB, F32, a_off=u_off)
    d_wb_i = _mm_tn("s5_db_im", proj, ai, Q, LANES, NB, F32, a_off=u_off)
    d_bbr = jnp.swapaxes(_extract_diag(d_wb_r, C, N), 1, 2)
    d_bbi = jnp.swapaxes(_extract_diag(d_wb_i, C, N), 1, 2)
    d_lr = jnp.sum(dlr8, axis=0).reshape(G, N)
    d_li = jnp.sum(dli8, axis=0).reshape(G, N)
    d_lre, d_lim, d_logdt, d_bre, d_bim = disc_vjp((d_lr, d_li, d_bbr, d_bbi))
    grads["s5_lambda_re"], grads["s5_lambda_im"], grads["s5_log_dt"] = d_lre[None], d_lim[None], d_logdt[None]
    grads["s5_b_re"], grads["s5_b_im"] = d_bre[None], d_bim[None]
    grads["s5_c_re"] = jnp.swapaxes(_extract_diag(d_wc_r, N, C), 1, 2)[None]
    grads["s5_c_im"] = -jnp.swapaxes(_extract_diag(d_wc_i, N, C), 1, 2)[None]

    def rg_out_bwd(i, hv, gv, d, gn):
        _, vjp = jax.vjp(_rg_out, hv, gv, gn)
        return vjp(d)

    dh_scan, dg_rg, grads["rg_out_norm"] = _rowcall(
        "rg_out_bwd", rg_out_bwd, Tp, tmr, [(h_rg, 0, 0, R), (proj, 0, 1, R), (dyn, 0, 0, R)], [rg_out_norm],
        [(R, F32), (R, BF16)], [(1, R)])
    db_t, da_t = _rg_scan_bwd(dh_scan, a_t, hprev, tb, _pick(R, 512, LANES))

    def rg_gates_bwd(i, xv, da, db, wa, ba, wx, bx, lam):
        _, vjp = jax.vjp(_rg_gates, xv, wa, ba, wx, bx, lam)
        return vjp((da, db))

    dxc, d_wa, d_ba, d_wx, d_bx, grads["rg_lambda"] = _rowcall(
        "rg_gates_bwd", rg_gates_bwd, Tp, _row_tile(Tp, R, 4, 1024 * 1024),
        [(xc, 0, 0, R), (da_t, 0, 0, R), (db_t, 0, 0, R)], [wa_full, ba_full, wx_full, bx_full, rg_lambda],
        [(R, F32)], [(heads, hd, hd), (1, R), (heads, hd, hd), (1, R), (1, R)])
    du_rg, d_conv_w, grads["rg_conv_b"] = _conv_bwd(dxc, proj, conv_w_full, R, tmr)

    dproj = jnp.concatenate([du_rg.astype(BF16), dg_rg, du_s5], axis=1)
    win = full["w_in"]
    Nb_in = win.shape[2]
    dn2 = _mm_red("w_in_dx", [(dproj, win)], NT, F32)
    g_win = _mm_tn("w_in_dw", n2, dproj, N_DEV, D, Nb_in, BF16)
    dh1, grads["mix_norm"] = _norm_bwd("mix_norm_bwd", h1, mix_norm, dn2, dh2, Tp)

    dh0, grads["ffn1_norm"], g_g1, g_u1, g_d1 = _ffn_bwd(
        "ffn1", dh1, h0, ffn1_norm, full["ffn1_w_gate"], full["ffn1_w_up"], full["ffn1_w_down"], ffn1_saved, Tp)
    grad_x = dh0[N_META:T][None]
    grads["final_norm"] = d_final_norm.reshape(D)

    partial = dict(ffn1_w_gate=g_g1, ffn1_w_up=g_u1, ffn1_w_down=g_d1, w_in=g_win, s5_glu_w=g_glu, w_out=g_wout,
                   ffn2_w_gate=g_g2, ffn2_w_up=g_u2, ffn2_w_down=g_d2)
    outs = {}
    for k in big:
        g = partial[k]
        r1 = _rs_d2d(f"rs1_{k}", g)
        p = _pair_add(f"add_{k}", g, r1, c_idx)
        r2 = _rs_ici(f"rs2_{k}", p)
        _, Rk, Ck = r2.shape
        w2, m2, v2 = weights[k][0], moments_m[k][0], moments_v[k][0]

        def big_update(i, wv, mv, vv, p0, p1, p2, p3):
            gsum = (p0.astype(F32) + p1.astype(F32)) + (p2.astype(F32) + p3.astype(F32))
            return (gsum,) + _adamw(wv, gsum, mv, vv)

        res = _rowcall(f"adam_{k}", big_update, Rk, _row_tile(Rk, Ck, 4, 1024 * 1024),
                       [(w2, 0, 0, Ck), (m2, 0, 0, Ck), (v2, 0, 0, Ck)] + [(r2, kk, 0, Ck) for kk in range(N_CHIP)],
                       [], [(Ck, F32)] * 4)
        outs[k] = [o[None] for o in res]

    replicated = ["ffn1_norm", "mix_norm", "rg_conv_b", "rg_lambda", "s5_lambda_re", "s5_lambda_im", "s5_log_dt",
                  "s5_b_re", "s5_b_im", "s5_c_re", "s5_c_im", "s5_d", "s5_glu_b", "rg_out_norm", "s5_out_norm",
                  "ffn2_norm", "final_norm"]
    small_full = [grads[k].reshape(weights[k].shape) for k in replicated]
    small_full += [dh0[:N_META], d_conv_w, d_wa, d_ba, d_wx, d_bx]
    small_shapes = [a.shape for a in small_full]
    gathered = _all_gather("ag_grads", _pack(small_full))
    n_rows = gathered.shape[1]

    def sum8(i, *parts):
        s = parts[0]
        for q in parts[1:]:
            s = s + q
        return (s,)

    (summed,) = _rowcall("small_sum", sum8, n_rows, _pick(n_rows, 512, SUBLANES),
                         [(gathered, d, 0, LANES) for d in range(N_DEV)], [], [(LANES, F32)])
    summed = _unpack(summed, small_shapes)
    g_small = dict(zip(replicated, summed[:len(replicated)]))
    d_meta, d_cw, d_wa_s, d_ba_s, d_wx_s, d_bx_s = summed[len(replicated):]

    def shard_of(a, axis):
        n = a.shape[axis] // N_DEV
        return lax.dynamic_slice_in_dim(a, me * n, n, axis)

    g_small["meta_tokens"] = shard_of(d_meta, 1)
    g_small["rg_conv_w"] = shard_of(d_cw, 1)[None]
    g_small["rg_w_a"] = shard_of(d_wa_s, 1)[None]
    g_small["rg_b_a"] = shard_of(d_ba_s.reshape(heads, hd), 1)[None]
    g_small["rg_w_x"] = shard_of(d_wx_s, 1)[None]
    g_small["rg_b_x"] = shard_of(d_bx_s.reshape(heads, hd), 1)[None]
    small = replicated + sharded_small
    shapes = [weights[k].shape for k in small]
    gp = _pack([g_small[k] for k in small])
    n_rows = gp.shape[0]

    def small_update(i, wv, gv, mv, vv):
        return _adamw(wv, gv, mv, vv)

    res = _rowcall("adam_small", small_update, n_rows, _pick(n_rows, 512, SUBLANES),
                   [(_pack([weights[k] for k in small]), 0, 0, LANES), (gp, 0, 0, LANES),
                    (_pack([moments_m[k] for k in small]), 0, 0, LANES), (_pack([moments_v[k] for k in small]), 0, 0, LANES)],
                   [], [(LANES, F32)] * 3)
    unpacked = [_unpack(r, shapes) for r in res]
    for idx, k in enumerate(small):
        outs[k] = [g_small[k].reshape(weights[k].shape)] + [u[idx] for u in unpacked]

    return (loss, grad_x, *[outs[k][0] for k in order], *[outs[k][1] for k in order],
            *[outs[k][2] for k in order], *[outs[k][3] for k in order])
```

```python
import functools
import math

import jax
import jax.numpy as jnp
from jax import lax
from jax.experimental import pallas as pl
from jax.experimental.pallas import tpu as pltpu

F32, BF16 = jnp.float32, jnp.bfloat16
SDS = jax.ShapeDtypeStruct
MESH_ID = pl.DeviceIdType.MESH
N_DEV = 8
N_CHIP = 4
LANES = 128
SUBLANES = 8
VMEM_LIMIT = 56 * 1024 * 1024

EPS = 1e-6
RG_C = 8.0
N_META = 16
CONV_WIDTH = 4
S5_GROUP = 16
S5_STATE = 64
GROUPS_PER_BLOCK = LANES // S5_GROUP
ADAM_LR, ADAM_B1, ADAM_B2, ADAM_EPS, ADAM_WD, ADAM_STEP = 0.001, 0.9, 0.999, 1e-08, 0.01, 10

NN = (((1,), (0,)), ((), ()))
NT = (((1,), (1,)), ((), ()))
TN = (((0,), (0,)), ((), ()))


def _pick(n, target, mult=16):
    if n <= target:
        return n
    best = None
    for d in range(mult, target + 1, mult):
        if n % d == 0:
            best = d
    assert best is not None, (n, target, mult)
    return best


def _row_tile(nrows, ncols, itembytes=4, budget=2 * 1024 * 1024):
    return _pick(nrows, max(16, budget // (ncols * itembytes)))


def _params(sem):
    return pltpu.CompilerParams(dimension_semantics=sem, vmem_limit_bytes=VMEM_LIMIT)


def _rowcall(name, fn, nrows, tm, rows, fulls, row_outs, acc_outs=()):
    n_in = len(rows) + len(fulls)
    in_specs = []
    for arr, lead, cb, C in rows:
        if arr.ndim == 3:
            in_specs.append(pl.BlockSpec((None, tm, C), lambda i, lead=lead, cb=cb: (lead, i, cb)))
        else:
            in_specs.append(pl.BlockSpec((tm, C), lambda i, cb=cb: (i, cb)))
    for f in fulls:
        in_specs.append(pl.BlockSpec(f.shape, lambda i, nd=f.ndim: (0,) * nd))
    out_specs = [pl.BlockSpec((tm, C), lambda i: (i, 0)) for C, _ in row_outs]
    out_shape = [SDS((nrows, C), dt) for C, dt in row_outs]
    for shp in acc_outs:
        out_specs.append(pl.BlockSpec(shp, lambda i, nd=len(shp): (0,) * nd))
        out_shape.append(SDS(shp, F32))
    n_row_out = len(row_outs)

    def body(*refs):
        i = pl.program_id(0)
        res = fn(i, *[r[...] for r in refs[:n_in]])
        outs = refs[n_in:]
        for k in range(n_row_out):
            outs[k][...] = res[k].astype(outs[k].dtype)
        if acc_outs:
            @pl.when(i == 0)
            def _():
                for o in outs[n_row_out:]:
                    o[...] = jnp.zeros_like(o)
            for k in range(n_row_out, len(outs)):
                outs[k][...] += res[k].astype(F32)

    return pl.pallas_call(
        body, grid=(nrows // tm,), in_specs=in_specs, out_specs=out_specs, out_shape=out_shape,
        name=name, compiler_params=_params(("arbitrary",)))(*[r[0] for r in rows], *fulls)


class _Rider:
    def __init__(self, ins, outs, sems, steps):
        self.ins, self.outs, self.sems, self.steps = list(ins), list(outs), list(sems), steps


def _rider_counts(riders):
    return (sum(len(r.ins) for r in riders), sum(len(r.outs) for r in riders), sum(len(r.sems) for r in riders))


def _rider_hooks(riders, in_refs, out_refs, sem_refs):
    hooks, i, o, s = [], 0, 0, 0
    for r in riders:
        hooks.append(r.steps(*in_refs[i:i + len(r.ins)], *out_refs[o:o + len(r.outs)], *sem_refs[s:s + len(r.sems)]))
        i, o, s = i + len(r.ins), o + len(r.outs), s + len(r.sems)
    return hooks


def _linear_step(grid):
    step = pl.program_id(0)
    for ax in range(1, len(grid)):
        step = step * grid[ax] + pl.program_id(ax)
    return step


def _ride_begin(step, hooks):
    if hooks:
        @pl.when(step == 0)
        def _():
            for start, _, _ in hooks:
                start()


def _ride_end(step, nsteps, hooks):
    if hooks:
        if any(mid is not None for _, mid, _ in hooks):
            @pl.when(step == nsteps // 2)
            def _():
                for _, mid, _ in hooks:
                    if mid is not None:
                        mid()

        @pl.when(step == nsteps - 1)
        def _():
            for _, _, finish in hooks:
                finish()


def _mm(name, terms, dims, grid, nk, out_shape, out_spec, acc_shape, scale=1.0, res=None, riders=()):
    n_t = len(terms)
    kax = len(grid) - 1
    n_in = 2 * n_t + (1 if res is not None else 0)
    r_in, r_out, _ = _rider_counts(riders)
    nsteps = math.prod(grid)
    n_acc = 1 if nk > 1 else 0

    def body(*refs):
        ins, rin = refs[:n_in], refs[n_in:n_in + r_in]
        o_ref = refs[n_in + r_in]
        rout = refs[n_in + r_in + 1:n_in + r_in + 1 + r_out]
        scratch = refs[n_in + r_in + 1 + r_out:]
        step = _linear_step(grid)
        k = pl.program_id(kax)
        hooks = _rider_hooks(riders, rin, rout, scratch[n_acc:])
        _ride_begin(step, hooks)

        def product():
            r = None
            for t in range(n_t):
                d = lax.dot_general(ins[2 * t][...].astype(BF16), ins[2 * t + 1][...].astype(BF16), dims,
                                    preferred_element_type=F32)
                r = d if r is None else r + d
            return r

        def emit(r):
            r = r * scale
            if res is not None:
                r = r + ins[2 * n_t][...].astype(F32)
            o_ref[...] = r.astype(o_ref.dtype)

        if nk == 1:
            emit(product())
        else:
            acc = scratch[0]

            @pl.when(k == 0)
            def _():
                acc[...] = jnp.zeros_like(acc)

            acc[...] += product()

            @pl.when(k == nk - 1)
            def _():
                emit(acc[...])

        _ride_end(step, nsteps, hooks)

    ops, specs = [], []
    for a, a_spec, b, b_spec in terms:
        ops += [a, b]
        specs += [a_spec, b_spec]
    if res is not None:
        ops.append(res[0])
        specs.append(res[1])
    out_shapes, out_specs, scratch = [out_shape], [out_spec], []
    if n_acc:
        scratch.append(pltpu.VMEM(acc_shape, F32))
    for r in riders:
        ops += r.ins
        specs += [_ANY] * len(r.ins)
        out_shapes += r.outs
        out_specs += [_ANY] * len(r.outs)
        scratch += r.sems
    sem = ("arbitrary",) * len(grid)
    outs = pl.pallas_call(
        body, grid=grid, in_specs=specs, out_specs=out_specs, out_shape=out_shapes,
        scratch_shapes=scratch, name=name, compiler_params=_params(sem))(*ops)
    return (outs[0], list(outs[1:])) if riders else outs[0]


def _mm_bcast(name, a, w, dims, std_out, out_dtype, tm_target=384, riders=()):
    M, K = a.shape
    J = w.shape[0]
    Nb = w.shape[2] if dims == NN else w.shape[1]
    tm = _pick(M, tm_target)
    a_spec = pl.BlockSpec((tm, K), lambda j, i, k: (i, 0))
    w_spec = pl.BlockSpec((None,) + w.shape[1:], lambda j, i, k: (j, 0, 0))
    if std_out:
        out_shape, out_spec = SDS((M, J * Nb), out_dtype), pl.BlockSpec((tm, Nb), lambda j, i, k: (i, j))
    else:
        out_shape, out_spec = SDS((J, M, Nb), out_dtype), pl.BlockSpec((None, tm, Nb), lambda j, i, k: (j, i, 0))
    return _mm(name, [(a, a_spec, w, w_spec)], dims, (J, M // tm, 1), 1, out_shape, out_spec, (tm, Nb), riders=riders)


def _a_blk(a, tm, Kb, off=0):
    if a.ndim == 3:
        return pl.BlockSpec((None, tm, Kb), lambda i, n, j: (j, i, 0))
    return pl.BlockSpec((tm, Kb), lambda i, n, j, off=off: (i, j + off))


def _mm_red(name, pairs, dims, out_dtype, scale=1.0, res=None, tm_target=704, tn_target=1024, n_off=0, n_len=None,
            riders=()):
    a0, w0 = pairs[0][0], pairs[0][1]
    J = w0.shape[0]
    M = a0.shape[-2]
    N = w0.shape[2] if dims == NN else w0.shape[1]
    Kb = w0.shape[1] if dims == NN else w0.shape[2]
    if n_len is not None:
        N = n_len
    tm, tn = _pick(M, tm_target), _pick(N, tn_target, LANES)
    nb0 = n_off // tn
    terms = []
    for a, w, *rest in pairs:
        off = rest[0] if rest else 0
        if dims == NN:
            w_spec = pl.BlockSpec((None, Kb, tn), lambda i, n, j: (j, 0, n + nb0))
        else:
            w_spec = pl.BlockSpec((None, tn, Kb), lambda i, n, j: (j, n + nb0, 0))
        terms.append((a, _a_blk(a, tm, Kb, off), w, w_spec))
    out_spec = pl.BlockSpec((tm, tn), lambda i, n, j: (i, n))
    r = None if res is None else (res, out_spec)
    return _mm(name, terms, dims, (M // tm, N // tn, J), J, SDS((M, N), out_dtype), out_spec, (tm, tn), scale, r, riders)


def _mm_tn(name, a, b, J, Ka, Nb, out_dtype, a_off=0, b_off=0, scale=1.0, tw_target=1024, tk_target=1408, riders=()):
    M = a.shape[-2]
    tk = _pick(M, tk_target)
    tka = _pick(Ka, tw_target, LANES) if Ka % LANES == 0 else Ka
    tnb = _pick(Nb, tw_target, LANES) if Nb % LANES == 0 else Nb

    def spec(arr, width, tw, is_a, off):
        nb = width // tw

        def wi(m, n):
            return m if is_a else n
        if arr.ndim == 3:
            return pl.BlockSpec((None, tk, tw), lambda j, m, n, k: (j, k, wi(m, n)))
        if arr.shape[1] == width:
            return pl.BlockSpec((tk, tw), lambda j, m, n, k: (k, wi(m, n)))
        return pl.BlockSpec((tk, tw), lambda j, m, n, k: (k, (j + off) * nb + wi(m, n)))

    a_spec = spec(a, Ka, tka, True, a_off)
    b_spec = spec(b, Nb, tnb, False, b_off)
    out_spec = pl.BlockSpec((None, tka, tnb), lambda j, m, n, k: (j, m, n))
    return _mm(name, [(a, a_spec, b, b_spec)], TN, (J, Ka // tka, Nb // tnb, M // tk), M // tk,
               SDS((J, Ka, Nb), out_dtype), out_spec, (tka, tnb), scale, riders=riders)


def _mm_bdiag(name, pairs, dims, Kb, Nb, out_dtype, res=None, tm_target=704, riders=()):
    a0, w0 = pairs[0][0], pairs[0][1]
    M, J = a0.shape[0], w0.shape[0]
    tm = _pick(M, tm_target)
    terms = []
    for a, w, off in pairs:
        a_spec = pl.BlockSpec((tm, Kb), lambda j, i, k, off=off: (i, j + off))
        w_spec = pl.BlockSpec((None,) + w.shape[1:], lambda j, i, k: (j, 0, 0))
        terms.append((a, a_spec, w, w_spec))
    out_spec = pl.BlockSpec((tm, Nb), lambda j, i, k: (i, j))
    r = None if res is None else (res[0], pl.BlockSpec((tm, Nb), lambda j, i, k, off=res[1]: (i, j + off)))
    return _mm(name, terms, dims, (J, M // tm, 1), 1, SDS((M, J * Nb), out_dtype), out_spec, (tm, Nb), 1.0, r, riders)


def _rmsnorm(x, g):
    x = x.astype(F32)
    return x * lax.rsqrt(jnp.mean(x * x, axis=-1, keepdims=True) + EPS) * g


def _swiglu_act(gate, up):
    gate, up = gate.astype(F32), up.astype(F32)
    return gate * jax.nn.sigmoid(gate) * up


def _neg_expm1(x):
    series = -x * (1.0 + x * (1.0 / 2 + x * (1.0 / 6 + x * (1.0 / 24 + x * (1.0 / 120 + x * (1.0 / 720))))))
    return jnp.where(x > -0.25, series, 1.0 - jnp.exp(x))


def _rg_gates(xc, wa, ba, wx, bx, lam):
    heads, hd = wa.shape[0], wa.shape[1]
    xb = xc.astype(BF16)
    rs, ig = [], []
    for h in range(heads):
        xh = xb[:, h * hd:(h + 1) * hd]
        rs.append(jnp.dot(xh, wa[h].astype(BF16), preferred_element_type=F32))
        ig.append(jnp.dot(xh, wx[h].astype(BF16), preferred_element_type=F32))
    r = jax.nn.sigmoid(jnp.concatenate(rs, axis=1) + ba)
    ii = jax.nn.sigmoid(jnp.concatenate(ig, axis=1) + bx)
    log_a = -RG_C * r * jax.nn.softplus(-lam)
    a = jnp.exp(log_a)
    mult = jnp.sqrt(_neg_expm1(2.0 * log_a))
    return a, mult * ii * xc


def _rg_out(h, g, gain):
    return _rmsnorm(h * jax.nn.gelu(g), gain)


def _s5_pre(y, u, d):
    return jax.nn.gelu(y + d * u)


def _s5_out(z, zz, glu_b, gain):
    return _rmsnorm(z * jax.nn.sigmoid(zz + glu_b), gain)


def _adamw(w, g, m, v):
    m = ADAM_B1 * m + (1.0 - ADAM_B1) * g
    v = ADAM_B2 * v + (1.0 - ADAM_B2) * jnp.square(g)
    m_hat = m / (1.0 - ADAM_B1 ** ADAM_STEP)
    v_hat = v / (1.0 - ADAM_B2 ** ADAM_STEP)
    delta = -ADAM_LR * (m_hat / (jnp.sqrt(v_hat) + ADAM_EPS) + ADAM_WD * w)
    return delta, m, v


def _conv_fwd(proj, conv_w, conv_b, R, tm):
    T = proj.shape[0]
    tpb = tm // SUBLANES

    def body(cur_ref, prev_ref, w_ref, b_ref, o_ref):
        i = pl.program_id(0)
        cur = cur_ref[...]
        prev = jnp.where(i > 0, prev_ref[...], 0.0)
        ext = jnp.concatenate([prev, cur], axis=0)
        acc = b_ref[...] + w_ref[CONV_WIDTH - 1:CONV_WIDTH, :] * cur
        for s in range(1, CONV_WIDTH):
            acc = acc + w_ref[CONV_WIDTH - 1 - s:CONV_WIDTH - s, :] * pltpu.roll(ext, s, 0)[SUBLANES:, :]
        o_ref[...] = acc

    return pl.pallas_call(
        body, grid=(T // tm,),
        in_specs=[pl.BlockSpec((tm, R), lambda i: (i, 0)),
                  pl.BlockSpec((SUBLANES, R), lambda i: (jnp.maximum(i * tpb - 1, 0), 0)),
                  pl.BlockSpec((CONV_WIDTH, R), lambda i: (0, 0)), pl.BlockSpec((1, R), lambda i: (0, 0))],
        out_specs=pl.BlockSpec((tm, R), lambda i: (i, 0)), out_shape=SDS((T, R), F32),
        name="rg_conv_fwd", compiler_params=_params(("arbitrary",)))(proj, proj, conv_w, conv_b)


def _conv_bwd(dxc, proj, conv_w, R, tm):
    T = dxc.shape[0]
    tpb = tm // SUBLANES
    nt = T // tm
    n_ext = tm + SUBLANES

    def body(d_ref, dnext_ref, u_ref, uprev_ref, w_ref, du_ref, dw_ref, db_ref):
        i = pl.program_id(0)
        d = d_ref[...]
        dnext = jnp.where(i < nt - 1, dnext_ref[...], 0.0)
        dext = jnp.concatenate([d, dnext], axis=0)
        u = u_ref[...]
        uprev = jnp.where(i > 0, uprev_ref[...], 0.0)
        uext = jnp.concatenate([uprev, u], axis=0)
        du = w_ref[CONV_WIDTH - 1:CONV_WIDTH, :] * d
        dws = [jnp.sum(d * u, axis=0, keepdims=True)]
        for s in range(1, CONV_WIDTH):
            du = du + w_ref[CONV_WIDTH - 1 - s:CONV_WIDTH - s, :] * pltpu.roll(dext, n_ext - s, 0)[:tm, :]
            dws.append(jnp.sum(d * pltpu.roll(uext, s, 0)[SUBLANES:, :], axis=0, keepdims=True))
        du_ref[...] = du

        @pl.when(i == 0)
        def _():
            dw_ref[...] = jnp.zeros_like(dw_ref)
            db_ref[...] = jnp.zeros_like(db_ref)

        dw_ref[...] += jnp.concatenate(dws[::-1], axis=0)
        db_ref[...] += jnp.sum(d, axis=0, keepdims=True)

    row = pl.BlockSpec((tm, R), lambda i: (i, 0))
    return pl.pallas_call(
        body, grid=(nt,),
        in_specs=[row, pl.BlockSpec((SUBLANES, R), lambda i: (jnp.minimum((i + 1) * tpb, T // SUBLANES - 1), 0)),
                  row, pl.BlockSpec((SUBLANES, R), lambda i: (jnp.maximum(i * tpb - 1, 0), 0)),
                  pl.BlockSpec((CONV_WIDTH, R), lambda i: (0, 0))],
        out_specs=[row, pl.BlockSpec((CONV_WIDTH, R), lambda i: (0, 0)), pl.BlockSpec((1, R), lambda i: (0, 0))],
        out_shape=[SDS((T, R), F32), SDS((CONV_WIDTH, R), F32), SDS((1, R), F32)],
        name="rg_conv_bwd", compiler_params=_params(("arbitrary",)))(dxc, dxc, proj, proj, conv_w)


def _scan_tiles(tb):
    return tb // SUBLANES


def _rg_scan_fwd(a, b, tb, cb):
    T, W = a.shape

    def body(a_ref, b_ref, h_ref, p_ref, carry):
        @pl.when(pl.program_id(1) == 0)
        def _():
            carry[...] = jnp.zeros_like(carry)

        def tile(t, h):
            ds = pl.ds(pl.multiple_of(t * SUBLANES, SUBLANES), SUBLANES)
            a8, b8 = a_ref[ds, :], b_ref[ds, :]
            hs, ps = [], []
            for j in range(SUBLANES):
                ps.append(h)
                h = a8[j:j + 1, :] * h + b8[j:j + 1, :]
                hs.append(h)
            h_ref[ds, :] = jnp.concatenate(hs, axis=0)
            p_ref[ds, :] = jnp.concatenate(ps, axis=0)
            return h

        carry[0:1, :] = lax.fori_loop(0, _scan_tiles(tb), tile, carry[0:1, :])

    blk = pl.BlockSpec((tb, cb), lambda c, i: (i, c))
    return pl.pallas_call(
        body, grid=(W // cb, T // tb), in_specs=[blk, blk], out_specs=[blk, blk],
        out_shape=[SDS((T, W), F32)] * 2, scratch_shapes=[pltpu.VMEM((SUBLANES, cb), F32)],
        name="rg_scan_fwd", compiler_params=_params(("arbitrary", "arbitrary")))(a, b)


def _rg_scan_bwd(dh, a, hprev, tb, cb):
    T, W = a.shape
    nt = T // tb

    def body(g_ref, a_ref, p_ref, db_ref, da_ref, carry):
        @pl.when(pl.program_id(1) == 0)
        def _():
            carry[...] = jnp.zeros_like(carry)

        def tile(tt, c):
            t = _scan_tiles(tb) - 1 - tt
            ds = pl.ds(pl.multiple_of(t * SUBLANES, SUBLANES), SUBLANES)
            g8, a8 = g_ref[ds, :], a_ref[ds, :]
            adjs = [None] * SUBLANES
            for j in range(SUBLANES - 1, -1, -1):
                adj = g8[j:j + 1, :] + c
                adjs[j] = adj
                c = a8[j:j + 1, :] * adj
            adj8 = jnp.concatenate(adjs, axis=0)
            db_ref[ds, :] = adj8
            da_ref[ds, :] = adj8 * p_ref[ds, :]
            return c

        carry[0:1, :] = lax.fori_loop(0, _scan_tiles(tb), tile, carry[0:1, :])

    blk = pl.BlockSpec((tb, cb), lambda c, i: (nt - 1 - i, c))
    return pl.pallas_call(
        body, grid=(W // cb, nt), in_specs=[blk, blk, blk], out_specs=[blk, blk],
        out_shape=[SDS((T, W), F32)] * 2, scratch_shapes=[pltpu.VMEM((SUBLANES, cb), F32)],
        name="rg_scan_bwd", compiler_params=_params(("arbitrary", "arbitrary")))(dh, a, hprev)


def _s5_scan_fwd(bur, bui, lr, li, tb, cb, riders=()):
    T, W = bur.shape
    grid = (W // cb, T // tb)
    r_in, r_out, _ = _rider_counts(riders)

    def body(*refs):
        br_ref, bi_ref, lr_ref, li_ref = refs[:4]
        xr_ref, xi_ref, pr_ref, pi_ref = refs[4 + r_in:8 + r_in]
        cr, ci = refs[8 + r_in + r_out:10 + r_in + r_out]
        step = _linear_step(grid)
        hooks = _rider_hooks(riders, refs[4:4 + r_in], refs[8 + r_in:8 + r_in + r_out], refs[10 + r_in + r_out:])
        _ride_begin(step, hooks)

        @pl.when(pl.program_id(1) == 0)
        def _():
            cr[...] = jnp.zeros_like(cr)
            ci[...] = jnp.zeros_like(ci)

        lrv, liv = lr_ref[...], li_ref[...]

        def tile(t, carry):
            xr, xi = carry
            ds = pl.ds(pl.multiple_of(t * SUBLANES, SUBLANES), SUBLANES)
            br8, bi8 = br_ref[ds, :], bi_ref[ds, :]
            rr, ri, qr, qi = [], [], [], []
            for j in range(SUBLANES):
                qr.append(xr)
                qi.append(xi)
                xr, xi = (lrv * xr - liv * xi + br8[j:j + 1, :], lrv * xi + liv * xr + bi8[j:j + 1, :])
                rr.append(xr)
                ri.append(xi)
            xr_ref[ds, :] = jnp.concatenate(rr, axis=0)
            xi_ref[ds, :] = jnp.concatenate(ri, axis=0)
            pr_ref[ds, :] = jnp.concatenate(qr, axis=0)
            pi_ref[ds, :] = jnp.concatenate(qi, axis=0)
            return xr, xi

        xr, xi = lax.fori_loop(0, _scan_tiles(tb), tile, (cr[0:1, :], ci[0:1, :]))
        cr[0:1, :] = xr
        ci[0:1, :] = xi
        _ride_end(step, math.prod(grid), hooks)

    blk = pl.BlockSpec((tb, cb), lambda c, i: (i, c))
    par = pl.BlockSpec((1, cb), lambda c, i: (0, c))
    outs = pl.pallas_call(
        body, grid=grid, in_specs=[blk, blk, par, par] + [_ANY] * r_in, out_specs=[blk] * 4 + [_ANY] * r_out,
        out_shape=[SDS((T, W), F32)] * 4 + [o for r in riders for o in r.outs],
        scratch_shapes=[pltpu.VMEM((SUBLANES, cb), F32), pltpu.VMEM((SUBLANES, cb), F32)] + [s for r in riders for s in r.sems],
        name="s5_scan_fwd", compiler_params=_params(("arbitrary", "arbitrary")))(
            bur, bui, lr, li, *[a for r in riders for a in r.ins])
    return tuple(outs[:4]) + (list(outs[4:]),)


def _s5_scan_bwd(gr, gi, pr, pi, lr, li, tb, cb, riders=()):
    T, W = gr.shape
    nt = T // tb
    grid = (W // cb, nt)
    r_in, r_out, _ = _rider_counts(riders)

    def body(*refs):
        gr_ref, gi_ref, pr_ref, pi_ref, lr_ref, li_ref = refs[:6]
        ar_ref, ai_ref, dlr_ref, dli_ref = refs[6 + r_in:10 + r_in]
        cr, ci = refs[10 + r_in + r_out:12 + r_in + r_out]
        step = _linear_step(grid)
        hooks = _rider_hooks(riders, refs[6:6 + r_in], refs[10 + r_in:10 + r_in + r_out], refs[12 + r_in + r_out:])
        _ride_begin(step, hooks)

        @pl.when(pl.program_id(1) == 0)
        def _():
            cr[...] = jnp.zeros_like(cr)
            ci[...] = jnp.zeros_like(ci)
            dlr_ref[...] = jnp.zeros_like(dlr_ref)
            dli_ref[...] = jnp.zeros_like(dli_ref)

        lrv, liv = lr_ref[...], li_ref[...]

        def tile(tt, carry):
            ar, ai = carry
            t = _scan_tiles(tb) - 1 - tt
            ds = pl.ds(pl.multiple_of(t * SUBLANES, SUBLANES), SUBLANES)
            gr8, gi8 = gr_ref[ds, :], gi_ref[ds, :]
            rr, ri = [None] * SUBLANES, [None] * SUBLANES
            for j in range(SUBLANES - 1, -1, -1):
                ar, ai = (gr8[j:j + 1, :] + lrv * ar + liv * ai, gi8[j:j + 1, :] - liv * ar + lrv * ai)
                rr[j], ri[j] = ar, ai
            ar8, ai8 = jnp.concatenate(rr, axis=0), jnp.concatenate(ri, axis=0)
            ar_ref[ds, :] = ar8
            ai_ref[ds, :] = ai8
            pr8, pi8 = pr_ref[ds, :], pi_ref[ds, :]
            dlr_ref[...] += ar8 * pr8 + ai8 * pi8
            dli_ref[...] += ai8 * pr8 - ar8 * pi8
            return ar, ai

        ar, ai = lax.fori_loop(0, _scan_tiles(tb), tile, (cr[0:1, :], ci[0:1, :]))
        cr[0:1, :] = ar
        ci[0:1, :] = ai
        _ride_end(step, math.prod(grid), hooks)

    blk = pl.BlockSpec((tb, cb), lambda c, i: (nt - 1 - i, c))
    par = pl.BlockSpec((1, cb), lambda c, i: (0, c))
    acc = pl.BlockSpec((SUBLANES, cb), lambda c, i: (0, c))
    outs = pl.pallas_call(
        body, grid=grid, in_specs=[blk] * 4 + [par, par] + [_ANY] * r_in, out_specs=[blk, blk, acc, acc] + [_ANY] * r_out,
        out_shape=[SDS((T, W), F32)] * 2 + [SDS((SUBLANES, W), F32)] * 2 + [o for r in riders for o in r.outs],
        scratch_shapes=[pltpu.VMEM((SUBLANES, cb), F32), pltpu.VMEM((SUBLANES, cb), F32)] + [s for r in riders for s in r.sems],
        name="s5_scan_bwd", compiler_params=_params(("arbitrary", "arbitrary")))(
            gr, gi, pr, pi, lr, li, *[a for r in riders for a in r.ins])
    return tuple(outs[:4]) + (list(outs[4:]),)


_ANY = pl.BlockSpec(memory_space=pl.ANY)


def _ag_steps(x_ref, out_ref, send_sems, recv_sems, local_sem):
    x, y, c = lax.axis_index("x"), lax.axis_index("y"), lax.axis_index("c")
    me, sibling = (x, y, c), (x, y, 1 - c)
    chips = [(1 - x, y), (x, 1 - y), (1 - x, 1 - y)]

    def slot(px, py, pc):
        return out_ref.at[4 * px + 2 * py + pc]

    def copy(k, block, to, src=None):
        return pltpu.make_async_remote_copy(
            src_ref=slot(*block) if src is None else src, dst_ref=slot(*block),
            send_sem=send_sems.at[k], recv_sem=recv_sems.at[k], device_id=to, device_id_type=MESH_ID)

    mine = pltpu.make_async_copy(x_ref, slot(*me), local_sem)
    first = [copy(0, me, sibling, src=x_ref)]
    first += [copy(1 + j, me, (*chip, c), src=x_ref) for j, chip in enumerate(chips)]
    passed = [copy(4 + j, (*chip, c), sibling) for j, chip in enumerate(chips)]

    def start():
        mine.start()
        for cp in first:
            cp.start()

    def mid():
        for j, chip in enumerate(chips):
            copy(1 + j, (*chip, c), me).wait_recv()
            passed[j].start()

    def finish():
        copy(0, sibling, me).wait_recv()
        for j, chip in enumerate(chips):
            copy(4 + j, (*chip, 1 - c), me).wait_recv()
        for cp in first + passed:
            cp.wait_send()
        mine.wait()

    return start, mid, finish


def _rs1_steps(g_ref, r_ref, send_sems, recv_sems):
    x, y, c = lax.axis_index("x"), lax.axis_index("y"), lax.axis_index("c")
    copies = [pltpu.make_async_remote_copy(
        src_ref=g_ref.at[2 * k + (1 - c)], dst_ref=r_ref.at[k], send_sem=send_sems.at[k], recv_sem=recv_sems.at[k],
        device_id=(x, y, 1 - c), device_id_type=MESH_ID) for k in range(N_CHIP)]

    def start():
        for cp in copies:
            cp.start()

    def finish():
        for cp in copies:
            cp.wait_recv()
        for cp in copies:
            cp.wait_send()

    return start, None, finish


def _rs2_steps(p_ref, r_ref, send_sems, recv_sems, local_sem):
    x, y, c = lax.axis_index("x"), lax.axis_index("y"), lax.axis_index("c")
    myk = 2 * x + y
    chips = [(1 - x, y), (x, 1 - y), (1 - x, 1 - y)]
    mine = pltpu.make_async_copy(p_ref.at[myk], r_ref.at[myk], local_sem)
    sends = [pltpu.make_async_remote_copy(
        src_ref=p_ref.at[2 * px + py], dst_ref=r_ref.at[myk], send_sem=send_sems.at[j], recv_sem=recv_sems.at[j],
        device_id=(px, py, c), device_id_type=MESH_ID) for j, (px, py) in enumerate(chips)]

    def start():
        mine.start()
        for cp in sends:
            cp.start()

    def finish():
        for j, (px, py) in enumerate(chips):
            pltpu.make_async_remote_copy(
                src_ref=p_ref.at[myk], dst_ref=r_ref.at[2 * px + py], send_sem=send_sems.at[j], recv_sem=recv_sems.at[j],
                device_id=(px, py, c), device_id_type=MESH_ID).wait_recv()
        for cp in sends:
            cp.wait_send()
        mine.wait()

    return start, None, finish


def _dma_sems(*counts):
    return [pltpu.SemaphoreType.DMA((n,)) if n else pltpu.SemaphoreType.DMA for n in counts]


def _ag_rider(shard):
    return _Rider([shard], [SDS((N_DEV,) + shard.shape, shard.dtype)], _dma_sems(7, 7, 0), _ag_steps)


def _rs1_rider(g):
    return _Rider([g], [SDS((N_CHIP,) + g.shape[1:], g.dtype)], _dma_sems(N_CHIP, N_CHIP), _rs1_steps)


def _rs2_rider(p):
    return _Rider([p], [SDS(p.shape, p.dtype)], _dma_sems(3, 3, 0), _rs2_steps)


def _comm_call(name, riders):
    r_in, r_out, _ = _rider_counts(riders)

    def body(*refs):
        hooks = _rider_hooks(riders, refs[:r_in], refs[r_in:r_in + r_out], refs[r_in + r_out:])
        for start, _, _ in hooks:
            start()
        for _, mid, _ in hooks:
            if mid is not None:
                mid()
        for _, _, finish in hooks:
            finish()

    ops = [a for r in riders for a in r.ins]
    outs = pl.pallas_call(
        body, out_shape=[o for r in riders for o in r.outs], in_specs=[_ANY] * r_in, out_specs=[_ANY] * r_out,
        scratch_shapes=[s for r in riders for s in r.sems], name=name)(*ops)
    return list(outs)


def _all_gather(name, shard):
    return _comm_call(name, [_ag_rider(shard)])[0]


def _pair_add(name, g, r1, c_idx):
    _, R, C = g.shape
    tr = _row_tile(R, C, 2)

    def body(c_ref, g_ref, r_ref, o_ref):
        o_ref[...] = (g_ref[...].astype(F32) + r_ref[...].astype(F32)).astype(o_ref.dtype)

    grid_spec = pltpu.PrefetchScalarGridSpec(
        num_scalar_prefetch=1, grid=(N_CHIP, R // tr),
        in_specs=[pl.BlockSpec((None, tr, C), lambda k, i, c_ref: (2 * k + c_ref[0], i, 0)),
                  pl.BlockSpec((None, tr, C), lambda k, i, c_ref: (k, i, 0))],
        out_specs=pl.BlockSpec((None, tr, C), lambda k, i, c_ref: (k, i, 0)))
    return pl.pallas_call(body, grid_spec=grid_spec, out_shape=SDS((N_CHIP, R, C), g.dtype), name=name,
                          compiler_params=_params(("arbitrary", "arbitrary")))(c_idx, g, r1)


class _GradReducer:
    def __init__(self, c_idx):
        self.c_idx, self.wait_d2d, self.wait_ici, self.done = c_idx, [], [], {}

    def push(self, name, g):
        self.wait_d2d.append((name, g))

    def take(self, ici=1, d2d=1):
        jobs = [("ici",) + self.wait_ici.pop(0) for _ in range(min(ici, len(self.wait_ici)))]
        jobs += [("d2d",) + self.wait_d2d.pop(0) for _ in range(min(d2d, len(self.wait_d2d)))]
        riders = [_rs2_rider(a) if kind == "ici" else _rs1_rider(a) for kind, _, a in jobs]

        def absorb(outs):
            for (kind, name, a), out in zip(jobs, outs, strict=True):
                if kind == "ici":
                    self.done[name] = out
                else:
                    self.wait_ici.append((name, _pair_add(f"add_{name}", a, out, self.c_idx)))

        return riders, absorb

    def flush(self):
        n_calls = 0
        while self.wait_d2d or self.wait_ici:
            riders, absorb = self.take(ici=len(self.wait_ici), d2d=len(self.wait_d2d))
            absorb(_comm_call(f"rs_tail_{n_calls}", riders))
            n_calls += 1
        return self.done


PACK_ROWS = 64


def _pack(arrs):
    flat = jnp.concatenate([a.reshape(-1).astype(F32) for a in arrs])
    n = flat.shape[0]
    unit = PACK_ROWS * LANES
    padded = -(-n // unit) * unit
    return jnp.pad(flat, (0, padded - n)).reshape(padded // LANES, LANES)


def _unpack(buf, shapes):
    flat = buf.reshape(-1)
    outs, off = [], 0
    for shp in shapes:
        n = math.prod(shp)
        outs.append(flat[off:off + n].reshape(shp))
        off += n
    return outs


def _s5_discretise(lre, lim, log_dt, bre, bim):
    dt = jnp.exp(log_dt)[:, None]
    e_m1 = jnp.expm1(lre * dt)
    th = lim * dt
    lr = (e_m1 + 1.0) * jnp.cos(th)
    li = (e_m1 + 1.0) * jnp.sin(th)
    lr_m1 = e_m1 * jnp.cos(th) - 2.0 * jnp.square(jnp.sin(0.5 * th))
    den = lre * lre + lim * lim
    cr = (lr_m1 * lre + li * lim) / den
    ci = (li * lre - lr_m1 * lim) / den
    bbr = cr[..., None] * bre - ci[..., None] * bim
    bbi = cr[..., None] * bim + ci[..., None] * bre
    return lr, li, bbr, bbi


def _expand_diag(m, rows_first):
    G, A, B = m.shape
    q = G // GROUPS_PER_BLOCK
    eye = jnp.eye(GROUPS_PER_BLOCK, dtype=m.dtype)
    m5 = m.reshape(q, GROUPS_PER_BLOCK, A, 1, B) * eye[None, :, None, :, None]
    return m5.reshape(q, GROUPS_PER_BLOCK * A, GROUPS_PER_BLOCK * B)


def _extract_diag(m, A, B):
    q = m.shape[0]
    m5 = m.reshape(q, GROUPS_PER_BLOCK, A, GROUPS_PER_BLOCK, B)
    d = jnp.stack([m5[:, g, :, g, :] for g in range(GROUPS_PER_BLOCK)], axis=1)
    return d.reshape(q * GROUPS_PER_BLOCK, A, B)


def _ffn_fwd(tag, h, gain, wg, wu_shard, wd_shard, tail_riders, Tp):
    D = h.shape[1]
    J, _, Fb = wg.shape
    tmn = _row_tile(Tp, D)
    (n,) = _rowcall(f"{tag}_norm", lambda i, x, g: (_rmsnorm(x, g),), Tp, tmn, [(h, 0, 0, D)], [gain], [(D, BF16)])
    gate, (wu,) = _mm_bcast(f"{tag}_gate", n, wg, NN, False, BF16, riders=[_ag_rider(wu_shard)])
    up, (wd,) = _mm_bcast(f"{tag}_up", n, wu, NN, False, BF16, riders=[_ag_rider(wd_shard)])
    rows = J * Tp
    tma = _row_tile(rows, Fb, 2, 1024 * 1024)
    g2, u2 = gate.reshape(rows, Fb), up.reshape(rows, Fb)
    (act,) = _rowcall(f"{tag}_act", lambda i, g, u: (_swiglu_act(g, u),), rows, tma,
                      [(g2, 0, 0, Fb), (u2, 0, 0, Fb)], [], [(Fb, BF16)])
    act = act.reshape(J, Tp, Fb)
    h_out = _mm_red(f"{tag}_down", [(act, wd)], NN, F32, scale=0.5, res=h, riders=tail_riders)
    tail = []
    if tail_riders:
        h_out, tail = h_out
    return h_out, (n, gate, up, act), wu, wd, tail


def _ffn_bwd(tag, dh, h, gain, wg, wu, wd, saved, Tp, red):
    n, gate, up, act = saved
    D = h.shape[1]
    J, _, Fb = wg.shape

    def carried(fn, *args, ici=1, d2d=1, **kw):
        riders, absorb = red.take(ici, d2d)
        if not riders:
            return fn(*args, **kw)
        out, routs = fn(*args, riders=riders, **kw)
        absorb(routs)
        return out

    dact = carried(_mm_bcast, f"{tag}_dact", dh, wd, NT, False, BF16)
    red.push(f"{tag}_w_down", carried(_mm_tn, f"{tag}_dwd", act, dh, J, Fb, D, BF16, scale=0.5))
    rows = J * Tp
    tma = _row_tile(rows, Fb, 2, 1024 * 1024)

    def act_bwd(i, g, u, d):
        _, vjp = jax.vjp(_swiglu_act, g, u)
        return vjp(0.5 * d.astype(F32))

    dgate, dup = _rowcall(f"{tag}_act_bwd", act_bwd, rows, tma,
                          [(gate.reshape(rows, Fb), 0, 0, Fb), (up.reshape(rows, Fb), 0, 0, Fb),
                           (dact.reshape(rows, Fb), 0, 0, Fb)], [], [(Fb, BF16), (Fb, BF16)])
    dgate, dup = dgate.reshape(J, Tp, Fb), dup.reshape(J, Tp, Fb)
    red.push(f"{tag}_w_gate", carried(_mm_tn, f"{tag}_dwg", n, dgate, J, D, Fb, BF16))
    red.push(f"{tag}_w_up", carried(_mm_tn, f"{tag}_dwu", n, dup, J, D, Fb, BF16))
    half = D // 2
    dn = [carried(_mm_red, f"{tag}_dn{part}", [(dgate, wg), (dup, wu)], NT, F32, n_off=part * half, n_len=half)
          for part in range(2)]
    return _norm_bwd(f"{tag}_norm_bwd", h, gain, dn, dh, Tp)


def _norm_bwd(name, h, gain, dn_parts, dres, Tp):
    D = h.shape[1]
    n_parts = len(dn_parts)

    def fn(i, x, r, *rest):
        d = jnp.concatenate([p.astype(F32) for p in rest[:n_parts]], axis=1) if n_parts > 1 else rest[0].astype(F32)
        _, vjp = jax.vjp(_rmsnorm, x, rest[n_parts])
        dx, dg = vjp(d)
        return r + dx, dg

    rows = [(h, 0, 0, D), (dres, 0, 0, D)] + [(p, 0, 0, p.shape[1]) for p in dn_parts]
    return _rowcall(name, fn, Tp, _row_tile(Tp, D), rows, [gain], [(D, F32)], [(1, D)])


def kernel(x, meta_tokens, ffn1_norm, ffn1_w_gate, ffn1_w_up, ffn1_w_down, mix_norm, w_in, rg_conv_w, rg_conv_b, rg_w_a, rg_b_a, rg_w_x, rg_b_x, rg_lambda, s5_lambda_re, s5_lambda_im, s5_log_dt, s5_b_re, s5_b_im, s5_c_re, s5_c_im, s5_d, s5_glu_w, s5_glu_b, rg_out_norm, s5_out_norm, w_out, ffn2_norm, ffn2_w_gate, ffn2_w_up, ffn2_w_down, final_norm, loss_target, m_meta_tokens, m_ffn1_norm, m_ffn1_w_gate, m_ffn1_w_up, m_ffn1_w_down, m_mix_norm, m_w_in, m_rg_conv_w, m_rg_conv_b, m_rg_w_a, m_rg_b_a, m_rg_w_x, m_rg_b_x, m_rg_lambda, m_s5_lambda_re, m_s5_lambda_im, m_s5_log_dt, m_s5_b_re, m_s5_b_im, m_s5_c_re, m_s5_c_im, m_s5_d, m_s5_glu_w, m_s5_glu_b, m_rg_out_norm, m_s5_out_norm, m_w_out, m_ffn2_norm, m_ffn2_w_gate, m_ffn2_w_up, m_ffn2_w_down, m_final_norm, v_meta_tokens, v_ffn1_norm, v_ffn1_w_gate, v_ffn1_w_up, v_ffn1_w_down, v_mix_norm, v_w_in, v_rg_conv_w, v_rg_conv_b, v_rg_w_a, v_rg_b_a, v_rg_w_x, v_rg_b_x, v_rg_lambda, v_s5_lambda_re, v_s5_lambda_im, v_s5_log_dt, v_s5_b_re, v_s5_b_im, v_s5_c_re, v_s5_c_im, v_s5_d, v_s5_glu_w, v_s5_glu_b, v_rg_out_norm, v_s5_out_norm, v_w_out, v_ffn2_norm, v_ffn2_w_gate, v_ffn2_w_up, v_ffn2_w_down, v_final_norm):
    weights = dict(
        meta_tokens=meta_tokens, ffn1_norm=ffn1_norm, ffn1_w_gate=ffn1_w_gate, ffn1_w_up=ffn1_w_up, ffn1_w_down=ffn1_w_down,
        mix_norm=mix_norm, w_in=w_in, rg_conv_w=rg_conv_w, rg_conv_b=rg_conv_b, rg_w_a=rg_w_a, rg_b_a=rg_b_a, rg_w_x=rg_w_x,
        rg_b_x=rg_b_x, rg_lambda=rg_lambda, s5_lambda_re=s5_lambda_re, s5_lambda_im=s5_lambda_im, s5_log_dt=s5_log_dt,
        s5_b_re=s5_b_re, s5_b_im=s5_b_im, s5_c_re=s5_c_re, s5_c_im=s5_c_im, s5_d=s5_d, s5_glu_w=s5_glu_w, s5_glu_b=s5_glu_b,
        rg_out_norm=rg_out_norm, s5_out_norm=s5_out_norm, w_out=w_out, ffn2_norm=ffn2_norm, ffn2_w_gate=ffn2_w_gate,
        ffn2_w_up=ffn2_w_up, ffn2_w_down=ffn2_w_down, final_norm=final_norm)
    moments_m = dict(
        meta_tokens=m_meta_tokens, ffn1_norm=m_ffn1_norm, ffn1_w_gate=m_ffn1_w_gate, ffn1_w_up=m_ffn1_w_up,
        ffn1_w_down=m_ffn1_w_down, mix_norm=m_mix_norm, w_in=m_w_in, rg_conv_w=m_rg_conv_w, rg_conv_b=m_rg_conv_b,
        rg_w_a=m_rg_w_a, rg_b_a=m_rg_b_a, rg_w_x=m_rg_w_x, rg_b_x=m_rg_b_x, rg_lambda=m_rg_lambda,
        s5_lambda_re=m_s5_lambda_re, s5_lambda_im=m_s5_lambda_im, s5_log_dt=m_s5_log_dt, s5_b_re=m_s5_b_re,
        s5_b_im=m_s5_b_im, s5_c_re=m_s5_c_re, s5_c_im=m_s5_c_im, s5_d=m_s5_d, s5_glu_w=m_s5_glu_w, s5_glu_b=m_s5_glu_b,
        rg_out_norm=m_rg_out_norm, s5_out_norm=m_s5_out_norm, w_out=m_w_out, ffn2_norm=m_ffn2_norm,
        ffn2_w_gate=m_ffn2_w_gate, ffn2_w_up=m_ffn2_w_up, ffn2_w_down=m_ffn2_w_down, final_norm=m_final_norm)
    moments_v = dict(
        meta_tokens=v_meta_tokens, ffn1_norm=v_ffn1_norm, ffn1_w_gate=v_ffn1_w_gate, ffn1_w_up=v_ffn1_w_up,
        ffn1_w_down=v_ffn1_w_down, mix_norm=v_mix_norm, w_in=v_w_in, rg_conv_w=v_rg_conv_w, rg_conv_b=v_rg_conv_b,
        rg_w_a=v_rg_w_a, rg_b_a=v_rg_b_a, rg_w_x=v_rg_w_x, rg_b_x=v_rg_b_x, rg_lambda=v_rg_lambda,
        s5_lambda_re=v_s5_lambda_re, s5_lambda_im=v_s5_lambda_im, s5_log_dt=v_s5_log_dt, s5_b_re=v_s5_b_re,
        s5_b_im=v_s5_b_im, s5_c_re=v_s5_c_re, s5_c_im=v_s5_c_im, s5_d=v_s5_d, s5_glu_w=v_s5_glu_w, s5_glu_b=v_s5_glu_b,
        rg_out_norm=v_rg_out_norm, s5_out_norm=v_s5_out_norm, w_out=v_w_out, ffn2_norm=v_ffn2_norm,
        ffn2_w_gate=v_ffn2_w_gate, ffn2_w_up=v_ffn2_w_up, ffn2_w_down=v_ffn2_w_down, final_norm=v_final_norm)
    order = list(weights)

    seq, D = x.shape[1], x.shape[2]
    R = rg_conv_b.shape[1]
    S = s5_d.shape[1]
    G, N, C = s5_b_re.shape[1:]
    heads, hd = rg_w_a.shape[1], rg_w_a.shape[3]
    Q = G // GROUPS_PER_BLOCK
    W = G * N
    NB = GROUPS_PER_BLOCK * N
    T = N_META + seq
    Tp = -(-T // LANES) * LANES
    me = 4 * lax.axis_index("x") + 2 * lax.axis_index("y") + lax.axis_index("c")
    c_idx = lax.axis_index("c").astype(jnp.int32).reshape(1)

    big = ["ffn1_w_gate", "ffn1_w_up", "ffn1_w_down", "w_in", "s5_glu_w", "w_out", "ffn2_w_gate", "ffn2_w_up", "ffn2_w_down"]
    shard16 = {k: weights[k][0].astype(BF16) for k in big}
    full = {}
    sharded_small = ["meta_tokens", "rg_conv_w", "rg_w_a", "rg_b_a", "rg_w_x", "rg_b_x"]
    sm, full["ffn1_w_gate"] = _comm_call(
        "ag_first", [_ag_rider(_pack([weights[k] for k in sharded_small])), _ag_rider(shard16["ffn1_w_gate"])])
    sm = [jnp.stack(p) for p in zip(*[_unpack(sm[d], [weights[k].shape for k in sharded_small]) for d in range(N_DEV)])]
    meta_full = jnp.moveaxis(sm[0], 0, 1).reshape(N_META, D)
    conv_w_full = jnp.moveaxis(sm[1][:, 0], 0, 1).reshape(CONV_WIDTH, R)
    wa_full = jnp.moveaxis(sm[2][:, 0], 0, 1).reshape(heads, hd, hd)
    ba_full = jnp.moveaxis(sm[3][:, 0], 0, 1).reshape(1, R)
    wx_full = jnp.moveaxis(sm[4][:, 0], 0, 1).reshape(heads, hd, hd)
    bx_full = jnp.moveaxis(sm[5][:, 0], 0, 1).reshape(1, R)

    lam_fn = functools.partial(_s5_discretise)
    (lr, li, bbr, bbi), disc_vjp = jax.vjp(lam_fn, s5_lambda_re[0], s5_lambda_im[0], s5_log_dt[0], s5_b_re[0], s5_b_im[0])
    lr_row, li_row = lr.reshape(1, W), li.reshape(1, W)
    wb_r = _expand_diag(jnp.swapaxes(bbr, 1, 2), True)
    wb_i = _expand_diag(jnp.swapaxes(bbi, 1, 2), True)
    wc_r = _expand_diag(jnp.swapaxes(s5_c_re[0], 1, 2), True)
    wc_i = _expand_diag(-jnp.swapaxes(s5_c_im[0], 1, 2), True)

    h0 = jnp.concatenate([meta_full, x[0], jnp.zeros((Tp - T, D), F32)], axis=0)
    tgt = jnp.concatenate([jnp.zeros((N_META, D), F32), loss_target[0], jnp.zeros((Tp - T, D), F32)], axis=0)

    h1, ffn1_saved, full["ffn1_w_up"], full["ffn1_w_down"], (full["w_in"],) = _ffn_fwd(
        "ffn1", h0, ffn1_norm, full["ffn1_w_gate"], shard16["ffn1_w_up"], shard16["ffn1_w_down"],
        [_ag_rider(shard16["w_in"])], Tp)

    tmd = _row_tile(Tp, D)
    (n2,) = _rowcall("mix_norm", lambda i, a, g: (_rmsnorm(a, g),), Tp, tmd, [(h1, 0, 0, D)], [mix_norm], [(D, BF16)])
    proj, (full["w_out"], full["s5_glu_w"]) = _mm_bcast(
        "w_in", n2, full["w_in"], NN, True, F32, riders=[_ag_rider(shard16["w_out"]), _ag_rider(shard16["s5_glu_w"])])

    tmr = _row_tile(Tp, R)
    tb = _pick(Tp, 256, SUBLANES)
    xc = _conv_fwd(proj, conv_w_full, rg_conv_b, R, tmr)
    a_t, b_t = _rowcall("rg_gates", lambda i, *a: _rg_gates(*a), Tp, tmr, [(xc, 0, 0, R)],
                        [wa_full, ba_full, wx_full, bx_full, rg_lambda], [(R, F32), (R, F32)])
    h_rg, hprev = _rg_scan_fwd(a_t, b_t, tb, _pick(R, 512, LANES))
    (yn_rg,) = _rowcall("rg_out", lambda i, *a: (_rg_out(*a),), Tp, tmr, [(h_rg, 0, 0, R), (proj, 0, 1, R)],
                        [rg_out_norm], [(R, BF16)])

    u_off = 2 * R // LANES
    bur = _mm_bdiag("s5_bu_re", [(proj, wb_r, u_off)], NN, LANES, NB, F32)
    bui = _mm_bdiag("s5_bu_im", [(proj, wb_i, u_off)], NN, LANES, NB, F32)
    cbs = _pick(W, 512, LANES)
    xr, xi, xpr, xpi, (full["ffn2_w_gate"],) = _s5_scan_fwd(
        bur, bui, lr_row, li_row, tb, cbs, riders=[_ag_rider(shard16["ffn2_w_gate"])])
    y_s5 = _mm_bdiag("s5_y", [(xr, wc_r, 0), (xi, wc_i, 0)], NN, NB, LANES, F32)
    tms = _row_tile(Tp, S)
    s_col = 2 * R // S
    (z,) = _rowcall("s5_pre", lambda i, *a: (_s5_pre(*a),), Tp, tms, [(y_s5, 0, 0, S), (proj, 0, s_col, S)], [s5_d],
                    [(S, F32)])
    gw = full["s5_glu_w"]
    zz = _mm_red("s5_glu", [(z, gw)], NN, F32)
    (yn_s5,) = _rowcall("s5_out", lambda i, *a: (_s5_out(*a),), Tp, tms, [(z, 0, 0, S), (zz, 0, 0, S)],
                        [s5_glu_b, s5_out_norm], [(S, BF16)])
    yn = jnp.concatenate([yn_rg, yn_s5], axis=1)
    h2 = _mm_red("w_out", [(yn, full["w_out"])], NN, F32, res=h1)

    h3, ffn2_saved, full["ffn2_w_up"], full["ffn2_w_down"], _ = _ffn_fwd(
        "ffn2", h2, ffn2_norm, full["ffn2_w_gate"], shard16["ffn2_w_up"], shard16["ffn2_w_down"], [], Tp)

    def final(i, hh, tt, g):
        out, vjp = jax.vjp(_rmsnorm, hh, g)
        row = i * tmd + lax.broadcasted_iota(jnp.int32, (tmd, 1), 0)
        valid = jnp.logical_and(row >= N_META, row < T)
        err = jnp.where(valid, out - tt, 0.0)
        part = 0.5 * jnp.sum(jnp.mean(err * err, axis=-1, keepdims=True))
        dx, dg = vjp(err * (1.0 / D))
        return dx, dg, jnp.full((SUBLANES, LANES), part, F32)

    dh3, d_final_norm, loss_part = _rowcall("final", final, Tp, tmd, [(h3, 0, 0, D), (tgt, 0, 0, D)],
                                            [final_norm.reshape(1, D)], [(D, F32)], [(1, D), (SUBLANES, LANES)])
    loss = lax.psum(loss_part[0, 0], ("x", "y", "c"))

    grads = {}
    red = _GradReducer(c_idx)

    def carried(fn, *args, ici=1, d2d=1, extra=(), **kw):
        riders, absorb = red.take(ici, d2d)
        if not riders and not extra:
            return fn(*args, **kw), []
        out, routs = fn(*args, riders=riders + list(extra), **kw)
        absorb(routs[:len(riders)])
        return out, routs[len(riders):]

    dh2, grads["ffn2_norm"] = _ffn_bwd(
        "ffn2", dh3, h2, ffn2_norm, full["ffn2_w_gate"], full["ffn2_w_up"], full["ffn2_w_down"], ffn2_saved, Tp, red)

    Kb_out = full["w_out"].shape[1]
    dyn = _mm_bcast("w_out_dx", dh2, full["w_out"], NT, True, F32)
    red.push("w_out", carried(_mm_tn, "w_out_dw", yn, dh2, N_DEV, Kb_out, D, BF16)[0])

    def s5_out_bwd(i, zv, zzv, d, gb, gn):
        _, vjp = jax.vjp(_s5_out, zv, zzv, gb, gn)
        return vjp(d)

    dz_a, dzz, grads["s5_glu_b"], grads["s5_out_norm"] = _rowcall(
        "s5_out_bwd", s5_out_bwd, Tp, tms, [(z, 0, 0, S), (zz, 0, 0, S), (dyn, 0, R // S, S)],
        [s5_glu_b, s5_out_norm], [(S, F32), (S, BF16)], [(1, S), (1, S)])
    Kb_glu = gw.shape[1]
    dz = _mm_bcast("s5_glu_dx", dzz, gw, NT, True, F32)
    red.push("s5_glu_w", carried(_mm_tn, "s5_glu_dw", z, dzz, N_DEV, Kb_glu, S, BF16, ici=0)[0])

    def s5_pre_bwd(i, yv, uv, d1, d2, dd):
        _, vjp = jax.vjp(_s5_pre, yv, uv, dd)
        return vjp(d1 + d2)

    dy, du_a, grads["s5_d"] = _rowcall(
        "s5_pre_bwd", s5_pre_bwd, Tp, tms, [(y_s5, 0, 0, S), (proj, 0, s_col, S), (dz_a, 0, 0, S), (dz, 0, 0, S)],
        [s5_d], [(S, F32), (S, F32)], [(1, S)])
    gxr = _mm_bdiag("s5_dx_re", [(dy, wc_r, 0)], NT, LANES, NB, F32)
    gxi = _mm_bdiag("s5_dx_im", [(dy, wc_i, 0)], NT, LANES, NB, F32)
    d_wc_r = _mm_tn("s5_dc_re", xr, dy, Q, NB, LANES, F32)
    d_wc_i = _mm_tn("s5_dc_im", xi, dy, Q, NB, LANES, F32)
    scan_riders, scan_absorb = red.take()
    ar, ai, dlr8, dli8, scan_routs = _s5_scan_bwd(gxr, gxi, xpr, xpi, lr_row, li_row, tb, cbs, riders=scan_riders)
    scan_absorb(scan_routs)
    du_s5 = _mm_bdiag("s5_du", [(ar, wb_r, 0), (ai, wb_i, 0)], NT, NB, LANES, BF16, res=(du_a, 0))
    d_wb_r = carried(_mm_tn, "s5_db_re", proj, ar, Q, LANES, NB, F32, a_off=u_off, d2d=0)[0]
    d_wb_i = _mm_tn("s5_db_im", proj, ai, Q, LANES, NB, F32, a_off=u_off)
    d_bbr = jnp.swapaxes(_extract_diag(d_wb_r, C, N), 1, 2)
    d_bbi = jnp.swapaxes(_extract_diag(d_wb_i, C, N), 1, 2)
    d_lr = jnp.sum(dlr8, axis=0).reshape(G, N)
    d_li = jnp.sum(dli8, axis=0).reshape(G, N)
    d_lre, d_lim, d_logdt, d_bre, d_bim = disc_vjp((d_lr, d_li, d_bbr, d_bbi))
    grads["s5_lambda_re"], grads["s5_lambda_im"], grads["s5_log_dt"] = d_lre[None], d_lim[None], d_logdt[None]
    grads["s5_b_re"], grads["s5_b_im"] = d_bre[None], d_bim[None]
    grads["s5_c_re"] = jnp.swapaxes(_extract_diag(d_wc_r, N, C), 1, 2)[None]
    grads["s5_c_im"] = -jnp.swapaxes(_extract_diag(d_wc_i, N, C), 1, 2)[None]

    def rg_out_bwd(i, hv, gv, d, gn):
        _, vjp = jax.vjp(_rg_out, hv, gv, gn)
        return vjp(d)

    dh_scan, dg_rg, grads["rg_out_norm"] = _rowcall(
        "rg_out_bwd", rg_out_bwd, Tp, tmr, [(h_rg, 0, 0, R), (proj, 0, 1, R), (dyn, 0, 0, R)], [rg_out_norm],
        [(R, F32), (R, BF16)], [(1, R)])
    db_t, da_t = _rg_scan_bwd(dh_scan, a_t, hprev, tb, _pick(R, 512, LANES))

    def rg_gates_bwd(i, xv, da, db, wa, ba, wx, bx, lam):
        _, vjp = jax.vjp(_rg_gates, xv, wa, ba, wx, bx, lam)
        return vjp((da, db))

    dxc, d_wa, d_ba, d_wx, d_bx, grads["rg_lambda"] = _rowcall(
        "rg_gates_bwd", rg_gates_bwd, Tp, _row_tile(Tp, R, 4, 1024 * 1024),
        [(xc, 0, 0, R), (da_t, 0, 0, R), (db_t, 0, 0, R)], [wa_full, ba_full, wx_full, bx_full, rg_lambda],
        [(R, F32)], [(heads, hd, hd), (1, R), (heads, hd, hd), (1, R), (1, R)])
    du_rg, d_conv_w, grads["rg_conv_b"] = _conv_bwd(dxc, proj, conv_w_full, R, tmr)

    grads["final_norm"] = d_final_norm.reshape(D)
    late = ["ffn1_norm", "mix_norm"]
    replicated = ["ffn1_norm", "mix_norm", "rg_conv_b", "rg_lambda", "s5_lambda_re", "s5_lambda_im", "s5_log_dt",
                  "s5_b_re", "s5_b_im", "s5_c_re", "s5_c_im", "s5_d", "s5_glu_b", "rg_out_norm", "s5_out_norm",
                  "ffn2_norm", "final_norm"]
    early = [k for k in replicated if k not in late]
    early_full = [grads[k].reshape(weights[k].shape) for k in early] + [d_conv_w, d_wa, d_ba, d_wx, d_bx]

    dproj = jnp.concatenate([du_rg.astype(BF16), dg_rg, du_s5], axis=1)
    win = full["w_in"]
    Nb_in = win.shape[2]
    dn2, (early_gathered,) = carried(_mm_red, "w_in_dx", [(dproj, win)], NT, F32, extra=[_ag_rider(_pack(early_full))])
    red.push("w_in", carried(_mm_tn, "w_in_dw", n2, dproj, N_DEV, D, Nb_in, BF16)[0])
    dh1, grads["mix_norm"] = _norm_bwd("mix_norm_bwd", h1, mix_norm, [dn2], dh2, Tp)

    dh0, grads["ffn1_norm"] = _ffn_bwd(
        "ffn1", dh1, h0, ffn1_norm, full["ffn1_w_gate"], full["ffn1_w_up"], full["ffn1_w_down"], ffn1_saved, Tp, red)
    grad_x = dh0[N_META:T][None]

    reduced = red.flush()
    outs = {}
    for k in big:
        r2 = reduced[k]
        _, Rk, Ck = r2.shape
        w2, m2, v2 = weights[k][0], moments_m[k][0], moments_v[k][0]

        def big_update(i, wv, mv, vv, p0, p1, p2, p3):
            gsum = (p0.astype(F32) + p1.astype(F32)) + (p2.astype(F32) + p3.astype(F32))
            return (gsum,) + _adamw(wv, gsum, mv, vv)

        res = _rowcall(f"adam_{k}", big_update, Rk, _row_tile(Rk, Ck, 4, 1024 * 1024),
                       [(w2, 0, 0, Ck), (m2, 0, 0, Ck), (v2, 0, 0, Ck)] + [(r2, kk, 0, Ck) for kk in range(N_CHIP)],
                       [], [(Ck, F32)] * 4)
        outs[k] = [o[None] for o in res]

    late_full = [grads[k].reshape(weights[k].shape) for k in late] + [dh0[:N_META]]
    late_gathered = _all_gather("ag_late_grads", _pack(late_full))

    def sum8(i, *parts):
        s = parts[0]
        for q in parts[1:]:
            s = s + q
        return (s,)

    def sum_devices(name, gathered, shapes):
        n_rows = gathered.shape[1]
        (summed,) = _rowcall(name, sum8, n_rows, _pick(n_rows, 512, SUBLANES),
                             [(gathered, d, 0, LANES) for d in range(N_DEV)], [], [(LANES, F32)])
        return _unpack(summed, shapes)

    early_sum = sum_devices("small_sum_early", early_gathered, [a.shape for a in early_full])
    late_sum = sum_devices("small_sum_late", late_gathered, [a.shape for a in late_full])
    g_small = dict(zip(early, early_sum[:len(early)]))
    g_small.update(zip(late, late_sum[:len(late)]))
    d_cw, d_wa_s, d_ba_s, d_wx_s, d_bx_s = early_sum[len(early):]
    d_meta = late_sum[len(late)]

    def shard_of(a, axis):
        n = a.shape[axis] // N_DEV
        return lax.dynamic_slice_in_dim(a, me * n, n, axis)

    g_small["meta_tokens"] = shard_of(d_meta, 1)
    g_small["rg_conv_w"] = shard_of(d_cw, 1)[None]
    g_small["rg_w_a"] = shard_of(d_wa_s, 1)[None]
    g_small["rg_b_a"] = shard_of(d_ba_s.reshape(heads, hd), 1)[None]
    g_small["rg_w_x"] = shard_of(d_wx_s, 1)[None]
    g_small["rg_b_x"] = shard_of(d_bx_s.reshape(heads, hd), 1)[None]
    small = replicated + sharded_small
    shapes = [weights[k].shape for k in small]
    gp = _pack([g_small[k] for k in small])
    n_rows = gp.shape[0]

    def small_update(i, wv, gv, mv, vv):
        return _adamw(wv, gv, mv, vv)

    res = _rowcall("adam_small", small_update, n_rows, _pick(n_rows, 512, SUBLANES),
                   [(_pack([weights[k] for k in small]), 0, 0, LANES), (gp, 0, 0, LANES),
                    (_pack([moments_m[k] for k in small]), 0, 0, LANES), (_pack([moments_v[k] for k in small]), 0, 0, LANES)],
                   [], [(LANES, F32)] * 3)
    unpacked = [_unpack(r, shapes) for r in res]
    for idx, k in enumerate(small):
        outs[k] = [g_small[k].reshape(weights[k].shape)] + [u[idx] for u in unpacked]

    return (loss, grad_x, *[outs[k][0] for k in order], *[outs[k][1] for k in order],
            *[outs[k][2] for k in order], *[outs[k][3] for k in order])
```

```python
import functools
import math

import jax
import jax.numpy as jnp
from jax import lax
from jax.experimental import pallas as pl
from jax.experimental.pallas import tpu as pltpu

F32, BF16 = jnp.float32, jnp.bfloat16
SDS = jax.ShapeDtypeStruct
MESH_ID = pl.DeviceIdType.MESH
N_DEV = 8
N_CHIP = 4
LANES = 128
SUBLANES = 8
VMEM_LIMIT = 56 * 1024 * 1024

EPS = 1e-6
RG_C = 8.0
N_META = 16
CONV_WIDTH = 4
S5_GROUP = 16
S5_STATE = 64
GROUPS_PER_BLOCK = LANES // S5_GROUP
ADAM_LR, ADAM_B1, ADAM_B2, ADAM_EPS, ADAM_WD, ADAM_STEP = 0.001, 0.9, 0.999, 1e-08, 0.01, 10

NN = (((1,), (0,)), ((), ()))
NT = (((1,), (1,)), ((), ()))
TN = (((0,), (0,)), ((), ()))


def _pick(n, target, mult=16):
    if n <= target:
        return n
    best = None
    for d in range(mult, target + 1, mult):
        if n % d == 0:
            best = d
    assert best is not None, (n, target, mult)
    return best


def _row_tile(nrows, ncols, itembytes=4, budget=2 * 1024 * 1024):
    return _pick(nrows, max(16, budget // (ncols * itembytes)))


def _params(sem):
    return pltpu.CompilerParams(dimension_semantics=sem, vmem_limit_bytes=VMEM_LIMIT)


def _rowcall(name, fn, nrows, tm, rows, fulls, row_outs, acc_outs=()):
    n_in = len(rows) + len(fulls)
    in_specs = []
    for arr, lead, cb, C in rows:
        if arr.ndim == 3:
            in_specs.append(pl.BlockSpec((None, tm, C), lambda i, lead=lead, cb=cb: (lead, i, cb)))
        else:
            in_specs.append(pl.BlockSpec((tm, C), lambda i, cb=cb: (i, cb)))
    for f in fulls:
        in_specs.append(pl.BlockSpec(f.shape, lambda i, nd=f.ndim: (0,) * nd))
    out_specs = [pl.BlockSpec((tm, C), lambda i: (i, 0)) for C, _ in row_outs]
    out_shape = [SDS((nrows, C), dt) for C, dt in row_outs]
    for shp in acc_outs:
        out_specs.append(pl.BlockSpec(shp, lambda i, nd=len(shp): (0,) * nd))
        out_shape.append(SDS(shp, F32))
    n_row_out = len(row_outs)

    def body(*refs):
        i = pl.program_id(0)
        res = fn(i, *[r[...] for r in refs[:n_in]])
        outs = refs[n_in:]
        for k in range(n_row_out):
            outs[k][...] = res[k].astype(outs[k].dtype)
        if acc_outs:
            @pl.when(i == 0)
            def _():
                for o in outs[n_row_out:]:
                    o[...] = jnp.zeros_like(o)
            for k in range(n_row_out, len(outs)):
                outs[k][...] += res[k].astype(F32)

    return pl.pallas_call(
        body, grid=(nrows // tm,), in_specs=in_specs, out_specs=out_specs, out_shape=out_shape,
        name=name, compiler_params=_params(("arbitrary",)))(*[r[0] for r in rows], *fulls)


class _Rider:
    def __init__(self, ins, outs, sems, steps):
        self.ins, self.outs, self.sems, self.steps = list(ins), list(outs), list(sems), steps


def _rider_counts(riders):
    return (sum(len(r.ins) for r in riders), sum(len(r.outs) for r in riders), sum(len(r.sems) for r in riders))


def _rider_hooks(riders, in_refs, out_refs, sem_refs):
    hooks, i, o, s = [], 0, 0, 0
    for r in riders:
        hooks.append(r.steps(*in_refs[i:i + len(r.ins)], *out_refs[o:o + len(r.outs)], *sem_refs[s:s + len(r.sems)]))
        i, o, s = i + len(r.ins), o + len(r.outs), s + len(r.sems)
    return hooks


def _linear_step(grid):
    step = pl.program_id(0)
    for ax in range(1, len(grid)):
        step = step * grid[ax] + pl.program_id(ax)
    return step


def _ride_begin(step, hooks):
    if hooks:
        @pl.when(step == 0)
        def _():
            for start, _, _ in hooks:
                start()


def _ride_end(step, nsteps, hooks):
    if hooks:
        @pl.when(step == nsteps - 1)
        def _():
            for _, mid, _ in hooks:
                if mid is not None:
                    mid()
            for _, _, finish in hooks:
                finish()


def _mm(name, terms, dims, grid, nk, out_shape, out_spec, acc_shape, scale=1.0, res=None, riders=()):
    n_t = len(terms)
    kax = len(grid) - 1
    n_in = 2 * n_t + (1 if res is not None else 0)
    r_in, r_out, _ = _rider_counts(riders)
    nsteps = math.prod(grid)
    n_acc = 1 if nk > 1 else 0

    def body(*refs):
        ins, rin = refs[:n_in], refs[n_in:n_in + r_in]
        o_ref = refs[n_in + r_in]
        rout = refs[n_in + r_in + 1:n_in + r_in + 1 + r_out]
        scratch = refs[n_in + r_in + 1 + r_out:]
        step = _linear_step(grid)
        k = pl.program_id(kax)
        hooks = _rider_hooks(riders, rin, rout, scratch[n_acc:])
        _ride_begin(step, hooks)

        def product():
            r = None
            for t in range(n_t):
                d = lax.dot_general(ins[2 * t][...].astype(BF16), ins[2 * t + 1][...].astype(BF16), dims,
                                    preferred_element_type=F32)
                r = d if r is None else r + d
            return r

        def emit(r):
            r = r * scale
            if res is not None:
                r = r + ins[2 * n_t][...].astype(F32)
            o_ref[...] = r.astype(o_ref.dtype)

        if nk == 1:
            emit(product())
        else:
            acc = scratch[0]

            @pl.when(k == 0)
            def _():
                acc[...] = jnp.zeros_like(acc)

            acc[...] += product()

            @pl.when(k == nk - 1)
            def _():
                emit(acc[...])

        _ride_end(step, nsteps, hooks)

    ops, specs = [], []
    for a, a_spec, b, b_spec in terms:
        ops += [a, b]
        specs += [a_spec, b_spec]
    if res is not None:
        ops.append(res[0])
        specs.append(res[1])
    out_shapes, out_specs, scratch = [out_shape], [out_spec], []
    if n_acc:
        scratch.append(pltpu.VMEM(acc_shape, F32))
    for r in riders:
        ops += r.ins
        specs += [_ANY] * len(r.ins)
        out_shapes += r.outs
        out_specs += [_ANY] * len(r.outs)
        scratch += r.sems
    sem = ("arbitrary",) * len(grid)
    outs = pl.pallas_call(
        body, grid=grid, in_specs=specs, out_specs=out_specs, out_shape=out_shapes,
        scratch_shapes=scratch, name=name, compiler_params=_params(sem))(*ops)
    return (outs[0], list(outs[1:])) if riders else outs[0]


def _mm_bcast(name, a, w, dims, std_out, out_dtype, tm_target=384, riders=()):
    M, K = a.shape
    J = w.shape[0]
    Nb = w.shape[2] if dims == NN else w.shape[1]
    tm = _pick(M, tm_target)
    a_spec = pl.BlockSpec((tm, K), lambda j, i, k: (i, 0))
    w_spec = pl.BlockSpec((None,) + w.shape[1:], lambda j, i, k: (j, 0, 0))
    if std_out:
        out_shape, out_spec = SDS((M, J * Nb), out_dtype), pl.BlockSpec((tm, Nb), lambda j, i, k: (i, j))
    else:
        out_shape, out_spec = SDS((J, M, Nb), out_dtype), pl.BlockSpec((None, tm, Nb), lambda j, i, k: (j, i, 0))
    return _mm(name, [(a, a_spec, w, w_spec)], dims, (J, M // tm, 1), 1, out_shape, out_spec, (tm, Nb), riders=riders)


def _a_blk(a, tm, Kb, off=0):
    if a.ndim == 3:
        return pl.BlockSpec((None, tm, Kb), lambda i, n, j: (j, i, 0))
    return pl.BlockSpec((tm, Kb), lambda i, n, j, off=off: (i, j + off))


def _mm_red(name, pairs, dims, out_dtype, scale=1.0, res=None, tm_target=704, tn_target=1024, n_off=0, n_len=None,
            riders=()):
    a0, w0 = pairs[0][0], pairs[0][1]
    J = w0.shape[0]
    M = a0.shape[-2]
    N = w0.shape[2] if dims == NN else w0.shape[1]
    Kb = w0.shape[1] if dims == NN else w0.shape[2]
    if n_len is not None:
        N = n_len
    tm, tn = _pick(M, tm_target), _pick(N, tn_target, LANES)
    nb0 = n_off // tn
    terms = []
    for a, w, *rest in pairs:
        off = rest[0] if rest else 0
        if dims == NN:
            w_spec = pl.BlockSpec((None, Kb, tn), lambda i, n, j: (j, 0, n + nb0))
        else:
            w_spec = pl.BlockSpec((None, tn, Kb), lambda i, n, j: (j, n + nb0, 0))
        terms.append((a, _a_blk(a, tm, Kb, off), w, w_spec))
    out_spec = pl.BlockSpec((tm, tn), lambda i, n, j: (i, n))
    r = None if res is None else (res, out_spec)
    return _mm(name, terms, dims, (M // tm, N // tn, J), J, SDS((M, N), out_dtype), out_spec, (tm, tn), scale, r, riders)


def _mm_tn(name, a, b, J, Ka, Nb, out_dtype, a_off=0, b_off=0, scale=1.0, tw_target=1024, tk_target=1408, riders=()):
    M = a.shape[-2]
    tk = _pick(M, tk_target)
    tka = _pick(Ka, tw_target, LANES) if Ka % LANES == 0 else Ka
    tnb = _pick(Nb, tw_target, LANES) if Nb % LANES == 0 else Nb

    def spec(arr, width, tw, is_a, off):
        nb = width // tw

        def wi(m, n):
            return m if is_a else n
        if arr.ndim == 3:
            return pl.BlockSpec((None, tk, tw), lambda j, m, n, k: (j, k, wi(m, n)))
        if arr.shape[1] == width:
            return pl.BlockSpec((tk, tw), lambda j, m, n, k: (k, wi(m, n)))
        return pl.BlockSpec((tk, tw), lambda j, m, n, k: (k, (j + off) * nb + wi(m, n)))

    a_spec = spec(a, Ka, tka, True, a_off)
    b_spec = spec(b, Nb, tnb, False, b_off)
    out_spec = pl.BlockSpec((None, tka, tnb), lambda j, m, n, k: (j, m, n))
    return _mm(name, [(a, a_spec, b, b_spec)], TN, (J, Ka // tka, Nb // tnb, M // tk), M // tk,
               SDS((J, Ka, Nb), out_dtype), out_spec, (tka, tnb), scale, riders=riders)


def _mm_bdiag(name, pairs, dims, Kb, Nb, out_dtype, res=None, tm_target=704, riders=()):
    a0, w0 = pairs[0][0], pairs[0][1]
    M, J = a0.shape[0], w0.shape[0]
    tm = _pick(M, tm_target)
    terms = []
    for a, w, off in pairs:
        a_spec = pl.BlockSpec((tm, Kb), lambda j, i, k, off=off: (i, j + off))
        w_spec = pl.BlockSpec((None,) + w.shape[1:], lambda j, i, k: (j, 0, 0))
        terms.append((a, a_spec, w, w_spec))
    out_spec = pl.BlockSpec((tm, Nb), lambda j, i, k: (i, j))
    r = None if res is None else (res[0], pl.BlockSpec((tm, Nb), lambda j, i, k, off=res[1]: (i, j + off)))
    return _mm(name, terms, dims, (J, M // tm, 1), 1, SDS((M, J * Nb), out_dtype), out_spec, (tm, Nb), 1.0, r, riders)


def _rmsnorm(x, g):
    x = x.astype(F32)
    return x * lax.rsqrt(jnp.mean(x * x, axis=-1, keepdims=True) + EPS) * g


def _swiglu_act(gate, up):
    gate, up = gate.astype(F32), up.astype(F32)
    return gate * jax.nn.sigmoid(gate) * up


def _neg_expm1(x):
    series = -x * (1.0 + x * (1.0 / 2 + x * (1.0 / 6 + x * (1.0 / 24 + x * (1.0 / 120 + x * (1.0 / 720))))))
    return jnp.where(x > -0.25, series, 1.0 - jnp.exp(x))


def _rg_gates(xc, wa, ba, wx, bx, lam):
    heads, hd = wa.shape[0], wa.shape[1]
    xb = xc.astype(BF16)
    rs, ig = [], []
    for h in range(heads):
        xh = xb[:, h * hd:(h + 1) * hd]
        rs.append(jnp.dot(xh, wa[h].astype(BF16), preferred_element_type=F32))
        ig.append(jnp.dot(xh, wx[h].astype(BF16), preferred_element_type=F32))
    r = jax.nn.sigmoid(jnp.concatenate(rs, axis=1) + ba)
    ii = jax.nn.sigmoid(jnp.concatenate(ig, axis=1) + bx)
    log_a = -RG_C * r * jax.nn.softplus(-lam)
    a = jnp.exp(log_a)
    mult = jnp.sqrt(_neg_expm1(2.0 * log_a))
    return a, mult * ii * xc


def _rg_out(h, g, gain):
    return _rmsnorm(h * jax.nn.gelu(g), gain)


def _s5_pre(y, u, d):
    return jax.nn.gelu(y + d * u)


def _s5_out(z, zz, glu_b, gain):
    return _rmsnorm(z * jax.nn.sigmoid(zz + glu_b), gain)


def _adamw(w, g, m, v):
    m = ADAM_B1 * m + (1.0 - ADAM_B1) * g
    v = ADAM_B2 * v + (1.0 - ADAM_B2) * jnp.square(g)
    m_hat = m / (1.0 - ADAM_B1 ** ADAM_STEP)
    v_hat = v / (1.0 - ADAM_B2 ** ADAM_STEP)
    delta = -ADAM_LR * (m_hat / (jnp.sqrt(v_hat) + ADAM_EPS) + ADAM_WD * w)
    return delta, m, v


def _conv_fwd(proj, conv_w, conv_b, R, tm):
    T = proj.shape[0]
    tpb = tm // SUBLANES

    def body(cur_ref, prev_ref, w_ref, b_ref, o_ref):
        i = pl.program_id(0)
        cur = cur_ref[...]
        prev = jnp.where(i > 0, prev_ref[...], 0.0)
        ext = jnp.concatenate([prev, cur], axis=0)
        acc = b_ref[...] + w_ref[CONV_WIDTH - 1:CONV_WIDTH, :] * cur
        for s in range(1, CONV_WIDTH):
            acc = acc + w_ref[CONV_WIDTH - 1 - s:CONV_WIDTH - s, :] * pltpu.roll(ext, s, 0)[SUBLANES:, :]
        o_ref[...] = acc

    return pl.pallas_call(
        body, grid=(T // tm,),
        in_specs=[pl.BlockSpec((tm, R), lambda i: (i, 0)),
                  pl.BlockSpec((SUBLANES, R), lambda i: (jnp.maximum(i * tpb - 1, 0), 0)),
                  pl.BlockSpec((CONV_WIDTH, R), lambda i: (0, 0)), pl.BlockSpec((1, R), lambda i: (0, 0))],
        out_specs=pl.BlockSpec((tm, R), lambda i: (i, 0)), out_shape=SDS((T, R), F32),
        name="rg_conv_fwd", compiler_params=_params(("arbitrary",)))(proj, proj, conv_w, conv_b)


def _conv_bwd(dxc, proj, conv_w, R, tm):
    T = dxc.shape[0]
    tpb = tm // SUBLANES
    nt = T // tm
    n_ext = tm + SUBLANES

    def body(d_ref, dnext_ref, u_ref, uprev_ref, w_ref, du_ref, dw_ref, db_ref):
        i = pl.program_id(0)
        d = d_ref[...]
        dnext = jnp.where(i < nt - 1, dnext_ref[...], 0.0)
        dext = jnp.concatenate([d, dnext], axis=0)
        u = u_ref[...]
        uprev = jnp.where(i > 0, uprev_ref[...], 0.0)
        uext = jnp.concatenate([uprev, u], axis=0)
        du = w_ref[CONV_WIDTH - 1:CONV_WIDTH, :] * d
        dws = [jnp.sum(d * u, axis=0, keepdims=True)]
        for s in range(1, CONV_WIDTH):
            du = du + w_ref[CONV_WIDTH - 1 - s:CONV_WIDTH - s, :] * pltpu.roll(dext, n_ext - s, 0)[:tm, :]
            dws.append(jnp.sum(d * pltpu.roll(uext, s, 0)[SUBLANES:, :], axis=0, keepdims=True))
        du_ref[...] = du

        @pl.when(i == 0)
        def _():
            dw_ref[...] = jnp.zeros_like(dw_ref)
            db_ref[...] = jnp.zeros_like(db_ref)

        dw_ref[...] += jnp.concatenate(dws[::-1], axis=0)
        db_ref[...] += jnp.sum(d, axis=0, keepdims=True)

    row = pl.BlockSpec((tm, R), lambda i: (i, 0))
    return pl.pallas_call(
        body, grid=(nt,),
        in_specs=[row, pl.BlockSpec((SUBLANES, R), lambda i: (jnp.minimum((i + 1) * tpb, T // SUBLANES - 1), 0)),
                  row, pl.BlockSpec((SUBLANES, R), lambda i: (jnp.maximum(i * tpb - 1, 0), 0)),
                  pl.BlockSpec((CONV_WIDTH, R), lambda i: (0, 0))],
        out_specs=[row, pl.BlockSpec((CONV_WIDTH, R), lambda i: (0, 0)), pl.BlockSpec((1, R), lambda i: (0, 0))],
        out_shape=[SDS((T, R), F32), SDS((CONV_WIDTH, R), F32), SDS((1, R), F32)],
        name="rg_conv_bwd", compiler_params=_params(("arbitrary",)))(dxc, dxc, proj, proj, conv_w)


def _scan_tiles(tb):
    return tb // SUBLANES


def _rg_scan_fwd(a, b, tb, cb):
    T, W = a.shape

    def body(a_ref, b_ref, h_ref, p_ref, carry):
        @pl.when(pl.program_id(1) == 0)
        def _():
            carry[...] = jnp.zeros_like(carry)

        def tile(t, h):
            ds = pl.ds(pl.multiple_of(t * SUBLANES, SUBLANES), SUBLANES)
            a8, b8 = a_ref[ds, :], b_ref[ds, :]
            hs, ps = [], []
            for j in range(SUBLANES):
                ps.append(h)
                h = a8[j:j + 1, :] * h + b8[j:j + 1, :]
                hs.append(h)
            h_ref[ds, :] = jnp.concatenate(hs, axis=0)
            p_ref[ds, :] = jnp.concatenate(ps, axis=0)
            return h

        carry[0:1, :] = lax.fori_loop(0, _scan_tiles(tb), tile, carry[0:1, :])

    blk = pl.BlockSpec((tb, cb), lambda c, i: (i, c))
    return pl.pallas_call(
        body, grid=(W // cb, T // tb), in_specs=[blk, blk], out_specs=[blk, blk],
        out_shape=[SDS((T, W), F32)] * 2, scratch_shapes=[pltpu.VMEM((SUBLANES, cb), F32)],
        name="rg_scan_fwd", compiler_params=_params(("arbitrary", "arbitrary")))(a, b)


def _rg_scan_bwd(dh, a, hprev, tb, cb):
    T, W = a.shape
    nt = T // tb

    def body(g_ref, a_ref, p_ref, db_ref, da_ref, carry):
        @pl.when(pl.program_id(1) == 0)
        def _():
            carry[...] = jnp.zeros_like(carry)

        def tile(tt, c):
            t = _scan_tiles(tb) - 1 - tt
            ds = pl.ds(pl.multiple_of(t * SUBLANES, SUBLANES), SUBLANES)
            g8, a8 = g_ref[ds, :], a_ref[ds, :]
            adjs = [None] * SUBLANES
            for j in range(SUBLANES - 1, -1, -1):
                adj = g8[j:j + 1, :] + c
                adjs[j] = adj
                c = a8[j:j + 1, :] * adj
            adj8 = jnp.concatenate(adjs, axis=0)
            db_ref[ds, :] = adj8
            da_ref[ds, :] = adj8 * p_ref[ds, :]
            return c

        carry[0:1, :] = lax.fori_loop(0, _scan_tiles(tb), tile, carry[0:1, :])

    blk = pl.BlockSpec((tb, cb), lambda c, i: (nt - 1 - i, c))
    return pl.pallas_call(
        body, grid=(W // cb, nt), in_specs=[blk, blk, blk], out_specs=[blk, blk],
        out_shape=[SDS((T, W), F32)] * 2, scratch_shapes=[pltpu.VMEM((SUBLANES, cb), F32)],
        name="rg_scan_bwd", compiler_params=_params(("arbitrary", "arbitrary")))(dh, a, hprev)


SCAN_LEVELS = (1, 2, 4)


def _scan_tables(lr, li, reverse):
    def cmul(a, b):
        return a[0] * b[0] - a[1] * b[1], a[0] * b[1] + a[1] * b[0]

    powers = [(lr, li)]
    for _ in range(SUBLANES - 1):
        powers.append(cmul(powers[-1], (lr, li)))
    row = jnp.arange(SUBLANES)[:, None]
    rows = []
    for k in SCAN_LEVELS:
        has = (row <= SUBLANES - 1 - k) if reverse else (row >= k)
        rows += [jnp.where(has, powers[k - 1][0], 0.0), jnp.where(has, powers[k - 1][1], 0.0)]
    order = list(range(SUBLANES - 1, -1, -1)) if reverse else list(range(SUBLANES))
    rows += [jnp.concatenate([powers[j][0] for j in order], axis=0), jnp.concatenate([powers[j][1] for j in order], axis=0)]
    return jnp.concatenate(rows, axis=0)


def _scan_tile(vr, vi, carry, tab_ref, reverse):
    n_rows = SUBLANES
    for lvl, k in enumerate(SCAN_LEVELS):
        mr = tab_ref[2 * lvl * n_rows:(2 * lvl + 1) * n_rows, :]
        mi = tab_ref[(2 * lvl + 1) * n_rows:(2 * lvl + 2) * n_rows, :]
        shift = n_rows - k if reverse else k
        sr, si = pltpu.roll(vr, shift, 0), pltpu.roll(vi, shift, 0)
        vr, vi = vr + mr * sr - mi * si, vi + mr * si + mi * sr
    base = 2 * len(SCAN_LEVELS) * n_rows
    pr, pi = tab_ref[base:base + n_rows, :], tab_ref[base + n_rows:base + 2 * n_rows, :]
    cr, ci = carry
    return vr + pr * cr - pi * ci, vi + pr * ci + pi * cr


def _s5_scan_fwd(bur, bui, tab, tb, cb, riders=()):
    T, W = bur.shape
    grid = (W // cb, T // tb)
    r_in, r_out, _ = _rider_counts(riders)

    def body(*refs):
        br_ref, bi_ref, tab_ref = refs[:3]
        xr_ref, xi_ref, pr_ref, pi_ref = refs[3 + r_in:7 + r_in]
        cr, ci = refs[7 + r_in + r_out:9 + r_in + r_out]
        step = _linear_step(grid)
        hooks = _rider_hooks(riders, refs[3:3 + r_in], refs[7 + r_in:7 + r_in + r_out], refs[9 + r_in + r_out:])
        _ride_begin(step, hooks)

        @pl.when(pl.program_id(1) == 0)
        def _():
            cr[...] = jnp.zeros_like(cr)
            ci[...] = jnp.zeros_like(ci)

        first_row = lax.broadcasted_iota(jnp.int32, (SUBLANES, cb), 0) == 0

        def tile(t, carry):
            ds = pl.ds(pl.multiple_of(t * SUBLANES, SUBLANES), SUBLANES)
            xr8, xi8 = _scan_tile(br_ref[ds, :], bi_ref[ds, :], carry, tab_ref, reverse=False)
            xr_ref[ds, :] = xr8
            xi_ref[ds, :] = xi8
            pr_ref[ds, :] = jnp.where(first_row, carry[0], pltpu.roll(xr8, 1, 0))
            pi_ref[ds, :] = jnp.where(first_row, carry[1], pltpu.roll(xi8, 1, 0))
            return xr8[SUBLANES - 1:SUBLANES, :], xi8[SUBLANES - 1:SUBLANES, :]

        xr, xi = lax.fori_loop(0, _scan_tiles(tb), tile, (cr[0:1, :], ci[0:1, :]))
        cr[0:1, :] = xr
        ci[0:1, :] = xi
        _ride_end(step, math.prod(grid), hooks)

    blk = pl.BlockSpec((tb, cb), lambda c, i: (i, c))
    tabs = pl.BlockSpec((tab.shape[0], cb), lambda c, i: (0, c))
    outs = pl.pallas_call(
        body, grid=grid, in_specs=[blk, blk, tabs] + [_ANY] * r_in, out_specs=[blk] * 4 + [_ANY] * r_out,
        out_shape=[SDS((T, W), F32)] * 4 + [o for r in riders for o in r.outs],
        scratch_shapes=[pltpu.VMEM((SUBLANES, cb), F32), pltpu.VMEM((SUBLANES, cb), F32)] + [s for r in riders for s in r.sems],
        name="s5_scan_fwd", compiler_params=_params(("arbitrary", "arbitrary")))(
            bur, bui, tab, *[a for r in riders for a in r.ins])
    return tuple(outs[:4]) + (list(outs[4:]),)


def _s5_scan_bwd(gr, gi, pr, pi, tab, tb, cb, riders=()):
    T, W = gr.shape
    nt = T // tb
    grid = (W // cb, nt)
    r_in, r_out, _ = _rider_counts(riders)

    def body(*refs):
        gr_ref, gi_ref, pr_ref, pi_ref, tab_ref = refs[:5]
        ar_ref, ai_ref, dlr_ref, dli_ref = refs[5 + r_in:9 + r_in]
        cr, ci = refs[9 + r_in + r_out:11 + r_in + r_out]
        step = _linear_step(grid)
        hooks = _rider_hooks(riders, refs[5:5 + r_in], refs[9 + r_in:9 + r_in + r_out], refs[11 + r_in + r_out:])
        _ride_begin(step, hooks)

        @pl.when(pl.program_id(1) == 0)
        def _():
            cr[...] = jnp.zeros_like(cr)
            ci[...] = jnp.zeros_like(ci)
            dlr_ref[...] = jnp.zeros_like(dlr_ref)
            dli_ref[...] = jnp.zeros_like(dli_ref)

        def tile(tt, carry):
            t = _scan_tiles(tb) - 1 - tt
            ds = pl.ds(pl.multiple_of(t * SUBLANES, SUBLANES), SUBLANES)
            ar8, ai8 = _scan_tile(gr_ref[ds, :], gi_ref[ds, :], carry, tab_ref, reverse=True)
            ar_ref[ds, :] = ar8
            ai_ref[ds, :] = ai8
            pr8, pi8 = pr_ref[ds, :], pi_ref[ds, :]
            dlr_ref[...] += ar8 * pr8 + ai8 * pi8
            dli_ref[...] += ai8 * pr8 - ar8 * pi8
            return ar8[0:1, :], ai8[0:1, :]

        ar, ai = lax.fori_loop(0, _scan_tiles(tb), tile, (cr[0:1, :], ci[0:1, :]))
        cr[0:1, :] = ar
        ci[0:1, :] = ai
        _ride_end(step, math.prod(grid), hooks)

    blk = pl.BlockSpec((tb, cb), lambda c, i: (nt - 1 - i, c))
    tabs = pl.BlockSpec((tab.shape[0], cb), lambda c, i: (0, c))
    acc = pl.BlockSpec((SUBLANES, cb), lambda c, i: (0, c))
    outs = pl.pallas_call(
        body, grid=grid, in_specs=[blk] * 4 + [tabs] + [_ANY] * r_in, out_specs=[blk, blk, acc, acc] + [_ANY] * r_out,
        out_shape=[SDS((T, W), F32)] * 2 + [SDS((SUBLANES, W), F32)] * 2 + [o for r in riders for o in r.outs],
        scratch_shapes=[pltpu.VMEM((SUBLANES, cb), F32), pltpu.VMEM((SUBLANES, cb), F32)] + [s for r in riders for s in r.sems],
        name="s5_scan_bwd", compiler_params=_params(("arbitrary", "arbitrary")))(
            gr, gi, pr, pi, tab, *[a for r in riders for a in r.ins])
    return tuple(outs[:4]) + (list(outs[4:]),)


_ANY = pl.BlockSpec(memory_space=pl.ANY)


def _ag_steps(x_ref, out_ref, send_sems, recv_sems, local_sem):
    x, y, c = lax.axis_index("x"), lax.axis_index("y"), lax.axis_index("c")
    me, sibling = (x, y, c), (x, y, 1 - c)
    chips = [(1 - x, y), (x, 1 - y), (1 - x, 1 - y)]

    def slot(px, py, pc):
        return out_ref.at[4 * px + 2 * py + pc]

    def copy(k, block, to, src=None):
        return pltpu.make_async_remote_copy(
            src_ref=slot(*block) if src is None else src, dst_ref=slot(*block),
            send_sem=send_sems.at[k], recv_sem=recv_sems.at[k], device_id=to, device_id_type=MESH_ID)

    mine = pltpu.make_async_copy(x_ref, slot(*me), local_sem)
    first = [copy(0, me, sibling, src=x_ref)]
    first += [copy(1 + j, me, (*chip, c), src=x_ref) for j, chip in enumerate(chips)]
    passed = [copy(4 + j, (*chip, c), sibling) for j, chip in enumerate(chips)]

    def start():
        mine.start()
        for cp in first:
            cp.start()

    def mid():
        for j, chip in enumerate(chips):
            copy(1 + j, (*chip, c), me).wait_recv()
            passed[j].start()

    def finish():
        copy(0, sibling, me).wait_recv()
        for j, chip in enumerate(chips):
            copy(4 + j, (*chip, 1 - c), me).wait_recv()
        for cp in first + passed:
            cp.wait_send()
        mine.wait()

    return start, mid, finish


def _rs1_steps(g_ref, r_ref, send_sems, recv_sems):
    x, y, c = lax.axis_index("x"), lax.axis_index("y"), lax.axis_index("c")
    copies = [pltpu.make_async_remote_copy(
        src_ref=g_ref.at[2 * k + (1 - c)], dst_ref=r_ref.at[k], send_sem=send_sems.at[k], recv_sem=recv_sems.at[k],
        device_id=(x, y, 1 - c), device_id_type=MESH_ID) for k in range(N_CHIP)]

    def start():
        for cp in copies:
            cp.start()

    def finish():
        for cp in copies:
            cp.wait_recv()
        for cp in copies:
            cp.wait_send()

    return start, None, finish


def _rs2_steps(p_ref, r_ref, send_sems, recv_sems, local_sem):
    x, y, c = lax.axis_index("x"), lax.axis_index("y"), lax.axis_index("c")
    myk = 2 * x + y
    chips = [(1 - x, y), (x, 1 - y), (1 - x, 1 - y)]
    mine = pltpu.make_async_copy(p_ref.at[myk], r_ref.at[myk], local_sem)
    sends = [pltpu.make_async_remote_copy(
        src_ref=p_ref.at[2 * px + py], dst_ref=r_ref.at[myk], send_sem=send_sems.at[j], recv_sem=recv_sems.at[j],
        device_id=(px, py, c), device_id_type=MESH_ID) for j, (px, py) in enumerate(chips)]

    def start():
        mine.start()
        for cp in sends:
            cp.start()

    def finish():
        for j, (px, py) in enumerate(chips):
            pltpu.make_async_remote_copy(
                src_ref=p_ref.at[myk], dst_ref=r_ref.at[2 * px + py], send_sem=send_sems.at[j], recv_sem=recv_sems.at[j],
                device_id=(px, py, c), device_id_type=MESH_ID).wait_recv()
        for cp in sends:
            cp.wait_send()
        mine.wait()

    return start, None, finish


def _dma_sems(*counts):
    return [pltpu.SemaphoreType.DMA((n,)) if n else pltpu.SemaphoreType.DMA for n in counts]


def _ag_rider(shard):
    return _Rider([shard], [SDS((N_DEV,) + shard.shape, shard.dtype)], _dma_sems(7, 7, 0), _ag_steps)


def _rs1_rider(g):
    return _Rider([g], [SDS((N_CHIP,) + g.shape[1:], g.dtype)], _dma_sems(N_CHIP, N_CHIP), _rs1_steps)


def _rs2_rider(p):
    return _Rider([p], [SDS(p.shape, p.dtype)], _dma_sems(3, 3, 0), _rs2_steps)


def _comm_call(name, riders):
    r_in, r_out, _ = _rider_counts(riders)

    def body(*refs):
        hooks = _rider_hooks(riders, refs[:r_in], refs[r_in:r_in + r_out], refs[r_in + r_out:])
        for start, _, _ in hooks:
            start()
        for _, mid, _ in hooks:
            if mid is not None:
                mid()
        for _, _, finish in hooks:
            finish()

    ops = [a for r in riders for a in r.ins]
    outs = pl.pallas_call(
        body, out_shape=[o for r in riders for o in r.outs], in_specs=[_ANY] * r_in, out_specs=[_ANY] * r_out,
        scratch_shapes=[s for r in riders for s in r.sems], name=name)(*ops)
    return list(outs)


def _all_gather(name, shard):
    return _comm_call(name, [_ag_rider(shard)])[0]


def _pair_add(name, g, r1, c_idx):
    _, R, C = g.shape
    tr = _row_tile(R, C, 2)

    def body(c_ref, g_ref, r_ref, o_ref):
        o_ref[...] = (g_ref[...].astype(F32) + r_ref[...].astype(F32)).astype(o_ref.dtype)

    grid_spec = pltpu.PrefetchScalarGridSpec(
        num_scalar_prefetch=1, grid=(N_CHIP, R // tr),
        in_specs=[pl.BlockSpec((None, tr, C), lambda k, i, c_ref: (2 * k + c_ref[0], i, 0)),
                  pl.BlockSpec((None, tr, C), lambda k, i, c_ref: (k, i, 0))],
        out_specs=pl.BlockSpec((None, tr, C), lambda k, i, c_ref: (k, i, 0)))
    return pl.pallas_call(body, grid_spec=grid_spec, out_shape=SDS((N_CHIP, R, C), g.dtype), name=name,
                          compiler_params=_params(("arbitrary", "arbitrary")))(c_idx, g, r1)


class _GradReducer:
    def __init__(self, c_idx):
        self.c_idx, self.wait_d2d, self.wait_ici, self.done = c_idx, [], [], {}

    def push(self, name, g):
        self.wait_d2d.append((name, g))

    def take(self, ici=1, d2d=1):
        jobs = [("ici",) + self.wait_ici.pop(0) for _ in range(min(ici, len(self.wait_ici)))]
        jobs += [("d2d",) + self.wait_d2d.pop(0) for _ in range(min(d2d, len(self.wait_d2d)))]
        riders = [_rs2_rider(a) if kind == "ici" else _rs1_rider(a) for kind, _, a in jobs]

        def absorb(outs):
            for (kind, name, a), out in zip(jobs, outs, strict=True):
                if kind == "ici":
                    self.done[name] = out
                else:
                    self.wait_ici.append((name, _pair_add(f"add_{name}", a, out, self.c_idx)))

        return riders, absorb

    def flush(self):
        n_calls = 0
        while self.wait_d2d or self.wait_ici:
            riders, absorb = self.take(ici=len(self.wait_ici), d2d=len(self.wait_d2d))
            absorb(_comm_call(f"rs_tail_{n_calls}", riders))
            n_calls += 1
        return self.done


PACK_ROWS = 64


def _pack(arrs):
    flat = jnp.concatenate([a.reshape(-1).astype(F32) for a in arrs])
    n = flat.shape[0]
    unit = PACK_ROWS * LANES
    padded = -(-n // unit) * unit
    return jnp.pad(flat, (0, padded - n)).reshape(padded // LANES, LANES)


def _unpack(buf, shapes):
    flat = buf.reshape(-1)
    outs, off = [], 0
    for shp in shapes:
        n = math.prod(shp)
        outs.append(flat[off:off + n].reshape(shp))
        off += n
    return outs


def _s5_discretise(lre, lim, log_dt, bre, bim):
    dt = jnp.exp(log_dt)[:, None]
    e_m1 = jnp.expm1(lre * dt)
    th = lim * dt
    lr = (e_m1 + 1.0) * jnp.cos(th)
    li = (e_m1 + 1.0) * jnp.sin(th)
    lr_m1 = e_m1 * jnp.cos(th) - 2.0 * jnp.square(jnp.sin(0.5 * th))
    den = lre * lre + lim * lim
    cr = (lr_m1 * lre + li * lim) / den
    ci = (li * lre - lr_m1 * lim) / den
    bbr = cr[..., None] * bre - ci[..., None] * bim
    bbi = cr[..., None] * bim + ci[..., None] * bre
    return lr, li, bbr, bbi


def _expand_diag(m, rows_first):
    G, A, B = m.shape
    q = G // GROUPS_PER_BLOCK
    eye = jnp.eye(GROUPS_PER_BLOCK, dtype=m.dtype)
    m5 = m.reshape(q, GROUPS_PER_BLOCK, A, 1, B) * eye[None, :, None, :, None]
    return m5.reshape(q, GROUPS_PER_BLOCK * A, GROUPS_PER_BLOCK * B)


def _extract_diag(m, A, B):
    q = m.shape[0]
    m5 = m.reshape(q, GROUPS_PER_BLOCK, A, GROUPS_PER_BLOCK, B)
    d = jnp.stack([m5[:, g, :, g, :] for g in range(GROUPS_PER_BLOCK)], axis=1)
    return d.reshape(q * GROUPS_PER_BLOCK, A, B)


def _ffn_fwd(tag, h, gain, wg, wu_shard, wd_shard, tail_riders, Tp):
    D = h.shape[1]
    J, _, Fb = wg.shape
    tmn = _row_tile(Tp, D)
    (n,) = _rowcall(f"{tag}_norm", lambda i, x, g: (_rmsnorm(x, g),), Tp, tmn, [(h, 0, 0, D)], [gain], [(D, BF16)])
    gate, (wu,) = _mm_bcast(f"{tag}_gate", n, wg, NN, False, BF16, riders=[_ag_rider(wu_shard)])
    up, (wd,) = _mm_bcast(f"{tag}_up", n, wu, NN, False, BF16, riders=[_ag_rider(wd_shard)])
    rows = J * Tp
    tma = _row_tile(rows, Fb, 2, 1024 * 1024)
    g2, u2 = gate.reshape(rows, Fb), up.reshape(rows, Fb)
    (act,) = _rowcall(f"{tag}_act", lambda i, g, u: (_swiglu_act(g, u),), rows, tma,
                      [(g2, 0, 0, Fb), (u2, 0, 0, Fb)], [], [(Fb, BF16)])
    act = act.reshape(J, Tp, Fb)
    h_out = _mm_red(f"{tag}_down", [(act, wd)], NN, F32, scale=0.5, res=h, riders=tail_riders)
    tail = []
    if tail_riders:
        h_out, tail = h_out
    return h_out, (n, gate, up, act), wu, wd, tail


def _ffn_bwd(tag, dh, h, gain, wg, wu, wd, saved, Tp, red):
    n, gate, up, act = saved
    D = h.shape[1]
    J, _, Fb = wg.shape

    def carried(fn, *args, ici=1, d2d=1, **kw):
        riders, absorb = red.take(ici, d2d)
        if not riders:
            return fn(*args, **kw)
        out, routs = fn(*args, riders=riders, **kw)
        absorb(routs)
        return out

    dact = carried(_mm_bcast, f"{tag}_dact", dh, wd, NT, False, BF16)
    red.push(f"{tag}_w_down", carried(_mm_tn, f"{tag}_dwd", act, dh, J, Fb, D, BF16, scale=0.5))
    rows = J * Tp
    tma = _row_tile(rows, Fb, 2, 1024 * 1024)

    def act_bwd(i, g, u, d):
        _, vjp = jax.vjp(_swiglu_act, g, u)
        return vjp(0.5 * d.astype(F32))

    dgate, dup = _rowcall(f"{tag}_act_bwd", act_bwd, rows, tma,
                          [(gate.reshape(rows, Fb), 0, 0, Fb), (up.reshape(rows, Fb), 0, 0, Fb),
                           (dact.reshape(rows, Fb), 0, 0, Fb)], [], [(Fb, BF16), (Fb, BF16)])
    dgate, dup = dgate.reshape(J, Tp, Fb), dup.reshape(J, Tp, Fb)
    red.push(f"{tag}_w_gate", carried(_mm_tn, f"{tag}_dwg", n, dgate, J, D, Fb, BF16))
    red.push(f"{tag}_w_up", carried(_mm_tn, f"{tag}_dwu", n, dup, J, D, Fb, BF16))
    half = D // 2
    dn = [carried(_mm_red, f"{tag}_dn{part}", [(dgate, wg), (dup, wu)], NT, F32, n_off=part * half, n_len=half)
          for part in range(2)]
    return _norm_bwd(f"{tag}_norm_bwd", h, gain, dn, dh, Tp)


def _norm_bwd(name, h, gain, dn_parts, dres, Tp):
    D = h.shape[1]
    n_parts = len(dn_parts)

    def fn(i, x, r, *rest):
        d = jnp.concatenate([p.astype(F32) for p in rest[:n_parts]], axis=1) if n_parts > 1 else rest[0].astype(F32)
        _, vjp = jax.vjp(_rmsnorm, x, rest[n_parts])
        dx, dg = vjp(d)
        return r + dx, dg

    rows = [(h, 0, 0, D), (dres, 0, 0, D)] + [(p, 0, 0, p.shape[1]) for p in dn_parts]
    return _rowcall(name, fn, Tp, _row_tile(Tp, D), rows, [gain], [(D, F32)], [(1, D)])


def kernel(x, meta_tokens, ffn1_norm, ffn1_w_gate, ffn1_w_up, ffn1_w_down, mix_norm, w_in, rg_conv_w, rg_conv_b, rg_w_a, rg_b_a, rg_w_x, rg_b_x, rg_lambda, s5_lambda_re, s5_lambda_im, s5_log_dt, s5_b_re, s5_b_im, s5_c_re, s5_c_im, s5_d, s5_glu_w, s5_glu_b, rg_out_norm, s5_out_norm, w_out, ffn2_norm, ffn2_w_gate, ffn2_w_up, ffn2_w_down, final_norm, loss_target, m_meta_tokens, m_ffn1_norm, m_ffn1_w_gate, m_ffn1_w_up, m_ffn1_w_down, m_mix_norm, m_w_in, m_rg_conv_w, m_rg_conv_b, m_rg_w_a, m_rg_b_a, m_rg_w_x, m_rg_b_x, m_rg_lambda, m_s5_lambda_re, m_s5_lambda_im, m_s5_log_dt, m_s5_b_re, m_s5_b_im, m_s5_c_re, m_s5_c_im, m_s5_d, m_s5_glu_w, m_s5_glu_b, m_rg_out_norm, m_s5_out_norm, m_w_out, m_ffn2_norm, m_ffn2_w_gate, m_ffn2_w_up, m_ffn2_w_down, m_final_norm, v_meta_tokens, v_ffn1_norm, v_ffn1_w_gate, v_ffn1_w_up, v_ffn1_w_down, v_mix_norm, v_w_in, v_rg_conv_w, v_rg_conv_b, v_rg_w_a, v_rg_b_a, v_rg_w_x, v_rg_b_x, v_rg_lambda, v_s5_lambda_re, v_s5_lambda_im, v_s5_log_dt, v_s5_b_re, v_s5_b_im, v_s5_c_re, v_s5_c_im, v_s5_d, v_s5_glu_w, v_s5_glu_b, v_rg_out_norm, v_s5_out_norm, v_w_out, v_ffn2_norm, v_ffn2_w_gate, v_ffn2_w_up, v_ffn2_w_down, v_final_norm):
    weights = dict(
        meta_tokens=meta_tokens, ffn1_norm=ffn1_norm, ffn1_w_gate=ffn1_w_gate, ffn1_w_up=ffn1_w_up, ffn1_w_down=ffn1_w_down,
        mix_norm=mix_norm, w_in=w_in, rg_conv_w=rg_conv_w, rg_conv_b=rg_conv_b, rg_w_a=rg_w_a, rg_b_a=rg_b_a, rg_w_x=rg_w_x,
        rg_b_x=rg_b_x, rg_lambda=rg_lambda, s5_lambda_re=s5_lambda_re, s5_lambda_im=s5_lambda_im, s5_log_dt=s5_log_dt,
        s5_b_re=s5_b_re, s5_b_im=s5_b_im, s5_c_re=s5_c_re, s5_c_im=s5_c_im, s5_d=s5_d, s5_glu_w=s5_glu_w, s5_glu_b=s5_glu_b,
        rg_out_norm=rg_out_norm, s5_out_norm=s5_out_norm, w_out=w_out, ffn2_norm=ffn2_norm, ffn2_w_gate=ffn2_w_gate,
        ffn2_w_up=ffn2_w_up, ffn2_w_down=ffn2_w_down, final_norm=final_norm)
    moments_m = dict(
        meta_tokens=m_meta_tokens, ffn1_norm=m_ffn1_norm, ffn1_w_gate=m_ffn1_w_gate, ffn1_w_up=m_ffn1_w_up,
        ffn1_w_down=m_ffn1_w_down, mix_norm=m_mix_norm, w_in=m_w_in, rg_conv_w=m_rg_conv_w, rg_conv_b=m_rg_conv_b,
        rg_w_a=m_rg_w_a, rg_b_a=m_rg_b_a, rg_w_x=m_rg_w_x, rg_b_x=m_rg_b_x, rg_lambda=m_rg_lambda,
        s5_lambda_re=m_s5_lambda_re, s5_lambda_im=m_s5_lambda_im, s5_log_dt=m_s5_log_dt, s5_b_re=m_s5_b_re,
        s5_b_im=m_s5_b_im, s5_c_re=m_s5_c_re, s5_c_im=m_s5_c_im, s5_d=m_s5_d, s5_glu_w=m_s5_glu_w, s5_glu_b=m_s5_glu_b,
        rg_out_norm=m_rg_out_norm, s5_out_norm=m_s5_out_norm, w_out=m_w_out, ffn2_norm=m_ffn2_norm,
        ffn2_w_gate=m_ffn2_w_gate, ffn2_w_up=m_ffn2_w_up, ffn2_w_down=m_ffn2_w_down, final_norm=m_final_norm)
    moments_v = dict(
        meta_tokens=v_meta_tokens, ffn1_norm=v_ffn1_norm, ffn1_w_gate=v_ffn1_w_gate, ffn1_w_up=v_ffn1_w_up,
        ffn1_w_down=v_ffn1_w_down, mix_norm=v_mix_norm, w_in=v_w_in, rg_conv_w=v_rg_conv_w, rg_conv_b=v_rg_conv_b,
        rg_w_a=v_rg_w_a, rg_b_a=v_rg_b_a, rg_w_x=v_rg_w_x, rg_b_x=v_rg_b_x, rg_lambda=v_rg_lambda,
        s5_lambda_re=v_s5_lambda_re, s5_lambda_im=v_s5_lambda_im, s5_log_dt=v_s5_log_dt, s5_b_re=v_s5_b_re,
        s5_b_im=v_s5_b_im, s5_c_re=v_s5_c_re, s5_c_im=v_s5_c_im, s5_d=v_s5_d, s5_glu_w=v_s5_glu_w, s5_glu_b=v_s5_glu_b,
        rg_out_norm=v_rg_out_norm, s5_out_norm=v_s5_out_norm, w_out=v_w_out, ffn2_norm=v_ffn2_norm,
        ffn2_w_gate=v_ffn2_w_gate, ffn2_w_up=v_ffn2_w_up, ffn2_w_down=v_ffn2_w_down, final_norm=v_final_norm)
    order = list(weights)

    seq, D = x.shape[1], x.shape[2]
    R = rg_conv_b.shape[1]
    S = s5_d.shape[1]
    G, N, C = s5_b_re.shape[1:]
    heads, hd = rg_w_a.shape[1], rg_w_a.shape[3]
    Q = G // GROUPS_PER_BLOCK
    W = G * N
    NB = GROUPS_PER_BLOCK * N
    T = N_META + seq
    Tp = -(-T // LANES) * LANES
    me = 4 * lax.axis_index("x") + 2 * lax.axis_index("y") + lax.axis_index("c")
    c_idx = lax.axis_index("c").astype(jnp.int32).reshape(1)

    big = ["ffn1_w_gate", "ffn1_w_up", "ffn1_w_down", "w_in", "s5_glu_w", "w_out", "ffn2_w_gate", "ffn2_w_up", "ffn2_w_down"]
    shard16 = {k: weights[k][0].astype(BF16) for k in big}
    full = {}
    sharded_small = ["meta_tokens", "rg_conv_w", "rg_w_a", "rg_b_a", "rg_w_x", "rg_b_x"]
    sm, full["ffn1_w_gate"] = _comm_call(
        "ag_first", [_ag_rider(_pack([weights[k] for k in sharded_small])), _ag_rider(shard16["ffn1_w_gate"])])
    sm = [jnp.stack(p) for p in zip(*[_unpack(sm[d], [weights[k].shape for k in sharded_small]) for d in range(N_DEV)])]
    meta_full = jnp.moveaxis(sm[0], 0, 1).reshape(N_META, D)
    conv_w_full = jnp.moveaxis(sm[1][:, 0], 0, 1).reshape(CONV_WIDTH, R)
    wa_full = jnp.moveaxis(sm[2][:, 0], 0, 1).reshape(heads, hd, hd)
    ba_full = jnp.moveaxis(sm[3][:, 0], 0, 1).reshape(1, R)
    wx_full = jnp.moveaxis(sm[4][:, 0], 0, 1).reshape(heads, hd, hd)
    bx_full = jnp.moveaxis(sm[5][:, 0], 0, 1).reshape(1, R)

    lam_fn = functools.partial(_s5_discretise)
    (lr, li, bbr, bbi), disc_vjp = jax.vjp(lam_fn, s5_lambda_re[0], s5_lambda_im[0], s5_log_dt[0], s5_b_re[0], s5_b_im[0])
    lr_row, li_row = lr.reshape(1, W), li.reshape(1, W)
    wb_r = _expand_diag(jnp.swapaxes(bbr, 1, 2), True)
    wb_i = _expand_diag(jnp.swapaxes(bbi, 1, 2), True)
    wc_r = _expand_diag(jnp.swapaxes(s5_c_re[0], 1, 2), True)
    wc_i = _expand_diag(-jnp.swapaxes(s5_c_im[0], 1, 2), True)

    h0 = jnp.concatenate([meta_full, x[0], jnp.zeros((Tp - T, D), F32)], axis=0)
    tgt = jnp.concatenate([jnp.zeros((N_META, D), F32), loss_target[0], jnp.zeros((Tp - T, D), F32)], axis=0)

    h1, ffn1_saved, full["ffn1_w_up"], full["ffn1_w_down"], (full["w_in"],) = _ffn_fwd(
        "ffn1", h0, ffn1_norm, full["ffn1_w_gate"], shard16["ffn1_w_up"], shard16["ffn1_w_down"],
        [_ag_rider(shard16["w_in"])], Tp)

    tmd = _row_tile(Tp, D)
    (n2,) = _rowcall("mix_norm", lambda i, a, g: (_rmsnorm(a, g),), Tp, tmd, [(h1, 0, 0, D)], [mix_norm], [(D, BF16)])
    proj, (full["w_out"], full["s5_glu_w"]) = _mm_bcast(
        "w_in", n2, full["w_in"], NN, True, F32, riders=[_ag_rider(shard16["w_out"]), _ag_rider(shard16["s5_glu_w"])])

    tmr = _row_tile(Tp, R)
    tb = _pick(Tp, 256, SUBLANES)
    xc = _conv_fwd(proj, conv_w_full, rg_conv_b, R, tmr)
    a_t, b_t = _rowcall("rg_gates", lambda i, *a: _rg_gates(*a), Tp, tmr, [(xc, 0, 0, R)],
                        [wa_full, ba_full, wx_full, bx_full, rg_lambda], [(R, F32), (R, F32)])
    h_rg, hprev = _rg_scan_fwd(a_t, b_t, tb, _pick(R, 512, LANES))
    (yn_rg,) = _rowcall("rg_out", lambda i, *a: (_rg_out(*a),), Tp, tmr, [(h_rg, 0, 0, R), (proj, 0, 1, R)],
                        [rg_out_norm], [(R, BF16)])

    u_off = 2 * R // LANES
    bur = _mm_bdiag("s5_bu_re", [(proj, wb_r, u_off)], NN, LANES, NB, F32)
    bui = _mm_bdiag("s5_bu_im", [(proj, wb_i, u_off)], NN, LANES, NB, F32)
    cbs = _pick(W, 512, LANES)
    xr, xi, xpr, xpi, (full["ffn2_w_gate"],) = _s5_scan_fwd(
        bur, bui, _scan_tables(lr_row, li_row, False), tb, cbs, riders=[_ag_rider(shard16["ffn2_w_gate"])])
    y_s5 = _mm_bdiag("s5_y", [(xr, wc_r, 0), (xi, wc_i, 0)], NN, NB, LANES, F32)
    tms = _row_tile(Tp, S)
    s_col = 2 * R // S
    (z,) = _rowcall("s5_pre", lambda i, *a: (_s5_pre(*a),), Tp, tms, [(y_s5, 0, 0, S), (proj, 0, s_col, S)], [s5_d],
                    [(S, F32)])
    gw = full["s5_glu_w"]
    zz = _mm_red("s5_glu", [(z, gw)], NN, F32)
    (yn_s5,) = _rowcall("s5_out", lambda i, *a: (_s5_out(*a),), Tp, tms, [(z, 0, 0, S), (zz, 0, 0, S)],
                        [s5_glu_b, s5_out_norm], [(S, BF16)])
    yn = jnp.concatenate([yn_rg, yn_s5], axis=1)
    h2 = _mm_red("w_out", [(yn, full["w_out"])], NN, F32, res=h1)

    h3, ffn2_saved, full["ffn2_w_up"], full["ffn2_w_down"], _ = _ffn_fwd(
        "ffn2", h2, ffn2_norm, full["ffn2_w_gate"], shard16["ffn2_w_up"], shard16["ffn2_w_down"], [], Tp)

    def final(i, hh, tt, g):
        out, vjp = jax.vjp(_rmsnorm, hh, g)
        row = i * tmd + lax.broadcasted_iota(jnp.int32, (tmd, 1), 0)
        valid = jnp.logical_and(row >= N_META, row < T)
        err = jnp.where(valid, out - tt, 0.0)
        part = 0.5 * jnp.sum(jnp.mean(err * err, axis=-1, keepdims=True))
        dx, dg = vjp(err * (1.0 / D))
        return dx, dg, jnp.full((SUBLANES, LANES), part, F32)

    dh3, d_final_norm, loss_part = _rowcall("final", final, Tp, tmd, [(h3, 0, 0, D), (tgt, 0, 0, D)],
                                            [final_norm.reshape(1, D)], [(D, F32)], [(1, D), (SUBLANES, LANES)])
    loss = lax.psum(loss_part[0, 0], ("x", "y", "c"))

    grads = {}
    red = _GradReducer(c_idx)

    def carried(fn, *args, ici=1, d2d=1, extra=(), **kw):
        riders, absorb = red.take(ici, d2d)
        if not riders and not extra:
            return fn(*args, **kw), []
        out, routs = fn(*args, riders=riders + list(extra), **kw)
        absorb(routs[:len(riders)])
        return out, routs[len(riders):]

    dh2, grads["ffn2_norm"] = _ffn_bwd(
        "ffn2", dh3, h2, ffn2_norm, full["ffn2_w_gate"], full["ffn2_w_up"], full["ffn2_w_down"], ffn2_saved, Tp, red)

    Kb_out = full["w_out"].shape[1]
    dyn = _mm_bcast("w_out_dx", dh2, full["w_out"], NT, True, F32)
    red.push("w_out", carried(_mm_tn, "w_out_dw", yn, dh2, N_DEV, Kb_out, D, BF16)[0])

    def s5_out_bwd(i, zv, zzv, d, gb, gn):
        _, vjp = jax.vjp(_s5_out, zv, zzv, gb, gn)
        return vjp(d)

    dz_a, dzz, grads["s5_glu_b"], grads["s5_out_norm"] = _rowcall(
        "s5_out_bwd", s5_out_bwd, Tp, tms, [(z, 0, 0, S), (zz, 0, 0, S), (dyn, 0, R // S, S)],
        [s5_glu_b, s5_out_norm], [(S, F32), (S, BF16)], [(1, S), (1, S)])
    Kb_glu = gw.shape[1]
    dz = _mm_bcast("s5_glu_dx", dzz, gw, NT, True, F32)
    red.push("s5_glu_w", carried(_mm_tn, "s5_glu_dw", z, dzz, N_DEV, Kb_glu, S, BF16, ici=0)[0])

    def s5_pre_bwd(i, yv, uv, d1, d2, dd):
        _, vjp = jax.vjp(_s5_pre, yv, uv, dd)
        return vjp(d1 + d2)

    dy, du_a, grads["s5_d"] = _rowcall(
        "s5_pre_bwd", s5_pre_bwd, Tp, tms, [(y_s5, 0, 0, S), (proj, 0, s_col, S), (dz_a, 0, 0, S), (dz, 0, 0, S)],
        [s5_d], [(S, F32), (S, F32)], [(1, S)])
    gxr = _mm_bdiag("s5_dx_re", [(dy, wc_r, 0)], NT, LANES, NB, F32)
    gxi = _mm_bdiag("s5_dx_im", [(dy, wc_i, 0)], NT, LANES, NB, F32)
    d_wc_r = _mm_tn("s5_dc_re", xr, dy, Q, NB, LANES, F32)
    d_wc_i = _mm_tn("s5_dc_im", xi, dy, Q, NB, LANES, F32)
    scan_riders, scan_absorb = red.take()
    ar, ai, dlr8, dli8, scan_routs = _s5_scan_bwd(
        gxr, gxi, xpr, xpi, _scan_tables(lr_row, -li_row, True), tb, cbs, riders=scan_riders)
    scan_absorb(scan_routs)
    du_s5 = _mm_bdiag("s5_du", [(ar, wb_r, 0), (ai, wb_i, 0)], NT, NB, LANES, BF16, res=(du_a, 0))
    d_wb_r = carried(_mm_tn, "s5_db_re", proj, ar, Q, LANES, NB, F32, a_off=u_off, d2d=0)[0]
    d_wb_i = _mm_tn("s5_db_im", proj, ai, Q, LANES, NB, F32, a_off=u_off)
    d_bbr = jnp.swapaxes(_extract_diag(d_wb_r, C, N), 1, 2)
    d_bbi = jnp.swapaxes(_extract_diag(d_wb_i, C, N), 1, 2)
    d_lr = jnp.sum(dlr8, axis=0).reshape(G, N)
    d_li = jnp.sum(dli8, axis=0).reshape(G, N)
    d_lre, d_lim, d_logdt, d_bre, d_bim = disc_vjp((d_lr, d_li, d_bbr, d_bbi))
    grads["s5_lambda_re"], grads["s5_lambda_im"], grads["s5_log_dt"] = d_lre[None], d_lim[None], d_logdt[None]
    grads["s5_b_re"], grads["s5_b_im"] = d_bre[None], d_bim[None]
    grads["s5_c_re"] = jnp.swapaxes(_extract_diag(d_wc_r, N, C), 1, 2)[None]
    grads["s5_c_im"] = -jnp.swapaxes(_extract_diag(d_wc_i, N, C), 1, 2)[None]

    def rg_out_bwd(i, hv, gv, d, gn):
        _, vjp = jax.vjp(_rg_out, hv, gv, gn)
        return vjp(d)

    dh_scan, dg_rg, grads["rg_out_norm"] = _rowcall(
        "rg_out_bwd", rg_out_bwd, Tp, tmr, [(h_rg, 0, 0, R), (proj, 0, 1, R), (dyn, 0, 0, R)], [rg_out_norm],
        [(R, F32), (R, BF16)], [(1, R)])
    db_t, da_t = _rg_scan_bwd(dh_scan, a_t, hprev, tb, _pick(R, 512, LANES))

    def rg_gates_bwd(i, xv, da, db, wa, ba, wx, bx, lam):
        _, vjp = jax.vjp(_rg_gates, xv, wa, ba, wx, bx, lam)
        return vjp((da, db))

    dxc, d_wa, d_ba, d_wx, d_bx, grads["rg_lambda"] = _rowcall(
        "rg_gates_bwd", rg_gates_bwd, Tp, _row_tile(Tp, R, 4, 1024 * 1024),
        [(xc, 0, 0, R), (da_t, 0, 0, R), (db_t, 0, 0, R)], [wa_full, ba_full, wx_full, bx_full, rg_lambda],
        [(R, F32)], [(heads, hd, hd), (1, R), (heads, hd, hd), (1, R), (1, R)])
    du_rg, d_conv_w, grads["rg_conv_b"] = _conv_bwd(dxc, proj, conv_w_full, R, tmr)

    grads["final_norm"] = d_final_norm.reshape(D)
    late = ["ffn1_norm", "mix_norm"]
    replicated = ["ffn1_norm", "mix_norm", "rg_conv_b", "rg_lambda", "s5_lambda_re", "s5_lambda_im", "s5_log_dt",
                  "s5_b_re", "s5_b_im", "s5_c_re", "s5_c_im", "s5_d", "s5_glu_b", "rg_out_norm", "s5_out_norm",
                  "ffn2_norm", "final_norm"]
    early = [k for k in replicated if k not in late]
    early_full = [grads[k].reshape(weights[k].shape) for k in early] + [d_conv_w, d_wa, d_ba, d_wx, d_bx]

    dproj = jnp.concatenate([du_rg.astype(BF16), dg_rg, du_s5], axis=1)
    win = full["w_in"]
    Nb_in = win.shape[2]
    dn2, (early_gathered,) = carried(_mm_red, "w_in_dx", [(dproj, win)], NT, F32, extra=[_ag_rider(_pack(early_full))])
    red.push("w_in", carried(_mm_tn, "w_in_dw", n2, dproj, N_DEV, D, Nb_in, BF16)[0])
    dh1, grads["mix_norm"] = _norm_bwd("mix_norm_bwd", h1, mix_norm, [dn2], dh2, Tp)

    dh0, grads["ffn1_norm"] = _ffn_bwd(
        "ffn1", dh1, h0, ffn1_norm, full["ffn1_w_gate"], full["ffn1_w_up"], full["ffn1_w_down"], ffn1_saved, Tp, red)
    grad_x = dh0[N_META:T][None]

    reduced = red.flush()
    outs = {}
    for k in big:
        r2 = reduced[k]
        _, Rk, Ck = r2.shape
        w2, m2, v2 = weights[k], moments_m[k], moments_v[k]

        def big_update(i, wv, mv, vv, p0, p1, p2, p3):
            gsum = (p0.astype(F32) + p1.astype(F32)) + (p2.astype(F32) + p3.astype(F32))
            return (gsum,) + _adamw(wv, gsum, mv, vv)

        res = _rowcall(f"adam_{k}", big_update, Rk, _row_tile(Rk, Ck, 4, 1024 * 1024),
                       [(w2, 0, 0, Ck), (m2, 0, 0, Ck), (v2, 0, 0, Ck)] + [(r2, kk, 0, Ck) for kk in range(N_CHIP)],
                       [], [(Ck, F32)] * 4)
        outs[k] = [o[None] for o in res]

    late_full = [grads[k].reshape(weights[k].shape) for k in late] + [dh0[:N_META]]
    late_gathered = _all_gather("ag_late_grads", _pack(late_full))

    def sum8(i, *parts):
        s = parts[0]
        for q in parts[1:]:
            s = s + q
        return (s,)

    def sum_devices(name, gathered, shapes):
        n_rows = gathered.shape[1]
        (summed,) = _rowcall(name, sum8, n_rows, _pick(n_rows, 512, SUBLANES),
                             [(gathered, d, 0, LANES) for d in range(N_DEV)], [], [(LANES, F32)])
        return _unpack(summed, shapes)

    early_sum = sum_devices("small_sum_early", early_gathered, [a.shape for a in early_full])
    late_sum = sum_devices("small_sum_late", late_gathered, [a.shape for a in late_full])
    g_small = dict(zip(early, early_sum[:len(early)]))
    g_small.update(zip(late, late_sum[:len(late)]))
    d_cw, d_wa_s, d_ba_s, d_wx_s, d_bx_s = early_sum[len(early):]
    d_meta = late_sum[len(late)]

    def shard_of(a, axis):
        n = a.shape[axis] // N_DEV
        return lax.dynamic_slice_in_dim(a, me * n, n, axis)

    g_small["meta_tokens"] = shard_of(d_meta, 1)
    g_small["rg_conv_w"] = shard_of(d_cw, 1)[None]
    g_small["rg_w_a"] = shard_of(d_wa_s, 1)[None]
    g_small["rg_b_a"] = shard_of(d_ba_s.reshape(heads, hd), 1)[None]
    g_small["rg_w_x"] = shard_of(d_wx_s, 1)[None]
    g_small["rg_b_x"] = shard_of(d_bx_s.reshape(heads, hd), 1)[None]
    small = replicated + sharded_small
    shapes = [weights[k].shape for k in small]
    gp = _pack([g_small[k] for k in small])
    n_rows = gp.shape[0]

    def small_update(i, wv, gv, mv, vv):
        return _adamw(wv, gv, mv, vv)

    res = _rowcall("adam_small", small_update, n_rows, _pick(n_rows, 512, SUBLANES),
                   [(_pack([weights[k] for k in small]), 0, 0, LANES), (gp, 0, 0, LANES),
                    (_pack([moments_m[k] for k in small]), 0, 0, LANES), (_pack([moments_v[k] for k in small]), 0, 0, LANES)],
                   [], [(LANES, F32)] * 3)
    unpacked = [_unpack(r, shapes) for r in res]
    for idx, k in enumerate(small):
        outs[k] = [g_small[k].reshape(weights[k].shape)] + [u[idx] for u in unpacked]

    return (loss, grad_x, *[outs[k][0] for k in order], *[outs[k][1] for k in order],
            *[outs[k][2] for k in order], *[outs[k][3] for k in order])
```

```python
import functools
import math

import jax
import jax.numpy as jnp
from jax import lax
from jax.experimental import pallas as pl
from jax.experimental.pallas import tpu as pltpu

F32, BF16 = jnp.float32, jnp.bfloat16
SDS = jax.ShapeDtypeStruct
MESH_ID = pl.DeviceIdType.MESH
N_DEV = 8
N_CHIP = 4
LANES = 128
SUBLANES = 8
VMEM_LIMIT = 56 * 1024 * 1024

EPS = 1e-6
RG_C = 8.0
N_META = 16
CONV_WIDTH = 4
S5_GROUP = 16
S5_STATE = 64
GROUPS_PER_BLOCK = LANES // S5_GROUP
ADAM_LR, ADAM_B1, ADAM_B2, ADAM_EPS, ADAM_WD, ADAM_STEP = 0.001, 0.9, 0.999, 1e-08, 0.01, 10

NN = (((1,), (0,)), ((), ()))
NT = (((1,), (1,)), ((), ()))
TN = (((0,), (0,)), ((), ()))


def _pick(n, target, mult=16):
    if n <= target:
        return n
    best = None
    for d in range(mult, target + 1, mult):
        if n % d == 0:
            best = d
    assert best is not None, (n, target, mult)
    return best


def _row_tile(nrows, ncols, itembytes=4, budget=2 * 1024 * 1024):
    return _pick(nrows, max(16, budget // (ncols * itembytes)))


def _params(sem):
    return pltpu.CompilerParams(dimension_semantics=sem, vmem_limit_bytes=VMEM_LIMIT)


def _rowcall(name, fn, nrows, tm, rows, fulls, row_outs, acc_outs=()):
    n_in = len(rows) + len(fulls)
    in_specs = []
    for arr, lead, cb, C in rows:
        if arr.ndim == 3:
            in_specs.append(pl.BlockSpec((None, tm, C), lambda i, lead=lead, cb=cb: (lead, i, cb)))
        else:
            in_specs.append(pl.BlockSpec((tm, C), lambda i, cb=cb: (i, cb)))
    for f in fulls:
        in_specs.append(pl.BlockSpec(f.shape, lambda i, nd=f.ndim: (0,) * nd))
    out_specs = [pl.BlockSpec((tm, C), lambda i: (i, 0)) for C, _ in row_outs]
    out_shape = [SDS((nrows, C), dt) for C, dt in row_outs]
    for shp in acc_outs:
        out_specs.append(pl.BlockSpec(shp, lambda i, nd=len(shp): (0,) * nd))
        out_shape.append(SDS(shp, F32))
    n_row_out = len(row_outs)

    def body(*refs):
        i = pl.program_id(0)
        res = fn(i, *[r[...] for r in refs[:n_in]])
        outs = refs[n_in:]
        for k in range(n_row_out):
            outs[k][...] = res[k].astype(outs[k].dtype)
        if acc_outs:
            @pl.when(i == 0)
            def _():
                for o in outs[n_row_out:]:
                    o[...] = jnp.zeros_like(o)
            for k in range(n_row_out, len(outs)):
                outs[k][...] += res[k].astype(F32)

    return pl.pallas_call(
        body, grid=(nrows // tm,), in_specs=in_specs, out_specs=out_specs, out_shape=out_shape,
        name=name, compiler_params=_params(("arbitrary",)))(*[r[0] for r in rows], *fulls)


class _Rider:
    def __init__(self, ins, outs, sems, steps):
        self.ins, self.outs, self.sems, self.steps = list(ins), list(outs), list(sems), steps


def _rider_counts(riders):
    return (sum(len(r.ins) for r in riders), sum(len(r.outs) for r in riders), sum(len(r.sems) for r in riders))


def _rider_hooks(riders, in_refs, out_refs, sem_refs):
    hooks, i, o, s = [], 0, 0, 0
    for r in riders:
        hooks.append(r.steps(*in_refs[i:i + len(r.ins)], *out_refs[o:o + len(r.outs)], *sem_refs[s:s + len(r.sems)]))
        i, o, s = i + len(r.ins), o + len(r.outs), s + len(r.sems)
    return hooks


def _linear_step(grid):
    step = pl.program_id(0)
    for ax in range(1, len(grid)):
        step = step * grid[ax] + pl.program_id(ax)
    return step


def _ride_begin(step, hooks):
    if hooks:
        @pl.when(step == 0)
        def _():
            for start, _, _ in hooks:
                start()


def _ride_end(step, nsteps, hooks):
    if hooks:
        if any(mid is not None for _, mid, _ in hooks):
            @pl.when(step == min((2 * nsteps) // 3, nsteps - 1))
            def _():
                for _, mid, _ in hooks:
                    if mid is not None:
                        mid()

        @pl.when(step == nsteps - 1)
        def _():
            for _, _, finish in hooks:
                finish()


def _mm(name, terms, dims, grid, nk, out_shape, out_spec, acc_shape, scale=1.0, res=None, riders=()):
    n_t = len(terms)
    kax = len(grid) - 1
    n_in = 2 * n_t + (1 if res is not None else 0)
    r_in, r_out, _ = _rider_counts(riders)
    nsteps = math.prod(grid)
    n_acc = 1 if nk > 1 else 0

    def body(*refs):
        ins, rin = refs[:n_in], refs[n_in:n_in + r_in]
        o_ref = refs[n_in + r_in]
        rout = refs[n_in + r_in + 1:n_in + r_in + 1 + r_out]
        scratch = refs[n_in + r_in + 1 + r_out:]
        step = _linear_step(grid)
        k = pl.program_id(kax)
        hooks = _rider_hooks(riders, rin, rout, scratch[n_acc:])
        _ride_begin(step, hooks)

        def product():
            r = None
            for t in range(n_t):
                d = lax.dot_general(ins[2 * t][...].astype(BF16), ins[2 * t + 1][...].astype(BF16), dims,
                                    preferred_element_type=F32)
                r = d if r is None else r + d
            return r

        def emit(r):
            r = r * scale
            if res is not None:
                r = r + ins[2 * n_t][...].astype(F32)
            o_ref[...] = r.astype(o_ref.dtype)

        if nk == 1:
            emit(product())
        else:
            acc = scratch[0]

            @pl.when(k == 0)
            def _():
                acc[...] = jnp.zeros_like(acc)

            acc[...] += product()

            @pl.when(k == nk - 1)
            def _():
                emit(acc[...])

        _ride_end(step, nsteps, hooks)

    ops, specs = [], []
    for a, a_spec, b, b_spec in terms:
        ops += [a, b]
        specs += [a_spec, b_spec]
    if res is not None:
        ops.append(res[0])
        specs.append(res[1])
    out_shapes, out_specs, scratch = [out_shape], [out_spec], []
    if n_acc:
        scratch.append(pltpu.VMEM(acc_shape, F32))
    for r in riders:
        ops += r.ins
        specs += [_ANY] * len(r.ins)
        out_shapes += r.outs
        out_specs += [_ANY] * len(r.outs)
        scratch += r.sems
    sem = ("arbitrary",) * len(grid)
    outs = pl.pallas_call(
        body, grid=grid, in_specs=specs, out_specs=out_specs, out_shape=out_shapes,
        scratch_shapes=scratch, name=name, compiler_params=_params(sem))(*ops)
    return (outs[0], list(outs[1:])) if riders else outs[0]


def _mm_bcast(name, a, w, dims, std_out, out_dtype, tm_target=384, riders=()):
    M, K = a.shape
    J = w.shape[0]
    Nb = w.shape[2] if dims == NN else w.shape[1]
    tm = _pick(M, tm_target)
    a_spec = pl.BlockSpec((tm, K), lambda j, i, k: (i, 0))
    w_spec = pl.BlockSpec((None,) + w.shape[1:], lambda j, i, k: (j, 0, 0))
    if std_out:
        out_shape, out_spec = SDS((M, J * Nb), out_dtype), pl.BlockSpec((tm, Nb), lambda j, i, k: (i, j))
    else:
        out_shape, out_spec = SDS((J, M, Nb), out_dtype), pl.BlockSpec((None, tm, Nb), lambda j, i, k: (j, i, 0))
    return _mm(name, [(a, a_spec, w, w_spec)], dims, (J, M // tm, 1), 1, out_shape, out_spec, (tm, Nb), riders=riders)


def _a_blk(a, tm, Kb, off, jb, s):
    if a.ndim == 3:
        return pl.BlockSpec((None, tm, Kb), lambda i, n, j: (j * jb + s, i, 0))
    return pl.BlockSpec((tm, Kb), lambda i, n, j: (i, j * jb + s + off))


def _mm_red(name, pairs, dims, out_dtype, scale=1.0, res=None, tm_target=704, tn_target=1024, n_off=0, n_len=None,
            jb=1, riders=()):
    a0, w0 = pairs[0][0], pairs[0][1]
    J = w0.shape[0]
    M = a0.shape[-2]
    N = w0.shape[2] if dims == NN else w0.shape[1]
    Kb = w0.shape[1] if dims == NN else w0.shape[2]
    if n_len is not None:
        N = n_len
    tm, tn = _pick(M, tm_target), _pick(N, tn_target, LANES)
    nb0 = n_off // tn
    assert J % jb == 0
    terms = []
    for a, w, *rest in pairs:
        off = rest[0] if rest else 0
        for s in range(jb):
            if dims == NN:
                w_spec = pl.BlockSpec((None, Kb, tn), lambda i, n, j, s=s: (j * jb + s, 0, n + nb0))
            else:
                w_spec = pl.BlockSpec((None, tn, Kb), lambda i, n, j, s=s: (j * jb + s, n + nb0, 0))
            terms.append((a, _a_blk(a, tm, Kb, off, jb, s), w, w_spec))
    out_spec = pl.BlockSpec((tm, tn), lambda i, n, j: (i, n))
    r = None if res is None else (res, out_spec)
    return _mm(name, terms, dims, (M // tm, N // tn, J // jb), J // jb, SDS((M, N), out_dtype), out_spec, (tm, tn),
               scale, r, riders)


def _mm_tn(name, a, b, J, Ka, Nb, out_dtype, a_off=0, b_off=0, scale=1.0, tw_target=1024, tk_target=1408, riders=()):
    M = a.shape[-2]
    tk = _pick(M, tk_target)
    tka = _pick(Ka, tw_target, LANES) if Ka % LANES == 0 else Ka
    tnb = _pick(Nb, tw_target, LANES) if Nb % LANES == 0 else Nb

    def spec(arr, width, tw, is_a, off):
        nb = width // tw

        def wi(m, n):
            return m if is_a else n
        if arr.ndim == 3:
            return pl.BlockSpec((None, tk, tw), lambda j, m, n, k: (j, k, wi(m, n)))
        if arr.shape[1] == width:
            return pl.BlockSpec((tk, tw), lambda j, m, n, k: (k, wi(m, n)))
        return pl.BlockSpec((tk, tw), lambda j, m, n, k: (k, (j + off) * nb + wi(m, n)))

    a_spec = spec(a, Ka, tka, True, a_off)
    b_spec = spec(b, Nb, tnb, False, b_off)
    out_spec = pl.BlockSpec((None, tka, tnb), lambda j, m, n, k: (j, m, n))
    return _mm(name, [(a, a_spec, b, b_spec)], TN, (J, Ka // tka, Nb // tnb, M // tk), M // tk,
               SDS((J, Ka, Nb), out_dtype), out_spec, (tka, tnb), scale, riders=riders)


def _mm_bdiag(name, pairs, dims, Kb, Nb, out_dtype, res=None, tm_target=704, riders=()):
    a0, w0 = pairs[0][0], pairs[0][1]
    M, J = a0.shape[0], w0.shape[0]
    tm = _pick(M, tm_target)
    terms = []
    for a, w, off in pairs:
        a_spec = pl.BlockSpec((tm, Kb), lambda j, i, k, off=off: (i, j + off))
        w_spec = pl.BlockSpec((None,) + w.shape[1:], lambda j, i, k: (j, 0, 0))
        terms.append((a, a_spec, w, w_spec))
    out_spec = pl.BlockSpec((tm, Nb), lambda j, i, k: (i, j))
    r = None if res is None else (res[0], pl.BlockSpec((tm, Nb), lambda j, i, k, off=res[1]: (i, j + off)))
    return _mm(name, terms, dims, (J, M // tm, 1), 1, SDS((M, J * Nb), out_dtype), out_spec, (tm, Nb), 1.0, r, riders)


def _rmsnorm(x, g):
    x = x.astype(F32)
    return x * lax.rsqrt(jnp.mean(x * x, axis=-1, keepdims=True) + EPS) * g


def _swiglu_act(gate, up):
    gate, up = gate.astype(F32), up.astype(F32)
    return gate * jax.nn.sigmoid(gate) * up


def _neg_expm1(x):
    series = -x * (1.0 + x * (1.0 / 2 + x * (1.0 / 6 + x * (1.0 / 24 + x * (1.0 / 120 + x * (1.0 / 720))))))
    return jnp.where(x > -0.25, series, 1.0 - jnp.exp(x))


def _rg_gates(xc, wa, ba, wx, bx, lam):
    heads, hd = wa.shape[0], wa.shape[1]
    xb = xc.astype(BF16)
    rs, ig = [], []
    for h in range(heads):
        xh = xb[:, h * hd:(h + 1) * hd]
        rs.append(jnp.dot(xh, wa[h].astype(BF16), preferred_element_type=F32))
        ig.append(jnp.dot(xh, wx[h].astype(BF16), preferred_element_type=F32))
    r = jax.nn.sigmoid(jnp.concatenate(rs, axis=1) + ba)
    ii = jax.nn.sigmoid(jnp.concatenate(ig, axis=1) + bx)
    log_a = -RG_C * r * jax.nn.softplus(-lam)
    a = jnp.exp(log_a)
    mult = jnp.sqrt(_neg_expm1(2.0 * log_a))
    return a, mult * ii * xc


def _rg_out(h, g, gain):
    return _rmsnorm(h * jax.nn.gelu(g), gain)


def _s5_pre(y, u, d):
    return jax.nn.gelu(y + d * u)


def _s5_out(z, zz, glu_b, gain):
    return _rmsnorm(z * jax.nn.sigmoid(zz + glu_b), gain)


def _adamw(w, g, m, v):
    m = ADAM_B1 * m + (1.0 - ADAM_B1) * g
    v = ADAM_B2 * v + (1.0 - ADAM_B2) * jnp.square(g)
    m_hat = m / (1.0 - ADAM_B1 ** ADAM_STEP)
    v_hat = v / (1.0 - ADAM_B2 ** ADAM_STEP)
    delta = -ADAM_LR * (m_hat / (jnp.sqrt(v_hat) + ADAM_EPS) + ADAM_WD * w)
    return delta, m, v


def _conv_fwd(proj, conv_w, conv_b, R, tm):
    T = proj.shape[0]
    tpb = tm // SUBLANES

    def body(cur_ref, prev_ref, w_ref, b_ref, o_ref):
        i = pl.program_id(0)
        cur = cur_ref[...]
        prev = jnp.where(i > 0, prev_ref[...], 0.0)
        ext = jnp.concatenate([prev, cur], axis=0)
        acc = b_ref[...] + w_ref[CONV_WIDTH - 1:CONV_WIDTH, :] * cur
        for s in range(1, CONV_WIDTH):
            acc = acc + w_ref[CONV_WIDTH - 1 - s:CONV_WIDTH - s, :] * pltpu.roll(ext, s, 0)[SUBLANES:, :]
        o_ref[...] = acc

    return pl.pallas_call(
        body, grid=(T // tm,),
        in_specs=[pl.BlockSpec((tm, R), lambda i: (i, 0)),
                  pl.BlockSpec((SUBLANES, R), lambda i: (jnp.maximum(i * tpb - 1, 0), 0)),
                  pl.BlockSpec((CONV_WIDTH, R), lambda i: (0, 0)), pl.BlockSpec((1, R), lambda i: (0, 0))],
        out_specs=pl.BlockSpec((tm, R), lambda i: (i, 0)), out_shape=SDS((T, R), F32),
        name="rg_conv_fwd", compiler_params=_params(("arbitrary",)))(proj, proj, conv_w, conv_b)


def _conv_bwd(dxc, proj, conv_w, R, tm):
    T = dxc.shape[0]
    tpb = tm // SUBLANES
    nt = T // tm
    n_ext = tm + SUBLANES

    def body(d_ref, dnext_ref, u_ref, uprev_ref, w_ref, du_ref, dw_ref, db_ref):
        i = pl.program_id(0)
        d = d_ref[...]
        dnext = jnp.where(i < nt - 1, dnext_ref[...], 0.0)
        dext = jnp.concatenate([d, dnext], axis=0)
        u = u_ref[...]
        uprev = jnp.where(i > 0, uprev_ref[...], 0.0)
        uext = jnp.concatenate([uprev, u], axis=0)
        du = w_ref[CONV_WIDTH - 1:CONV_WIDTH, :] * d
        dws = [jnp.sum(d * u, axis=0, keepdims=True)]
        for s in range(1, CONV_WIDTH):
            du = du + w_ref[CONV_WIDTH - 1 - s:CONV_WIDTH - s, :] * pltpu.roll(dext, n_ext - s, 0)[:tm, :]
            dws.append(jnp.sum(d * pltpu.roll(uext, s, 0)[SUBLANES:, :], axis=0, keepdims=True))
        du_ref[...] = du

        @pl.when(i == 0)
        def _():
            dw_ref[...] = jnp.zeros_like(dw_ref)
            db_ref[...] = jnp.zeros_like(db_ref)

        dw_ref[...] += jnp.concatenate(dws[::-1], axis=0)
        db_ref[...] += jnp.sum(d, axis=0, keepdims=True)

    row = pl.BlockSpec((tm, R), lambda i: (i, 0))
    return pl.pallas_call(
        body, grid=(nt,),
        in_specs=[row, pl.BlockSpec((SUBLANES, R), lambda i: (jnp.minimum((i + 1) * tpb, T // SUBLANES - 1), 0)),
                  row, pl.BlockSpec((SUBLANES, R), lambda i: (jnp.maximum(i * tpb - 1, 0), 0)),
                  pl.BlockSpec((CONV_WIDTH, R), lambda i: (0, 0))],
        out_specs=[row, pl.BlockSpec((CONV_WIDTH, R), lambda i: (0, 0)), pl.BlockSpec((1, R), lambda i: (0, 0))],
        out_shape=[SDS((T, R), F32), SDS((CONV_WIDTH, R), F32), SDS((1, R), F32)],
        name="rg_conv_bwd", compiler_params=_params(("arbitrary",)))(dxc, dxc, proj, proj, conv_w)


def _scan_tiles(tb):
    return tb // SUBLANES


def _rg_scan_fwd(a, b, tb, cb):
    T, W = a.shape

    def body(a_ref, b_ref, h_ref, p_ref, carry):
        @pl.when(pl.program_id(1) == 0)
        def _():
            carry[...] = jnp.zeros_like(carry)

        def tile(t, h):
            ds = pl.ds(pl.multiple_of(t * SUBLANES, SUBLANES), SUBLANES)
            a8, b8 = a_ref[ds, :], b_ref[ds, :]
            hs, ps = [], []
            for j in range(SUBLANES):
                ps.append(h)
                h = a8[j:j + 1, :] * h + b8[j:j + 1, :]
                hs.append(h)
            h_ref[ds, :] = jnp.concatenate(hs, axis=0)
            p_ref[ds, :] = jnp.concatenate(ps, axis=0)
            return h

        carry[0:1, :] = lax.fori_loop(0, _scan_tiles(tb), tile, carry[0:1, :])

    blk = pl.BlockSpec((tb, cb), lambda c, i: (i, c))
    return pl.pallas_call(
        body, grid=(W // cb, T // tb), in_specs=[blk, blk], out_specs=[blk, blk],
        out_shape=[SDS((T, W), F32)] * 2, scratch_shapes=[pltpu.VMEM((SUBLANES, cb), F32)],
        name="rg_scan_fwd", compiler_params=_params(("arbitrary", "arbitrary")))(a, b)


def _rg_scan_bwd(dh, a, hprev, tb, cb):
    T, W = a.shape
    nt = T // tb

    def body(g_ref, a_ref, p_ref, db_ref, da_ref, carry):
        @pl.when(pl.program_id(1) == 0)
        def _():
            carry[...] = jnp.zeros_like(carry)

        def tile(tt, c):
            t = _scan_tiles(tb) - 1 - tt
            ds = pl.ds(pl.multiple_of(t * SUBLANES, SUBLANES), SUBLANES)
            g8, a8 = g_ref[ds, :], a_ref[ds, :]
            adjs = [None] * SUBLANES
            for j in range(SUBLANES - 1, -1, -1):
                adj = g8[j:j + 1, :] + c
                adjs[j] = adj
                c = a8[j:j + 1, :] * adj
            adj8 = jnp.concatenate(adjs, axis=0)
            db_ref[ds, :] = adj8
            da_ref[ds, :] = adj8 * p_ref[ds, :]
            return c

        carry[0:1, :] = lax.fori_loop(0, _scan_tiles(tb), tile, carry[0:1, :])

    blk = pl.BlockSpec((tb, cb), lambda c, i: (nt - 1 - i, c))
    return pl.pallas_call(
        body, grid=(W // cb, nt), in_specs=[blk, blk, blk], out_specs=[blk, blk],
        out_shape=[SDS((T, W), F32)] * 2, scratch_shapes=[pltpu.VMEM((SUBLANES, cb), F32)],
        name="rg_scan_bwd", compiler_params=_params(("arbitrary", "arbitrary")))(dh, a, hprev)


SCAN_LEVELS = (1, 2, 4)


def _scan_tables(lr, li, reverse):
    def cmul(a, b):
        return a[0] * b[0] - a[1] * b[1], a[0] * b[1] + a[1] * b[0]

    powers = [(lr, li)]
    for _ in range(SUBLANES - 1):
        powers.append(cmul(powers[-1], (lr, li)))
    row = jnp.arange(SUBLANES)[:, None]
    rows = []
    for k in SCAN_LEVELS:
        has = (row <= SUBLANES - 1 - k) if reverse else (row >= k)
        rows += [jnp.where(has, powers[k - 1][0], 0.0), jnp.where(has, powers[k - 1][1], 0.0)]
    order = list(range(SUBLANES - 1, -1, -1)) if reverse else list(range(SUBLANES))
    rows += [jnp.concatenate([powers[j][0] for j in order], axis=0), jnp.concatenate([powers[j][1] for j in order], axis=0)]
    return jnp.concatenate(rows, axis=0)


def _scan_tile(vr, vi, carry, tab_ref, reverse):
    n_rows = SUBLANES
    for lvl, k in enumerate(SCAN_LEVELS):
        mr = tab_ref[2 * lvl * n_rows:(2 * lvl + 1) * n_rows, :]
        mi = tab_ref[(2 * lvl + 1) * n_rows:(2 * lvl + 2) * n_rows, :]
        shift = n_rows - k if reverse else k
        sr, si = pltpu.roll(vr, shift, 0), pltpu.roll(vi, shift, 0)
        vr, vi = vr + mr * sr - mi * si, vi + mr * si + mi * sr
    base = 2 * len(SCAN_LEVELS) * n_rows
    pr, pi = tab_ref[base:base + n_rows, :], tab_ref[base + n_rows:base + 2 * n_rows, :]
    cr, ci = carry
    return vr + pr * cr - pi * ci, vi + pr * ci + pi * cr


def _s5_scan_fwd(bur, bui, tab, tb, cb, riders=()):
    T, W = bur.shape
    grid = (W // cb, T // tb)
    r_in, r_out, _ = _rider_counts(riders)

    def body(*refs):
        br_ref, bi_ref, tab_ref = refs[:3]
        xr_ref, xi_ref, pr_ref, pi_ref = refs[3 + r_in:7 + r_in]
        cr, ci = refs[7 + r_in + r_out:9 + r_in + r_out]
        step = _linear_step(grid)
        hooks = _rider_hooks(riders, refs[3:3 + r_in], refs[7 + r_in:7 + r_in + r_out], refs[9 + r_in + r_out:])
        _ride_begin(step, hooks)

        @pl.when(pl.program_id(1) == 0)
        def _():
            cr[...] = jnp.zeros_like(cr)
            ci[...] = jnp.zeros_like(ci)

        first_row = lax.broadcasted_iota(jnp.int32, (SUBLANES, cb), 0) == 0

        def tile(t, carry):
            ds = pl.ds(pl.multiple_of(t * SUBLANES, SUBLANES), SUBLANES)
            xr8, xi8 = _scan_tile(br_ref[ds, :], bi_ref[ds, :], carry, tab_ref, reverse=False)
            xr_ref[ds, :] = xr8
            xi_ref[ds, :] = xi8
            pr_ref[ds, :] = jnp.where(first_row, carry[0], pltpu.roll(xr8, 1, 0))
            pi_ref[ds, :] = jnp.where(first_row, carry[1], pltpu.roll(xi8, 1, 0))
            return xr8[SUBLANES - 1:SUBLANES, :], xi8[SUBLANES - 1:SUBLANES, :]

        xr, xi = lax.fori_loop(0, _scan_tiles(tb), tile, (cr[0:1, :], ci[0:1, :]))
        cr[0:1, :] = xr
        ci[0:1, :] = xi
        _ride_end(step, math.prod(grid), hooks)

    blk = pl.BlockSpec((tb, cb), lambda c, i: (i, c))
    tabs = pl.BlockSpec((tab.shape[0], cb), lambda c, i: (0, c))
    outs = pl.pallas_call(
        body, grid=grid, in_specs=[blk, blk, tabs] + [_ANY] * r_in, out_specs=[blk] * 4 + [_ANY] * r_out,
        out_shape=[SDS((T, W), F32)] * 4 + [o for r in riders for o in r.outs],
        scratch_shapes=[pltpu.VMEM((SUBLANES, cb), F32), pltpu.VMEM((SUBLANES, cb), F32)] + [s for r in riders for s in r.sems],
        name="s5_scan_fwd", compiler_params=_params(("arbitrary", "arbitrary")))(
            bur, bui, tab, *[a for r in riders for a in r.ins])
    return tuple(outs[:4]) + (list(outs[4:]),)


def _s5_scan_bwd(gr, gi, pr, pi, tab, tb, cb, riders=()):
    T, W = gr.shape
    nt = T // tb
    grid = (W // cb, nt)
    r_in, r_out, _ = _rider_counts(riders)

    def body(*refs):
        gr_ref, gi_ref, pr_ref, pi_ref, tab_ref = refs[:5]
        ar_ref, ai_ref, dlr_ref, dli_ref = refs[5 + r_in:9 + r_in]
        cr, ci = refs[9 + r_in + r_out:11 + r_in + r_out]
        step = _linear_step(grid)
        hooks = _rider_hooks(riders, refs[5:5 + r_in], refs[9 + r_in:9 + r_in + r_out], refs[11 + r_in + r_out:])
        _ride_begin(step, hooks)

        @pl.when(pl.program_id(1) == 0)
        def _():
            cr[...] = jnp.zeros_like(cr)
            ci[...] = jnp.zeros_like(ci)
            dlr_ref[...] = jnp.zeros_like(dlr_ref)
            dli_ref[...] = jnp.zeros_like(dli_ref)

        def tile(tt, carry):
            t = _scan_tiles(tb) - 1 - tt
            ds = pl.ds(pl.multiple_of(t * SUBLANES, SUBLANES), SUBLANES)
            ar8, ai8 = _scan_tile(gr_ref[ds, :], gi_ref[ds, :], carry, tab_ref, reverse=True)
            ar_ref[ds, :] = ar8
            ai_ref[ds, :] = ai8
            pr8, pi8 = pr_ref[ds, :], pi_ref[ds, :]
            dlr_ref[...] += ar8 * pr8 + ai8 * pi8
            dli_ref[...] += ai8 * pr8 - ar8 * pi8
            return ar8[0:1, :], ai8[0:1, :]

        ar, ai = lax.fori_loop(0, _scan_tiles(tb), tile, (cr[0:1, :], ci[0:1, :]))
        cr[0:1, :] = ar
        ci[0:1, :] = ai
        _ride_end(step, math.prod(grid), hooks)

    blk = pl.BlockSpec((tb, cb), lambda c, i: (nt - 1 - i, c))
    tabs = pl.BlockSpec((tab.shape[0], cb), lambda c, i: (0, c))
    acc = pl.BlockSpec((SUBLANES, cb), lambda c, i: (0, c))
    outs = pl.pallas_call(
        body, grid=grid, in_specs=[blk] * 4 + [tabs] + [_ANY] * r_in, out_specs=[blk, blk, acc, acc] + [_ANY] * r_out,
        out_shape=[SDS((T, W), F32)] * 2 + [SDS((SUBLANES, W), F32)] * 2 + [o for r in riders for o in r.outs],
        scratch_shapes=[pltpu.VMEM((SUBLANES, cb), F32), pltpu.VMEM((SUBLANES, cb), F32)] + [s for r in riders for s in r.sems],
        name="s5_scan_bwd", compiler_params=_params(("arbitrary", "arbitrary")))(
            gr, gi, pr, pi, tab, *[a for r in riders for a in r.ins])
    return tuple(outs[:4]) + (list(outs[4:]),)


_ANY = pl.BlockSpec(memory_space=pl.ANY)


def _ag_steps(x_ref, out_ref, send_sems, recv_sems, local_sem):
    x, y, c = lax.axis_index("x"), lax.axis_index("y"), lax.axis_index("c")
    me, sibling = (x, y, c), (x, y, 1 - c)
    nbr_a = ((x + c) % 2, (y + 1 - c) % 2)
    nbr_b = ((x + 1 - c) % 2, (y + c) % 2)
    diag = (1 - x, 1 - y)

    def slot(px, py, pc):
        return out_ref.at[4 * px + 2 * py + pc]

    def copy(k, block, to, src=None):
        return pltpu.make_async_remote_copy(
            src_ref=slot(*block) if src is None else src, dst_ref=slot(*block),
            send_sem=send_sems.at[k], recv_sem=recv_sems.at[k], device_id=to, device_id_type=MESH_ID)

    mine = pltpu.make_async_copy(x_ref, slot(*me), local_sem)
    first = [copy(0, me, sibling, src=x_ref), copy(1, me, (*nbr_a, c), src=x_ref), copy(2, me, (*nbr_b, c), src=x_ref)]
    relay = copy(3, (*nbr_a, c), (*nbr_b, c))
    passed = [copy(4, (*nbr_a, c), sibling), copy(5, (*nbr_b, c), sibling), copy(6, (*diag, c), sibling)]

    def start():
        mine.start()
        for cp in first:
            cp.start()

    def mid():
        copy(1, (*nbr_a, c), me).wait_recv()
        relay.start()
        passed[0].start()

    def finish():
        copy(2, (*nbr_b, c), me).wait_recv()
        passed[1].start()
        copy(3, (*diag, c), me).wait_recv()
        passed[2].start()
        copy(0, sibling, me).wait_recv()
        copy(4, (*nbr_b, 1 - c), me).wait_recv()
        copy(5, (*nbr_a, 1 - c), me).wait_recv()
        copy(6, (*diag, 1 - c), me).wait_recv()
        for cp in first + [relay] + passed:
            cp.wait_send()
        mine.wait()

    return start, mid, finish


def _rs1_steps(g_ref, r_ref, send_sems, recv_sems):
    x, y, c = lax.axis_index("x"), lax.axis_index("y"), lax.axis_index("c")
    copies = [pltpu.make_async_remote_copy(
        src_ref=g_ref.at[2 * k + (1 - c)], dst_ref=r_ref.at[k], send_sem=send_sems.at[k], recv_sem=recv_sems.at[k],
        device_id=(x, y, 1 - c), device_id_type=MESH_ID) for k in range(N_CHIP)]

    def start():
        for cp in copies:
            cp.start()

    def finish():
        for cp in copies:
            cp.wait_recv()
        for cp in copies:
            cp.wait_send()

    return start, None, finish


def _rs2_steps(p_ref, r_ref, send_sems, recv_sems, local_sem):
    x, y, c = lax.axis_index("x"), lax.axis_index("y"), lax.axis_index("c")
    myk = 2 * x + y
    chips = [(1 - x, y), (x, 1 - y), (1 - x, 1 - y)]
    mine = pltpu.make_async_copy(p_ref.at[myk], r_ref.at[myk], local_sem)
    sends = [pltpu.make_async_remote_copy(
        src_ref=p_ref.at[2 * px + py], dst_ref=r_ref.at[myk], send_sem=send_sems.at[j], recv_sem=recv_sems.at[j],
        device_id=(px, py, c), device_id_type=MESH_ID) for j, (px, py) in enumerate(chips)]

    def start():
        mine.start()
        for cp in sends:
            cp.start()

    def finish():
        for j, (px, py) in enumerate(chips):
            pltpu.make_async_remote_copy(
                src_ref=p_ref.at[myk], dst_ref=r_ref.at[2 * px + py], send_sem=send_sems.at[j], recv_sem=recv_sems.at[j],
                device_id=(px, py, c), device_id_type=MESH_ID).wait_recv()
        for cp in sends:
            cp.wait_send()
        mine.wait()

    return start, None, finish


def _dma_sems(*counts):
    return [pltpu.SemaphoreType.DMA((n,)) if n else pltpu.SemaphoreType.DMA for n in counts]


def _ag_rider(shard):
    return _Rider([shard], [SDS((N_DEV,) + shard.shape, shard.dtype)], _dma_sems(7, 7, 0), _ag_steps)


def _rs1_rider(g):
    return _Rider([g], [SDS((N_CHIP,) + g.shape[1:], g.dtype)], _dma_sems(N_CHIP, N_CHIP), _rs1_steps)


def _rs2_rider(p):
    return _Rider([p], [SDS(p.shape, p.dtype)], _dma_sems(3, 3, 0), _rs2_steps)


def _comm_call(name, riders):
    r_in, r_out, _ = _rider_counts(riders)

    def body(*refs):
        hooks = _rider_hooks(riders, refs[:r_in], refs[r_in:r_in + r_out], refs[r_in + r_out:])
        for start, _, _ in hooks:
            start()
        for _, mid, _ in hooks:
            if mid is not None:
                mid()
        for _, _, finish in hooks:
            finish()

    ops = [a for r in riders for a in r.ins]
    outs = pl.pallas_call(
        body, out_shape=[o for r in riders for o in r.outs], in_specs=[_ANY] * r_in, out_specs=[_ANY] * r_out,
        scratch_shapes=[s for r in riders for s in r.sems], name=name)(*ops)
    return list(outs)


def _all_gather(name, shard):
    return _comm_call(name, [_ag_rider(shard)])[0]


def _pair_add(name, g, r1, c_idx):
    _, R, C = g.shape
    tr = _row_tile(R, C, 2, 6 * 1024 * 1024)

    def body(c_ref, g_ref, r_ref, o_ref):
        o_ref[...] = (g_ref[...].astype(F32) + r_ref[...].astype(F32)).astype(o_ref.dtype)

    grid_spec = pltpu.PrefetchScalarGridSpec(
        num_scalar_prefetch=1, grid=(N_CHIP, R // tr),
        in_specs=[pl.BlockSpec((None, tr, C), lambda k, i, c_ref: (2 * k + c_ref[0], i, 0)),
                  pl.BlockSpec((None, tr, C), lambda k, i, c_ref: (k, i, 0))],
        out_specs=pl.BlockSpec((None, tr, C), lambda k, i, c_ref: (k, i, 0)))
    return pl.pallas_call(body, grid_spec=grid_spec, out_shape=SDS((N_CHIP, R, C), g.dtype), name=name,
                          compiler_params=_params(("arbitrary", "arbitrary")))(c_idx, g, r1)


class _GradReducer:
    def __init__(self, c_idx):
        self.c_idx, self.wait_d2d, self.wait_ici, self.done = c_idx, [], [], {}

    def push(self, name, g):
        self.wait_d2d.append((name, g))

    def take(self, ici=1, d2d=1):
        jobs = [("ici",) + self.wait_ici.pop(0) for _ in range(min(ici, len(self.wait_ici)))]
        jobs += [("d2d",) + self.wait_d2d.pop(0) for _ in range(min(d2d, len(self.wait_d2d)))]
        riders = [_rs2_rider(a) if kind == "ici" else _rs1_rider(a) for kind, _, a in jobs]

        def absorb(outs):
            for (kind, name, a), out in zip(jobs, outs, strict=True):
                if kind == "ici":
                    self.done[name] = out
                else:
                    self.wait_ici.append((name, _pair_add(f"add_{name}", a, out, self.c_idx)))

        return riders, absorb

    def flush(self):
        n_calls = 0
        while self.wait_d2d or self.wait_ici:
            riders, absorb = self.take(ici=len(self.wait_ici), d2d=len(self.wait_d2d))
            absorb(_comm_call(f"rs_tail_{n_calls}", riders))
            n_calls += 1
        return self.done


PACK_ROWS = 64


def _pack(arrs):
    flat = jnp.concatenate([a.reshape(-1).astype(F32) for a in arrs])
    n = flat.shape[0]
    unit = PACK_ROWS * LANES
    padded = -(-n // unit) * unit
    return jnp.pad(flat, (0, padded - n)).reshape(padded // LANES, LANES)


def _unpack(buf, shapes):
    flat = buf.reshape(-1)
    outs, off = [], 0
    for shp in shapes:
        n = math.prod(shp)
        outs.append(flat[off:off + n].reshape(shp))
        off += n
    return outs


def _s5_discretise(lre, lim, log_dt, bre, bim):
    dt = jnp.exp(log_dt)[:, None]
    e_m1 = jnp.expm1(lre * dt)
    th = lim * dt
    lr = (e_m1 + 1.0) * jnp.cos(th)
    li = (e_m1 + 1.0) * jnp.sin(th)
    lr_m1 = e_m1 * jnp.cos(th) - 2.0 * jnp.square(jnp.sin(0.5 * th))
    den = lre * lre + lim * lim
    cr = (lr_m1 * lre + li * lim) / den
    ci = (li * lre - lr_m1 * lim) / den
    bbr = cr[..., None] * bre - ci[..., None] * bim
    bbi = cr[..., None] * bim + ci[..., None] * bre
    return lr, li, bbr, bbi


def _expand_diag(m, rows_first):
    G, A, B = m.shape
    q = G // GROUPS_PER_BLOCK
    eye = jnp.eye(GROUPS_PER_BLOCK, dtype=m.dtype)
    m5 = m.reshape(q, GROUPS_PER_BLOCK, A, 1, B) * eye[None, :, None, :, None]
    return m5.reshape(q, GROUPS_PER_BLOCK * A, GROUPS_PER_BLOCK * B)


def _extract_diag(m, A, B):
    q = m.shape[0]
    m5 = m.reshape(q, GROUPS_PER_BLOCK, A, GROUPS_PER_BLOCK, B)
    d = jnp.stack([m5[:, g, :, g, :] for g in range(GROUPS_PER_BLOCK)], axis=1)
    return d.reshape(q * GROUPS_PER_BLOCK, A, B)


def _ffn_fwd(tag, h, gain, wg, wu_shard, wd_shard, tail_riders, Tp):
    D = h.shape[1]
    J, _, Fb = wg.shape
    tmn = _row_tile(Tp, D)
    (n,) = _rowcall(f"{tag}_norm", lambda i, x, g: (_rmsnorm(x, g),), Tp, tmn, [(h, 0, 0, D)], [gain], [(D, BF16)])
    gate, (wu,) = _mm_bcast(f"{tag}_gate", n, wg, NN, False, BF16, riders=[_ag_rider(wu_shard)])
    up, (wd,) = _mm_bcast(f"{tag}_up", n, wu, NN, False, BF16, riders=[_ag_rider(wd_shard)])
    rows = J * Tp
    tma = _row_tile(rows, Fb, 2, 1024 * 1024)
    g2, u2 = gate.reshape(rows, Fb), up.reshape(rows, Fb)
    (act,) = _rowcall(f"{tag}_act", lambda i, g, u: (_swiglu_act(g, u),), rows, tma,
                      [(g2, 0, 0, Fb), (u2, 0, 0, Fb)], [], [(Fb, BF16)])
    act = act.reshape(J, Tp, Fb)
    h_out = _mm_red(f"{tag}_down", [(act, wd)], NN, F32, scale=0.5, res=h, jb=2, riders=tail_riders)
    tail = []
    if tail_riders:
        h_out, tail = h_out
    return h_out, (n, gate, up, act), wu, wd, tail


def _ffn_bwd(tag, dh, dh16, h, gain, wg, wu, wd, saved, Tp, red):
    n, gate, up, act = saved
    D = h.shape[1]
    J, _, Fb = wg.shape

    def carried(fn, *args, ici=1, d2d=1, **kw):
        riders, absorb = red.take(ici, d2d)
        if not riders:
            return fn(*args, **kw)
        out, routs = fn(*args, riders=riders, **kw)
        absorb(routs)
        return out

    dact = carried(_mm_bcast, f"{tag}_dact", dh16, wd, NT, False, BF16)
    red.push(f"{tag}_w_down", carried(_mm_tn, f"{tag}_dwd", act, dh16, J, Fb, D, BF16, scale=0.5))
    rows = J * Tp
    tma = _row_tile(rows, Fb, 2, 1024 * 1024)

    def act_bwd(i, g, u, d):
        _, vjp = jax.vjp(_swiglu_act, g, u)
        return vjp(0.5 * d.astype(F32))

    dgate, dup = _rowcall(f"{tag}_act_bwd", act_bwd, rows, tma,
                          [(gate.reshape(rows, Fb), 0, 0, Fb), (up.reshape(rows, Fb), 0, 0, Fb),
                           (dact.reshape(rows, Fb), 0, 0, Fb)], [], [(Fb, BF16), (Fb, BF16)])
    dgate, dup = dgate.reshape(J, Tp, Fb), dup.reshape(J, Tp, Fb)
    red.push(f"{tag}_w_gate", carried(_mm_tn, f"{tag}_dwg", n, dgate, J, D, Fb, BF16))
    red.push(f"{tag}_w_up", carried(_mm_tn, f"{tag}_dwu", n, dup, J, D, Fb, BF16))
    half = D // 2
    dn = [carried(_mm_red, f"{tag}_dn{part}", [(dgate, wg), (dup, wu)], NT, F32, n_off=part * half, n_len=half)
          for part in range(2)]
    return _norm_bwd(f"{tag}_norm_bwd", h, gain, dn, dh, Tp)


def _norm_bwd(name, h, gain, dn_parts, dres, Tp):
    D = h.shape[1]
    n_parts = len(dn_parts)

    def fn(i, x, r, *rest):
        d = jnp.concatenate([p.astype(F32) for p in rest[:n_parts]], axis=1) if n_parts > 1 else rest[0].astype(F32)
        _, vjp = jax.vjp(_rmsnorm, x, rest[n_parts])
        dx, dg = vjp(d)
        return r + dx, r + dx, dg

    rows = [(h, 0, 0, D), (dres, 0, 0, D)] + [(p, 0, 0, p.shape[1]) for p in dn_parts]
    return _rowcall(name, fn, Tp, _row_tile(Tp, D), rows, [gain], [(D, F32), (D, BF16)], [(1, D)])


def kernel(x, meta_tokens, ffn1_norm, ffn1_w_gate, ffn1_w_up, ffn1_w_down, mix_norm, w_in, rg_conv_w, rg_conv_b, rg_w_a, rg_b_a, rg_w_x, rg_b_x, rg_lambda, s5_lambda_re, s5_lambda_im, s5_log_dt, s5_b_re, s5_b_im, s5_c_re, s5_c_im, s5_d, s5_glu_w, s5_glu_b, rg_out_norm, s5_out_norm, w_out, ffn2_norm, ffn2_w_gate, ffn2_w_up, ffn2_w_down, final_norm, loss_target, m_meta_tokens, m_ffn1_norm, m_ffn1_w_gate, m_ffn1_w_up, m_ffn1_w_down, m_mix_norm, m_w_in, m_rg_conv_w, m_rg_conv_b, m_rg_w_a, m_rg_b_a, m_rg_w_x, m_rg_b_x, m_rg_lambda, m_s5_lambda_re, m_s5_lambda_im, m_s5_log_dt, m_s5_b_re, m_s5_b_im, m_s5_c_re, m_s5_c_im, m_s5_d, m_s5_glu_w, m_s5_glu_b, m_rg_out_norm, m_s5_out_norm, m_w_out, m_ffn2_norm, m_ffn2_w_gate, m_ffn2_w_up, m_ffn2_w_down, m_final_norm, v_meta_tokens, v_ffn1_norm, v_ffn1_w_gate, v_ffn1_w_up, v_ffn1_w_down, v_mix_norm, v_w_in, v_rg_conv_w, v_rg_conv_b, v_rg_w_a, v_rg_b_a, v_rg_w_x, v_rg_b_x, v_rg_lambda, v_s5_lambda_re, v_s5_lambda_im, v_s5_log_dt, v_s5_b_re, v_s5_b_im, v_s5_c_re, v_s5_c_im, v_s5_d, v_s5_glu_w, v_s5_glu_b, v_rg_out_norm, v_s5_out_norm, v_w_out, v_ffn2_norm, v_ffn2_w_gate, v_ffn2_w_up, v_ffn2_w_down, v_final_norm):
    weights = dict(
        meta_tokens=meta_tokens, ffn1_norm=ffn1_norm, ffn1_w_gate=ffn1_w_gate, ffn1_w_up=ffn1_w_up, ffn1_w_down=ffn1_w_down,
        mix_norm=mix_norm, w_in=w_in, rg_conv_w=rg_conv_w, rg_conv_b=rg_conv_b, rg_w_a=rg_w_a, rg_b_a=rg_b_a, rg_w_x=rg_w_x,
        rg_b_x=rg_b_x, rg_lambda=rg_lambda, s5_lambda_re=s5_lambda_re, s5_lambda_im=s5_lambda_im, s5_log_dt=s5_log_dt,
        s5_b_re=s5_b_re, s5_b_im=s5_b_im, s5_c_re=s5_c_re, s5_c_im=s5_c_im, s5_d=s5_d, s5_glu_w=s5_glu_w, s5_glu_b=s5_glu_b,
        rg_out_norm=rg_out_norm, s5_out_norm=s5_out_norm, w_out=w_out, ffn2_norm=ffn2_norm, ffn2_w_gate=ffn2_w_gate,
        ffn2_w_up=ffn2_w_up, ffn2_w_down=ffn2_w_down, final_norm=final_norm)
    moments_m = dict(
        meta_tokens=m_meta_tokens, ffn1_norm=m_ffn1_norm, ffn1_w_gate=m_ffn1_w_gate, ffn1_w_up=m_ffn1_w_up,
        ffn1_w_down=m_ffn1_w_down, mix_norm=m_mix_norm, w_in=m_w_in, rg_conv_w=m_rg_conv_w, rg_conv_b=m_rg_conv_b,
        rg_w_a=m_rg_w_a, rg_b_a=m_rg_b_a, rg_w_x=m_rg_w_x, rg_b_x=m_rg_b_x, rg_lambda=m_rg_lambda,
        s5_lambda_re=m_s5_lambda_re, s5_lambda_im=m_s5_lambda_im, s5_log_dt=m_s5_log_dt, s5_b_re=m_s5_b_re,
        s5_b_im=m_s5_b_im, s5_c_re=m_s5_c_re, s5_c_im=m_s5_c_im, s5_d=m_s5_d, s5_glu_w=m_s5_glu_w, s5_glu_b=m_s5_glu_b,
        rg_out_norm=m_rg_out_norm, s5_out_norm=m_s5_out_norm, w_out=m_w_out, ffn2_norm=m_ffn2_norm,
        ffn2_w_gate=m_ffn2_w_gate, ffn2_w_up=m_ffn2_w_up, ffn2_w_down=m_ffn2_w_down, final_norm=m_final_norm)
    moments_v = dict(
        meta_tokens=v_meta_tokens, ffn1_norm=v_ffn1_norm, ffn1_w_gate=v_ffn1_w_gate, ffn1_w_up=v_ffn1_w_up,
        ffn1_w_down=v_ffn1_w_down, mix_norm=v_mix_norm, w_in=v_w_in, rg_conv_w=v_rg_conv_w, rg_conv_b=v_rg_conv_b,
        rg_w_a=v_rg_w_a, rg_b_a=v_rg_b_a, rg_w_x=v_rg_w_x, rg_b_x=v_rg_b_x, rg_lambda=v_rg_lambda,
        s5_lambda_re=v_s5_lambda_re, s5_lambda_im=v_s5_lambda_im, s5_log_dt=v_s5_log_dt, s5_b_re=v_s5_b_re,
        s5_b_im=v_s5_b_im, s5_c_re=v_s5_c_re, s5_c_im=v_s5_c_im, s5_d=v_s5_d, s5_glu_w=v_s5_glu_w, s5_glu_b=v_s5_glu_b,
        rg_out_norm=v_rg_out_norm, s5_out_norm=v_s5_out_norm, w_out=v_w_out, ffn2_norm=v_ffn2_norm,
        ffn2_w_gate=v_ffn2_w_gate, ffn2_w_up=v_ffn2_w_up, ffn2_w_down=v_ffn2_w_down, final_norm=v_final_norm)
    order = list(weights)

    seq, D = x.shape[1], x.shape[2]
    R = rg_conv_b.shape[1]
    S = s5_d.shape[1]
    G, N, C = s5_b_re.shape[1:]
    heads, hd = rg_w_a.shape[1], rg_w_a.shape[3]
    Q = G // GROUPS_PER_BLOCK
    W = G * N
    NB = GROUPS_PER_BLOCK * N
    T = N_META + seq
    Tp = -(-T // LANES) * LANES
    me = 4 * lax.axis_index("x") + 2 * lax.axis_index("y") + lax.axis_index("c")
    c_idx = lax.axis_index("c").astype(jnp.int32).reshape(1)

    big = ["ffn1_w_gate", "ffn1_w_up", "ffn1_w_down", "w_in", "s5_glu_w", "w_out", "ffn2_w_gate", "ffn2_w_up", "ffn2_w_down"]
    shard16 = {k: weights[k][0].astype(BF16) for k in big}
    full = {}
    sharded_small = ["meta_tokens", "rg_conv_w", "rg_w_a", "rg_b_a", "rg_w_x", "rg_b_x"]
    sm, full["ffn1_w_gate"] = _comm_call(
        "ag_first", [_ag_rider(_pack([weights[k] for k in sharded_small])), _ag_rider(shard16["ffn1_w_gate"])])
    sm = [jnp.stack(p) for p in zip(*[_unpack(sm[d], [weights[k].shape for k in sharded_small]) for d in range(N_DEV)])]
    meta_full = jnp.moveaxis(sm[0], 0, 1).reshape(N_META, D)
    conv_w_full = jnp.moveaxis(sm[1][:, 0], 0, 1).reshape(CONV_WIDTH, R)
    wa_full = jnp.moveaxis(sm[2][:, 0], 0, 1).reshape(heads, hd, hd)
    ba_full = jnp.moveaxis(sm[3][:, 0], 0, 1).reshape(1, R)
    wx_full = jnp.moveaxis(sm[4][:, 0], 0, 1).reshape(heads, hd, hd)
    bx_full = jnp.moveaxis(sm[5][:, 0], 0, 1).reshape(1, R)

    lam_fn = functools.partial(_s5_discretise)
    (lr, li, bbr, bbi), disc_vjp = jax.vjp(lam_fn, s5_lambda_re[0], s5_lambda_im[0], s5_log_dt[0], s5_b_re[0], s5_b_im[0])
    lr_row, li_row = lr.reshape(1, W), li.reshape(1, W)
    wb_r = _expand_diag(jnp.swapaxes(bbr, 1, 2), True)
    wb_i = _expand_diag(jnp.swapaxes(bbi, 1, 2), True)
    wc_r = _expand_diag(jnp.swapaxes(s5_c_re[0], 1, 2), True)
    wc_i = _expand_diag(-jnp.swapaxes(s5_c_im[0], 1, 2), True)

    h0 = jnp.concatenate([meta_full, x[0], jnp.zeros((Tp - T, D), F32)], axis=0)
    tgt = jnp.concatenate([jnp.zeros((N_META, D), F32), loss_target[0], jnp.zeros((Tp - T, D), F32)], axis=0)

    h1, ffn1_saved, full["ffn1_w_up"], full["ffn1_w_down"], (full["w_in"],) = _ffn_fwd(
        "ffn1", h0, ffn1_norm, full["ffn1_w_gate"], shard16["ffn1_w_up"], shard16["ffn1_w_down"],
        [_ag_rider(shard16["w_in"])], Tp)

    tmd = _row_tile(Tp, D)
    (n2,) = _rowcall("mix_norm", lambda i, a, g: (_rmsnorm(a, g),), Tp, tmd, [(h1, 0, 0, D)], [mix_norm], [(D, BF16)])
    proj, (full["w_out"], full["s5_glu_w"]) = _mm_bcast(
        "w_in", n2, full["w_in"], NN, True, F32, riders=[_ag_rider(shard16["w_out"]), _ag_rider(shard16["s5_glu_w"])])

    tmr = _row_tile(Tp, R)
    tb = _pick(Tp, 256, SUBLANES)
    xc = _conv_fwd(proj, conv_w_full, rg_conv_b, R, tmr)
    a_t, b_t = _rowcall("rg_gates", lambda i, *a: _rg_gates(*a), Tp, tmr, [(xc, 0, 0, R)],
                        [wa_full, ba_full, wx_full, bx_full, rg_lambda], [(R, F32), (R, F32)])
    h_rg, hprev = _rg_scan_fwd(a_t, b_t, tb, _pick(R, 512, LANES))
    (yn_rg,) = _rowcall("rg_out", lambda i, *a: (_rg_out(*a),), Tp, tmr, [(h_rg, 0, 0, R), (proj, 0, 1, R)],
                        [rg_out_norm], [(R, BF16)])

    u_off = 2 * R // LANES
    bur = _mm_bdiag("s5_bu_re", [(proj, wb_r, u_off)], NN, LANES, NB, F32)
    bui = _mm_bdiag("s5_bu_im", [(proj, wb_i, u_off)], NN, LANES, NB, F32)
    cbs = _pick(W, 512, LANES)
    xr, xi, xpr, xpi, (full["ffn2_w_gate"],) = _s5_scan_fwd(
        bur, bui, _scan_tables(lr_row, li_row, False), tb, cbs, riders=[_ag_rider(shard16["ffn2_w_gate"])])
    y_s5 = _mm_bdiag("s5_y", [(xr, wc_r, 0), (xi, wc_i, 0)], NN, NB, LANES, F32)
    tms = _row_tile(Tp, S)
    s_col = 2 * R // S
    (z,) = _rowcall("s5_pre", lambda i, *a: (_s5_pre(*a),), Tp, tms, [(y_s5, 0, 0, S), (proj, 0, s_col, S)], [s5_d],
                    [(S, F32)])
    gw = full["s5_glu_w"]
    zz = _mm_red("s5_glu", [(z, gw)], NN, F32, jb=N_DEV)
    (yn_s5,) = _rowcall("s5_out", lambda i, *a: (_s5_out(*a),), Tp, tms, [(z, 0, 0, S), (zz, 0, 0, S)],
                        [s5_glu_b, s5_out_norm], [(S, BF16)])
    yn = jnp.concatenate([yn_rg, yn_s5], axis=1)
    h2 = _mm_red("w_out", [(yn, full["w_out"])], NN, F32, res=h1, jb=4)

    h3, ffn2_saved, full["ffn2_w_up"], full["ffn2_w_down"], _ = _ffn_fwd(
        "ffn2", h2, ffn2_norm, full["ffn2_w_gate"], shard16["ffn2_w_up"], shard16["ffn2_w_down"], [], Tp)

    def final(i, hh, tt, g):
        out, vjp = jax.vjp(_rmsnorm, hh, g)
        row = i * tmd + lax.broadcasted_iota(jnp.int32, (tmd, 1), 0)
        valid = jnp.logical_and(row >= N_META, row < T)
        err = jnp.where(valid, out - tt, 0.0)
        part = 0.5 * jnp.sum(jnp.mean(err * err, axis=-1, keepdims=True))
        dx, dg = vjp(err * (1.0 / D))
        return dx, dx, dg, jnp.full((SUBLANES, LANES), part, F32)

    dh3, dh3_16, d_final_norm, loss_part = _rowcall(
        "final", final, Tp, tmd, [(h3, 0, 0, D), (tgt, 0, 0, D)], [final_norm.reshape(1, D)],
        [(D, F32), (D, BF16)], [(1, D), (SUBLANES, LANES)])
    loss = lax.psum(loss_part[0, 0], ("x", "y", "c"))

    grads = {}
    red = _GradReducer(c_idx)

    def carried(fn, *args, ici=1, d2d=1, extra=(), **kw):
        riders, absorb = red.take(ici, d2d)
        if not riders and not extra:
            return fn(*args, **kw), []
        out, routs = fn(*args, riders=riders + list(extra), **kw)
        absorb(routs[:len(riders)])
        return out, routs[len(riders):]

    dh2, dh2_16, grads["ffn2_norm"] = _ffn_bwd(
        "ffn2", dh3, dh3_16, h2, ffn2_norm, full["ffn2_w_gate"], full["ffn2_w_up"], full["ffn2_w_down"], ffn2_saved,
        Tp, red)

    Kb_out = full["w_out"].shape[1]
    dyn = _mm_bcast("w_out_dx", dh2_16, full["w_out"], NT, True, F32)
    red.push("w_out", carried(_mm_tn, "w_out_dw", yn, dh2_16, N_DEV, Kb_out, D, BF16)[0])

    def s5_out_bwd(i, zv, zzv, d, gb, gn):
        _, vjp = jax.vjp(_s5_out, zv, zzv, gb, gn)
        return vjp(d)

    dz_a, dzz, grads["s5_glu_b"], grads["s5_out_norm"] = _rowcall(
        "s5_out_bwd", s5_out_bwd, Tp, tms, [(z, 0, 0, S), (zz, 0, 0, S), (dyn, 0, R // S, S)],
        [s5_glu_b, s5_out_norm], [(S, F32), (S, BF16)], [(1, S), (1, S)])
    Kb_glu = gw.shape[1]
    dz = _mm_bcast("s5_glu_dx", dzz, gw, NT, True, F32)
    red.push("s5_glu_w", carried(_mm_tn, "s5_glu_dw", z, dzz, N_DEV, Kb_glu, S, BF16, ici=0)[0])

    def s5_pre_bwd(i, yv, uv, d1, d2, dd):
        _, vjp = jax.vjp(_s5_pre, yv, uv, dd)
        return vjp(d1 + d2)

    dy, du_a, grads["s5_d"] = _rowcall(
        "s5_pre_bwd", s5_pre_bwd, Tp, tms, [(y_s5, 0, 0, S), (proj, 0, s_col, S), (dz_a, 0, 0, S), (dz, 0, 0, S)],
        [s5_d], [(S, F32), (S, F32)], [(1, S)])
    gxr = _mm_bdiag("s5_dx_re", [(dy, wc_r, 0)], NT, LANES, NB, F32)
    gxi = _mm_bdiag("s5_dx_im", [(dy, wc_i, 0)], NT, LANES, NB, F32)
    d_wc_r = _mm_tn("s5_dc_re", xr, dy, Q, NB, LANES, F32)
    d_wc_i = _mm_tn("s5_dc_im", xi, dy, Q, NB, LANES, F32)
    scan_riders, scan_absorb = red.take()
    ar, ai, dlr8, dli8, scan_routs = _s5_scan_bwd(
        gxr, gxi, xpr, xpi, _scan_tables(lr_row, -li_row, True), tb, cbs, riders=scan_riders)
    scan_absorb(scan_routs)
    du_s5 = _mm_bdiag("s5_du", [(ar, wb_r, 0), (ai, wb_i, 0)], NT, NB, LANES, BF16, res=(du_a, 0))
    d_wb_r = carried(_mm_tn, "s5_db_re", proj, ar, Q, LANES, NB, F32, a_off=u_off, d2d=0)[0]
    d_wb_i = _mm_tn("s5_db_im", proj, ai, Q, LANES, NB, F32, a_off=u_off)
    d_bbr = jnp.swapaxes(_extract_diag(d_wb_r, C, N), 1, 2)
    d_bbi = jnp.swapaxes(_extract_diag(d_wb_i, C, N), 1, 2)
    d_lr = jnp.sum(dlr8, axis=0).reshape(G, N)
    d_li = jnp.sum(dli8, axis=0).reshape(G, N)
    d_lre, d_lim, d_logdt, d_bre, d_bim = disc_vjp((d_lr, d_li, d_bbr, d_bbi))
    grads["s5_lambda_re"], grads["s5_lambda_im"], grads["s5_log_dt"] = d_lre[None], d_lim[None], d_logdt[None]
    grads["s5_b_re"], grads["s5_b_im"] = d_bre[None], d_bim[None]
    grads["s5_c_re"] = jnp.swapaxes(_extract_diag(d_wc_r, N, C), 1, 2)[None]
    grads["s5_c_im"] = -jnp.swapaxes(_extract_diag(d_wc_i, N, C), 1, 2)[None]

    def rg_out_bwd(i, hv, gv, d, gn):
        _, vjp = jax.vjp(_rg_out, hv, gv, gn)
        return vjp(d)

    dh_scan, dg_rg, grads["rg_out_norm"] = _rowcall(
        "rg_out_bwd", rg_out_bwd, Tp, tmr, [(h_rg, 0, 0, R), (proj, 0, 1, R), (dyn, 0, 0, R)], [rg_out_norm],
        [(R, F32), (R, BF16)], [(1, R)])
    db_t, da_t = _rg_scan_bwd(dh_scan, a_t, hprev, tb, _pick(R, 512, LANES))

    def rg_gates_bwd(i, xv, da, db, wa, ba, wx, bx, lam):
        _, vjp = jax.vjp(_rg_gates, xv, wa, ba, wx, bx, lam)
        return vjp((da, db))

    dxc, d_wa, d_ba, d_wx, d_bx, grads["rg_lambda"] = _rowcall(
        "rg_gates_bwd", rg_gates_bwd, Tp, _row_tile(Tp, R, 4, 1024 * 1024),
        [(xc, 0, 0, R), (da_t, 0, 0, R), (db_t, 0, 0, R)], [wa_full, ba_full, wx_full, bx_full, rg_lambda],
        [(R, F32)], [(heads, hd, hd), (1, R), (heads, hd, hd), (1, R), (1, R)])
    du_rg, d_conv_w, grads["rg_conv_b"] = _conv_bwd(dxc, proj, conv_w_full, R, tmr)

    grads["final_norm"] = d_final_norm.reshape(D)
    late = ["ffn1_norm", "mix_norm"]
    replicated = ["ffn1_norm", "mix_norm", "rg_conv_b", "rg_lambda", "s5_lambda_re", "s5_lambda_im", "s5_log_dt",
                  "s5_b_re", "s5_b_im", "s5_c_re", "s5_c_im", "s5_d", "s5_glu_b", "rg_out_norm", "s5_out_norm",
                  "ffn2_norm", "final_norm"]
    early = [k for k in replicated if k not in late]
    early_full = [grads[k].reshape(weights[k].shape) for k in early] + [d_conv_w, d_wa, d_ba, d_wx, d_bx]

    dproj = jnp.concatenate([du_rg.astype(BF16), dg_rg, du_s5], axis=1)
    win = full["w_in"]
    Nb_in = win.shape[2]
    dn2, (early_gathered,) = carried(_mm_red, "w_in_dx", [(dproj, win)], NT, F32, jb=4,
                                     extra=[_ag_rider(_pack(early_full))])
    red.push("w_in", carried(_mm_tn, "w_in_dw", n2, dproj, N_DEV, D, Nb_in, BF16)[0])
    dh1, dh1_16, grads["mix_norm"] = _norm_bwd("mix_norm_bwd", h1, mix_norm, [dn2], dh2, Tp)

    dh0, _, grads["ffn1_norm"] = _ffn_bwd(
        "ffn1", dh1, dh1_16, h0, ffn1_norm, full["ffn1_w_gate"], full["ffn1_w_up"], full["ffn1_w_down"], ffn1_saved,
        Tp, red)
    grad_x = dh0[N_META:T][None]

    reduced = red.flush()
    outs = {}
    for k in big:
        r2 = reduced[k]
        _, Rk, Ck = r2.shape
        w2, m2, v2 = weights[k], moments_m[k], moments_v[k]

        def big_update(i, wv, mv, vv, p0, p1, p2, p3):
            gsum = (p0.astype(F32) + p1.astype(F32)) + (p2.astype(F32) + p3.astype(F32))
            return (gsum,) + _adamw(wv, gsum, mv, vv)

        res = _rowcall(f"adam_{k}", big_update, Rk, _row_tile(Rk, Ck, 4, 1024 * 1024),
                       [(w2, 0, 0, Ck), (m2, 0, 0, Ck), (v2, 0, 0, Ck)] + [(r2, kk, 0, Ck) for kk in range(N_CHIP)],
                       [], [(Ck, F32)] * 4)
        outs[k] = [o[None] for o in res]

    late_full = [grads[k].reshape(weights[k].shape) for k in late] + [dh0[:N_META]]
    late_gathered = _all_gather("ag_late_grads", _pack(late_full))

    def sum8(i, *parts):
        s = parts[0]
        for q in parts[1:]:
            s = s + q
        return (s,)

    def sum_devices(name, gathered, shapes):
        n_rows = gathered.shape[1]
        (summed,) = _rowcall(name, sum8, n_rows, _pick(n_rows, 512, SUBLANES),
                             [(gathered, d, 0, LANES) for d in range(N_DEV)], [], [(LANES, F32)])
        return _unpack(summed, shapes)

    early_sum = sum_devices("small_sum_early", early_gathered, [a.shape for a in early_full])
    late_sum = sum_devices("small_sum_late", late_gathered, [a.shape for a in late_full])
    g_small = dict(zip(early, early_sum[:len(early)]))
    g_small.update(zip(late, late_sum[:len(late)]))
    d_cw, d_wa_s, d_ba_s, d_wx_s, d_bx_s = early_sum[len(early):]
    d_meta = late_sum[len(late)]

    def shard_of(a, axis):
        n = a.shape[axis] // N_DEV
        return lax.dynamic_slice_in_dim(a, me * n, n, axis)

    g_small["meta_tokens"] = shard_of(d_meta, 1)
    g_small["rg_conv_w"] = shard_of(d_cw, 1)[None]
    g_small["rg_w_a"] = shard_of(d_wa_s, 1)[None]
    g_small["rg_b_a"] = shard_of(d_ba_s.reshape(heads, hd), 1)[None]
    g_small["rg_w_x"] = shard_of(d_wx_s, 1)[None]
    g_small["rg_b_x"] = shard_of(d_bx_s.reshape(heads, hd), 1)[None]
    small = replicated + sharded_small
    shapes = [weights[k].shape for k in small]
    gp = _pack([g_small[k] for k in small])
    n_rows = gp.shape[0]

    def small_update(i, wv, gv, mv, vv):
        return _adamw(wv, gv, mv, vv)

    res = _rowcall("adam_small", small_update, n_rows, _pick(n_rows, 512, SUBLANES),
                   [(_pack([weights[k] for k in small]), 0, 0, LANES), (gp, 0, 0, LANES),
                    (_pack([moments_m[k] for k in small]), 0, 0, LANES), (_pack([moments_v[k] for k in small]), 0, 0, LANES)],
                   [], [(LANES, F32)] * 3)
    unpacked = [_unpack(r, shapes) for r in res]
    for idx, k in enumerate(small):
        outs[k] = [g_small[k].reshape(weights[k].shape)] + [u[idx] for u in unpacked]

    return (loss, grad_x, *[outs[k][0] for k in order], *[outs[k][1] for k in order],
            *[outs[k][2] for k in order], *[outs[k][3] for k in order])
```

```python
import functools
import math

import jax
import jax.numpy as jnp
from jax import lax
from jax.experimental import pallas as pl
from jax.experimental.pallas import tpu as pltpu

F32, BF16 = jnp.float32, jnp.bfloat16
SDS = jax.ShapeDtypeStruct
MESH_ID = pl.DeviceIdType.MESH
N_DEV = 8
N_CHIP = 4
LANES = 128
SUBLANES = 8
VMEM_LIMIT = 56 * 1024 * 1024

EPS = 1e-6
RG_C = 8.0
N_META = 16
CONV_WIDTH = 4
S5_GROUP = 16
S5_STATE = 64
GROUPS_PER_BLOCK = LANES // S5_GROUP
ADAM_LR, ADAM_B1, ADAM_B2, ADAM_EPS, ADAM_WD, ADAM_STEP = 0.001, 0.9, 0.999, 1e-08, 0.01, 10

NN = (((1,), (0,)), ((), ()))
NT = (((1,), (1,)), ((), ()))
TN = (((0,), (0,)), ((), ()))


def _pick(n, target, mult=16):
    if n <= target:
        return n
    best = None
    for d in range(mult, target + 1, mult):
        if n % d == 0:
            best = d
    assert best is not None, (n, target, mult)
    return best


def _row_tile(nrows, ncols, itembytes=4, budget=2 * 1024 * 1024):
    return _pick(nrows, max(16, budget // (ncols * itembytes)))


def _params(sem):
    return pltpu.CompilerParams(dimension_semantics=sem, vmem_limit_bytes=VMEM_LIMIT)


def _rowcall(name, fn, nrows, tm, rows, fulls, row_outs, acc_outs=()):
    n_in = len(rows) + len(fulls)
    in_specs = []
    for arr, lead, cb, C in rows:
        if arr.ndim == 3:
            in_specs.append(pl.BlockSpec((None, tm, C), lambda i, lead=lead, cb=cb: (lead, i, cb)))
        else:
            in_specs.append(pl.BlockSpec((tm, C), lambda i, cb=cb: (i, cb)))
    for f in fulls:
        in_specs.append(pl.BlockSpec(f.shape, lambda i, nd=f.ndim: (0,) * nd))
    out_specs = [pl.BlockSpec((tm, C), lambda i: (i, 0)) for C, _ in row_outs]
    out_shape = [SDS((nrows, C), dt) for C, dt in row_outs]
    for shp in acc_outs:
        out_specs.append(pl.BlockSpec(shp, lambda i, nd=len(shp): (0,) * nd))
        out_shape.append(SDS(shp, F32))
    n_row_out = len(row_outs)

    def body(*refs):
        i = pl.program_id(0)
        res = fn(i, *[r[...] for r in refs[:n_in]])
        outs = refs[n_in:]
        for k in range(n_row_out):
            outs[k][...] = res[k].astype(outs[k].dtype)
        if acc_outs:
            @pl.when(i == 0)
            def _():
                for o in outs[n_row_out:]:
                    o[...] = jnp.zeros_like(o)
            for k in range(n_row_out, len(outs)):
                outs[k][...] += res[k].astype(F32)

    return pl.pallas_call(
        body, grid=(nrows // tm,), in_specs=in_specs, out_specs=out_specs, out_shape=out_shape,
        name=name, compiler_params=_params(("arbitrary",)))(*[r[0] for r in rows], *fulls)


class _Rider:
    def __init__(self, ins, outs, sems, steps):
        self.ins, self.outs, self.sems, self.steps = list(ins), list(outs), list(sems), steps


def _rider_counts(riders):
    return (sum(len(r.ins) for r in riders), sum(len(r.outs) for r in riders), sum(len(r.sems) for r in riders))


def _rider_hooks(riders, in_refs, out_refs, sem_refs):
    hooks, i, o, s = [], 0, 0, 0
    for r in riders:
        hooks.append(r.steps(*in_refs[i:i + len(r.ins)], *out_refs[o:o + len(r.outs)], *sem_refs[s:s + len(r.sems)]))
        i, o, s = i + len(r.ins), o + len(r.outs), s + len(r.sems)
    return hooks


def _linear_step(grid):
    step = pl.program_id(0)
    for ax in range(1, len(grid)):
        step = step * grid[ax] + pl.program_id(ax)
    return step


def _ride_begin(step, hooks):
    if hooks:
        @pl.when(step == 0)
        def _():
            for start, _, _ in hooks:
                start()


def _ride_end(step, nsteps, hooks):
    if hooks:
        if any(mid is not None for _, mid, _ in hooks):
            @pl.when(step == min((2 * nsteps) // 3, nsteps - 1))
            def _():
                for _, mid, _ in hooks:
                    if mid is not None:
                        mid()

        @pl.when(step == nsteps - 1)
        def _():
            for _, _, finish in hooks:
                finish()


def _mm(name, terms, dims, grid, nk, out_shape, out_spec, acc_shape, scale=1.0, res=None, riders=()):
    n_t = len(terms)
    kax = len(grid) - 1
    n_in = 2 * n_t + (1 if res is not None else 0)
    r_in, r_out, _ = _rider_counts(riders)
    nsteps = math.prod(grid)
    n_acc = 1 if nk > 1 else 0

    def body(*refs):
        ins, rin = refs[:n_in], refs[n_in:n_in + r_in]
        o_ref = refs[n_in + r_in]
        rout = refs[n_in + r_in + 1:n_in + r_in + 1 + r_out]
        scratch = refs[n_in + r_in + 1 + r_out:]
        step = _linear_step(grid)
        k = pl.program_id(kax)
        hooks = _rider_hooks(riders, rin, rout, scratch[n_acc:])
        _ride_begin(step, hooks)

        def product():
            r = None
            for t in range(n_t):
                d = lax.dot_general(ins[2 * t][...].astype(BF16), ins[2 * t + 1][...].astype(BF16), dims,
                                    preferred_element_type=F32)
                r = d if r is None else r + d
            return r

        def emit(r):
            r = r * scale
            if res is not None:
                r = r + ins[2 * n_t][...].astype(F32)
            o_ref[...] = r.astype(o_ref.dtype)

        if nk == 1:
            emit(product())
        else:
            acc = scratch[0]

            @pl.when(k == 0)
            def _():
                acc[...] = jnp.zeros_like(acc)

            acc[...] += product()

            @pl.when(k == nk - 1)
            def _():
                emit(acc[...])

        _ride_end(step, nsteps, hooks)

    ops, specs = [], []
    for a, a_spec, b, b_spec in terms:
        ops += [a, b]
        specs += [a_spec, b_spec]
    if res is not None:
        ops.append(res[0])
        specs.append(res[1])
    out_shapes, out_specs, scratch = [out_shape], [out_spec], []
    if n_acc:
        scratch.append(pltpu.VMEM(acc_shape, F32))
    for r in riders:
        ops += r.ins
        specs += [_ANY] * len(r.ins)
        out_shapes += r.outs
        out_specs += [_ANY] * len(r.outs)
        scratch += r.sems
    sem = ("arbitrary",) * len(grid)
    outs = pl.pallas_call(
        body, grid=grid, in_specs=specs, out_specs=out_specs, out_shape=out_shapes,
        scratch_shapes=scratch, name=name, compiler_params=_params(sem))(*ops)
    return (outs[0], list(outs[1:])) if riders else outs[0]


def _mm_bcast(name, a, w, dims, std_out, out_dtype, tm_target=384, riders=()):
    M, K = a.shape
    J = w.shape[0]
    Nb = w.shape[2] if dims == NN else w.shape[1]
    tm = _pick(M, tm_target)
    a_spec = pl.BlockSpec((tm, K), lambda j, i, k: (i, 0))
    w_spec = pl.BlockSpec((None,) + w.shape[1:], lambda j, i, k: (j, 0, 0))
    if std_out:
        out_shape, out_spec = SDS((M, J * Nb), out_dtype), pl.BlockSpec((tm, Nb), lambda j, i, k: (i, j))
    else:
        out_shape, out_spec = SDS((J, M, Nb), out_dtype), pl.BlockSpec((None, tm, Nb), lambda j, i, k: (j, i, 0))
    return _mm(name, [(a, a_spec, w, w_spec)], dims, (J, M // tm, 1), 1, out_shape, out_spec, (tm, Nb), riders=riders)


def _a_blk(a, tm, Kb, off, jb, s):
    if a.ndim == 3:
        return pl.BlockSpec((None, tm, Kb), lambda i, n, j: (j * jb + s, i, 0))
    return pl.BlockSpec((tm, Kb), lambda i, n, j: (i, j * jb + s + off))


def _mm_red(name, pairs, dims, out_dtype, scale=1.0, res=None, tm_target=704, tn_target=1024, n_off=0, n_len=None,
            jb=1, riders=()):
    a0, w0 = pairs[0][0], pairs[0][1]
    J = w0.shape[0]
    M = a0.shape[-2]
    N = w0.shape[2] if dims == NN else w0.shape[1]
    Kb = w0.shape[1] if dims == NN else w0.shape[2]
    if n_len is not None:
        N = n_len
    tm, tn = _pick(M, tm_target), _pick(N, tn_target, LANES)
    nb0 = n_off // tn
    assert J % jb == 0
    terms = []
    for a, w, *rest in pairs:
        off = rest[0] if rest else 0
        for s in range(jb):
            if dims == NN:
                w_spec = pl.BlockSpec((None, Kb, tn), lambda i, n, j, s=s: (j * jb + s, 0, n + nb0))
            else:
                w_spec = pl.BlockSpec((None, tn, Kb), lambda i, n, j, s=s: (j * jb + s, n + nb0, 0))
            terms.append((a, _a_blk(a, tm, Kb, off, jb, s), w, w_spec))
    out_spec = pl.BlockSpec((tm, tn), lambda i, n, j: (i, n))
    r = None if res is None else (res, out_spec)
    return _mm(name, terms, dims, (M // tm, N // tn, J // jb), J // jb, SDS((M, N), out_dtype), out_spec, (tm, tn),
               scale, r, riders)


def _mm_tn(name, a, b, J, Ka, Nb, out_dtype, a_off=0, b_off=0, scale=1.0, tw_target=1024, tk_target=1408, riders=()):
    M = a.shape[-2]
    tk = _pick(M, tk_target)
    tka = _pick(Ka, tw_target, LANES) if Ka % LANES == 0 else Ka
    tnb = _pick(Nb, tw_target, LANES) if Nb % LANES == 0 else Nb

    def spec(arr, width, tw, is_a, off):
        nb = width // tw

        def wi(m, n):
            return m if is_a else n
        if arr.ndim == 3:
            return pl.BlockSpec((None, tk, tw), lambda j, m, n, k: (j, k, wi(m, n)))
        if arr.shape[1] == width:
            return pl.BlockSpec((tk, tw), lambda j, m, n, k: (k, wi(m, n)))
        return pl.BlockSpec((tk, tw), lambda j, m, n, k: (k, (j + off) * nb + wi(m, n)))

    a_spec = spec(a, Ka, tka, True, a_off)
    b_spec = spec(b, Nb, tnb, False, b_off)
    out_spec = pl.BlockSpec((None, tka, tnb), lambda j, m, n, k: (j, m, n))
    return _mm(name, [(a, a_spec, b, b_spec)], TN, (J, Ka // tka, Nb // tnb, M // tk), M // tk,
               SDS((J, Ka, Nb), out_dtype), out_spec, (tka, tnb), scale, riders=riders)


def _mm_bdiag(name, pairs, dims, Kb, Nb, out_dtype, res=None, tm_target=704, riders=()):
    a0, w0 = pairs[0][0], pairs[0][1]
    M, J = a0.shape[0], w0.shape[0]
    tm = _pick(M, tm_target)
    terms = []
    for a, w, off in pairs:
        a_spec = pl.BlockSpec((tm, Kb), lambda j, i, k, off=off: (i, j + off))
        w_spec = pl.BlockSpec((None,) + w.shape[1:], lambda j, i, k: (j, 0, 0))
        terms.append((a, a_spec, w, w_spec))
    out_spec = pl.BlockSpec((tm, Nb), lambda j, i, k: (i, j))
    r = None if res is None else (res[0], pl.BlockSpec((tm, Nb), lambda j, i, k, off=res[1]: (i, j + off)))
    return _mm(name, terms, dims, (J, M // tm, 1), 1, SDS((M, J * Nb), out_dtype), out_spec, (tm, Nb), 1.0, r, riders)


def _rmsnorm(x, g):
    x = x.astype(F32)
    return x * lax.rsqrt(jnp.mean(x * x, axis=-1, keepdims=True) + EPS) * g


def _swiglu_act(gate, up):
    gate, up = gate.astype(F32), up.astype(F32)
    return gate * jax.nn.sigmoid(gate) * up


def _neg_expm1(x):
    series = -x * (1.0 + x * (1.0 / 2 + x * (1.0 / 6 + x * (1.0 / 24 + x * (1.0 / 120 + x * (1.0 / 720))))))
    return jnp.where(x > -0.25, series, 1.0 - jnp.exp(x))


def _rg_gates(xc, wa, ba, wx, bx, lam):
    heads, hd = wa.shape[0], wa.shape[1]
    xb = xc.astype(BF16)
    rs, ig = [], []
    for h in range(heads):
        xh = xb[:, h * hd:(h + 1) * hd]
        rs.append(jnp.dot(xh, wa[h].astype(BF16), preferred_element_type=F32))
        ig.append(jnp.dot(xh, wx[h].astype(BF16), preferred_element_type=F32))
    r = jax.nn.sigmoid(jnp.concatenate(rs, axis=1) + ba)
    ii = jax.nn.sigmoid(jnp.concatenate(ig, axis=1) + bx)
    log_a = -RG_C * r * jax.nn.softplus(-lam)
    a = jnp.exp(log_a)
    mult = jnp.sqrt(_neg_expm1(2.0 * log_a))
    return a, mult * ii * xc


def _rg_out(h, g, gain):
    return _rmsnorm(h * jax.nn.gelu(g), gain)


def _s5_pre(y, u, d):
    return jax.nn.gelu(y + d * u)


def _s5_out(z, zz, glu_b, gain):
    return _rmsnorm(z * jax.nn.sigmoid(zz + glu_b), gain)


def _adamw(w, g, m, v):
    m = ADAM_B1 * m + (1.0 - ADAM_B1) * g
    v = ADAM_B2 * v + (1.0 - ADAM_B2) * jnp.square(g)
    m_hat = m / (1.0 - ADAM_B1 ** ADAM_STEP)
    v_hat = v / (1.0 - ADAM_B2 ** ADAM_STEP)
    delta = -ADAM_LR * (m_hat / (jnp.sqrt(v_hat) + ADAM_EPS) + ADAM_WD * w)
    return delta, m, v


def _conv_fwd(proj, conv_w, conv_b, R, tm):
    T = proj.shape[0]
    tpb = tm // SUBLANES

    def body(cur_ref, prev_ref, w_ref, b_ref, o_ref):
        i = pl.program_id(0)
        cur = cur_ref[...]
        prev = jnp.where(i > 0, prev_ref[...], 0.0)
        ext = jnp.concatenate([prev, cur], axis=0)
        acc = b_ref[...] + w_ref[CONV_WIDTH - 1:CONV_WIDTH, :] * cur
        for s in range(1, CONV_WIDTH):
            acc = acc + w_ref[CONV_WIDTH - 1 - s:CONV_WIDTH - s, :] * pltpu.roll(ext, s, 0)[SUBLANES:, :]
        o_ref[...] = acc

    return pl.pallas_call(
        body, grid=(T // tm,),
        in_specs=[pl.BlockSpec((tm, R), lambda i: (i, 0)),
                  pl.BlockSpec((SUBLANES, R), lambda i: (jnp.maximum(i * tpb - 1, 0), 0)),
                  pl.BlockSpec((CONV_WIDTH, R), lambda i: (0, 0)), pl.BlockSpec((1, R), lambda i: (0, 0))],
        out_specs=pl.BlockSpec((tm, R), lambda i: (i, 0)), out_shape=SDS((T, R), F32),
        name="rg_conv_fwd", compiler_params=_params(("arbitrary",)))(proj, proj, conv_w, conv_b)


def _conv_bwd(dxc, proj, conv_w, R, tm):
    T = dxc.shape[0]
    tpb = tm // SUBLANES
    nt = T // tm
    n_ext = tm + SUBLANES

    def body(d_ref, dnext_ref, u_ref, uprev_ref, w_ref, du_ref, dw_ref, db_ref):
        i = pl.program_id(0)
        d = d_ref[...]
        dnext = jnp.where(i < nt - 1, dnext_ref[...], 0.0)
        dext = jnp.concatenate([d, dnext], axis=0)
        u = u_ref[...]
        uprev = jnp.where(i > 0, uprev_ref[...], 0.0)
        uext = jnp.concatenate([uprev, u], axis=0)
        du = w_ref[CONV_WIDTH - 1:CONV_WIDTH, :] * d
        dws = [jnp.sum(d * u, axis=0, keepdims=True)]
        for s in range(1, CONV_WIDTH):
            du = du + w_ref[CONV_WIDTH - 1 - s:CONV_WIDTH - s, :] * pltpu.roll(dext, n_ext - s, 0)[:tm, :]
            dws.append(jnp.sum(d * pltpu.roll(uext, s, 0)[SUBLANES:, :], axis=0, keepdims=True))
        du_ref[...] = du

        @pl.when(i == 0)
        def _():
            dw_ref[...] = jnp.zeros_like(dw_ref)
            db_ref[...] = jnp.zeros_like(db_ref)

        dw_ref[...] += jnp.concatenate(dws[::-1], axis=0)
        db_ref[...] += jnp.sum(d, axis=0, keepdims=True)

    row = pl.BlockSpec((tm, R), lambda i: (i, 0))
    return pl.pallas_call(
        body, grid=(nt,),
        in_specs=[row, pl.BlockSpec((SUBLANES, R), lambda i: (jnp.minimum((i + 1) * tpb, T // SUBLANES - 1), 0)),
                  row, pl.BlockSpec((SUBLANES, R), lambda i: (jnp.maximum(i * tpb - 1, 0), 0)),
                  pl.BlockSpec((CONV_WIDTH, R), lambda i: (0, 0))],
        out_specs=[row, pl.BlockSpec((CONV_WIDTH, R), lambda i: (0, 0)), pl.BlockSpec((1, R), lambda i: (0, 0))],
        out_shape=[SDS((T, R), F32), SDS((CONV_WIDTH, R), F32), SDS((1, R), F32)],
        name="rg_conv_bwd", compiler_params=_params(("arbitrary",)))(dxc, dxc, proj, proj, conv_w)


def _scan_tiles(tb):
    return tb // SUBLANES


def _rg_scan_fwd(a, b, tb, cb):
    T, W = a.shape

    def body(a_ref, b_ref, h_ref, p_ref, carry):
        @pl.when(pl.program_id(1) == 0)
        def _():
            carry[...] = jnp.zeros_like(carry)

        def tile(t, h):
            ds = pl.ds(pl.multiple_of(t * SUBLANES, SUBLANES), SUBLANES)
            a8, b8 = a_ref[ds, :], b_ref[ds, :]
            hs, ps = [], []
            for j in range(SUBLANES):
                ps.append(h)
                h = a8[j:j + 1, :] * h + b8[j:j + 1, :]
                hs.append(h)
            h_ref[ds, :] = jnp.concatenate(hs, axis=0)
            p_ref[ds, :] = jnp.concatenate(ps, axis=0)
            return h

        carry[0:1, :] = lax.fori_loop(0, _scan_tiles(tb), tile, carry[0:1, :])

    blk = pl.BlockSpec((tb, cb), lambda c, i: (i, c))
    return pl.pallas_call(
        body, grid=(W // cb, T // tb), in_specs=[blk, blk], out_specs=[blk, blk],
        out_shape=[SDS((T, W), F32)] * 2, scratch_shapes=[pltpu.VMEM((SUBLANES, cb), F32)],
        name="rg_scan_fwd", compiler_params=_params(("arbitrary", "arbitrary")))(a, b)


def _rg_scan_bwd(dh, a, hprev, tb, cb):
    T, W = a.shape
    nt = T // tb

    def body(g_ref, a_ref, p_ref, db_ref, da_ref, carry):
        @pl.when(pl.program_id(1) == 0)
        def _():
            carry[...] = jnp.zeros_like(carry)

        def tile(tt, c):
            t = _scan_tiles(tb) - 1 - tt
            ds = pl.ds(pl.multiple_of(t * SUBLANES, SUBLANES), SUBLANES)
            g8, a8 = g_ref[ds, :], a_ref[ds, :]
            adjs = [None] * SUBLANES
            for j in range(SUBLANES - 1, -1, -1):
                adj = g8[j:j + 1, :] + c
                adjs[j] = adj
                c = a8[j:j + 1, :] * adj
            adj8 = jnp.concatenate(adjs, axis=0)
            db_ref[ds, :] = adj8
            da_ref[ds, :] = adj8 * p_ref[ds, :]
            return c

        carry[0:1, :] = lax.fori_loop(0, _scan_tiles(tb), tile, carry[0:1, :])

    blk = pl.BlockSpec((tb, cb), lambda c, i: (nt - 1 - i, c))
    return pl.pallas_call(
        body, grid=(W // cb, nt), in_specs=[blk, blk, blk], out_specs=[blk, blk],
        out_shape=[SDS((T, W), F32)] * 2, scratch_shapes=[pltpu.VMEM((SUBLANES, cb), F32)],
        name="rg_scan_bwd", compiler_params=_params(("arbitrary", "arbitrary")))(dh, a, hprev)


SCAN_LEVELS = (1, 2, 4)


def _scan_tables(lr, li, reverse):
    def cmul(a, b):
        return a[0] * b[0] - a[1] * b[1], a[0] * b[1] + a[1] * b[0]

    powers = [(lr, li)]
    for _ in range(SUBLANES - 1):
        powers.append(cmul(powers[-1], (lr, li)))
    row = jnp.arange(SUBLANES)[:, None]
    rows = []
    for k in SCAN_LEVELS:
        has = (row <= SUBLANES - 1 - k) if reverse else (row >= k)
        rows += [jnp.where(has, powers[k - 1][0], 0.0), jnp.where(has, powers[k - 1][1], 0.0)]
    order = list(range(SUBLANES - 1, -1, -1)) if reverse else list(range(SUBLANES))
    rows += [jnp.concatenate([powers[j][0] for j in order], axis=0), jnp.concatenate([powers[j][1] for j in order], axis=0)]
    return jnp.concatenate(rows, axis=0)


def _scan_tile(vr, vi, carry, tab_ref, reverse):
    n_rows = SUBLANES
    for lvl, k in enumerate(SCAN_LEVELS):
        mr = tab_ref[2 * lvl * n_rows:(2 * lvl + 1) * n_rows, :]
        mi = tab_ref[(2 * lvl + 1) * n_rows:(2 * lvl + 2) * n_rows, :]
        shift = n_rows - k if reverse else k
        sr, si = pltpu.roll(vr, shift, 0), pltpu.roll(vi, shift, 0)
        vr, vi = vr + mr * sr - mi * si, vi + mr * si + mi * sr
    base = 2 * len(SCAN_LEVELS) * n_rows
    pr, pi = tab_ref[base:base + n_rows, :], tab_ref[base + n_rows:base + 2 * n_rows, :]
    cr, ci = carry
    return vr + pr * cr - pi * ci, vi + pr * ci + pi * cr


def _s5_scan_fwd(bur, bui, tab, tb, cb, riders=()):
    T, W = bur.shape
    grid = (W // cb, T // tb)
    r_in, r_out, _ = _rider_counts(riders)

    def body(*refs):
        br_ref, bi_ref, tab_ref = refs[:3]
        xr_ref, xi_ref, pr_ref, pi_ref = refs[3 + r_in:7 + r_in]
        cr, ci = refs[7 + r_in + r_out:9 + r_in + r_out]
        step = _linear_step(grid)
        hooks = _rider_hooks(riders, refs[3:3 + r_in], refs[7 + r_in:7 + r_in + r_out], refs[9 + r_in + r_out:])
        _ride_begin(step, hooks)

        @pl.when(pl.program_id(1) == 0)
        def _():
            cr[...] = jnp.zeros_like(cr)
            ci[...] = jnp.zeros_like(ci)

        first_row = lax.broadcasted_iota(jnp.int32, (SUBLANES, cb), 0) == 0

        def tile(t, carry):
            ds = pl.ds(pl.multiple_of(t * SUBLANES, SUBLANES), SUBLANES)
            xr8, xi8 = _scan_tile(br_ref[ds, :], bi_ref[ds, :], carry, tab_ref, reverse=False)
            xr_ref[ds, :] = xr8
            xi_ref[ds, :] = xi8
            pr_ref[ds, :] = jnp.where(first_row, carry[0], pltpu.roll(xr8, 1, 0))
            pi_ref[ds, :] = jnp.where(first_row, carry[1], pltpu.roll(xi8, 1, 0))
            return xr8[SUBLANES - 1:SUBLANES, :], xi8[SUBLANES - 1:SUBLANES, :]

        xr, xi = lax.fori_loop(0, _scan_tiles(tb), tile, (cr[0:1, :], ci[0:1, :]))
        cr[0:1, :] = xr
        ci[0:1, :] = xi
        _ride_end(step, math.prod(grid), hooks)

    blk = pl.BlockSpec((tb, cb), lambda c, i: (i, c))
    tabs = pl.BlockSpec((tab.shape[0], cb), lambda c, i: (0, c))
    outs = pl.pallas_call(
        body, grid=grid, in_specs=[blk, blk, tabs] + [_ANY] * r_in, out_specs=[blk] * 4 + [_ANY] * r_out,
        out_shape=[SDS((T, W), F32)] * 4 + [o for r in riders for o in r.outs],
        scratch_shapes=[pltpu.VMEM((SUBLANES, cb), F32), pltpu.VMEM((SUBLANES, cb), F32)] + [s for r in riders for s in r.sems],
        name="s5_scan_fwd", compiler_params=_params(("arbitrary", "arbitrary")))(
            bur, bui, tab, *[a for r in riders for a in r.ins])
    return tuple(outs[:4]) + (list(outs[4:]),)


def _s5_scan_bwd(gr, gi, pr, pi, tab, tb, cb, riders=()):
    T, W = gr.shape
    nt = T // tb
    grid = (W // cb, nt)
    r_in, r_out, _ = _rider_counts(riders)

    def body(*refs):
        gr_ref, gi_ref, pr_ref, pi_ref, tab_ref = refs[:5]
        ar_ref, ai_ref, dlr_ref, dli_ref = refs[5 + r_in:9 + r_in]
        cr, ci = refs[9 + r_in + r_out:11 + r_in + r_out]
        step = _linear_step(grid)
        hooks = _rider_hooks(riders, refs[5:5 + r_in], refs[9 + r_in:9 + r_in + r_out], refs[11 + r_in + r_out:])
        _ride_begin(step, hooks)

        @pl.when(pl.program_id(1) == 0)
        def _():
            cr[...] = jnp.zeros_like(cr)
            ci[...] = jnp.zeros_like(ci)
            dlr_ref[...] = jnp.zeros_like(dlr_ref)
            dli_ref[...] = jnp.zeros_like(dli_ref)

        def tile(tt, carry):
            t = _scan_tiles(tb) - 1 - tt
            ds = pl.ds(pl.multiple_of(t * SUBLANES, SUBLANES), SUBLANES)
            ar8, ai8 = _scan_tile(gr_ref[ds, :], gi_ref[ds, :], carry, tab_ref, reverse=True)
            ar_ref[ds, :] = ar8
            ai_ref[ds, :] = ai8
            pr8, pi8 = pr_ref[ds, :], pi_ref[ds, :]
            dlr_ref[...] += ar8 * pr8 + ai8 * pi8
            dli_ref[...] += ai8 * pr8 - ar8 * pi8
            return ar8[0:1, :], ai8[0:1, :]

        ar, ai = lax.fori_loop(0, _scan_tiles(tb), tile, (cr[0:1, :], ci[0:1, :]))
        cr[0:1, :] = ar
        ci[0:1, :] = ai
        _ride_end(step, math.prod(grid), hooks)

    blk = pl.BlockSpec((tb, cb), lambda c, i: (nt - 1 - i, c))
    tabs = pl.BlockSpec((tab.shape[0], cb), lambda c, i: (0, c))
    acc = pl.BlockSpec((SUBLANES, cb), lambda c, i: (0, c))
    outs = pl.pallas_call(
        body, grid=grid, in_specs=[blk] * 4 + [tabs] + [_ANY] * r_in, out_specs=[blk, blk, acc, acc] + [_ANY] * r_out,
        out_shape=[SDS((T, W), F32)] * 2 + [SDS((SUBLANES, W), F32)] * 2 + [o for r in riders for o in r.outs],
        scratch_shapes=[pltpu.VMEM((SUBLANES, cb), F32), pltpu.VMEM((SUBLANES, cb), F32)] + [s for r in riders for s in r.sems],
        name="s5_scan_bwd", compiler_params=_params(("arbitrary", "arbitrary")))(
            gr, gi, pr, pi, tab, *[a for r in riders for a in r.ins])
    return tuple(outs[:4]) + (list(outs[4:]),)


_ANY = pl.BlockSpec(memory_space=pl.ANY)


def _ag_steps(x_ref, out_ref, send_sems, recv_sems, local_sem):
    x, y, c = lax.axis_index("x"), lax.axis_index("y"), lax.axis_index("c")
    me, sibling = (x, y, c), (x, y, 1 - c)
    nbr_a = ((x + c) % 2, (y + 1 - c) % 2)
    nbr_b = ((x + 1 - c) % 2, (y + c) % 2)
    diag = (1 - x, 1 - y)

    def slot(px, py, pc):
        return out_ref.at[4 * px + 2 * py + pc]

    def copy(k, block, to, src=None):
        return pltpu.make_async_remote_copy(
            src_ref=slot(*block) if src is None else src, dst_ref=slot(*block),
            send_sem=send_sems.at[k], recv_sem=recv_sems.at[k], device_id=to, device_id_type=MESH_ID)

    mine = pltpu.make_async_copy(x_ref, slot(*me), local_sem)
    first = [copy(0, me, sibling, src=x_ref), copy(1, me, (*nbr_a, c), src=x_ref), copy(2, me, (*nbr_b, c), src=x_ref)]
    relay = copy(3, (*nbr_a, c), (*nbr_b, c))
    passed = [copy(4, (*nbr_a, c), sibling), copy(5, (*nbr_b, c), sibling), copy(6, (*diag, c), sibling)]

    def start():
        mine.start()
        for cp in first:
            cp.start()

    def mid():
        copy(1, (*nbr_a, c), me).wait_recv()
        relay.start()
        passed[0].start()
        copy(2, (*nbr_b, c), me).wait_recv()
        passed[1].start()

    def finish():
        copy(3, (*diag, c), me).wait_recv()
        passed[2].start()
        copy(0, sibling, me).wait_recv()
        copy(4, (*nbr_b, 1 - c), me).wait_recv()
        copy(5, (*nbr_a, 1 - c), me).wait_recv()
        copy(6, (*diag, 1 - c), me).wait_recv()
        for cp in first + [relay] + passed:
            cp.wait_send()
        mine.wait()

    return start, mid, finish


def _rs1_steps(g_ref, r_ref, send_sems, recv_sems):
    x, y, c = lax.axis_index("x"), lax.axis_index("y"), lax.axis_index("c")
    copies = [pltpu.make_async_remote_copy(
        src_ref=g_ref.at[2 * k + (1 - c)], dst_ref=r_ref.at[k], send_sem=send_sems.at[k], recv_sem=recv_sems.at[k],
        device_id=(x, y, 1 - c), device_id_type=MESH_ID) for k in range(N_CHIP)]

    def start():
        for cp in copies:
            cp.start()

    def finish():
        for cp in copies:
            cp.wait_recv()
        for cp in copies:
            cp.wait_send()

    return start, None, finish


def _rs2_steps(p_ref, r_ref, send_sems, recv_sems, local_sem):
    x, y, c = lax.axis_index("x"), lax.axis_index("y"), lax.axis_index("c")
    myk = 2 * x + y
    chips = [(1 - x, y), (x, 1 - y), (1 - x, 1 - y)]
    mine = pltpu.make_async_copy(p_ref.at[myk], r_ref.at[myk], local_sem)
    sends = [pltpu.make_async_remote_copy(
        src_ref=p_ref.at[2 * px + py], dst_ref=r_ref.at[myk], send_sem=send_sems.at[j], recv_sem=recv_sems.at[j],
        device_id=(px, py, c), device_id_type=MESH_ID) for j, (px, py) in enumerate(chips)]

    def start():
        mine.start()
        for cp in sends:
            cp.start()

    def finish():
        for j, (px, py) in enumerate(chips):
            pltpu.make_async_remote_copy(
                src_ref=p_ref.at[myk], dst_ref=r_ref.at[2 * px + py], send_sem=send_sems.at[j], recv_sem=recv_sems.at[j],
                device_id=(px, py, c), device_id_type=MESH_ID).wait_recv()
        for cp in sends:
            cp.wait_send()
        mine.wait()

    return start, None, finish


def _dma_sems(*counts):
    return [pltpu.SemaphoreType.DMA((n,)) if n else pltpu.SemaphoreType.DMA for n in counts]


def _ag_rider(shard):
    return _Rider([shard], [SDS((N_DEV,) + shard.shape, shard.dtype)], _dma_sems(7, 7, 0), _ag_steps)


def _rs1_rider(g):
    return _Rider([g], [SDS((N_CHIP,) + g.shape[1:], g.dtype)], _dma_sems(N_CHIP, N_CHIP), _rs1_steps)


def _rs2_rider(p):
    return _Rider([p], [SDS(p.shape, p.dtype)], _dma_sems(3, 3, 0), _rs2_steps)


def _comm_call(name, riders):
    r_in, r_out, _ = _rider_counts(riders)

    def body(*refs):
        hooks = _rider_hooks(riders, refs[:r_in], refs[r_in:r_in + r_out], refs[r_in + r_out:])
        for start, _, _ in hooks:
            start()
        for _, mid, _ in hooks:
            if mid is not None:
                mid()
        for _, _, finish in hooks:
            finish()

    ops = [a for r in riders for a in r.ins]
    outs = pl.pallas_call(
        body, out_shape=[o for r in riders for o in r.outs], in_specs=[_ANY] * r_in, out_specs=[_ANY] * r_out,
        scratch_shapes=[s for r in riders for s in r.sems], name=name)(*ops)
    return list(outs)


def _all_gather(name, shard):
    return _comm_call(name, [_ag_rider(shard)])[0]


def _pair_add(name, g, r1, c_idx):
    _, R, C = g.shape
    tr = _row_tile(R, C, 2, 6 * 1024 * 1024)

    def body(c_ref, g_ref, r_ref, o_ref):
        o_ref[...] = (g_ref[...].astype(F32) + r_ref[...].astype(F32)).astype(o_ref.dtype)

    grid_spec = pltpu.PrefetchScalarGridSpec(
        num_scalar_prefetch=1, grid=(N_CHIP, R // tr),
        in_specs=[pl.BlockSpec((None, tr, C), lambda k, i, c_ref: (2 * k + c_ref[0], i, 0)),
                  pl.BlockSpec((None, tr, C), lambda k, i, c_ref: (k, i, 0))],
        out_specs=pl.BlockSpec((None, tr, C), lambda k, i, c_ref: (k, i, 0)))
    return pl.pallas_call(body, grid_spec=grid_spec, out_shape=SDS((N_CHIP, R, C), g.dtype), name=name,
                          compiler_params=_params(("arbitrary", "arbitrary")))(c_idx, g, r1)


class _GradReducer:
    def __init__(self, c_idx):
        self.c_idx, self.wait_d2d, self.wait_ici, self.done = c_idx, [], [], {}

    def push(self, name, g):
        self.wait_d2d.append((name, g))

    def take(self, ici=1, d2d=1):
        jobs = [("ici",) + self.wait_ici.pop(0) for _ in range(min(ici, len(self.wait_ici)))]
        jobs += [("d2d",) + self.wait_d2d.pop(0) for _ in range(min(d2d, len(self.wait_d2d)))]
        riders = [_rs2_rider(a) if kind == "ici" else _rs1_rider(a) for kind, _, a in jobs]

        def absorb(outs):
            for (kind, name, a), out in zip(jobs, outs, strict=True):
                if kind == "ici":
                    self.done[name] = out
                else:
                    self.wait_ici.append((name, _pair_add(f"add_{name}", a, out, self.c_idx)))

        return riders, absorb

    def flush(self):
        n_calls = 0
        while self.wait_d2d or self.wait_ici:
            riders, absorb = self.take(ici=len(self.wait_ici), d2d=len(self.wait_d2d))
            absorb(_comm_call(f"rs_tail_{n_calls}", riders))
            n_calls += 1
        return self.done


PACK_ROWS = 64


def _pack(arrs):
    flat = jnp.concatenate([a.reshape(-1).astype(F32) for a in arrs])
    n = flat.shape[0]
    unit = PACK_ROWS * LANES
    padded = -(-n // unit) * unit
    return jnp.pad(flat, (0, padded - n)).reshape(padded // LANES, LANES)


def _unpack(buf, shapes):
    flat = buf.reshape(-1)
    outs, off = [], 0
    for shp in shapes:
        n = math.prod(shp)
        outs.append(flat[off:off + n].reshape(shp))
        off += n
    return outs


def _s5_discretise(lre, lim, log_dt, bre, bim):
    dt = jnp.exp(log_dt)[:, None]
    e_m1 = jnp.expm1(lre * dt)
    th = lim * dt
    lr = (e_m1 + 1.0) * jnp.cos(th)
    li = (e_m1 + 1.0) * jnp.sin(th)
    lr_m1 = e_m1 * jnp.cos(th) - 2.0 * jnp.square(jnp.sin(0.5 * th))
    den = lre * lre + lim * lim
    cr = (lr_m1 * lre + li * lim) / den
    ci = (li * lre - lr_m1 * lim) / den
    bbr = cr[..., None] * bre - ci[..., None] * bim
    bbi = cr[..., None] * bim + ci[..., None] * bre
    return lr, li, bbr, bbi


def _expand_diag(m, rows_first):
    G, A, B = m.shape
    q = G // GROUPS_PER_BLOCK
    eye = jnp.eye(GROUPS_PER_BLOCK, dtype=m.dtype)
    m5 = m.reshape(q, GROUPS_PER_BLOCK, A, 1, B) * eye[None, :, None, :, None]
    return m5.reshape(q, GROUPS_PER_BLOCK * A, GROUPS_PER_BLOCK * B)


def _extract_diag(m, A, B):
    q = m.shape[0]
    m5 = m.reshape(q, GROUPS_PER_BLOCK, A, GROUPS_PER_BLOCK, B)
    d = jnp.stack([m5[:, g, :, g, :] for g in range(GROUPS_PER_BLOCK)], axis=1)
    return d.reshape(q * GROUPS_PER_BLOCK, A, B)


def _ffn_fwd(tag, h, gain, wg, wu_shard, wd_shard, tail_riders, Tp):
    D = h.shape[1]
    J, Fb, _ = wg.shape
    tmn = _row_tile(Tp, D)
    (n,) = _rowcall(f"{tag}_norm", lambda i, x, g: (_rmsnorm(x, g),), Tp, tmn, [(h, 0, 0, D)], [gain], [(D, BF16)])
    gate, (wu,) = _mm_bcast(f"{tag}_gate", n, wg, NT, False, BF16, riders=[_ag_rider(wu_shard)])
    up, (wd,) = _mm_bcast(f"{tag}_up", n, wu, NT, False, BF16, riders=[_ag_rider(wd_shard)])
    rows = J * Tp
    tma = _row_tile(rows, Fb, 2, 1024 * 1024)
    g2, u2 = gate.reshape(rows, Fb), up.reshape(rows, Fb)
    (act,) = _rowcall(f"{tag}_act", lambda i, g, u: (_swiglu_act(g, u),), rows, tma,
                      [(g2, 0, 0, Fb), (u2, 0, 0, Fb)], [], [(Fb, BF16)])
    act = act.reshape(J, Tp, Fb)
    h_out = _mm_red(f"{tag}_down", [(act, wd)], NN, F32, scale=0.5, res=h, jb=2, riders=tail_riders)
    tail = []
    if tail_riders:
        h_out, tail = h_out
    return h_out, (n, gate, up, act), wu, wd, tail


def _ffn_bwd(tag, dh, dh16, h, gain, wg, wu, wd, saved, Tp, red):
    n, gate, up, act = saved
    D = h.shape[1]
    J, Fb, _ = wg.shape

    def carried(fn, *args, ici=1, d2d=1, **kw):
        riders, absorb = red.take(ici, d2d)
        if not riders:
            return fn(*args, **kw)
        out, routs = fn(*args, riders=riders, **kw)
        absorb(routs)
        return out

    dact = carried(_mm_bcast, f"{tag}_dact", dh16, wd, NT, False, BF16)
    red.push(f"{tag}_w_down", carried(_mm_tn, f"{tag}_dwd", act, dh16, J, Fb, D, BF16, scale=0.5))
    rows = J * Tp
    tma = _row_tile(rows, Fb, 2, 1024 * 1024)

    def act_bwd(i, g, u, d):
        _, vjp = jax.vjp(_swiglu_act, g, u)
        return vjp(0.5 * d.astype(F32))

    dgate, dup = _rowcall(f"{tag}_act_bwd", act_bwd, rows, tma,
                          [(gate.reshape(rows, Fb), 0, 0, Fb), (up.reshape(rows, Fb), 0, 0, Fb),
                           (dact.reshape(rows, Fb), 0, 0, Fb)], [], [(Fb, BF16), (Fb, BF16)])
    dgate, dup = dgate.reshape(J, Tp, Fb), dup.reshape(J, Tp, Fb)
    red.push(f"{tag}_w_gate", carried(_mm_tn, f"{tag}_dwg", dgate, n, J, Fb, D, BF16))
    red.push(f"{tag}_w_up", carried(_mm_tn, f"{tag}_dwu", dup, n, J, Fb, D, BF16))
    half = D // 2
    dn = [carried(_mm_red, f"{tag}_dn{part}", [(dgate, wg), (dup, wu)], NN, F32, n_off=part * half, n_len=half)
          for part in range(2)]
    return _norm_bwd(f"{tag}_norm_bwd", h, gain, dn, dh, Tp)


def _norm_bwd(name, h, gain, dn_parts, dres, Tp):
    D = h.shape[1]
    n_parts = len(dn_parts)

    def fn(i, x, r, *rest):
        d = jnp.concatenate([p.astype(F32) for p in rest[:n_parts]], axis=1) if n_parts > 1 else rest[0].astype(F32)
        _, vjp = jax.vjp(_rmsnorm, x, rest[n_parts])
        dx, dg = vjp(d)
        return r + dx, r + dx, dg

    rows = [(h, 0, 0, D), (dres, 0, 0, D)] + [(p, 0, 0, p.shape[1]) for p in dn_parts]
    return _rowcall(name, fn, Tp, _row_tile(Tp, D), rows, [gain], [(D, F32), (D, BF16)], [(1, D)])


def kernel(x, meta_tokens, ffn1_norm, ffn1_w_gate, ffn1_w_up, ffn1_w_down, mix_norm, w_in, rg_conv_w, rg_conv_b, rg_w_a, rg_b_a, rg_w_x, rg_b_x, rg_lambda, s5_lambda_re, s5_lambda_im, s5_log_dt, s5_b_re, s5_b_im, s5_c_re, s5_c_im, s5_d, s5_glu_w, s5_glu_b, rg_out_norm, s5_out_norm, w_out, ffn2_norm, ffn2_w_gate, ffn2_w_up, ffn2_w_down, final_norm, loss_target, m_meta_tokens, m_ffn1_norm, m_ffn1_w_gate, m_ffn1_w_up, m_ffn1_w_down, m_mix_norm, m_w_in, m_rg_conv_w, m_rg_conv_b, m_rg_w_a, m_rg_b_a, m_rg_w_x, m_rg_b_x, m_rg_lambda, m_s5_lambda_re, m_s5_lambda_im, m_s5_log_dt, m_s5_b_re, m_s5_b_im, m_s5_c_re, m_s5_c_im, m_s5_d, m_s5_glu_w, m_s5_glu_b, m_rg_out_norm, m_s5_out_norm, m_w_out, m_ffn2_norm, m_ffn2_w_gate, m_ffn2_w_up, m_ffn2_w_down, m_final_norm, v_meta_tokens, v_ffn1_norm, v_ffn1_w_gate, v_ffn1_w_up, v_ffn1_w_down, v_mix_norm, v_w_in, v_rg_conv_w, v_rg_conv_b, v_rg_w_a, v_rg_b_a, v_rg_w_x, v_rg_b_x, v_rg_lambda, v_s5_lambda_re, v_s5_lambda_im, v_s5_log_dt, v_s5_b_re, v_s5_b_im, v_s5_c_re, v_s5_c_im, v_s5_d, v_s5_glu_w, v_s5_glu_b, v_rg_out_norm, v_s5_out_norm, v_w_out, v_ffn2_norm, v_ffn2_w_gate, v_ffn2_w_up, v_ffn2_w_down, v_final_norm):
    weights = dict(
        meta_tokens=meta_tokens, ffn1_norm=ffn1_norm, ffn1_w_gate=ffn1_w_gate, ffn1_w_up=ffn1_w_up, ffn1_w_down=ffn1_w_down,
        mix_norm=mix_norm, w_in=w_in, rg_conv_w=rg_conv_w, rg_conv_b=rg_conv_b, rg_w_a=rg_w_a, rg_b_a=rg_b_a, rg_w_x=rg_w_x,
        rg_b_x=rg_b_x, rg_lambda=rg_lambda, s5_lambda_re=s5_lambda_re, s5_lambda_im=s5_lambda_im, s5_log_dt=s5_log_dt,
        s5_b_re=s5_b_re, s5_b_im=s5_b_im, s5_c_re=s5_c_re, s5_c_im=s5_c_im, s5_d=s5_d, s5_glu_w=s5_glu_w, s5_glu_b=s5_glu_b,
        rg_out_norm=rg_out_norm, s5_out_norm=s5_out_norm, w_out=w_out, ffn2_norm=ffn2_norm, ffn2_w_gate=ffn2_w_gate,
        ffn2_w_up=ffn2_w_up, ffn2_w_down=ffn2_w_down, final_norm=final_norm)
    moments_m = dict(
        meta_tokens=m_meta_tokens, ffn1_norm=m_ffn1_norm, ffn1_w_gate=m_ffn1_w_gate, ffn1_w_up=m_ffn1_w_up,
        ffn1_w_down=m_ffn1_w_down, mix_norm=m_mix_norm, w_in=m_w_in, rg_conv_w=m_rg_conv_w, rg_conv_b=m_rg_conv_b,
        rg_w_a=m_rg_w_a, rg_b_a=m_rg_b_a, rg_w_x=m_rg_w_x, rg_b_x=m_rg_b_x, rg_lambda=m_rg_lambda,
        s5_lambda_re=m_s5_lambda_re, s5_lambda_im=m_s5_lambda_im, s5_log_dt=m_s5_log_dt, s5_b_re=m_s5_b_re,
        s5_b_im=m_s5_b_im, s5_c_re=m_s5_c_re, s5_c_im=m_s5_c_im, s5_d=m_s5_d, s5_glu_w=m_s5_glu_w, s5_glu_b=m_s5_glu_b,
        rg_out_norm=m_rg_out_norm, s5_out_norm=m_s5_out_norm, w_out=m_w_out, ffn2_norm=m_ffn2_norm,
        ffn2_w_gate=m_ffn2_w_gate, ffn2_w_up=m_ffn2_w_up, ffn2_w_down=m_ffn2_w_down, final_norm=m_final_norm)
    moments_v = dict(
        meta_tokens=v_meta_tokens, ffn1_norm=v_ffn1_norm, ffn1_w_gate=v_ffn1_w_gate, ffn1_w_up=v_ffn1_w_up,
        ffn1_w_down=v_ffn1_w_down, mix_norm=v_mix_norm, w_in=v_w_in, rg_conv_w=v_rg_conv_w, rg_conv_b=v_rg_conv_b,
        rg_w_a=v_rg_w_a, rg_b_a=v_rg_b_a, rg_w_x=v_rg_w_x, rg_b_x=v_rg_b_x, rg_lambda=v_rg_lambda,
        s5_lambda_re=v_s5_lambda_re, s5_lambda_im=v_s5_lambda_im, s5_log_dt=v_s5_log_dt, s5_b_re=v_s5_b_re,
        s5_b_im=v_s5_b_im, s5_c_re=v_s5_c_re, s5_c_im=v_s5_c_im, s5_d=v_s5_d, s5_glu_w=v_s5_glu_w, s5_glu_b=v_s5_glu_b,
        rg_out_norm=v_rg_out_norm, s5_out_norm=v_s5_out_norm, w_out=v_w_out, ffn2_norm=v_ffn2_norm,
        ffn2_w_gate=v_ffn2_w_gate, ffn2_w_up=v_ffn2_w_up, ffn2_w_down=v_ffn2_w_down, final_norm=v_final_norm)
    order = list(weights)

    seq, D = x.shape[1], x.shape[2]
    R = rg_conv_b.shape[1]
    S = s5_d.shape[1]
    G, N, C = s5_b_re.shape[1:]
    heads, hd = rg_w_a.shape[1], rg_w_a.shape[3]
    Q = G // GROUPS_PER_BLOCK
    W = G * N
    NB = GROUPS_PER_BLOCK * N
    T = N_META + seq
    Tp = -(-T // LANES) * LANES
    me = 4 * lax.axis_index("x") + 2 * lax.axis_index("y") + lax.axis_index("c")
    c_idx = lax.axis_index("c").astype(jnp.int32).reshape(1)

    big = ["ffn1_w_gate", "ffn1_w_up", "ffn1_w_down", "w_in", "s5_glu_w", "w_out", "ffn2_w_gate", "ffn2_w_up", "ffn2_w_down"]
    transposed = ["ffn1_w_gate", "ffn1_w_up", "ffn2_w_gate", "ffn2_w_up"]

    def as_held(k, a):
        return jnp.swapaxes(a, 1, 2) if k in transposed else a

    shard16 = {k: as_held(k, weights[k])[0].astype(BF16) for k in big}
    full = {}
    sharded_small = ["meta_tokens", "rg_conv_w", "rg_w_a", "rg_b_a", "rg_w_x", "rg_b_x"]
    sm, full["ffn1_w_gate"] = _comm_call(
        "ag_first", [_ag_rider(_pack([weights[k] for k in sharded_small])), _ag_rider(shard16["ffn1_w_gate"])])
    sm = [jnp.stack(p) for p in zip(*[_unpack(sm[d], [weights[k].shape for k in sharded_small]) for d in range(N_DEV)])]
    meta_full = jnp.moveaxis(sm[0], 0, 1).reshape(N_META, D)
    conv_w_full = jnp.moveaxis(sm[1][:, 0], 0, 1).reshape(CONV_WIDTH, R)
    wa_full = jnp.moveaxis(sm[2][:, 0], 0, 1).reshape(heads, hd, hd)
    ba_full = jnp.moveaxis(sm[3][:, 0], 0, 1).reshape(1, R)
    wx_full = jnp.moveaxis(sm[4][:, 0], 0, 1).reshape(heads, hd, hd)
    bx_full = jnp.moveaxis(sm[5][:, 0], 0, 1).reshape(1, R)

    lam_fn = functools.partial(_s5_discretise)
    (lr, li, bbr, bbi), disc_vjp = jax.vjp(lam_fn, s5_lambda_re[0], s5_lambda_im[0], s5_log_dt[0], s5_b_re[0], s5_b_im[0])
    lr_row, li_row = lr.reshape(1, W), li.reshape(1, W)
    wb_r = _expand_diag(jnp.swapaxes(bbr, 1, 2), True)
    wb_i = _expand_diag(jnp.swapaxes(bbi, 1, 2), True)
    wc_r = _expand_diag(jnp.swapaxes(s5_c_re[0], 1, 2), True)
    wc_i = _expand_diag(-jnp.swapaxes(s5_c_im[0], 1, 2), True)

    h0 = jnp.concatenate([meta_full, x[0], jnp.zeros((Tp - T, D), F32)], axis=0)
    tgt = jnp.concatenate([jnp.zeros((N_META, D), F32), loss_target[0], jnp.zeros((Tp - T, D), F32)], axis=0)

    h1, ffn1_saved, full["ffn1_w_up"], full["ffn1_w_down"], (full["w_in"],) = _ffn_fwd(
        "ffn1", h0, ffn1_norm, full["ffn1_w_gate"], shard16["ffn1_w_up"], shard16["ffn1_w_down"],
        [_ag_rider(shard16["w_in"])], Tp)

    tmd = _row_tile(Tp, D)
    (n2,) = _rowcall("mix_norm", lambda i, a, g: (_rmsnorm(a, g),), Tp, tmd, [(h1, 0, 0, D)], [mix_norm], [(D, BF16)])
    proj, (full["w_out"], full["s5_glu_w"]) = _mm_bcast(
        "w_in", n2, full["w_in"], NN, True, F32, riders=[_ag_rider(shard16["w_out"]), _ag_rider(shard16["s5_glu_w"])])

    tmr = _row_tile(Tp, R)
    tb = _pick(Tp, 768, SUBLANES)
    xc = _conv_fwd(proj, conv_w_full, rg_conv_b, R, tmr)
    a_t, b_t = _rowcall("rg_gates", lambda i, *a: _rg_gates(*a), Tp, tmr, [(xc, 0, 0, R)],
                        [wa_full, ba_full, wx_full, bx_full, rg_lambda], [(R, F32), (R, F32)])
    h_rg, hprev = _rg_scan_fwd(a_t, b_t, tb, _pick(R, 512, LANES))
    (yn_rg,) = _rowcall("rg_out", lambda i, *a: (_rg_out(*a),), Tp, tmr, [(h_rg, 0, 0, R), (proj, 0, 1, R)],
                        [rg_out_norm], [(R, BF16)])

    u_off = 2 * R // LANES
    bur = _mm_bdiag("s5_bu_re", [(proj, wb_r, u_off)], NN, LANES, NB, F32)
    bui = _mm_bdiag("s5_bu_im", [(proj, wb_i, u_off)], NN, LANES, NB, F32)
    cbs = _pick(W, 512, LANES)
    xr, xi, xpr, xpi, (full["ffn2_w_gate"],) = _s5_scan_fwd(
        bur, bui, _scan_tables(lr_row, li_row, False), tb, cbs, riders=[_ag_rider(shard16["ffn2_w_gate"])])
    y_s5 = _mm_bdiag("s5_y", [(xr, wc_r, 0), (xi, wc_i, 0)], NN, NB, LANES, F32)
    tms = _row_tile(Tp, S)
    s_col = 2 * R // S
    (z,) = _rowcall("s5_pre", lambda i, *a: (_s5_pre(*a),), Tp, tms, [(y_s5, 0, 0, S), (proj, 0, s_col, S)], [s5_d],
                    [(S, F32)])
    gw = full["s5_glu_w"]
    zz = _mm_red("s5_glu", [(z, gw)], NN, F32, jb=N_DEV)
    (yn_s5,) = _rowcall("s5_out", lambda i, *a: (_s5_out(*a),), Tp, tms, [(z, 0, 0, S), (zz, 0, 0, S)],
                        [s5_glu_b, s5_out_norm], [(S, BF16)])
    yn = jnp.concatenate([yn_rg, yn_s5], axis=1)
    h2 = _mm_red("w_out", [(yn, full["w_out"])], NN, F32, res=h1, jb=4)

    h3, ffn2_saved, full["ffn2_w_up"], full["ffn2_w_down"], _ = _ffn_fwd(
        "ffn2", h2, ffn2_norm, full["ffn2_w_gate"], shard16["ffn2_w_up"], shard16["ffn2_w_down"], [], Tp)

    def final(i, hh, tt, g):
        out, vjp = jax.vjp(_rmsnorm, hh, g)
        row = i * tmd + lax.broadcasted_iota(jnp.int32, (tmd, 1), 0)
        valid = jnp.logical_and(row >= N_META, row < T)
        err = jnp.where(valid, out - tt, 0.0)
        part = 0.5 * jnp.sum(jnp.mean(err * err, axis=-1, keepdims=True))
        dx, dg = vjp(err * (1.0 / D))
        return dx, dx, dg, jnp.full((SUBLANES, LANES), part, F32)

    dh3, dh3_16, d_final_norm, loss_part = _rowcall(
        "final", final, Tp, tmd, [(h3, 0, 0, D), (tgt, 0, 0, D)], [final_norm.reshape(1, D)],
        [(D, F32), (D, BF16)], [(1, D), (SUBLANES, LANES)])
    loss = lax.psum(loss_part[0, 0], ("x", "y", "c"))

    grads = {}
    red = _GradReducer(c_idx)

    def carried(fn, *args, ici=1, d2d=1, extra=(), **kw):
        riders, absorb = red.take(ici, d2d)
        if not riders and not extra:
            return fn(*args, **kw), []
        out, routs = fn(*args, riders=riders + list(extra), **kw)
        absorb(routs[:len(riders)])
        return out, routs[len(riders):]

    dh2, dh2_16, grads["ffn2_norm"] = _ffn_bwd(
        "ffn2", dh3, dh3_16, h2, ffn2_norm, full["ffn2_w_gate"], full["ffn2_w_up"], full["ffn2_w_down"], ffn2_saved,
        Tp, red)

    Kb_out = full["w_out"].shape[1]
    dyn = _mm_bcast("w_out_dx", dh2_16, full["w_out"], NT, True, F32)
    red.push("w_out", carried(_mm_tn, "w_out_dw", yn, dh2_16, N_DEV, Kb_out, D, BF16)[0])

    def s5_out_bwd(i, zv, zzv, d, gb, gn):
        _, vjp = jax.vjp(_s5_out, zv, zzv, gb, gn)
        return vjp(d)

    dz_a, dzz, grads["s5_glu_b"], grads["s5_out_norm"] = _rowcall(
        "s5_out_bwd", s5_out_bwd, Tp, tms, [(z, 0, 0, S), (zz, 0, 0, S), (dyn, 0, R // S, S)],
        [s5_glu_b, s5_out_norm], [(S, F32), (S, BF16)], [(1, S), (1, S)])
    Kb_glu = gw.shape[1]
    dz = _mm_bcast("s5_glu_dx", dzz, gw, NT, True, F32)
    red.push("s5_glu_w", carried(_mm_tn, "s5_glu_dw", z, dzz, N_DEV, Kb_glu, S, BF16, ici=0)[0])

    def s5_pre_bwd(i, yv, uv, d1, d2, dd):
        _, vjp = jax.vjp(_s5_pre, yv, uv, dd)
        return vjp(d1 + d2)

    dy, du_a, grads["s5_d"] = _rowcall(
        "s5_pre_bwd", s5_pre_bwd, Tp, tms, [(y_s5, 0, 0, S), (proj, 0, s_col, S), (dz_a, 0, 0, S), (dz, 0, 0, S)],
        [s5_d], [(S, F32), (S, F32)], [(1, S)])
    gxr = _mm_bdiag("s5_dx_re", [(dy, wc_r, 0)], NT, LANES, NB, F32)
    gxi = _mm_bdiag("s5_dx_im", [(dy, wc_i, 0)], NT, LANES, NB, F32)
    d_wc_r = _mm_tn("s5_dc_re", xr, dy, Q, NB, LANES, F32)
    d_wc_i = _mm_tn("s5_dc_im", xi, dy, Q, NB, LANES, F32)
    scan_riders, scan_absorb = red.take()
    ar, ai, dlr8, dli8, scan_routs = _s5_scan_bwd(
        gxr, gxi, xpr, xpi, _scan_tables(lr_row, -li_row, True), tb, cbs, riders=scan_riders)
    scan_absorb(scan_routs)
    du_s5 = _mm_bdiag("s5_du", [(ar, wb_r, 0), (ai, wb_i, 0)], NT, NB, LANES, BF16, res=(du_a, 0))
    d_wb_r = carried(_mm_tn, "s5_db_re", proj, ar, Q, LANES, NB, F32, a_off=u_off, d2d=0)[0]
    d_wb_i = _mm_tn("s5_db_im", proj, ai, Q, LANES, NB, F32, a_off=u_off)
    d_bbr = jnp.swapaxes(_extract_diag(d_wb_r, C, N), 1, 2)
    d_bbi = jnp.swapaxes(_extract_diag(d_wb_i, C, N), 1, 2)
    d_lr = jnp.sum(dlr8, axis=0).reshape(G, N)
    d_li = jnp.sum(dli8, axis=0).reshape(G, N)
    d_lre, d_lim, d_logdt, d_bre, d_bim = disc_vjp((d_lr, d_li, d_bbr, d_bbi))
    grads["s5_lambda_re"], grads["s5_lambda_im"], grads["s5_log_dt"] = d_lre[None], d_lim[None], d_logdt[None]
    grads["s5_b_re"], grads["s5_b_im"] = d_bre[None], d_bim[None]
    grads["s5_c_re"] = jnp.swapaxes(_extract_diag(d_wc_r, N, C), 1, 2)[None]
    grads["s5_c_im"] = -jnp.swapaxes(_extract_diag(d_wc_i, N, C), 1, 2)[None]

    def rg_out_bwd(i, hv, gv, d, gn):
        _, vjp = jax.vjp(_rg_out, hv, gv, gn)
        return vjp(d)

    dh_scan, dg_rg, grads["rg_out_norm"] = _rowcall(
        "rg_out_bwd", rg_out_bwd, Tp, tmr, [(h_rg, 0, 0, R), (proj, 0, 1, R), (dyn, 0, 0, R)], [rg_out_norm],
        [(R, F32), (R, BF16)], [(1, R)])
    db_t, da_t = _rg_scan_bwd(dh_scan, a_t, hprev, tb, _pick(R, 512, LANES))

    def rg_gates_bwd(i, xv, da, db, wa, ba, wx, bx, lam):
        _, vjp = jax.vjp(_rg_gates, xv, wa, ba, wx, bx, lam)
        return vjp((da, db))

    dxc, d_wa, d_ba, d_wx, d_bx, grads["rg_lambda"] = _rowcall(
        "rg_gates_bwd", rg_gates_bwd, Tp, _row_tile(Tp, R, 4, 1024 * 1024),
        [(xc, 0, 0, R), (da_t, 0, 0, R), (db_t, 0, 0, R)], [wa_full, ba_full, wx_full, bx_full, rg_lambda],
        [(R, F32)], [(heads, hd, hd), (1, R), (heads, hd, hd), (1, R), (1, R)])
    du_rg, d_conv_w, grads["rg_conv_b"] = _conv_bwd(dxc, proj, conv_w_full, R, tmr)

    grads["final_norm"] = d_final_norm.reshape(D)
    late = ["ffn1_norm", "mix_norm"]
    replicated = ["ffn1_norm", "mix_norm", "rg_conv_b", "rg_lambda", "s5_lambda_re", "s5_lambda_im", "s5_log_dt",
                  "s5_b_re", "s5_b_im", "s5_c_re", "s5_c_im", "s5_d", "s5_glu_b", "rg_out_norm", "s5_out_norm",
                  "ffn2_norm", "final_norm"]
    early = [k for k in replicated if k not in late]
    early_full = [grads[k].reshape(weights[k].shape) for k in early] + [d_conv_w, d_wa, d_ba, d_wx, d_bx]

    dproj = jnp.concatenate([du_rg.astype(BF16), dg_rg, du_s5], axis=1)
    win = full["w_in"]
    Nb_in = win.shape[2]
    dn2, (early_gathered,) = carried(_mm_red, "w_in_dx", [(dproj, win)], NT, F32, jb=4,
                                     extra=[_ag_rider(_pack(early_full))])
    red.push("w_in", carried(_mm_tn, "w_in_dw", n2, dproj, N_DEV, D, Nb_in, BF16)[0])
    dh1, dh1_16, grads["mix_norm"] = _norm_bwd("mix_norm_bwd", h1, mix_norm, [dn2], dh2, Tp)

    dh0, _, grads["ffn1_norm"] = _ffn_bwd(
        "ffn1", dh1, dh1_16, h0, ffn1_norm, full["ffn1_w_gate"], full["ffn1_w_up"], full["ffn1_w_down"], ffn1_saved,
        Tp, red)
    grad_x = dh0[N_META:T][None]

    reduced = red.flush()
    outs = {}
    for k in big:
        r2 = reduced[k]
        _, Rk, Ck = r2.shape
        w2, m2, v2 = (as_held(k, a) for a in (weights[k], moments_m[k], moments_v[k]))

        def big_update(i, wv, mv, vv, p0, p1, p2, p3):
            gsum = (p0.astype(F32) + p1.astype(F32)) + (p2.astype(F32) + p3.astype(F32))
            return (gsum,) + _adamw(wv, gsum, mv, vv)

        res = _rowcall(f"adam_{k}", big_update, Rk, _row_tile(Rk, Ck, 4, 1024 * 1024),
                       [(w2, 0, 0, Ck), (m2, 0, 0, Ck), (v2, 0, 0, Ck)] + [(r2, kk, 0, Ck) for kk in range(N_CHIP)],
                       [], [(Ck, F32)] * 4)
        outs[k] = [as_held(k, o[None]) for o in res]

    late_full = [grads[k].reshape(weights[k].shape) for k in late] + [dh0[:N_META]]
    late_gathered = _all_gather("ag_late_grads", _pack(late_full))

    def sum8(i, *parts):
        s = parts[0]
        for q in parts[1:]:
            s = s + q
        return (s,)

    def sum_devices(name, gathered, shapes):
        n_rows = gathered.shape[1]
        (summed,) = _rowcall(name, sum8, n_rows, _pick(n_rows, 512, SUBLANES),
                             [(gathered, d, 0, LANES) for d in range(N_DEV)], [], [(LANES, F32)])
        return _unpack(summed, shapes)

    early_sum = sum_devices("small_sum_early", early_gathered, [a.shape for a in early_full])
    late_sum = sum_devices("small_sum_late", late_gathered, [a.shape for a in late_full])
    g_small = dict(zip(early, early_sum[:len(early)]))
    g_small.update(zip(late, late_sum[:len(late)]))
    d_cw, d_wa_s, d_ba_s, d_wx_s, d_bx_s = early_sum[len(early):]
    d_meta = late_sum[len(late)]

    def shard_of(a, axis):
        n = a.shape[axis] // N_DEV
        return lax.dynamic_slice_in_dim(a, me * n, n, axis)

    g_small["meta_tokens"] = shard_of(d_meta, 1)
    g_small["rg_conv_w"] = shard_of(d_cw, 1)[None]
    g_small["rg_w_a"] = shard_of(d_wa_s, 1)[None]
    g_small["rg_b_a"] = shard_of(d_ba_s.reshape(heads, hd), 1)[None]
    g_small["rg_w_x"] = shard_of(d_wx_s, 1)[None]
    g_small["rg_b_x"] = shard_of(d_bx_s.reshape(heads, hd), 1)[None]
    small = replicated + sharded_small
    shapes = [weights[k].shape for k in small]
    gp = _pack([g_small[k] for k in small])
    n_rows = gp.shape[0]

    def small_update(i, wv, gv, mv, vv):
        return _adamw(wv, gv, mv, vv)

    res = _rowcall("adam_small", small_update, n_rows, _pick(n_rows, 512, SUBLANES),
                   [(_pack([weights[k] for k in small]), 0, 0, LANES), (gp, 0, 0, LANES),
                    (_pack([moments_m[k] for k in small]), 0, 0, LANES), (_pack([moments_v[k] for k in small]), 0, 0, LANES)],
                   [], [(LANES, F32)] * 3)
    unpacked = [_unpack(r, shapes) for r in res]
    for idx, k in enumerate(small):
        outs[k] = [g_small[k].reshape(weights[k].shape)] + [u[idx] for u in unpacked]

    return (loss, grad_x, *[outs[k][0] for k in order], *[outs[k][1] for k in order],
            *[outs[k][2] for k in order], *[outs[k][3] for k in order])
```

```python
import functools
import math

import jax
import jax.numpy as jnp
from jax import lax
from jax.experimental import pallas as pl
from jax.experimental.pallas import tpu as pltpu

F32, BF16 = jnp.float32, jnp.bfloat16
SDS = jax.ShapeDtypeStruct
MESH_ID = pl.DeviceIdType.MESH
N_DEV = 8
N_CHIP = 4
LANES = 128
SUBLANES = 8
VMEM_LIMIT = 56 * 1024 * 1024

EPS = 1e-6
RG_C = 8.0
N_META = 16
CONV_WIDTH = 4
S5_GROUP = 16
S5_STATE = 64
GROUPS_PER_BLOCK = LANES // S5_GROUP
ADAM_LR, ADAM_B1, ADAM_B2, ADAM_EPS, ADAM_WD, ADAM_STEP = 0.001, 0.9, 0.999, 1e-08, 0.01, 10

NN = (((1,), (0,)), ((), ()))
NT = (((1,), (1,)), ((), ()))
TN = (((0,), (0,)), ((), ()))


def _pick(n, target, mult=16):
    if n <= target:
        return n
    best = None
    for d in range(mult, target + 1, mult):
        if n % d == 0:
            best = d
    assert best is not None, (n, target, mult)
    return best


def _row_tile(nrows, ncols, itembytes=4, budget=2 * 1024 * 1024):
    return _pick(nrows, max(16, budget // (ncols * itembytes)))


def _params(sem):
    return pltpu.CompilerParams(dimension_semantics=sem, vmem_limit_bytes=VMEM_LIMIT)


def _rowcall(name, fn, nrows, tm, rows, fulls, row_outs, acc_outs=()):
    n_in = len(rows) + len(fulls)
    in_specs = []
    for arr, lead, cb, C in rows:
        if arr.ndim == 3:
            in_specs.append(pl.BlockSpec((None, tm, C), lambda i, lead=lead, cb=cb: (lead, i, cb)))
        else:
            in_specs.append(pl.BlockSpec((tm, C), lambda i, cb=cb: (i, cb)))
    for f in fulls:
        in_specs.append(pl.BlockSpec(f.shape, lambda i, nd=f.ndim: (0,) * nd))
    out_specs = [pl.BlockSpec((tm, C), lambda i: (i, 0)) for C, _ in row_outs]
    out_shape = [SDS((nrows, C), dt) for C, dt in row_outs]
    for shp in acc_outs:
        out_specs.append(pl.BlockSpec(shp, lambda i, nd=len(shp): (0,) * nd))
        out_shape.append(SDS(shp, F32))
    n_row_out = len(row_outs)

    def body(*refs):
        i = pl.program_id(0)
        res = fn(i, *[r[...] for r in refs[:n_in]])
        outs = refs[n_in:]
        for k in range(n_row_out):
            outs[k][...] = res[k].astype(outs[k].dtype)
        if acc_outs:
            @pl.when(i == 0)
            def _():
                for o in outs[n_row_out:]:
                    o[...] = jnp.zeros_like(o)
            for k in range(n_row_out, len(outs)):
                outs[k][...] += res[k].astype(F32)

    return pl.pallas_call(
        body, grid=(nrows // tm,), in_specs=in_specs, out_specs=out_specs, out_shape=out_shape,
        name=name, compiler_params=_params(("arbitrary",)))(*[r[0] for r in rows], *fulls)


class _Rider:
    def __init__(self, ins, outs, sems, steps):
        self.ins, self.outs, self.sems, self.steps = list(ins), list(outs), list(sems), steps


def _rider_counts(riders):
    return (sum(len(r.ins) for r in riders), sum(len(r.outs) for r in riders), sum(len(r.sems) for r in riders))


def _rider_hooks(riders, in_refs, out_refs, sem_refs):
    hooks, i, o, s = [], 0, 0, 0
    for r in riders:
        hooks.append(r.steps(*in_refs[i:i + len(r.ins)], *out_refs[o:o + len(r.outs)], *sem_refs[s:s + len(r.sems)]))
        i, o, s = i + len(r.ins), o + len(r.outs), s + len(r.sems)
    return hooks


def _linear_step(grid):
    step = pl.program_id(0)
    for ax in range(1, len(grid)):
        step = step * grid[ax] + pl.program_id(ax)
    return step


def _ride_begin(step, hooks):
    if hooks:
        @pl.when(step == 0)
        def _():
            for start, _, _ in hooks:
                start()


def _ride_end(step, nsteps, hooks):
    if hooks:
        if any(mid is not None for _, mid, _ in hooks):
            @pl.when(step == min((2 * nsteps) // 3, nsteps - 1))
            def _():
                for _, mid, _ in hooks:
                    if mid is not None:
                        mid()

        @pl.when(step == nsteps - 1)
        def _():
            for _, _, finish in hooks:
                finish()


def _mm(name, terms, dims, grid, nk, out_shape, out_spec, acc_shape, scale=1.0, res=None, riders=()):
    n_t = len(terms)
    kax = len(grid) - 1
    n_in = 2 * n_t + (1 if res is not None else 0)
    r_in, r_out, _ = _rider_counts(riders)
    nsteps = math.prod(grid)
    n_acc = 1 if nk > 1 else 0

    def body(*refs):
        ins, rin = refs[:n_in], refs[n_in:n_in + r_in]
        o_ref = refs[n_in + r_in]
        rout = refs[n_in + r_in + 1:n_in + r_in + 1 + r_out]
        scratch = refs[n_in + r_in + 1 + r_out:]
        step = _linear_step(grid)
        k = pl.program_id(kax)
        hooks = _rider_hooks(riders, rin, rout, scratch[n_acc:])
        _ride_begin(step, hooks)

        def product():
            r = None
            for t in range(n_t):
                d = lax.dot_general(ins[2 * t][...].astype(BF16), ins[2 * t + 1][...].astype(BF16), dims,
                                    preferred_element_type=F32)
                r = d if r is None else r + d
            return r

        def emit(r):
            r = r * scale
            if res is not None:
                r = r + ins[2 * n_t][...].astype(F32)
            o_ref[...] = r.astype(o_ref.dtype)

        if nk == 1:
            emit(product())
        else:
            acc = scratch[0]

            @pl.when(k == 0)
            def _():
                acc[...] = jnp.zeros_like(acc)

            acc[...] += product()

            @pl.when(k == nk - 1)
            def _():
                emit(acc[...])

        _ride_end(step, nsteps, hooks)

    ops, specs = [], []
    for a, a_spec, b, b_spec in terms:
        ops += [a, b]
        specs += [a_spec, b_spec]
    if res is not None:
        ops.append(res[0])
        specs.append(res[1])
    out_shapes, out_specs, scratch = [out_shape], [out_spec], []
    if n_acc:
        scratch.append(pltpu.VMEM(acc_shape, F32))
    for r in riders:
        ops += r.ins
        specs += [_ANY] * len(r.ins)
        out_shapes += r.outs
        out_specs += [_ANY] * len(r.outs)
        scratch += r.sems
    sem = ("arbitrary",) * len(grid)
    outs = pl.pallas_call(
        body, grid=grid, in_specs=specs, out_specs=out_specs, out_shape=out_shapes,
        scratch_shapes=scratch, name=name, compiler_params=_params(sem))(*ops)
    return (outs[0], list(outs[1:])) if riders else outs[0]


def _mm_bcast(name, a, w, dims, std_out, out_dtype, tm_target=384, riders=()):
    M, K = a.shape
    J = w.shape[0]
    Nb = w.shape[2] if dims == NN else w.shape[1]
    tm = _pick(M, tm_target)
    a_spec = pl.BlockSpec((tm, K), lambda j, i, k: (i, 0))
    w_spec = pl.BlockSpec((None,) + w.shape[1:], lambda j, i, k: (j, 0, 0))
    if std_out:
        out_shape, out_spec = SDS((M, J * Nb), out_dtype), pl.BlockSpec((tm, Nb), lambda j, i, k: (i, j))
    else:
        out_shape, out_spec = SDS((J, M, Nb), out_dtype), pl.BlockSpec((None, tm, Nb), lambda j, i, k: (j, i, 0))
    return _mm(name, [(a, a_spec, w, w_spec)], dims, (J, M // tm, 1), 1, out_shape, out_spec, (tm, Nb), riders=riders)


def _a_blk(a, tm, Kb, off, jb, s):
    if a.ndim == 3:
        return pl.BlockSpec((None, tm, Kb), lambda i, n, j: (j * jb + s, i, 0))
    return pl.BlockSpec((tm, Kb), lambda i, n, j: (i, j * jb + s + off))


def _mm_red(name, pairs, dims, out_dtype, scale=1.0, res=None, tm_target=704, tn_target=1024, n_off=0, n_len=None,
            jb=1, riders=()):
    a0, w0 = pairs[0][0], pairs[0][1]
    J = w0.shape[0]
    M = a0.shape[-2]
    N = w0.shape[2] if dims == NN else w0.shape[1]
    Kb = w0.shape[1] if dims == NN else w0.shape[2]
    if n_len is not None:
        N = n_len
    tm, tn = _pick(M, tm_target), _pick(N, tn_target, LANES)
    nb0 = n_off // tn
    assert J % jb == 0
    terms = []
    for a, w, *rest in pairs:
        off = rest[0] if rest else 0
        for s in range(jb):
            if dims == NN:
                w_spec = pl.BlockSpec((None, Kb, tn), lambda i, n, j, s=s: (j * jb + s, 0, n + nb0))
            else:
                w_spec = pl.BlockSpec((None, tn, Kb), lambda i, n, j, s=s: (j * jb + s, n + nb0, 0))
            terms.append((a, _a_blk(a, tm, Kb, off, jb, s), w, w_spec))
    out_spec = pl.BlockSpec((tm, tn), lambda i, n, j: (i, n))
    r = None if res is None else (res, out_spec)
    return _mm(name, terms, dims, (M // tm, N // tn, J // jb), J // jb, SDS((M, N), out_dtype), out_spec, (tm, tn),
               scale, r, riders)


def _mm_tn(name, a, b, J, Ka, Nb, out_dtype, a_off=0, b_off=0, scale=1.0, tw_target=1024, tk_target=1408, riders=()):
    M = a.shape[-2]
    tk = _pick(M, tk_target)
    tka = _pick(Ka, tw_target, LANES) if Ka % LANES == 0 else Ka
    tnb = _pick(Nb, tw_target, LANES) if Nb % LANES == 0 else Nb

    def spec(arr, width, tw, is_a, off):
        nb = width // tw

        def wi(m, n):
            return m if is_a else n
        if arr.ndim == 3:
            return pl.BlockSpec((None, tk, tw), lambda j, m, n, k: (j, k, wi(m, n)))
        if arr.shape[1] == width:
            return pl.BlockSpec((tk, tw), lambda j, m, n, k: (k, wi(m, n)))
        return pl.BlockSpec((tk, tw), lambda j, m, n, k: (k, (j + off) * nb + wi(m, n)))

    a_spec = spec(a, Ka, tka, True, a_off)
    b_spec = spec(b, Nb, tnb, False, b_off)
    out_spec = pl.BlockSpec((None, tka, tnb), lambda j, m, n, k: (j, m, n))
    return _mm(name, [(a, a_spec, b, b_spec)], TN, (J, Ka // tka, Nb // tnb, M // tk), M // tk,
               SDS((J, Ka, Nb), out_dtype), out_spec, (tka, tnb), scale, riders=riders)


def _mm_bdiag(name, pairs, dims, Kb, Nb, out_dtype, res=None, tm_target=704, riders=()):
    a0, w0 = pairs[0][0], pairs[0][1]
    M, J = a0.shape[0], w0.shape[0]
    tm = _pick(M, tm_target)
    terms = []
    for a, w, off in pairs:
        a_spec = pl.BlockSpec((tm, Kb), lambda j, i, k, off=off: (i, j + off))
        w_spec = pl.BlockSpec((None,) + w.shape[1:], lambda j, i, k: (j, 0, 0))
        terms.append((a, a_spec, w, w_spec))
    out_spec = pl.BlockSpec((tm, Nb), lambda j, i, k: (i, j))
    r = None if res is None else (res[0], pl.BlockSpec((tm, Nb), lambda j, i, k, off=res[1]: (i, j + off)))
    return _mm(name, terms, dims, (J, M // tm, 1), 1, SDS((M, J * Nb), out_dtype), out_spec, (tm, Nb), 1.0, r, riders)


def _rmsnorm(x, g):
    x = x.astype(F32)
    return x * lax.rsqrt(jnp.mean(x * x, axis=-1, keepdims=True) + EPS) * g


def _swiglu_act(gate, up):
    gate, up = gate.astype(F32), up.astype(F32)
    return gate * jax.nn.sigmoid(gate) * up


def _neg_expm1(x):
    series = -x * (1.0 + x * (1.0 / 2 + x * (1.0 / 6 + x * (1.0 / 24 + x * (1.0 / 120 + x * (1.0 / 720))))))
    return jnp.where(x > -0.25, series, 1.0 - jnp.exp(x))


def _rg_gates(xc, wa, ba, wx, bx, lam):
    heads, hd = wa.shape[0], wa.shape[1]
    xb = xc.astype(BF16)
    rs, ig = [], []
    for h in range(heads):
        xh = xb[:, h * hd:(h + 1) * hd]
        rs.append(jnp.dot(xh, wa[h].astype(BF16), preferred_element_type=F32))
        ig.append(jnp.dot(xh, wx[h].astype(BF16), preferred_element_type=F32))
    r = jax.nn.sigmoid(jnp.concatenate(rs, axis=1) + ba)
    ii = jax.nn.sigmoid(jnp.concatenate(ig, axis=1) + bx)
    log_a = -RG_C * r * jax.nn.softplus(-lam)
    a = jnp.exp(log_a)
    mult = jnp.sqrt(_neg_expm1(2.0 * log_a))
    return a, mult * ii * xc


def _rg_out(h, g, gain):
    return _rmsnorm(h * jax.nn.gelu(g), gain)


def _s5_pre(y, u, d):
    return jax.nn.gelu(y + d * u)


def _s5_out(z, zz, glu_b, gain):
    return _rmsnorm(z * jax.nn.sigmoid(zz + glu_b), gain)


def _adamw(w, g, m, v):
    m = ADAM_B1 * m + (1.0 - ADAM_B1) * g
    v = ADAM_B2 * v + (1.0 - ADAM_B2) * jnp.square(g)
    m_hat = m / (1.0 - ADAM_B1 ** ADAM_STEP)
    v_hat = v / (1.0 - ADAM_B2 ** ADAM_STEP)
    delta = -ADAM_LR * (m_hat / (jnp.sqrt(v_hat) + ADAM_EPS) + ADAM_WD * w)
    return delta, m, v


def _conv_fwd(proj, conv_w, conv_b, R, tm):
    T = proj.shape[0]
    tpb = tm // SUBLANES

    def body(cur_ref, prev_ref, w_ref, b_ref, o_ref):
        i = pl.program_id(0)
        cur = cur_ref[...]
        prev = jnp.where(i > 0, prev_ref[...], 0.0)
        ext = jnp.concatenate([prev, cur], axis=0)
        acc = b_ref[...] + w_ref[CONV_WIDTH - 1:CONV_WIDTH, :] * cur
        for s in range(1, CONV_WIDTH):
            acc = acc + w_ref[CONV_WIDTH - 1 - s:CONV_WIDTH - s, :] * pltpu.roll(ext, s, 0)[SUBLANES:, :]
        o_ref[...] = acc

    return pl.pallas_call(
        body, grid=(T // tm,),
        in_specs=[pl.BlockSpec((tm, R), lambda i: (i, 0)),
                  pl.BlockSpec((SUBLANES, R), lambda i: (jnp.maximum(i * tpb - 1, 0), 0)),
                  pl.BlockSpec((CONV_WIDTH, R), lambda i: (0, 0)), pl.BlockSpec((1, R), lambda i: (0, 0))],
        out_specs=pl.BlockSpec((tm, R), lambda i: (i, 0)), out_shape=SDS((T, R), F32),
        name="rg_conv_fwd", compiler_params=_params(("arbitrary",)))(proj, proj, conv_w, conv_b)


def _conv_bwd(dxc, proj, conv_w, R, tm):
    T = dxc.shape[0]
    tpb = tm // SUBLANES
    nt = T // tm
    n_ext = tm + SUBLANES

    def body(d_ref, dnext_ref, u_ref, uprev_ref, w_ref, du_ref, dw_ref, db_ref):
        i = pl.program_id(0)
        d = d_ref[...]
        dnext = jnp.where(i < nt - 1, dnext_ref[...], 0.0)
        dext = jnp.concatenate([d, dnext], axis=0)
        u = u_ref[...]
        uprev = jnp.where(i > 0, uprev_ref[...], 0.0)
        uext = jnp.concatenate([uprev, u], axis=0)
        du = w_ref[CONV_WIDTH - 1:CONV_WIDTH, :] * d
        dws = [jnp.sum(d * u, axis=0, keepdims=True)]
        for s in range(1, CONV_WIDTH):
            du = du + w_ref[CONV_WIDTH - 1 - s:CONV_WIDTH - s, :] * pltpu.roll(dext, n_ext - s, 0)[:tm, :]
            dws.append(jnp.sum(d * pltpu.roll(uext, s, 0)[SUBLANES:, :], axis=0, keepdims=True))
        du_ref[...] = du

        @pl.when(i == 0)
        def _():
            dw_ref[...] = jnp.zeros_like(dw_ref)
            db_ref[...] = jnp.zeros_like(db_ref)

        dw_ref[...] += jnp.concatenate(dws[::-1], axis=0)
        db_ref[...] += jnp.sum(d, axis=0, keepdims=True)

    row = pl.BlockSpec((tm, R), lambda i: (i, 0))
    return pl.pallas_call(
        body, grid=(nt,),
        in_specs=[row, pl.BlockSpec((SUBLANES, R), lambda i: (jnp.minimum((i + 1) * tpb, T // SUBLANES - 1), 0)),
                  row, pl.BlockSpec((SUBLANES, R), lambda i: (jnp.maximum(i * tpb - 1, 0), 0)),
                  pl.BlockSpec((CONV_WIDTH, R), lambda i: (0, 0))],
        out_specs=[row, pl.BlockSpec((CONV_WIDTH, R), lambda i: (0, 0)), pl.BlockSpec((1, R), lambda i: (0, 0))],
        out_shape=[SDS((T, R), F32), SDS((CONV_WIDTH, R), F32), SDS((1, R), F32)],
        name="rg_conv_bwd", compiler_params=_params(("arbitrary",)))(dxc, dxc, proj, proj, conv_w)


def _scan_tiles(tb):
    return tb // SUBLANES


def _rg_scan_fwd(a, b, tb, cb):
    T, W = a.shape

    def body(a_ref, b_ref, h_ref, p_ref, carry):
        @pl.when(pl.program_id(1) == 0)
        def _():
            carry[...] = jnp.zeros_like(carry)

        def tile(t, h):
            ds = pl.ds(pl.multiple_of(t * SUBLANES, SUBLANES), SUBLANES)
            a8, b8 = a_ref[ds, :], b_ref[ds, :]
            hs, ps = [], []
            for j in range(SUBLANES):
                ps.append(h)
                h = a8[j:j + 1, :] * h + b8[j:j + 1, :]
                hs.append(h)
            h_ref[ds, :] = jnp.concatenate(hs, axis=0)
            p_ref[ds, :] = jnp.concatenate(ps, axis=0)
            return h

        carry[0:1, :] = lax.fori_loop(0, _scan_tiles(tb), tile, carry[0:1, :])

    blk = pl.BlockSpec((tb, cb), lambda c, i: (i, c))
    return pl.pallas_call(
        body, grid=(W // cb, T // tb), in_specs=[blk, blk], out_specs=[blk, blk],
        out_shape=[SDS((T, W), F32)] * 2, scratch_shapes=[pltpu.VMEM((SUBLANES, cb), F32)],
        name="rg_scan_fwd", compiler_params=_params(("arbitrary", "arbitrary")))(a, b)


def _rg_scan_bwd(dh, a, hprev, tb, cb):
    T, W = a.shape
    nt = T // tb

    def body(g_ref, a_ref, p_ref, db_ref, da_ref, carry):
        @pl.when(pl.program_id(1) == 0)
        def _():
            carry[...] = jnp.zeros_like(carry)

        def tile(tt, c):
            t = _scan_tiles(tb) - 1 - tt
            ds = pl.ds(pl.multiple_of(t * SUBLANES, SUBLANES), SUBLANES)
            g8, a8 = g_ref[ds, :], a_ref[ds, :]
            adjs = [None] * SUBLANES
            for j in range(SUBLANES - 1, -1, -1):
                adj = g8[j:j + 1, :] + c
                adjs[j] = adj
                c = a8[j:j + 1, :] * adj
            adj8 = jnp.concatenate(adjs, axis=0)
            db_ref[ds, :] = adj8
            da_ref[ds, :] = adj8 * p_ref[ds, :]
            return c

        carry[0:1, :] = lax.fori_loop(0, _scan_tiles(tb), tile, carry[0:1, :])

    blk = pl.BlockSpec((tb, cb), lambda c, i: (nt - 1 - i, c))
    return pl.pallas_call(
        body, grid=(W // cb, nt), in_specs=[blk, blk, blk], out_specs=[blk, blk],
        out_shape=[SDS((T, W), F32)] * 2, scratch_shapes=[pltpu.VMEM((SUBLANES, cb), F32)],
        name="rg_scan_bwd", compiler_params=_params(("arbitrary", "arbitrary")))(dh, a, hprev)


SCAN_LEVELS = (1, 2, 4)


def _scan_tables(lr, li, reverse):
    def cmul(a, b):
        return a[0] * b[0] - a[1] * b[1], a[0] * b[1] + a[1] * b[0]

    powers = [(lr, li)]
    for _ in range(SUBLANES - 1):
        powers.append(cmul(powers[-1], (lr, li)))
    row = jnp.arange(SUBLANES)[:, None]
    rows = []
    for k in SCAN_LEVELS:
        has = (row <= SUBLANES - 1 - k) if reverse else (row >= k)
        rows += [jnp.where(has, powers[k - 1][0], 0.0), jnp.where(has, powers[k - 1][1], 0.0)]
    order = list(range(SUBLANES - 1, -1, -1)) if reverse else list(range(SUBLANES))
    rows += [jnp.concatenate([powers[j][0] for j in order], axis=0), jnp.concatenate([powers[j][1] for j in order], axis=0)]
    return jnp.concatenate(rows, axis=0)


def _scan_tile(vr, vi, carry, tab_ref, reverse):
    n_rows = SUBLANES
    for lvl, k in enumerate(SCAN_LEVELS):
        mr = tab_ref[2 * lvl * n_rows:(2 * lvl + 1) * n_rows, :]
        mi = tab_ref[(2 * lvl + 1) * n_rows:(2 * lvl + 2) * n_rows, :]
        shift = n_rows - k if reverse else k
        sr, si = pltpu.roll(vr, shift, 0), pltpu.roll(vi, shift, 0)
        vr, vi = vr + mr * sr - mi * si, vi + mr * si + mi * sr
    base = 2 * len(SCAN_LEVELS) * n_rows
    pr, pi = tab_ref[base:base + n_rows, :], tab_ref[base + n_rows:base + 2 * n_rows, :]
    cr, ci = carry
    return vr + pr * cr - pi * ci, vi + pr * ci + pi * cr


def _s5_scan_fwd(bu, tab, tb, cb, riders=()):
    T, W = bu.shape[0], bu.shape[1] // 2
    grid = (W // cb, T // tb)
    r_in, r_out, _ = _rider_counts(riders)
    re, im = slice(0, cb), slice(cb, 2 * cb)

    def body(*refs):
        b_ref, tab_ref = refs[:2]
        x_ref, p_ref = refs[2 + r_in:4 + r_in]
        cr, ci = refs[4 + r_in + r_out:6 + r_in + r_out]
        step = _linear_step(grid)
        hooks = _rider_hooks(riders, refs[2:2 + r_in], refs[4 + r_in:4 + r_in + r_out], refs[6 + r_in + r_out:])
        _ride_begin(step, hooks)

        @pl.when(pl.program_id(1) == 0)
        def _():
            cr[...] = jnp.zeros_like(cr)
            ci[...] = jnp.zeros_like(ci)

        first_row = lax.broadcasted_iota(jnp.int32, (SUBLANES, cb), 0) == 0

        def tile(t, carry):
            ds = pl.ds(pl.multiple_of(t * SUBLANES, SUBLANES), SUBLANES)
            xr8, xi8 = _scan_tile(b_ref[ds, re], b_ref[ds, im], carry, tab_ref, reverse=False)
            x_ref[ds, re] = xr8
            x_ref[ds, im] = xi8
            p_ref[ds, re] = jnp.where(first_row, carry[0], pltpu.roll(xr8, 1, 0))
            p_ref[ds, im] = jnp.where(first_row, carry[1], pltpu.roll(xi8, 1, 0))
            return xr8[SUBLANES - 1:SUBLANES, :], xi8[SUBLANES - 1:SUBLANES, :]

        xr, xi = lax.fori_loop(0, _scan_tiles(tb), tile, (cr[0:1, :], ci[0:1, :]))
        cr[0:1, :] = xr
        ci[0:1, :] = xi
        _ride_end(step, math.prod(grid), hooks)

    blk = pl.BlockSpec((tb, 2 * cb), lambda c, i: (i, c))
    tabs = pl.BlockSpec((tab.shape[0], cb), lambda c, i: (0, c))
    outs = pl.pallas_call(
        body, grid=grid, in_specs=[blk, tabs] + [_ANY] * r_in, out_specs=[blk] * 2 + [_ANY] * r_out,
        out_shape=[SDS((T, 2 * W), F32)] * 2 + [o for r in riders for o in r.outs],
        scratch_shapes=[pltpu.VMEM((SUBLANES, cb), F32), pltpu.VMEM((SUBLANES, cb), F32)] + [s for r in riders for s in r.sems],
        name="s5_scan_fwd", compiler_params=_params(("arbitrary", "arbitrary")))(
            bu, tab, *[a for r in riders for a in r.ins])
    return tuple(outs[:2]) + (list(outs[2:]),)


def _s5_scan_bwd(g, xp, tab, tb, cb, riders=()):
    T, W = g.shape[0], g.shape[1] // 2
    nt = T // tb
    grid = (W // cb, nt)
    r_in, r_out, _ = _rider_counts(riders)
    re, im = slice(0, cb), slice(cb, 2 * cb)

    def body(*refs):
        g_ref, p_ref, tab_ref = refs[:3]
        a_ref, dlr_ref, dli_ref = refs[3 + r_in:6 + r_in]
        cr, ci = refs[6 + r_in + r_out:8 + r_in + r_out]
        step = _linear_step(grid)
        hooks = _rider_hooks(riders, refs[3:3 + r_in], refs[6 + r_in:6 + r_in + r_out], refs[8 + r_in + r_out:])
        _ride_begin(step, hooks)

        @pl.when(pl.program_id(1) == 0)
        def _():
            cr[...] = jnp.zeros_like(cr)
            ci[...] = jnp.zeros_like(ci)
            dlr_ref[...] = jnp.zeros_like(dlr_ref)
            dli_ref[...] = jnp.zeros_like(dli_ref)

        def tile(tt, carry):
            t = _scan_tiles(tb) - 1 - tt
            ds = pl.ds(pl.multiple_of(t * SUBLANES, SUBLANES), SUBLANES)
            ar8, ai8 = _scan_tile(g_ref[ds, re], g_ref[ds, im], carry, tab_ref, reverse=True)
            a_ref[ds, re] = ar8
            a_ref[ds, im] = ai8
            pr8, pi8 = p_ref[ds, re], p_ref[ds, im]
            dlr_ref[...] += ar8 * pr8 + ai8 * pi8
            dli_ref[...] += ai8 * pr8 - ar8 * pi8
            return ar8[0:1, :], ai8[0:1, :]

        ar, ai = lax.fori_loop(0, _scan_tiles(tb), tile, (cr[0:1, :], ci[0:1, :]))
        cr[0:1, :] = ar
        ci[0:1, :] = ai
        _ride_end(step, math.prod(grid), hooks)

    blk = pl.BlockSpec((tb, 2 * cb), lambda c, i: (nt - 1 - i, c))
    tabs = pl.BlockSpec((tab.shape[0], cb), lambda c, i: (0, c))
    acc = pl.BlockSpec((SUBLANES, cb), lambda c, i: (0, c))
    outs = pl.pallas_call(
        body, grid=grid, in_specs=[blk, blk, tabs] + [_ANY] * r_in, out_specs=[blk, acc, acc] + [_ANY] * r_out,
        out_shape=[SDS((T, 2 * W), F32)] + [SDS((SUBLANES, W), F32)] * 2 + [o for r in riders for o in r.outs],
        scratch_shapes=[pltpu.VMEM((SUBLANES, cb), F32), pltpu.VMEM((SUBLANES, cb), F32)] + [s for r in riders for s in r.sems],
        name="s5_scan_bwd", compiler_params=_params(("arbitrary", "arbitrary")))(
            g, xp, tab, *[a for r in riders for a in r.ins])
    return tuple(outs[:3]) + (list(outs[3:]),)


_ANY = pl.BlockSpec(memory_space=pl.ANY)


def _ag_steps(x_ref, out_ref, send_sems, recv_sems, local_sem):
    x, y, c = lax.axis_index("x"), lax.axis_index("y"), lax.axis_index("c")
    me, sibling = (x, y, c), (x, y, 1 - c)
    nbr_a = ((x + c) % 2, (y + 1 - c) % 2)
    nbr_b = ((x + 1 - c) % 2, (y + c) % 2)
    diag = (1 - x, 1 - y)

    def slot(px, py, pc):
        return out_ref.at[4 * px + 2 * py + pc]

    def copy(k, block, to, src=None):
        return pltpu.make_async_remote_copy(
            src_ref=slot(*block) if src is None else src, dst_ref=slot(*block),
            send_sem=send_sems.at[k], recv_sem=recv_sems.at[k], device_id=to, device_id_type=MESH_ID)

    mine = pltpu.make_async_copy(x_ref, slot(*me), local_sem)
    first = [copy(0, me, sibling, src=x_ref), copy(1, me, (*nbr_a, c), src=x_ref), copy(2, me, (*nbr_b, c), src=x_ref)]
    relay = copy(3, (*nbr_a, c), (*nbr_b, c))
    passed = [copy(4, (*nbr_a, c), sibling), copy(5, (*nbr_b, c), sibling), copy(6, (*diag, c), sibling)]

    def start():
        mine.start()
        for cp in first:
            cp.start()

    def mid():
        copy(1, (*nbr_a, c), me).wait_recv()
        relay.start()
        passed[0].start()
        copy(2, (*nbr_b, c), me).wait_recv()
        passed[1].start()

    def finish():
        copy(3, (*diag, c), me).wait_recv()
        passed[2].start()
        copy(0, sibling, me).wait_recv()
        copy(4, (*nbr_b, 1 - c), me).wait_recv()
        copy(5, (*nbr_a, 1 - c), me).wait_recv()
        copy(6, (*diag, 1 - c), me).wait_recv()
        for cp in first + [relay] + passed:
            cp.wait_send()
        mine.wait()

    return start, mid, finish


def _rs1_steps(g_ref, r_ref, send_sems, recv_sems):
    x, y, c = lax.axis_index("x"), lax.axis_index("y"), lax.axis_index("c")
    copies = [pltpu.make_async_remote_copy(
        src_ref=g_ref.at[2 * k + (1 - c)], dst_ref=r_ref.at[k], send_sem=send_sems.at[k], recv_sem=recv_sems.at[k],
        device_id=(x, y, 1 - c), device_id_type=MESH_ID) for k in range(N_CHIP)]

    def start():
        for cp in copies:
            cp.start()

    def finish():
        for cp in copies:
            cp.wait_recv()
        for cp in copies:
            cp.wait_send()

    return start, None, finish


def _rs2_steps(p_ref, r_ref, send_sems, recv_sems, local_sem):
    x, y, c = lax.axis_index("x"), lax.axis_index("y"), lax.axis_index("c")
    myk = 2 * x + y
    chips = [(1 - x, y), (x, 1 - y), (1 - x, 1 - y)]
    mine = pltpu.make_async_copy(p_ref.at[myk], r_ref.at[myk], local_sem)
    sends = [pltpu.make_async_remote_copy(
        src_ref=p_ref.at[2 * px + py], dst_ref=r_ref.at[myk], send_sem=send_sems.at[j], recv_sem=recv_sems.at[j],
        device_id=(px, py, c), device_id_type=MESH_ID) for j, (px, py) in enumerate(chips)]

    def start():
        mine.start()
        for cp in sends:
            cp.start()

    def finish():
        for j, (px, py) in enumerate(chips):
            pltpu.make_async_remote_copy(
                src_ref=p_ref.at[myk], dst_ref=r_ref.at[2 * px + py], send_sem=send_sems.at[j], recv_sem=recv_sems.at[j],
                device_id=(px, py, c), device_id_type=MESH_ID).wait_recv()
        for cp in sends:
            cp.wait_send()
        mine.wait()

    return start, None, finish


def _dma_sems(*counts):
    return [pltpu.SemaphoreType.DMA((n,)) if n else pltpu.SemaphoreType.DMA for n in counts]


def _ag_rider(shard):
    return _Rider([shard], [SDS((N_DEV,) + shard.shape, shard.dtype)], _dma_sems(7, 7, 0), _ag_steps)


def _rs1_rider(g):
    return _Rider([g], [SDS((N_CHIP,) + g.shape[1:], g.dtype)], _dma_sems(N_CHIP, N_CHIP), _rs1_steps)


def _rs2_rider(p):
    return _Rider([p], [SDS(p.shape, p.dtype)], _dma_sems(3, 3, 0), _rs2_steps)


def _comm_call(name, riders):
    r_in, r_out, _ = _rider_counts(riders)

    def body(*refs):
        hooks = _rider_hooks(riders, refs[:r_in], refs[r_in:r_in + r_out], refs[r_in + r_out:])
        for start, _, _ in hooks:
            start()
        for _, mid, _ in hooks:
            if mid is not None:
                mid()
        for _, _, finish in hooks:
            finish()

    ops = [a for r in riders for a in r.ins]
    outs = pl.pallas_call(
        body, out_shape=[o for r in riders for o in r.outs], in_specs=[_ANY] * r_in, out_specs=[_ANY] * r_out,
        scratch_shapes=[s for r in riders for s in r.sems], name=name)(*ops)
    return list(outs)


def _all_gather(name, shard):
    return _comm_call(name, [_ag_rider(shard)])[0]


def _pair_add(name, g, r1, c_idx):
    _, R, C = g.shape
    tr = _row_tile(R, C, 2, 6 * 1024 * 1024)

    def body(c_ref, g_ref, r_ref, o_ref):
        o_ref[...] = (g_ref[...].astype(F32) + r_ref[...].astype(F32)).astype(o_ref.dtype)

    grid_spec = pltpu.PrefetchScalarGridSpec(
        num_scalar_prefetch=1, grid=(N_CHIP, R // tr),
        in_specs=[pl.BlockSpec((None, tr, C), lambda k, i, c_ref: (2 * k + c_ref[0], i, 0)),
                  pl.BlockSpec((None, tr, C), lambda k, i, c_ref: (k, i, 0))],
        out_specs=pl.BlockSpec((None, tr, C), lambda k, i, c_ref: (k, i, 0)))
    return pl.pallas_call(body, grid_spec=grid_spec, out_shape=SDS((N_CHIP, R, C), g.dtype), name=name,
                          compiler_params=_params(("arbitrary", "arbitrary")))(c_idx, g, r1)


class _GradReducer:
    def __init__(self, c_idx):
        self.c_idx, self.wait_d2d, self.wait_ici, self.done = c_idx, [], [], {}

    def push(self, name, g):
        self.wait_d2d.append((name, g))

    def take(self, ici=1, d2d=1):
        jobs = [("ici",) + self.wait_ici.pop(0) for _ in range(min(ici, len(self.wait_ici)))]
        jobs += [("d2d",) + self.wait_d2d.pop(0) for _ in range(min(d2d, len(self.wait_d2d)))]
        riders = [_rs2_rider(a) if kind == "ici" else _rs1_rider(a) for kind, _, a in jobs]

        def absorb(outs):
            for (kind, name, a), out in zip(jobs, outs, strict=True):
                if kind == "ici":
                    self.done[name] = out
                else:
                    self.wait_ici.append((name, _pair_add(f"add_{name}", a, out, self.c_idx)))

        return riders, absorb

    def flush(self):
        n_calls = 0
        while self.wait_d2d or self.wait_ici:
            riders, absorb = self.take(ici=len(self.wait_ici), d2d=len(self.wait_d2d))
            absorb(_comm_call(f"rs_tail_{n_calls}", riders))
            n_calls += 1
        return self.done


PACK_ROWS = 64


def _pack(arrs):
    flat = jnp.concatenate([a.reshape(-1).astype(F32) for a in arrs])
    n = flat.shape[0]
    unit = PACK_ROWS * LANES
    padded = -(-n // unit) * unit
    return jnp.pad(flat, (0, padded - n)).reshape(padded // LANES, LANES)


def _unpack(buf, shapes):
    flat = buf.reshape(-1)
    outs, off = [], 0
    for shp in shapes:
        n = math.prod(shp)
        outs.append(flat[off:off + n].reshape(shp))
        off += n
    return outs


def _s5_discretise(lre, lim, log_dt, bre, bim):
    dt = jnp.exp(log_dt)[:, None]
    e_m1 = jnp.expm1(lre * dt)
    th = lim * dt
    lr = (e_m1 + 1.0) * jnp.cos(th)
    li = (e_m1 + 1.0) * jnp.sin(th)
    lr_m1 = e_m1 * jnp.cos(th) - 2.0 * jnp.square(jnp.sin(0.5 * th))
    den = lre * lre + lim * lim
    cr = (lr_m1 * lre + li * lim) / den
    ci = (li * lre - lr_m1 * lim) / den
    bbr = cr[..., None] * bre - ci[..., None] * bim
    bbi = cr[..., None] * bim + ci[..., None] * bre
    return lr, li, bbr, bbi


def _expand_diag(m, rows_first):
    G, A, B = m.shape
    q = G // GROUPS_PER_BLOCK
    eye = jnp.eye(GROUPS_PER_BLOCK, dtype=m.dtype)
    m5 = m.reshape(q, GROUPS_PER_BLOCK, A, 1, B) * eye[None, :, None, :, None]
    return m5.reshape(q, GROUPS_PER_BLOCK * A, GROUPS_PER_BLOCK * B)


def _extract_diag(m, A, B):
    q = m.shape[0]
    m5 = m.reshape(q, GROUPS_PER_BLOCK, A, GROUPS_PER_BLOCK, B)
    d = jnp.stack([m5[:, g, :, g, :] for g in range(GROUPS_PER_BLOCK)], axis=1)
    return d.reshape(q * GROUPS_PER_BLOCK, A, B)


def _ffn_fwd(tag, h, gain, wg, wu_shard, wd_shard, tail_riders, Tp):
    D = h.shape[1]
    J, Fb, _ = wg.shape
    tmn = _row_tile(Tp, D)
    (n,) = _rowcall(f"{tag}_norm", lambda i, x, g: (_rmsnorm(x, g),), Tp, tmn, [(h, 0, 0, D)], [gain], [(D, BF16)])
    gate, (wu,) = _mm_bcast(f"{tag}_gate", n, wg, NT, False, BF16, riders=[_ag_rider(wu_shard)])
    up, (wd,) = _mm_bcast(f"{tag}_up", n, wu, NT, False, BF16, riders=[_ag_rider(wd_shard)])
    rows = J * Tp
    tma = _row_tile(rows, Fb, 2, 1024 * 1024)
    g2, u2 = gate.reshape(rows, Fb), up.reshape(rows, Fb)
    (act,) = _rowcall(f"{tag}_act", lambda i, g, u: (_swiglu_act(g, u),), rows, tma,
                      [(g2, 0, 0, Fb), (u2, 0, 0, Fb)], [], [(Fb, BF16)])
    act = act.reshape(J, Tp, Fb)
    h_out = _mm_red(f"{tag}_down", [(act, wd)], NN, F32, scale=0.5, res=h, jb=2, riders=tail_riders)
    tail = []
    if tail_riders:
        h_out, tail = h_out
    return h_out, (n, gate, up, act), wu, wd, tail


def _ffn_bwd(tag, dh, dh16, h, gain, wg, wu, wd, saved, Tp, red):
    n, gate, up, act = saved
    D = h.shape[1]
    J, Fb, _ = wg.shape

    def carried(fn, *args, ici=1, d2d=1, **kw):
        riders, absorb = red.take(ici, d2d)
        if not riders:
            return fn(*args, **kw)
        out, routs = fn(*args, riders=riders, **kw)
        absorb(routs)
        return out

    dact = carried(_mm_bcast, f"{tag}_dact", dh16, wd, NT, False, BF16)
    tk_w = Tp // 2 if Tp % 32 == 0 else Tp
    red.push(f"{tag}_w_down", carried(_mm_tn, f"{tag}_dwd", act, dh16, J, Fb, D, BF16, scale=0.5, tk_target=tk_w))
    rows = J * Tp
    tma = _row_tile(rows, Fb, 2, 1024 * 1024)

    def act_bwd(i, g, u, d):
        _, vjp = jax.vjp(_swiglu_act, g, u)
        return vjp(0.5 * d.astype(F32))

    dgate, dup = _rowcall(f"{tag}_act_bwd", act_bwd, rows, tma,
                          [(gate.reshape(rows, Fb), 0, 0, Fb), (up.reshape(rows, Fb), 0, 0, Fb),
                           (dact.reshape(rows, Fb), 0, 0, Fb)], [], [(Fb, BF16), (Fb, BF16)])
    dgate, dup = dgate.reshape(J, Tp, Fb), dup.reshape(J, Tp, Fb)
    red.push(f"{tag}_w_gate", carried(_mm_tn, f"{tag}_dwg", dgate, n, J, Fb, D, BF16, tk_target=tk_w))
    red.push(f"{tag}_w_up", carried(_mm_tn, f"{tag}_dwu", dup, n, J, Fb, D, BF16, tk_target=tk_w))
    half = D // 2
    dn = [carried(_mm_red, f"{tag}_dn{part}", [(dgate, wg), (dup, wu)], NN, F32, n_off=part * half, n_len=half)
          for part in range(2)]
    return _norm_bwd(f"{tag}_norm_bwd", h, gain, dn, dh, Tp)


def _norm_bwd(name, h, gain, dn_parts, dres, Tp):
    D = h.shape[1]
    n_parts = len(dn_parts)

    def fn(i, x, r, *rest):
        d = jnp.concatenate([p.astype(F32) for p in rest[:n_parts]], axis=1) if n_parts > 1 else rest[0].astype(F32)
        _, vjp = jax.vjp(_rmsnorm, x, rest[n_parts])
        dx, dg = vjp(d)
        return r + dx, r + dx, dg

    rows = [(h, 0, 0, D), (dres, 0, 0, D)] + [(p, 0, 0, p.shape[1]) for p in dn_parts]
    return _rowcall(name, fn, Tp, _row_tile(Tp, D), rows, [gain], [(D, F32), (D, BF16)], [(1, D)])


def kernel(x, meta_tokens, ffn1_norm, ffn1_w_gate, ffn1_w_up, ffn1_w_down, mix_norm, w_in, rg_conv_w, rg_conv_b, rg_w_a, rg_b_a, rg_w_x, rg_b_x, rg_lambda, s5_lambda_re, s5_lambda_im, s5_log_dt, s5_b_re, s5_b_im, s5_c_re, s5_c_im, s5_d, s5_glu_w, s5_glu_b, rg_out_norm, s5_out_norm, w_out, ffn2_norm, ffn2_w_gate, ffn2_w_up, ffn2_w_down, final_norm, loss_target, m_meta_tokens, m_ffn1_norm, m_ffn1_w_gate, m_ffn1_w_up, m_ffn1_w_down, m_mix_norm, m_w_in, m_rg_conv_w, m_rg_conv_b, m_rg_w_a, m_rg_b_a, m_rg_w_x, m_rg_b_x, m_rg_lambda, m_s5_lambda_re, m_s5_lambda_im, m_s5_log_dt, m_s5_b_re, m_s5_b_im, m_s5_c_re, m_s5_c_im, m_s5_d, m_s5_glu_w, m_s5_glu_b, m_rg_out_norm, m_s5_out_norm, m_w_out, m_ffn2_norm, m_ffn2_w_gate, m_ffn2_w_up, m_ffn2_w_down, m_final_norm, v_meta_tokens, v_ffn1_norm, v_ffn1_w_gate, v_ffn1_w_up, v_ffn1_w_down, v_mix_norm, v_w_in, v_rg_conv_w, v_rg_conv_b, v_rg_w_a, v_rg_b_a, v_rg_w_x, v_rg_b_x, v_rg_lambda, v_s5_lambda_re, v_s5_lambda_im, v_s5_log_dt, v_s5_b_re, v_s5_b_im, v_s5_c_re, v_s5_c_im, v_s5_d, v_s5_glu_w, v_s5_glu_b, v_rg_out_norm, v_s5_out_norm, v_w_out, v_ffn2_norm, v_ffn2_w_gate, v_ffn2_w_up, v_ffn2_w_down, v_final_norm):
    weights = dict(
        meta_tokens=meta_tokens, ffn1_norm=ffn1_norm, ffn1_w_gate=ffn1_w_gate, ffn1_w_up=ffn1_w_up, ffn1_w_down=ffn1_w_down,
        mix_norm=mix_norm, w_in=w_in, rg_conv_w=rg_conv_w, rg_conv_b=rg_conv_b, rg_w_a=rg_w_a, rg_b_a=rg_b_a, rg_w_x=rg_w_x,
        rg_b_x=rg_b_x, rg_lambda=rg_lambda, s5_lambda_re=s5_lambda_re, s5_lambda_im=s5_lambda_im, s5_log_dt=s5_log_dt,
        s5_b_re=s5_b_re, s5_b_im=s5_b_im, s5_c_re=s5_c_re, s5_c_im=s5_c_im, s5_d=s5_d, s5_glu_w=s5_glu_w, s5_glu_b=s5_glu_b,
        rg_out_norm=rg_out_norm, s5_out_norm=s5_out_norm, w_out=w_out, ffn2_norm=ffn2_norm, ffn2_w_gate=ffn2_w_gate,
        ffn2_w_up=ffn2_w_up, ffn2_w_down=ffn2_w_down, final_norm=final_norm)
    moments_m = dict(
        meta_tokens=m_meta_tokens, ffn1_norm=m_ffn1_norm, ffn1_w_gate=m_ffn1_w_gate, ffn1_w_up=m_ffn1_w_up,
        ffn1_w_down=m_ffn1_w_down, mix_norm=m_mix_norm, w_in=m_w_in, rg_conv_w=m_rg_conv_w, rg_conv_b=m_rg_conv_b,
        rg_w_a=m_rg_w_a, rg_b_a=m_rg_b_a, rg_w_x=m_rg_w_x, rg_b_x=m_rg_b_x, rg_lambda=m_rg_lambda,
        s5_lambda_re=m_s5_lambda_re, s5_lambda_im=m_s5_lambda_im, s5_log_dt=m_s5_log_dt, s5_b_re=m_s5_b_re,
        s5_b_im=m_s5_b_im, s5_c_re=m_s5_c_re, s5_c_im=m_s5_c_im, s5_d=m_s5_d, s5_glu_w=m_s5_glu_w, s5_glu_b=m_s5_glu_b,
        rg_out_norm=m_rg_out_norm, s5_out_norm=m_s5_out_norm, w_out=m_w_out, ffn2_norm=m_ffn2_norm,
        ffn2_w_gate=m_ffn2_w_gate, ffn2_w_up=m_ffn2_w_up, ffn2_w_down=m_ffn2_w_down, final_norm=m_final_norm)
    moments_v = dict(
        meta_tokens=v_meta_tokens, ffn1_norm=v_ffn1_norm, ffn1_w_gate=v_ffn1_w_gate, ffn1_w_up=v_ffn1_w_up,
        ffn1_w_down=v_ffn1_w_down, mix_norm=v_mix_norm, w_in=v_w_in, rg_conv_w=v_rg_conv_w, rg_conv_b=v_rg_conv_b,
        rg_w_a=v_rg_w_a, rg_b_a=v_rg_b_a, rg_w_x=v_rg_w_x, rg_b_x=v_rg_b_x, rg_lambda=v_rg_lambda,
        s5_lambda_re=v_s5_lambda_re, s5_lambda_im=v_s5_lambda_im, s5_log_dt=v_s5_log_dt, s5_b_re=v_s5_b_re,
        s5_b_im=v_s5_b_im, s5_c_re=v_s5_c_re, s5_c_im=v_s5_c_im, s5_d=v_s5_d, s5_glu_w=v_s5_glu_w, s5_glu_b=v_s5_glu_b,
        rg_out_norm=v_rg_out_norm, s5_out_norm=v_s5_out_norm, w_out=v_w_out, ffn2_norm=v_ffn2_norm,
        ffn2_w_gate=v_ffn2_w_gate, ffn2_w_up=v_ffn2_w_up, ffn2_w_down=v_ffn2_w_down, final_norm=v_final_norm)
    order = list(weights)

    seq, D = x.shape[1], x.shape[2]
    R = rg_conv_b.shape[1]
    S = s5_d.shape[1]
    G, N, C = s5_b_re.shape[1:]
    heads, hd = rg_w_a.shape[1], rg_w_a.shape[3]
    Q = G // GROUPS_PER_BLOCK
    W = G * N
    NB = GROUPS_PER_BLOCK * N
    T = N_META + seq
    Tp = -(-T // LANES) * LANES
    me = 4 * lax.axis_index("x") + 2 * lax.axis_index("y") + lax.axis_index("c")
    c_idx = lax.axis_index("c").astype(jnp.int32).reshape(1)

    big = ["ffn1_w_gate", "ffn1_w_up", "ffn1_w_down", "w_in", "s5_glu_w", "w_out", "ffn2_w_gate", "ffn2_w_up", "ffn2_w_down"]
    transposed = ["ffn1_w_gate", "ffn1_w_up", "ffn2_w_gate", "ffn2_w_up"]

    def as_held(k, a):
        return jnp.swapaxes(a, 1, 2) if k in transposed else a

    shard16 = {k: as_held(k, weights[k])[0].astype(BF16) for k in big}
    full = {}
    sharded_small = ["meta_tokens", "rg_conv_w", "rg_w_a", "rg_b_a", "rg_w_x", "rg_b_x"]
    sm, full["ffn1_w_gate"] = _comm_call(
        "ag_first", [_ag_rider(_pack([weights[k] for k in sharded_small])), _ag_rider(shard16["ffn1_w_gate"])])
    sm = [jnp.stack(p) for p in zip(*[_unpack(sm[d], [weights[k].shape for k in sharded_small]) for d in range(N_DEV)])]
    meta_full = jnp.moveaxis(sm[0], 0, 1).reshape(N_META, D)
    conv_w_full = jnp.moveaxis(sm[1][:, 0], 0, 1).reshape(CONV_WIDTH, R)
    wa_full = jnp.moveaxis(sm[2][:, 0], 0, 1).reshape(heads, hd, hd)
    ba_full = jnp.moveaxis(sm[3][:, 0], 0, 1).reshape(1, R)
    wx_full = jnp.moveaxis(sm[4][:, 0], 0, 1).reshape(heads, hd, hd)
    bx_full = jnp.moveaxis(sm[5][:, 0], 0, 1).reshape(1, R)

    lam_fn = functools.partial(_s5_discretise)
    (lr, li, bbr, bbi), disc_vjp = jax.vjp(lam_fn, s5_lambda_re[0], s5_lambda_im[0], s5_log_dt[0], s5_b_re[0], s5_b_im[0])
    lr_row, li_row = lr.reshape(1, W), li.reshape(1, W)
    wb = jnp.concatenate([_expand_diag(jnp.swapaxes(bbr, 1, 2), True),
                          _expand_diag(jnp.swapaxes(bbi, 1, 2), True)], axis=2)
    wc = jnp.concatenate([_expand_diag(jnp.swapaxes(s5_c_re[0], 1, 2), True),
                          _expand_diag(-jnp.swapaxes(s5_c_im[0], 1, 2), True)], axis=1)

    h0 = jnp.concatenate([meta_full, x[0], jnp.zeros((Tp - T, D), F32)], axis=0)
    tgt = jnp.concatenate([jnp.zeros((N_META, D), F32), loss_target[0], jnp.zeros((Tp - T, D), F32)], axis=0)

    h1, ffn1_saved, full["ffn1_w_up"], full["ffn1_w_down"], (full["w_in"],) = _ffn_fwd(
        "ffn1", h0, ffn1_norm, full["ffn1_w_gate"], shard16["ffn1_w_up"], shard16["ffn1_w_down"],
        [_ag_rider(shard16["w_in"])], Tp)

    tmd = _row_tile(Tp, D)
    (n2,) = _rowcall("mix_norm", lambda i, a, g: (_rmsnorm(a, g),), Tp, tmd, [(h1, 0, 0, D)], [mix_norm], [(D, BF16)])
    proj, (full["w_out"], full["s5_glu_w"]) = _mm_bcast(
        "w_in", n2, full["w_in"], NN, True, F32, riders=[_ag_rider(shard16["w_out"]), _ag_rider(shard16["s5_glu_w"])])

    tmr = _row_tile(Tp, R)
    tb = _pick(Tp, 768, SUBLANES)
    xc = _conv_fwd(proj, conv_w_full, rg_conv_b, R, tmr)
    a_t, b_t = _rowcall("rg_gates", lambda i, *a: _rg_gates(*a), Tp, tmr, [(xc, 0, 0, R)],
                        [wa_full, ba_full, wx_full, bx_full, rg_lambda], [(R, F32), (R, F32)])
    h_rg, hprev = _rg_scan_fwd(a_t, b_t, tb, _pick(R, 512, LANES))
    (yn_rg,) = _rowcall("rg_out", lambda i, *a: (_rg_out(*a),), Tp, tmr, [(h_rg, 0, 0, R), (proj, 0, 1, R)],
                        [rg_out_norm], [(R, BF16)])

    u_off = 2 * R // LANES
    bu = _mm_bdiag("s5_bu", [(proj, wb, u_off)], NN, LANES, 2 * NB, F32)
    xs, xps, (full["ffn2_w_gate"],) = _s5_scan_fwd(
        bu, _scan_tables(lr_row, li_row, False), tb, NB, riders=[_ag_rider(shard16["ffn2_w_gate"])])
    y_s5 = _mm_bdiag("s5_y", [(xs, wc, 0)], NN, 2 * NB, LANES, F32)
    tms = _row_tile(Tp, S)
    s_col = 2 * R // S
    (z,) = _rowcall("s5_pre", lambda i, *a: (_s5_pre(*a),), Tp, tms, [(y_s5, 0, 0, S), (proj, 0, s_col, S)], [s5_d],
                    [(S, F32)])
    gw = full["s5_glu_w"]
    zz = _mm_red("s5_glu", [(z, gw)], NN, F32, jb=N_DEV)
    (yn_s5,) = _rowcall("s5_out", lambda i, *a: (_s5_out(*a),), Tp, tms, [(z, 0, 0, S), (zz, 0, 0, S)],
                        [s5_glu_b, s5_out_norm], [(S, BF16)])
    yn = jnp.concatenate([yn_rg, yn_s5], axis=1)
    h2 = _mm_red("w_out", [(yn, full["w_out"])], NN, F32, res=h1, jb=4)

    h3, ffn2_saved, full["ffn2_w_up"], full["ffn2_w_down"], _ = _ffn_fwd(
        "ffn2", h2, ffn2_norm, full["ffn2_w_gate"], shard16["ffn2_w_up"], shard16["ffn2_w_down"], [], Tp)

    def final(i, hh, tt, g):
        out, vjp = jax.vjp(_rmsnorm, hh, g)
        row = i * tmd + lax.broadcasted_iota(jnp.int32, (tmd, 1), 0)
        valid = jnp.logical_and(row >= N_META, row < T)
        err = jnp.where(valid, out - tt, 0.0)
        part = 0.5 * jnp.sum(jnp.mean(err * err, axis=-1, keepdims=True))
        dx, dg = vjp(err * (1.0 / D))
        return dx, dx, dg, jnp.full((SUBLANES, LANES), part, F32)

    dh3, dh3_16, d_final_norm, loss_part = _rowcall(
        "final", final, Tp, tmd, [(h3, 0, 0, D), (tgt, 0, 0, D)], [final_norm.reshape(1, D)],
        [(D, F32), (D, BF16)], [(1, D), (SUBLANES, LANES)])
    loss = lax.psum(loss_part[0, 0], ("x", "y", "c"))

    grads = {}
    red = _GradReducer(c_idx)

    def carried(fn, *args, ici=1, d2d=1, extra=(), **kw):
        riders, absorb = red.take(ici, d2d)
        if not riders and not extra:
            return fn(*args, **kw), []
        out, routs = fn(*args, riders=riders + list(extra), **kw)
        absorb(routs[:len(riders)])
        return out, routs[len(riders):]

    dh2, dh2_16, grads["ffn2_norm"] = _ffn_bwd(
        "ffn2", dh3, dh3_16, h2, ffn2_norm, full["ffn2_w_gate"], full["ffn2_w_up"], full["ffn2_w_down"], ffn2_saved,
        Tp, red)

    Kb_out = full["w_out"].shape[1]
    dyn = _mm_bcast("w_out_dx", dh2_16, full["w_out"], NT, True, F32)
    red.push("w_out", carried(_mm_tn, "w_out_dw", yn, dh2_16, N_DEV, Kb_out, D, BF16)[0])

    def s5_out_bwd(i, zv, zzv, d, gb, gn):
        _, vjp = jax.vjp(_s5_out, zv, zzv, gb, gn)
        return vjp(d)

    dz_a, dzz, grads["s5_glu_b"], grads["s5_out_norm"] = _rowcall(
        "s5_out_bwd", s5_out_bwd, Tp, tms, [(z, 0, 0, S), (zz, 0, 0, S), (dyn, 0, R // S, S)],
        [s5_glu_b, s5_out_norm], [(S, F32), (S, BF16)], [(1, S), (1, S)])
    Kb_glu = gw.shape[1]
    dz = _mm_bcast("s5_glu_dx", dzz, gw, NT, True, F32)
    red.push("s5_glu_w", carried(_mm_tn, "s5_glu_dw", z, dzz, N_DEV, Kb_glu, S, BF16, ici=0)[0])

    def s5_pre_bwd(i, yv, uv, d1, d2, dd):
        _, vjp = jax.vjp(_s5_pre, yv, uv, dd)
        return vjp(d1 + d2)

    dy, du_a, grads["s5_d"] = _rowcall(
        "s5_pre_bwd", s5_pre_bwd, Tp, tms, [(y_s5, 0, 0, S), (proj, 0, s_col, S), (dz_a, 0, 0, S), (dz, 0, 0, S)],
        [s5_d], [(S, F32), (S, F32)], [(1, S)])
    gxs = _mm_bdiag("s5_dx", [(dy, wc, 0)], NT, LANES, 2 * NB, F32)
    d_wc = _mm_tn("s5_dc", xs, dy, Q, 2 * NB, LANES, F32)
    d_wc_r, d_wc_i = d_wc[:, :NB], d_wc[:, NB:]
    scan_riders, scan_absorb = red.take()
    adj, dlr8, dli8, scan_routs = _s5_scan_bwd(
        gxs, xps, _scan_tables(lr_row, -li_row, True), tb, NB, riders=scan_riders)
    scan_absorb(scan_routs)
    du_s5 = _mm_bdiag("s5_du", [(adj, wb, 0)], NT, 2 * NB, LANES, BF16, res=(du_a, 0))
    d_wb = carried(_mm_tn, "s5_db", proj, adj, Q, LANES, 2 * NB, F32, a_off=u_off, d2d=0)[0]
    d_bbr = jnp.swapaxes(_extract_diag(d_wb[:, :, :NB], C, N), 1, 2)
    d_bbi = jnp.swapaxes(_extract_diag(d_wb[:, :, NB:], C, N), 1, 2)
    d_lr = jnp.sum(dlr8, axis=0).reshape(G, N)
    d_li = jnp.sum(dli8, axis=0).reshape(G, N)
    d_lre, d_lim, d_logdt, d_bre, d_bim = disc_vjp((d_lr, d_li, d_bbr, d_bbi))
    grads["s5_lambda_re"], grads["s5_lambda_im"], grads["s5_log_dt"] = d_lre[None], d_lim[None], d_logdt[None]
    grads["s5_b_re"], grads["s5_b_im"] = d_bre[None], d_bim[None]
    grads["s5_c_re"] = jnp.swapaxes(_extract_diag(d_wc_r, N, C), 1, 2)[None]
    grads["s5_c_im"] = -jnp.swapaxes(_extract_diag(d_wc_i, N, C), 1, 2)[None]

    def rg_out_bwd(i, hv, gv, d, gn):
        _, vjp = jax.vjp(_rg_out, hv, gv, gn)
        return vjp(d)

    dh_scan, dg_rg, grads["rg_out_norm"] = _rowcall(
        "rg_out_bwd", rg_out_bwd, Tp, tmr, [(h_rg, 0, 0, R), (proj, 0, 1, R), (dyn, 0, 0, R)], [rg_out_norm],
        [(R, F32), (R, BF16)], [(1, R)])
    db_t, da_t = _rg_scan_bwd(dh_scan, a_t, hprev, tb, _pick(R, 512, LANES))

    def rg_gates_bwd(i, xv, da, db, wa, ba, wx, bx, lam):
        _, vjp = jax.vjp(_rg_gates, xv, wa, ba, wx, bx, lam)
        return vjp((da, db))

    dxc, d_wa, d_ba, d_wx, d_bx, grads["rg_lambda"] = _rowcall(
        "rg_gates_bwd", rg_gates_bwd, Tp, _row_tile(Tp, R, 4, 1024 * 1024),
        [(xc, 0, 0, R), (da_t, 0, 0, R), (db_t, 0, 0, R)], [wa_full, ba_full, wx_full, bx_full, rg_lambda],
        [(R, F32)], [(heads, hd, hd), (1, R), (heads, hd, hd), (1, R), (1, R)])
    du_rg, d_conv_w, grads["rg_conv_b"] = _conv_bwd(dxc, proj, conv_w_full, R, tmr)

    grads["final_norm"] = d_final_norm.reshape(D)
    late = ["ffn1_norm", "mix_norm"]
    replicated = ["ffn1_norm", "mix_norm", "rg_conv_b", "rg_lambda", "s5_lambda_re", "s5_lambda_im", "s5_log_dt",
                  "s5_b_re", "s5_b_im", "s5_c_re", "s5_c_im", "s5_d", "s5_glu_b", "rg_out_norm", "s5_out_norm",
                  "ffn2_norm", "final_norm"]
    early = [k for k in replicated if k not in late]
    early_full = [grads[k].reshape(weights[k].shape) for k in early] + [d_conv_w, d_wa, d_ba, d_wx, d_bx]

    dproj = jnp.concatenate([du_rg.astype(BF16), dg_rg, du_s5], axis=1)
    win = full["w_in"]
    Nb_in = win.shape[2]
    dn2, (early_gathered,) = carried(_mm_red, "w_in_dx", [(dproj, win)], NT, F32, jb=4,
                                     extra=[_ag_rider(_pack(early_full))])
    red.push("w_in", carried(_mm_tn, "w_in_dw", n2, dproj, N_DEV, D, Nb_in, BF16)[0])
    dh1, dh1_16, grads["mix_norm"] = _norm_bwd("mix_norm_bwd", h1, mix_norm, [dn2], dh2, Tp)

    dh0, _, grads["ffn1_norm"] = _ffn_bwd(
        "ffn1", dh1, dh1_16, h0, ffn1_norm, full["ffn1_w_gate"], full["ffn1_w_up"], full["ffn1_w_down"], ffn1_saved,
        Tp, red)
    grad_x = dh0[N_META:T][None]

    reduced = red.flush()
    outs = {}
    for k in big:
        r2 = reduced[k]
        _, Rk, Ck = r2.shape
        w2, m2, v2 = (as_held(k, a) for a in (weights[k], moments_m[k], moments_v[k]))

        def big_update(i, wv, mv, vv, p0, p1, p2, p3):
            gsum = (p0.astype(F32) + p1.astype(F32)) + (p2.astype(F32) + p3.astype(F32))
            return (gsum,) + _adamw(wv, gsum, mv, vv)

        res = _rowcall(f"adam_{k}", big_update, Rk, _row_tile(Rk, Ck, 4, 1024 * 1024),
                       [(w2, 0, 0, Ck), (m2, 0, 0, Ck), (v2, 0, 0, Ck)] + [(r2, kk, 0, Ck) for kk in range(N_CHIP)],
                       [], [(Ck, F32)] * 4)
        outs[k] = [as_held(k, o[None]) for o in res]

    late_full = [grads[k].reshape(weights[k].shape) for k in late] + [dh0[:N_META]]
    late_gathered = _all_gather("ag_late_grads", _pack(late_full))

    def sum8(i, *parts):
        s = parts[0]
        for q in parts[1:]:
            s = s + q
        return (s,)

    def sum_devices(name, gathered, shapes):
        n_rows = gathered.shape[1]
        (summed,) = _rowcall(name, sum8, n_rows, _pick(n_rows, 4096, SUBLANES),
                             [(gathered, d, 0, LANES) for d in range(N_DEV)], [], [(LANES, F32)])
        return _unpack(summed, shapes)

    early_sum = sum_devices("small_sum_early", early_gathered, [a.shape for a in early_full])
    late_sum = sum_devices("small_sum_late", late_gathered, [a.shape for a in late_full])
    g_small = dict(zip(early, early_sum[:len(early)]))
    g_small.update(zip(late, late_sum[:len(late)]))
    d_cw, d_wa_s, d_ba_s, d_wx_s, d_bx_s = early_sum[len(early):]
    d_meta = late_sum[len(late)]

    def shard_of(a, axis):
        n = a.shape[axis] // N_DEV
        return lax.dynamic_slice_in_dim(a, me * n, n, axis)

    g_small["meta_tokens"] = shard_of(d_meta, 1)
    g_small["rg_conv_w"] = shard_of(d_cw, 1)[None]
    g_small["rg_w_a"] = shard_of(d_wa_s, 1)[None]
    g_small["rg_b_a"] = shard_of(d_ba_s.reshape(heads, hd), 1)[None]
    g_small["rg_w_x"] = shard_of(d_wx_s, 1)[None]
    g_small["rg_b_x"] = shard_of(d_bx_s.reshape(heads, hd), 1)[None]
    small = replicated + sharded_small
    shapes = [weights[k].shape for k in small]
    gp = _pack([g_small[k] for k in small])
    n_rows = gp.shape[0]

    def small_update(i, wv, gv, mv, vv):
        return _adamw(wv, gv, mv, vv)

    res = _rowcall("adam_small", small_update, n_rows, _pick(n_rows, 512, SUBLANES),
                   [(_pack([weights[k] for k in small]), 0, 0, LANES), (gp, 0, 0, LANES),
                    (_pack([moments_m[k] for k in small]), 0, 0, LANES), (_pack([moments_v[k] for k in small]), 0, 0, LANES)],
                   [], [(LANES, F32)] * 3)
    unpacked = [_unpack(r, shapes) for r in res]
    for idx, k in enumerate(small):
        outs[k] = [g_small[k].reshape(weights[k].shape)] + [u[idx] for u in unpacked]

    return (loss, grad_x, *[outs[k][0] for k in order], *[outs[k][1] for k in order],
            *[outs[k][2] for k in order], *[outs[k][3] for k in order])
```

```python
import functools
import math

import jax
import jax.numpy as jnp
from jax import lax
from jax.experimental import pallas as pl
from jax.experimental.pallas import tpu as pltpu

F32, BF16 = jnp.float32, jnp.bfloat16
SDS = jax.ShapeDtypeStruct
MESH_ID = pl.DeviceIdType.MESH
N_DEV = 8
N_CHIP = 4
LANES = 128
SUBLANES = 8
VMEM_LIMIT = 56 * 1024 * 1024

EPS = 1e-6
RG_C = 8.0
N_META = 16
CONV_WIDTH = 4
S5_GROUP = 16
S5_STATE = 64
GROUPS_PER_BLOCK = LANES // S5_GROUP
ADAM_LR, ADAM_B1, ADAM_B2, ADAM_EPS, ADAM_WD, ADAM_STEP = 0.001, 0.9, 0.999, 1e-08, 0.01, 10

NN = (((1,), (0,)), ((), ()))
NT = (((1,), (1,)), ((), ()))
TN = (((0,), (0,)), ((), ()))


def _pick(n, target, mult=16):
    if n <= target:
        return n
    best = None
    for d in range(mult, target + 1, mult):
        if n % d == 0:
            best = d
    assert best is not None, (n, target, mult)
    return best


def _row_tile(nrows, ncols, itembytes=4, budget=2 * 1024 * 1024):
    return _pick(nrows, max(16, budget // (ncols * itembytes)))


def _params(sem):
    return pltpu.CompilerParams(dimension_semantics=sem, vmem_limit_bytes=VMEM_LIMIT)


def _rowcall(name, fn, nrows, tm, rows, fulls, row_outs, acc_outs=()):
    n_in = len(rows) + len(fulls)
    in_specs = []
    for arr, lead, cb, C in rows:
        if arr.ndim == 3:
            in_specs.append(pl.BlockSpec((None, tm, C), lambda i, lead=lead, cb=cb: (lead, i, cb)))
        else:
            in_specs.append(pl.BlockSpec((tm, C), lambda i, cb=cb: (i, cb)))
    for f in fulls:
        in_specs.append(pl.BlockSpec(f.shape, lambda i, nd=f.ndim: (0,) * nd))
    out_specs = [pl.BlockSpec((tm, C), lambda i: (i, 0)) for C, _ in row_outs]
    out_shape = [SDS((nrows, C), dt) for C, dt in row_outs]
    for shp in acc_outs:
        out_specs.append(pl.BlockSpec(shp, lambda i, nd=len(shp): (0,) * nd))
        out_shape.append(SDS(shp, F32))
    n_row_out = len(row_outs)

    def body(*refs):
        i = pl.program_id(0)
        res = fn(i, *[r[...] for r in refs[:n_in]])
        outs = refs[n_in:]
        for k in range(n_row_out):
            outs[k][...] = res[k].astype(outs[k].dtype)
        if acc_outs:
            @pl.when(i == 0)
            def _():
                for o in outs[n_row_out:]:
                    o[...] = jnp.zeros_like(o)
            for k in range(n_row_out, len(outs)):
                outs[k][...] += res[k].astype(F32)

    return pl.pallas_call(
        body, grid=(nrows // tm,), in_specs=in_specs, out_specs=out_specs, out_shape=out_shape,
        name=name, compiler_params=_params(("arbitrary",)))(*[r[0] for r in rows], *fulls)


class _Rider:
    def __init__(self, ins, outs, sems, steps):
        self.ins, self.outs, self.sems, self.steps = list(ins), list(outs), list(sems), steps


def _rider_counts(riders):
    return (sum(len(r.ins) for r in riders), sum(len(r.outs) for r in riders), sum(len(r.sems) for r in riders))


def _rider_hooks(riders, in_refs, out_refs, sem_refs):
    hooks, i, o, s = [], 0, 0, 0
    for r in riders:
        hooks.append(r.steps(*in_refs[i:i + len(r.ins)], *out_refs[o:o + len(r.outs)], *sem_refs[s:s + len(r.sems)]))
        i, o, s = i + len(r.ins), o + len(r.outs), s + len(r.sems)
    return hooks


def _linear_step(grid):
    step = pl.program_id(0)
    for ax in range(1, len(grid)):
        step = step * grid[ax] + pl.program_id(ax)
    return step


def _ride_begin(step, hooks):
    if hooks:
        @pl.when(step == 0)
        def _():
            for start, _, _ in hooks:
                start()


def _ride_end(step, nsteps, hooks):
    if hooks:
        mids = sorted(((min(int(frac * nsteps), nsteps - 1), k, fn) for k, (_, mid, _) in enumerate(hooks)
                       for frac, fn in (mid or ())), key=lambda m: m[:2])
        for at in sorted({m[0] for m in mids}):
            @pl.when(step == at)
            def _(at=at):
                for s, _, fn in mids:
                    if s == at:
                        fn()

        @pl.when(step == nsteps - 1)
        def _():
            for _, _, finish in hooks:
                finish()


def _mm(name, terms, dims, grid, nk, out_shape, out_spec, acc_shape, scale=1.0, res=None, riders=()):
    n_t = len(terms)
    kax = len(grid) - 1
    n_in = 2 * n_t + (1 if res is not None else 0)
    r_in, r_out, _ = _rider_counts(riders)
    nsteps = math.prod(grid)
    n_acc = 1 if nk > 1 else 0

    def body(*refs):
        ins, rin = refs[:n_in], refs[n_in:n_in + r_in]
        o_ref = refs[n_in + r_in]
        rout = refs[n_in + r_in + 1:n_in + r_in + 1 + r_out]
        scratch = refs[n_in + r_in + 1 + r_out:]
        step = _linear_step(grid)
        k = pl.program_id(kax)
        hooks = _rider_hooks(riders, rin, rout, scratch[n_acc:])
        _ride_begin(step, hooks)

        def product():
            r = None
            for t in range(n_t):
                d = lax.dot_general(ins[2 * t][...].astype(BF16), ins[2 * t + 1][...].astype(BF16), dims,
                                    preferred_element_type=F32)
                r = d if r is None else r + d
            return r

        def emit(r):
            r = r * scale
            if res is not None:
                r = r + ins[2 * n_t][...].astype(F32)
            o_ref[...] = r.astype(o_ref.dtype)

        if nk == 1:
            emit(product())
        else:
            acc = scratch[0]

            @pl.when(k == 0)
            def _():
                acc[...] = jnp.zeros_like(acc)

            acc[...] += product()

            @pl.when(k == nk - 1)
            def _():
                emit(acc[...])

        _ride_end(step, nsteps, hooks)

    ops, specs = [], []
    for a, a_spec, b, b_spec in terms:
        ops += [a, b]
        specs += [a_spec, b_spec]
    if res is not None:
        ops.append(res[0])
        specs.append(res[1])
    out_shapes, out_specs, scratch = [out_shape], [out_spec], []
    if n_acc:
        scratch.append(pltpu.VMEM(acc_shape, F32))
    for r in riders:
        ops += r.ins
        specs += [_ANY] * len(r.ins)
        out_shapes += r.outs
        out_specs += [_ANY] * len(r.outs)
        scratch += r.sems
    sem = ("arbitrary",) * len(grid)
    outs = pl.pallas_call(
        body, grid=grid, in_specs=specs, out_specs=out_specs, out_shape=out_shapes,
        scratch_shapes=scratch, name=name, compiler_params=_params(sem))(*ops)
    return (outs[0], list(outs[1:])) if riders else outs[0]


def _mm_bcast(name, a, w, dims, std_out, out_dtype, tm_target=384, riders=()):
    M, K = a.shape
    J = w.shape[0]
    Nb = w.shape[2] if dims == NN else w.shape[1]
    tm = _pick(M, tm_target)
    a_spec = pl.BlockSpec((tm, K), lambda j, i, k: (i, 0))
    w_spec = pl.BlockSpec((None,) + w.shape[1:], lambda j, i, k: (j, 0, 0))
    if std_out:
        out_shape, out_spec = SDS((M, J * Nb), out_dtype), pl.BlockSpec((tm, Nb), lambda j, i, k: (i, j))
    else:
        out_shape, out_spec = SDS((J, M, Nb), out_dtype), pl.BlockSpec((None, tm, Nb), lambda j, i, k: (j, i, 0))
    return _mm(name, [(a, a_spec, w, w_spec)], dims, (J, M // tm, 1), 1, out_shape, out_spec, (tm, Nb), riders=riders)


def _a_blk(a, tm, Kb, off, jb, s):
    if a.ndim == 3:
        return pl.BlockSpec((None, tm, Kb), lambda i, n, j: (j * jb + s, i, 0))
    return pl.BlockSpec((tm, Kb), lambda i, n, j: (i, j * jb + s + off))


def _mm_red(name, pairs, dims, out_dtype, scale=1.0, res=None, tm_target=704, tn_target=1024, n_off=0, n_len=None,
            jb=1, riders=()):
    a0, w0 = pairs[0][0], pairs[0][1]
    J = w0.shape[0]
    M = a0.shape[-2]
    N = w0.shape[2] if dims == NN else w0.shape[1]
    Kb = w0.shape[1] if dims == NN else w0.shape[2]
    if n_len is not None:
        N = n_len
    tm, tn = _pick(M, tm_target), _pick(N, tn_target, LANES)
    nb0 = n_off // tn
    assert J % jb == 0
    terms = []
    for a, w, *rest in pairs:
        off = rest[0] if rest else 0
        for s in range(jb):
            if dims == NN:
                w_spec = pl.BlockSpec((None, Kb, tn), lambda i, n, j, s=s: (j * jb + s, 0, n + nb0))
            else:
                w_spec = pl.BlockSpec((None, tn, Kb), lambda i, n, j, s=s: (j * jb + s, n + nb0, 0))
            terms.append((a, _a_blk(a, tm, Kb, off, jb, s), w, w_spec))
    out_spec = pl.BlockSpec((tm, tn), lambda i, n, j: (i, n))
    r = None if res is None else (res, out_spec)
    return _mm(name, terms, dims, (M // tm, N // tn, J // jb), J // jb, SDS((M, N), out_dtype), out_spec, (tm, tn),
               scale, r, riders)


def _mm_tn(name, a, b, J, Ka, Nb, out_dtype, a_off=0, b_off=0, scale=1.0, tw_target=1024, tk_target=1408, riders=()):
    M = a.shape[-2]
    tk = _pick(M, tk_target)
    tka = _pick(Ka, tw_target, LANES) if Ka % LANES == 0 else Ka
    tnb = _pick(Nb, tw_target, LANES) if Nb % LANES == 0 else Nb

    def spec(arr, width, tw, is_a, off):
        nb = width // tw

        def wi(m, n):
            return m if is_a else n
        if arr.ndim == 3:
            return pl.BlockSpec((None, tk, tw), lambda j, m, n, k: (j, k, wi(m, n)))
        if arr.shape[1] == width:
            return pl.BlockSpec((tk, tw), lambda j, m, n, k: (k, wi(m, n)))
        return pl.BlockSpec((tk, tw), lambda j, m, n, k: (k, (j + off) * nb + wi(m, n)))

    a_spec = spec(a, Ka, tka, True, a_off)
    b_spec = spec(b, Nb, tnb, False, b_off)
    out_spec = pl.BlockSpec((None, tka, tnb), lambda j, m, n, k: (j, m, n))
    return _mm(name, [(a, a_spec, b, b_spec)], TN, (J, Ka // tka, Nb // tnb, M // tk), M // tk,
               SDS((J, Ka, Nb), out_dtype), out_spec, (tka, tnb), scale, riders=riders)


def _mm_bdiag(name, pairs, dims, Kb, Nb, out_dtype, res=None, tm_target=704, riders=()):
    a0, w0 = pairs[0][0], pairs[0][1]
    M, J = a0.shape[0], w0.shape[0]
    tm = _pick(M, tm_target)
    terms = []
    for a, w, off in pairs:
        a_spec = pl.BlockSpec((tm, Kb), lambda j, i, k, off=off: (i, j + off))
        w_spec = pl.BlockSpec((None,) + w.shape[1:], lambda j, i, k: (j, 0, 0))
        terms.append((a, a_spec, w, w_spec))
    out_spec = pl.BlockSpec((tm, Nb), lambda j, i, k: (i, j))
    r = None if res is None else (res[0], pl.BlockSpec((tm, Nb), lambda j, i, k, off=res[1]: (i, j + off)))
    return _mm(name, terms, dims, (J, M // tm, 1), 1, SDS((M, J * Nb), out_dtype), out_spec, (tm, Nb), 1.0, r, riders)


def _rmsnorm(x, g):
    x = x.astype(F32)
    return x * lax.rsqrt(jnp.mean(x * x, axis=-1, keepdims=True) + EPS) * g


def _swiglu_act(gate, up):
    gate, up = gate.astype(F32), up.astype(F32)
    return gate * jax.nn.sigmoid(gate) * up


def _neg_expm1(x):
    series = -x * (1.0 + x * (1.0 / 2 + x * (1.0 / 6 + x * (1.0 / 24 + x * (1.0 / 120 + x * (1.0 / 720))))))
    return jnp.where(x > -0.25, series, 1.0 - jnp.exp(x))


@jax.custom_vjp
def _dot16(x, w):
    return jnp.dot(x.astype(BF16), w.astype(BF16), preferred_element_type=F32)


def _dot16_fwd(x, w):
    return _dot16(x, w), (x, w)


def _dot16_bwd(saved, ct):
    x, w = saved
    ct16 = ct.astype(BF16)
    dx = lax.dot_general(ct16, w.astype(BF16), NT, preferred_element_type=F32)
    dw = lax.dot_general(x.astype(BF16), ct16, TN, preferred_element_type=F32)
    return dx.astype(x.dtype), dw.astype(w.dtype)


_dot16.defvjp(_dot16_fwd, _dot16_bwd)


def _rg_gates(xc, wa, ba, wx, bx, lam):
    heads, hd = wa.shape[0], wa.shape[1]
    rs, ig = [], []
    for h in range(heads):
        xh = xc[:, h * hd:(h + 1) * hd]
        rs.append(_dot16(xh, wa[h]))
        ig.append(_dot16(xh, wx[h]))
    r = jax.nn.sigmoid(jnp.concatenate(rs, axis=1) + ba)
    ii = jax.nn.sigmoid(jnp.concatenate(ig, axis=1) + bx)
    log_a = -RG_C * r * jax.nn.softplus(-lam)
    a = jnp.exp(log_a)
    mult = jnp.sqrt(_neg_expm1(2.0 * log_a))
    return a, mult * ii * xc


def _rg_out(h, g, gain):
    return _rmsnorm(h * jax.nn.gelu(g), gain)


def _s5_pre(y, u, d):
    return jax.nn.gelu(y + d * u)


def _s5_out(z, zz, glu_b, gain):
    return _rmsnorm(z * jax.nn.sigmoid(zz + glu_b), gain)


def _adamw(w, g, m, v):
    m = ADAM_B1 * m + (1.0 - ADAM_B1) * g
    v = ADAM_B2 * v + (1.0 - ADAM_B2) * jnp.square(g)
    m_hat = m / (1.0 - ADAM_B1 ** ADAM_STEP)
    v_hat = v / (1.0 - ADAM_B2 ** ADAM_STEP)
    delta = -ADAM_LR * (m_hat / (jnp.sqrt(v_hat) + ADAM_EPS) + ADAM_WD * w)
    return delta, m, v


def _conv_fwd(proj, conv_w, conv_b, R, tm):
    T = proj.shape[0]
    tpb = tm // SUBLANES

    def body(cur_ref, prev_ref, w_ref, b_ref, o_ref):
        i = pl.program_id(0)
        cur = cur_ref[...]
        prev = jnp.where(i > 0, prev_ref[...], 0.0)
        ext = jnp.concatenate([prev, cur], axis=0)
        acc = b_ref[...] + w_ref[CONV_WIDTH - 1:CONV_WIDTH, :] * cur
        for s in range(1, CONV_WIDTH):
            acc = acc + w_ref[CONV_WIDTH - 1 - s:CONV_WIDTH - s, :] * pltpu.roll(ext, s, 0)[SUBLANES:, :]
        o_ref[...] = acc

    return pl.pallas_call(
        body, grid=(T // tm,),
        in_specs=[pl.BlockSpec((tm, R), lambda i: (i, 0)),
                  pl.BlockSpec((SUBLANES, R), lambda i: (jnp.maximum(i * tpb - 1, 0), 0)),
                  pl.BlockSpec((CONV_WIDTH, R), lambda i: (0, 0)), pl.BlockSpec((1, R), lambda i: (0, 0))],
        out_specs=pl.BlockSpec((tm, R), lambda i: (i, 0)), out_shape=SDS((T, R), F32),
        name="rg_conv_fwd", compiler_params=_params(("arbitrary",)))(proj, proj, conv_w, conv_b)


def _conv_bwd(dxc, proj, conv_w, R, tm):
    T = dxc.shape[0]
    tpb = tm // SUBLANES
    nt = T // tm
    n_ext = tm + SUBLANES

    def body(d_ref, dnext_ref, u_ref, uprev_ref, w_ref, du_ref, dw_ref, db_ref):
        i = pl.program_id(0)
        d = d_ref[...]
        dnext = jnp.where(i < nt - 1, dnext_ref[...], 0.0)
        dext = jnp.concatenate([d, dnext], axis=0)
        u = u_ref[...]
        uprev = jnp.where(i > 0, uprev_ref[...], 0.0)
        uext = jnp.concatenate([uprev, u], axis=0)
        du = w_ref[CONV_WIDTH - 1:CONV_WIDTH, :] * d
        dws = [jnp.sum(d * u, axis=0, keepdims=True)]
        for s in range(1, CONV_WIDTH):
            du = du + w_ref[CONV_WIDTH - 1 - s:CONV_WIDTH - s, :] * pltpu.roll(dext, n_ext - s, 0)[:tm, :]
            dws.append(jnp.sum(d * pltpu.roll(uext, s, 0)[SUBLANES:, :], axis=0, keepdims=True))
        du_ref[...] = du

        @pl.when(i == 0)
        def _():
            dw_ref[...] = jnp.zeros_like(dw_ref)
            db_ref[...] = jnp.zeros_like(db_ref)

        dw_ref[...] += jnp.concatenate(dws[::-1], axis=0)
        db_ref[...] += jnp.sum(d, axis=0, keepdims=True)

    row = pl.BlockSpec((tm, R), lambda i: (i, 0))
    return pl.pallas_call(
        body, grid=(nt,),
        in_specs=[row, pl.BlockSpec((SUBLANES, R), lambda i: (jnp.minimum((i + 1) * tpb, T // SUBLANES - 1), 0)),
                  row, pl.BlockSpec((SUBLANES, R), lambda i: (jnp.maximum(i * tpb - 1, 0), 0)),
                  pl.BlockSpec((CONV_WIDTH, R), lambda i: (0, 0))],
        out_specs=[row, pl.BlockSpec((CONV_WIDTH, R), lambda i: (0, 0)), pl.BlockSpec((1, R), lambda i: (0, 0))],
        out_shape=[SDS((T, R), F32), SDS((CONV_WIDTH, R), F32), SDS((1, R), F32)],
        name="rg_conv_bwd", compiler_params=_params(("arbitrary",)))(dxc, dxc, proj, proj, conv_w)


def _scan_tiles(tb):
    return tb // SUBLANES


def _rg_scan_fwd(a, b, tb, cb):
    T, W = a.shape

    def body(a_ref, b_ref, h_ref, p_ref, carry):
        @pl.when(pl.program_id(1) == 0)
        def _():
            carry[...] = jnp.zeros_like(carry)

        def tile(t, h):
            ds = pl.ds(pl.multiple_of(t * SUBLANES, SUBLANES), SUBLANES)
            a8, b8 = a_ref[ds, :], b_ref[ds, :]
            hs, ps = [], []
            for j in range(SUBLANES):
                ps.append(h)
                h = a8[j:j + 1, :] * h + b8[j:j + 1, :]
                hs.append(h)
            h_ref[ds, :] = jnp.concatenate(hs, axis=0)
            p_ref[ds, :] = jnp.concatenate(ps, axis=0)
            return h

        carry[0:1, :] = lax.fori_loop(0, _scan_tiles(tb), tile, carry[0:1, :])

    blk = pl.BlockSpec((tb, cb), lambda c, i: (i, c))
    return pl.pallas_call(
        body, grid=(W // cb, T // tb), in_specs=[blk, blk], out_specs=[blk, blk],
        out_shape=[SDS((T, W), F32)] * 2, scratch_shapes=[pltpu.VMEM((SUBLANES, cb), F32)],
        name="rg_scan_fwd", compiler_params=_params(("arbitrary", "arbitrary")))(a, b)


def _rg_scan_bwd(dh, a, hprev, tb, cb):
    T, W = a.shape
    nt = T // tb

    def body(g_ref, a_ref, p_ref, db_ref, da_ref, carry):
        @pl.when(pl.program_id(1) == 0)
        def _():
            carry[...] = jnp.zeros_like(carry)

        def tile(tt, c):
            t = _scan_tiles(tb) - 1 - tt
            ds = pl.ds(pl.multiple_of(t * SUBLANES, SUBLANES), SUBLANES)
            g8, a8 = g_ref[ds, :], a_ref[ds, :]
            adjs = [None] * SUBLANES
            for j in range(SUBLANES - 1, -1, -1):
                adj = g8[j:j + 1, :] + c
                adjs[j] = adj
                c = a8[j:j + 1, :] * adj
            adj8 = jnp.concatenate(adjs, axis=0)
            db_ref[ds, :] = adj8
            da_ref[ds, :] = adj8 * p_ref[ds, :]
            return c

        carry[0:1, :] = lax.fori_loop(0, _scan_tiles(tb), tile, carry[0:1, :])

    blk = pl.BlockSpec((tb, cb), lambda c, i: (nt - 1 - i, c))
    return pl.pallas_call(
        body, grid=(W // cb, nt), in_specs=[blk, blk, blk], out_specs=[blk, blk],
        out_shape=[SDS((T, W), F32)] * 2, scratch_shapes=[pltpu.VMEM((SUBLANES, cb), F32)],
        name="rg_scan_bwd", compiler_params=_params(("arbitrary", "arbitrary")))(dh, a, hprev)


SCAN_LEVELS = (1, 2, 4)


def _scan_tables(lr, li, reverse):
    def cmul(a, b):
        return a[0] * b[0] - a[1] * b[1], a[0] * b[1] + a[1] * b[0]

    powers = [(lr, li)]
    for _ in range(SUBLANES - 1):
        powers.append(cmul(powers[-1], (lr, li)))
    row = jnp.arange(SUBLANES)[:, None]
    rows = []
    for k in SCAN_LEVELS:
        has = (row <= SUBLANES - 1 - k) if reverse else (row >= k)
        rows += [jnp.where(has, powers[k - 1][0], 0.0), jnp.where(has, powers[k - 1][1], 0.0)]
    order = list(range(SUBLANES - 1, -1, -1)) if reverse else list(range(SUBLANES))
    rows += [jnp.concatenate([powers[j][0] for j in order], axis=0), jnp.concatenate([powers[j][1] for j in order], axis=0)]
    return jnp.concatenate(rows, axis=0)


def _scan_tile(vr, vi, carry, tab_ref, reverse):
    n_rows = SUBLANES
    for lvl, k in enumerate(SCAN_LEVELS):
        mr = tab_ref[2 * lvl * n_rows:(2 * lvl + 1) * n_rows, :]
        mi = tab_ref[(2 * lvl + 1) * n_rows:(2 * lvl + 2) * n_rows, :]
        shift = n_rows - k if reverse else k
        sr, si = pltpu.roll(vr, shift, 0), pltpu.roll(vi, shift, 0)
        vr, vi = vr + mr * sr - mi * si, vi + mr * si + mi * sr
    base = 2 * len(SCAN_LEVELS) * n_rows
    pr, pi = tab_ref[base:base + n_rows, :], tab_ref[base + n_rows:base + 2 * n_rows, :]
    cr, ci = carry
    return vr + pr * cr - pi * ci, vi + pr * ci + pi * cr


def _s5_scan_fwd(bu, tab, tb, cb, riders=()):
    T, W = bu.shape[0], bu.shape[1] // 2
    grid = (W // cb, T // tb)
    r_in, r_out, _ = _rider_counts(riders)
    re, im = slice(0, cb), slice(cb, 2 * cb)

    def body(*refs):
        b_ref, tab_ref = refs[:2]
        x_ref, p_ref = refs[2 + r_in:4 + r_in]
        cr, ci = refs[4 + r_in + r_out:6 + r_in + r_out]
        step = _linear_step(grid)
        hooks = _rider_hooks(riders, refs[2:2 + r_in], refs[4 + r_in:4 + r_in + r_out], refs[6 + r_in + r_out:])
        _ride_begin(step, hooks)

        @pl.when(pl.program_id(1) == 0)
        def _():
            cr[...] = jnp.zeros_like(cr)
            ci[...] = jnp.zeros_like(ci)

        first_row = lax.broadcasted_iota(jnp.int32, (SUBLANES, cb), 0) == 0

        def tile(t, carry):
            ds = pl.ds(pl.multiple_of(t * SUBLANES, SUBLANES), SUBLANES)
            xr8, xi8 = _scan_tile(b_ref[ds, re], b_ref[ds, im], carry, tab_ref, reverse=False)
            x_ref[ds, re] = xr8
            x_ref[ds, im] = xi8
            p_ref[ds, re] = jnp.where(first_row, carry[0], pltpu.roll(xr8, 1, 0))
            p_ref[ds, im] = jnp.where(first_row, carry[1], pltpu.roll(xi8, 1, 0))
            return xr8[SUBLANES - 1:SUBLANES, :], xi8[SUBLANES - 1:SUBLANES, :]

        xr, xi = lax.fori_loop(0, _scan_tiles(tb), tile, (cr[0:1, :], ci[0:1, :]))
        cr[0:1, :] = xr
        ci[0:1, :] = xi
        _ride_end(step, math.prod(grid), hooks)

    blk = pl.BlockSpec((tb, 2 * cb), lambda c, i: (i, c))
    tabs = pl.BlockSpec((tab.shape[0], cb), lambda c, i: (0, c))
    outs = pl.pallas_call(
        body, grid=grid, in_specs=[blk, tabs] + [_ANY] * r_in, out_specs=[blk] * 2 + [_ANY] * r_out,
        out_shape=[SDS((T, 2 * W), F32)] * 2 + [o for r in riders for o in r.outs],
        scratch_shapes=[pltpu.VMEM((SUBLANES, cb), F32), pltpu.VMEM((SUBLANES, cb), F32)] + [s for r in riders for s in r.sems],
        name="s5_scan_fwd", compiler_params=_params(("arbitrary", "arbitrary")))(
            bu, tab, *[a for r in riders for a in r.ins])
    return tuple(outs[:2]) + (list(outs[2:]),)


def _s5_scan_bwd(g, xp, tab, tb, cb, riders=()):
    T, W = g.shape[0], g.shape[1] // 2
    nt = T // tb
    grid = (W // cb, nt)
    r_in, r_out, _ = _rider_counts(riders)
    re, im = slice(0, cb), slice(cb, 2 * cb)

    def body(*refs):
        g_ref, p_ref, tab_ref = refs[:3]
        a_ref, dlr_ref, dli_ref = refs[3 + r_in:6 + r_in]
        cr, ci = refs[6 + r_in + r_out:8 + r_in + r_out]
        step = _linear_step(grid)
        hooks = _rider_hooks(riders, refs[3:3 + r_in], refs[6 + r_in:6 + r_in + r_out], refs[8 + r_in + r_out:])
        _ride_begin(step, hooks)

        @pl.when(pl.program_id(1) == 0)
        def _():
            cr[...] = jnp.zeros_like(cr)
            ci[...] = jnp.zeros_like(ci)
            dlr_ref[...] = jnp.zeros_like(dlr_ref)
            dli_ref[...] = jnp.zeros_like(dli_ref)

        def tile(tt, carry):
            t = _scan_tiles(tb) - 1 - tt
            ds = pl.ds(pl.multiple_of(t * SUBLANES, SUBLANES), SUBLANES)
            ar8, ai8 = _scan_tile(g_ref[ds, re], g_ref[ds, im], carry, tab_ref, reverse=True)
            a_ref[ds, re] = ar8
            a_ref[ds, im] = ai8
            pr8, pi8 = p_ref[ds, re], p_ref[ds, im]
            dlr_ref[...] += ar8 * pr8 + ai8 * pi8
            dli_ref[...] += ai8 * pr8 - ar8 * pi8
            return ar8[0:1, :], ai8[0:1, :]

        ar, ai = lax.fori_loop(0, _scan_tiles(tb), tile, (cr[0:1, :], ci[0:1, :]))
        cr[0:1, :] = ar
        ci[0:1, :] = ai
        _ride_end(step, math.prod(grid), hooks)

    blk = pl.BlockSpec((tb, 2 * cb), lambda c, i: (nt - 1 - i, c))
    tabs = pl.BlockSpec((tab.shape[0], cb), lambda c, i: (0, c))
    acc = pl.BlockSpec((SUBLANES, cb), lambda c, i: (0, c))
    outs = pl.pallas_call(
        body, grid=grid, in_specs=[blk, blk, tabs] + [_ANY] * r_in, out_specs=[blk, acc, acc] + [_ANY] * r_out,
        out_shape=[SDS((T, 2 * W), F32)] + [SDS((SUBLANES, W), F32)] * 2 + [o for r in riders for o in r.outs],
        scratch_shapes=[pltpu.VMEM((SUBLANES, cb), F32), pltpu.VMEM((SUBLANES, cb), F32)] + [s for r in riders for s in r.sems],
        name="s5_scan_bwd", compiler_params=_params(("arbitrary", "arbitrary")))(
            g, xp, tab, *[a for r in riders for a in r.ins])
    return tuple(outs[:3]) + (list(outs[3:]),)


_ANY = pl.BlockSpec(memory_space=pl.ANY)


def _ag_steps(x_ref, out_ref, send_sems, recv_sems, local_sem):
    x, y, c = lax.axis_index("x"), lax.axis_index("y"), lax.axis_index("c")
    me, sibling = (x, y, c), (x, y, 1 - c)
    nbr_a = ((x + c) % 2, (y + 1 - c) % 2)
    nbr_b = ((x + 1 - c) % 2, (y + c) % 2)
    diag = (1 - x, 1 - y)

    def slot(px, py, pc):
        return out_ref.at[4 * px + 2 * py + pc]

    def copy(k, block, to, src=None):
        return pltpu.make_async_remote_copy(
            src_ref=slot(*block) if src is None else src, dst_ref=slot(*block),
            send_sem=send_sems.at[k], recv_sem=recv_sems.at[k], device_id=to, device_id_type=MESH_ID)

    mine = pltpu.make_async_copy(x_ref, slot(*me), local_sem)
    first = [copy(0, me, sibling, src=x_ref), copy(1, me, (*nbr_a, c), src=x_ref), copy(2, me, (*nbr_b, c), src=x_ref)]
    relay = copy(3, (*nbr_a, c), (*nbr_b, c))
    passed = [copy(4, (*nbr_a, c), sibling), copy(5, (*nbr_b, c), sibling), copy(6, (*diag, c), sibling)]

    def start():
        mine.start()
        for cp in first:
            cp.start()

    def mid():
        copy(1, (*nbr_a, c), me).wait_recv()
        relay.start()
        passed[0].start()
        copy(2, (*nbr_b, c), me).wait_recv()
        passed[1].start()

    def late():
        copy(3, (*diag, c), me).wait_recv()
        passed[2].start()

    def finish():
        copy(0, sibling, me).wait_recv()
        copy(4, (*nbr_b, 1 - c), me).wait_recv()
        copy(5, (*nbr_a, 1 - c), me).wait_recv()
        copy(6, (*diag, 1 - c), me).wait_recv()
        for cp in first + [relay] + passed:
            cp.wait_send()
        mine.wait()

    return start, [(2 / 3, mid), (0.92, late)], finish


def _rs1_steps(g_ref, r_ref, send_sems, recv_sems):
    x, y, c = lax.axis_index("x"), lax.axis_index("y"), lax.axis_index("c")
    copies = [pltpu.make_async_remote_copy(
        src_ref=g_ref.at[2 * k + (1 - c)], dst_ref=r_ref.at[k], send_sem=send_sems.at[k], recv_sem=recv_sems.at[k],
        device_id=(x, y, 1 - c), device_id_type=MESH_ID) for k in range(N_CHIP)]

    def start():
        for cp in copies:
            cp.start()

    def finish():
        for cp in copies:
            cp.wait_recv()
        for cp in copies:
            cp.wait_send()

    return start, None, finish


def _rs2_steps(p_ref, r_ref, send_sems, recv_sems, local_sem):
    x, y, c = lax.axis_index("x"), lax.axis_index("y"), lax.axis_index("c")
    myk = 2 * x + y
    chips = [(1 - x, y), (x, 1 - y), (1 - x, 1 - y)]
    mine = pltpu.make_async_copy(p_ref.at[myk], r_ref.at[myk], local_sem)
    sends = [pltpu.make_async_remote_copy(
        src_ref=p_ref.at[2 * px + py], dst_ref=r_ref.at[myk], send_sem=send_sems.at[j], recv_sem=recv_sems.at[j],
        device_id=(px, py, c), device_id_type=MESH_ID) for j, (px, py) in enumerate(chips)]

    def start():
        mine.start()
        for cp in sends:
            cp.start()

    def finish():
        for j, (px, py) in enumerate(chips):
            pltpu.make_async_remote_copy(
                src_ref=p_ref.at[myk], dst_ref=r_ref.at[2 * px + py], send_sem=send_sems.at[j], recv_sem=recv_sems.at[j],
                device_id=(px, py, c), device_id_type=MESH_ID).wait_recv()
        for cp in sends:
            cp.wait_send()
        mine.wait()

    return start, None, finish


def _dma_sems(*counts):
    return [pltpu.SemaphoreType.DMA((n,)) if n else pltpu.SemaphoreType.DMA for n in counts]


def _ag_rider(shard):
    return _Rider([shard], [SDS((N_DEV,) + shard.shape, shard.dtype)], _dma_sems(7, 7, 0), _ag_steps)


def _rs1_rider(g):
    return _Rider([g], [SDS((N_CHIP,) + g.shape[1:], g.dtype)], _dma_sems(N_CHIP, N_CHIP), _rs1_steps)


def _rs2_rider(p):
    return _Rider([p], [SDS(p.shape, p.dtype)], _dma_sems(3, 3, 0), _rs2_steps)


def _comm_call(name, riders):
    r_in, r_out, _ = _rider_counts(riders)

    def body(*refs):
        hooks = _rider_hooks(riders, refs[:r_in], refs[r_in:r_in + r_out], refs[r_in + r_out:])
        for start, _, _ in hooks:
            start()
        for _, mid, _ in hooks:
            for _, fn in (mid or ()):
                fn()
        for _, _, finish in hooks:
            finish()

    ops = [a for r in riders for a in r.ins]
    outs = pl.pallas_call(
        body, out_shape=[o for r in riders for o in r.outs], in_specs=[_ANY] * r_in, out_specs=[_ANY] * r_out,
        scratch_shapes=[s for r in riders for s in r.sems], name=name)(*ops)
    return list(outs)


def _all_gather(name, shard):
    return _comm_call(name, [_ag_rider(shard)])[0]


def _pair_add(name, g, r1, c_idx):
    _, R, C = g.shape
    tr = _row_tile(R, C, 2, 6 * 1024 * 1024)

    def body(c_ref, g_ref, r_ref, o_ref):
        o_ref[...] = (g_ref[...].astype(F32) + r_ref[...].astype(F32)).astype(o_ref.dtype)

    grid_spec = pltpu.PrefetchScalarGridSpec(
        num_scalar_prefetch=1, grid=(N_CHIP, R // tr),
        in_specs=[pl.BlockSpec((None, tr, C), lambda k, i, c_ref: (2 * k + c_ref[0], i, 0)),
                  pl.BlockSpec((None, tr, C), lambda k, i, c_ref: (k, i, 0))],
        out_specs=pl.BlockSpec((None, tr, C), lambda k, i, c_ref: (k, i, 0)))
    return pl.pallas_call(body, grid_spec=grid_spec, out_shape=SDS((N_CHIP, R, C), g.dtype), name=name,
                          compiler_params=_params(("arbitrary", "arbitrary")))(c_idx, g, r1)


class _GradReducer:
    def __init__(self, c_idx):
        self.c_idx, self.wait_d2d, self.wait_ici, self.done = c_idx, [], [], {}

    def push(self, name, g):
        self.wait_d2d.append((name, g))

    def take(self, ici=1, d2d=1):
        jobs = [("ici",) + self.wait_ici.pop(0) for _ in range(min(ici, len(self.wait_ici)))]
        jobs += [("d2d",) + self.wait_d2d.pop(0) for _ in range(min(d2d, len(self.wait_d2d)))]
        riders = [_rs2_rider(a) if kind == "ici" else _rs1_rider(a) for kind, _, a in jobs]

        def absorb(outs):
            for (kind, name, a), out in zip(jobs, outs, strict=True):
                if kind == "ici":
                    self.done[name] = out
                else:
                    self.wait_ici.append((name, _pair_add(f"add_{name}", a, out, self.c_idx)))

        return riders, absorb

    def flush(self):
        n_calls = 0
        while self.wait_d2d or self.wait_ici:
            riders, absorb = self.take(ici=len(self.wait_ici), d2d=len(self.wait_d2d))
            absorb(_comm_call(f"rs_tail_{n_calls}", riders))
            n_calls += 1
        return self.done


PACK_ROWS = 64


def _pack(arrs):
    flat = jnp.concatenate([a.reshape(-1).astype(F32) for a in arrs])
    n = flat.shape[0]
    unit = PACK_ROWS * LANES
    padded = -(-n // unit) * unit
    return jnp.pad(flat, (0, padded - n)).reshape(padded // LANES, LANES)


def _unpack(buf, shapes):
    flat = buf.reshape(-1)
    outs, off = [], 0
    for shp in shapes:
        n = math.prod(shp)
        outs.append(flat[off:off + n].reshape(shp))
        off += n
    return outs


def _s5_discretise(lre, lim, log_dt, bre, bim):
    dt = jnp.exp(log_dt)[:, None]
    e_m1 = jnp.expm1(lre * dt)
    th = lim * dt
    lr = (e_m1 + 1.0) * jnp.cos(th)
    li = (e_m1 + 1.0) * jnp.sin(th)
    lr_m1 = e_m1 * jnp.cos(th) - 2.0 * jnp.square(jnp.sin(0.5 * th))
    den = lre * lre + lim * lim
    cr = (lr_m1 * lre + li * lim) / den
    ci = (li * lre - lr_m1 * lim) / den
    bbr = cr[..., None] * bre - ci[..., None] * bim
    bbi = cr[..., None] * bim + ci[..., None] * bre
    return lr, li, bbr, bbi


def _expand_diag(m, rows_first):
    G, A, B = m.shape
    q = G // GROUPS_PER_BLOCK
    eye = jnp.eye(GROUPS_PER_BLOCK, dtype=m.dtype)
    m5 = m.reshape(q, GROUPS_PER_BLOCK, A, 1, B) * eye[None, :, None, :, None]
    return m5.reshape(q, GROUPS_PER_BLOCK * A, GROUPS_PER_BLOCK * B)


def _extract_diag(m, A, B):
    q = m.shape[0]
    m5 = m.reshape(q, GROUPS_PER_BLOCK, A, GROUPS_PER_BLOCK, B)
    d = jnp.stack([m5[:, g, :, g, :] for g in range(GROUPS_PER_BLOCK)], axis=1)
    return d.reshape(q * GROUPS_PER_BLOCK, A, B)


def _ffn_fwd(tag, h, gain, wg, wu_shard, wd_shard, tail_riders, Tp):
    D = h.shape[1]
    J, Fb, _ = wg.shape
    tmn = _row_tile(Tp, D)
    (n,) = _rowcall(f"{tag}_norm", lambda i, x, g: (_rmsnorm(x, g),), Tp, tmn, [(h, 0, 0, D)], [gain], [(D, BF16)])
    gate, (wu,) = _mm_bcast(f"{tag}_gate", n, wg, NT, False, BF16, riders=[_ag_rider(wu_shard)])
    up, (wd,) = _mm_bcast(f"{tag}_up", n, wu, NT, False, BF16, riders=[_ag_rider(wd_shard)])
    rows = J * Tp
    tma = _row_tile(rows, Fb, 2, 1024 * 1024)
    g2, u2 = gate.reshape(rows, Fb), up.reshape(rows, Fb)
    (act,) = _rowcall(f"{tag}_act", lambda i, g, u: (_swiglu_act(g, u),), rows, tma,
                      [(g2, 0, 0, Fb), (u2, 0, 0, Fb)], [], [(Fb, BF16)])
    act = act.reshape(J, Tp, Fb)
    h_out = _mm_red(f"{tag}_down", [(act, wd)], NN, F32, scale=0.5, res=h, jb=2, riders=tail_riders)
    tail = []
    if tail_riders:
        h_out, tail = h_out
    return h_out, (n, gate, up, act), wu, wd, tail


def _ffn_bwd(tag, dh, dh16, h, gain, wg, wu, wd, saved, Tp, red):
    n, gate, up, act = saved
    D = h.shape[1]
    J, Fb, _ = wg.shape

    def carried(fn, *args, ici=1, d2d=1, **kw):
        riders, absorb = red.take(ici, d2d)
        if not riders:
            return fn(*args, **kw)
        out, routs = fn(*args, riders=riders, **kw)
        absorb(routs)
        return out

    dact = carried(_mm_bcast, f"{tag}_dact", dh16, wd, NT, False, BF16)
    tk_w = Tp // 2 if Tp % 32 == 0 else Tp
    red.push(f"{tag}_w_down", carried(_mm_tn, f"{tag}_dwd", act, dh16, J, Fb, D, BF16, scale=0.5, tk_target=tk_w))
    rows = J * Tp
    tma = _row_tile(rows, Fb, 2, 1024 * 1024)

    def act_bwd(i, g, u, d):
        _, vjp = jax.vjp(_swiglu_act, g, u)
        return vjp(0.5 * d.astype(F32))

    dgate, dup = _rowcall(f"{tag}_act_bwd", act_bwd, rows, tma,
                          [(gate.reshape(rows, Fb), 0, 0, Fb), (up.reshape(rows, Fb), 0, 0, Fb),
                           (dact.reshape(rows, Fb), 0, 0, Fb)], [], [(Fb, BF16), (Fb, BF16)])
    dgate, dup = dgate.reshape(J, Tp, Fb), dup.reshape(J, Tp, Fb)
    red.push(f"{tag}_w_gate", carried(_mm_tn, f"{tag}_dwg", dgate, n, J, Fb, D, BF16, tk_target=tk_w))
    red.push(f"{tag}_w_up", carried(_mm_tn, f"{tag}_dwu", dup, n, J, Fb, D, BF16, tk_target=tk_w))
    half = D // 2
    dn = [carried(_mm_red, f"{tag}_dn{part}", [(dgate, wg), (dup, wu)], NN, F32, n_off=part * half, n_len=half)
          for part in range(2)]
    return _norm_bwd(f"{tag}_norm_bwd", h, gain, dn, dh, Tp)


def _norm_bwd(name, h, gain, dn_parts, dres, Tp):
    D = h.shape[1]
    n_parts = len(dn_parts)

    def fn(i, x, r, *rest):
        d = jnp.concatenate([p.astype(F32) for p in rest[:n_parts]], axis=1) if n_parts > 1 else rest[0].astype(F32)
        _, vjp = jax.vjp(_rmsnorm, x, rest[n_parts])
        dx, dg = vjp(d)
        return r + dx, r + dx, dg

    rows = [(h, 0, 0, D), (dres, 0, 0, D)] + [(p, 0, 0, p.shape[1]) for p in dn_parts]
    return _rowcall(name, fn, Tp, _row_tile(Tp, D), rows, [gain], [(D, F32), (D, BF16)], [(1, D)])


def kernel(x, meta_tokens, ffn1_norm, ffn1_w_gate, ffn1_w_up, ffn1_w_down, mix_norm, w_in, rg_conv_w, rg_conv_b, rg_w_a, rg_b_a, rg_w_x, rg_b_x, rg_lambda, s5_lambda_re, s5_lambda_im, s5_log_dt, s5_b_re, s5_b_im, s5_c_re, s5_c_im, s5_d, s5_glu_w, s5_glu_b, rg_out_norm, s5_out_norm, w_out, ffn2_norm, ffn2_w_gate, ffn2_w_up, ffn2_w_down, final_norm, loss_target, m_meta_tokens, m_ffn1_norm, m_ffn1_w_gate, m_ffn1_w_up, m_ffn1_w_down, m_mix_norm, m_w_in, m_rg_conv_w, m_rg_conv_b, m_rg_w_a, m_rg_b_a, m_rg_w_x, m_rg_b_x, m_rg_lambda, m_s5_lambda_re, m_s5_lambda_im, m_s5_log_dt, m_s5_b_re, m_s5_b_im, m_s5_c_re, m_s5_c_im, m_s5_d, m_s5_glu_w, m_s5_glu_b, m_rg_out_norm, m_s5_out_norm, m_w_out, m_ffn2_norm, m_ffn2_w_gate, m_ffn2_w_up, m_ffn2_w_down, m_final_norm, v_meta_tokens, v_ffn1_norm, v_ffn1_w_gate, v_ffn1_w_up, v_ffn1_w_down, v_mix_norm, v_w_in, v_rg_conv_w, v_rg_conv_b, v_rg_w_a, v_rg_b_a, v_rg_w_x, v_rg_b_x, v_rg_lambda, v_s5_lambda_re, v_s5_lambda_im, v_s5_log_dt, v_s5_b_re, v_s5_b_im, v_s5_c_re, v_s5_c_im, v_s5_d, v_s5_glu_w, v_s5_glu_b, v_rg_out_norm, v_s5_out_norm, v_w_out, v_ffn2_norm, v_ffn2_w_gate, v_ffn2_w_up, v_ffn2_w_down, v_final_norm):
    weights = dict(
        meta_tokens=meta_tokens, ffn1_norm=ffn1_norm, ffn1_w_gate=ffn1_w_gate, ffn1_w_up=ffn1_w_up, ffn1_w_down=ffn1_w_down,
        mix_norm=mix_norm, w_in=w_in, rg_conv_w=rg_conv_w, rg_conv_b=rg_conv_b, rg_w_a=rg_w_a, rg_b_a=rg_b_a, rg_w_x=rg_w_x,
        rg_b_x=rg_b_x, rg_lambda=rg_lambda, s5_lambda_re=s5_lambda_re, s5_lambda_im=s5_lambda_im, s5_log_dt=s5_log_dt,
        s5_b_re=s5_b_re, s5_b_im=s5_b_im, s5_c_re=s5_c_re, s5_c_im=s5_c_im, s5_d=s5_d, s5_glu_w=s5_glu_w, s5_glu_b=s5_glu_b,
        rg_out_norm=rg_out_norm, s5_out_norm=s5_out_norm, w_out=w_out, ffn2_norm=ffn2_norm, ffn2_w_gate=ffn2_w_gate,
        ffn2_w_up=ffn2_w_up, ffn2_w_down=ffn2_w_down, final_norm=final_norm)
    moments_m = dict(
        meta_tokens=m_meta_tokens, ffn1_norm=m_ffn1_norm, ffn1_w_gate=m_ffn1_w_gate, ffn1_w_up=m_ffn1_w_up,
        ffn1_w_down=m_ffn1_w_down, mix_norm=m_mix_norm, w_in=m_w_in, rg_conv_w=m_rg_conv_w, rg_conv_b=m_rg_conv_b,
        rg_w_a=m_rg_w_a, rg_b_a=m_rg_b_a, rg_w_x=m_rg_w_x, rg_b_x=m_rg_b_x, rg_lambda=m_rg_lambda,
        s5_lambda_re=m_s5_lambda_re, s5_lambda_im=m_s5_lambda_im, s5_log_dt=m_s5_log_dt, s5_b_re=m_s5_b_re,
        s5_b_im=m_s5_b_im, s5_c_re=m_s5_c_re, s5_c_im=m_s5_c_im, s5_d=m_s5_d, s5_glu_w=m_s5_glu_w, s5_glu_b=m_s5_glu_b,
        rg_out_norm=m_rg_out_norm, s5_out_norm=m_s5_out_norm, w_out=m_w_out, ffn2_norm=m_ffn2_norm,
        ffn2_w_gate=m_ffn2_w_gate, ffn2_w_up=m_ffn2_w_up, ffn2_w_down=m_ffn2_w_down, final_norm=m_final_norm)
    moments_v = dict(
        meta_tokens=v_meta_tokens, ffn1_norm=v_ffn1_norm, ffn1_w_gate=v_ffn1_w_gate, ffn1_w_up=v_ffn1_w_up,
        ffn1_w_down=v_ffn1_w_down, mix_norm=v_mix_norm, w_in=v_w_in, rg_conv_w=v_rg_conv_w, rg_conv_b=v_rg_conv_b,
        rg_w_a=v_rg_w_a, rg_b_a=v_rg_b_a, rg_w_x=v_rg_w_x, rg_b_x=v_rg_b_x, rg_lambda=v_rg_lambda,
        s5_lambda_re=v_s5_lambda_re, s5_lambda_im=v_s5_lambda_im, s5_log_dt=v_s5_log_dt, s5_b_re=v_s5_b_re,
        s5_b_im=v_s5_b_im, s5_c_re=v_s5_c_re, s5_c_im=v_s5_c_im, s5_d=v_s5_d, s5_glu_w=v_s5_glu_w, s5_glu_b=v_s5_glu_b,
        rg_out_norm=v_rg_out_norm, s5_out_norm=v_s5_out_norm, w_out=v_w_out, ffn2_norm=v_ffn2_norm,
        ffn2_w_gate=v_ffn2_w_gate, ffn2_w_up=v_ffn2_w_up, ffn2_w_down=v_ffn2_w_down, final_norm=v_final_norm)
    order = list(weights)

    seq, D = x.shape[1], x.shape[2]
    R = rg_conv_b.shape[1]
    S = s5_d.shape[1]
    G, N, C = s5_b_re.shape[1:]
    heads, hd = rg_w_a.shape[1], rg_w_a.shape[3]
    Q = G // GROUPS_PER_BLOCK
    W = G * N
    NB = GROUPS_PER_BLOCK * N
    T = N_META + seq
    Tp = -(-T // LANES) * LANES
    me = 4 * lax.axis_index("x") + 2 * lax.axis_index("y") + lax.axis_index("c")
    c_idx = lax.axis_index("c").astype(jnp.int32).reshape(1)

    big = ["ffn1_w_gate", "ffn1_w_up", "ffn1_w_down", "w_in", "s5_glu_w", "w_out", "ffn2_w_gate", "ffn2_w_up", "ffn2_w_down"]
    transposed = ["ffn1_w_gate", "ffn1_w_up", "ffn2_w_gate", "ffn2_w_up"]

    def as_held(k, a):
        return jnp.swapaxes(a, 1, 2) if k in transposed else a

    shard16 = {k: as_held(k, weights[k])[0].astype(BF16) for k in big}
    full = {}
    sharded_small = ["meta_tokens", "rg_conv_w", "rg_w_a", "rg_b_a", "rg_w_x", "rg_b_x"]
    sm, full["ffn1_w_gate"] = _comm_call(
        "ag_first", [_ag_rider(_pack([weights[k] for k in sharded_small])), _ag_rider(shard16["ffn1_w_gate"])])
    sm = [jnp.stack(p) for p in zip(*[_unpack(sm[d], [weights[k].shape for k in sharded_small]) for d in range(N_DEV)])]
    meta_full = jnp.moveaxis(sm[0], 0, 1).reshape(N_META, D)
    conv_w_full = jnp.moveaxis(sm[1][:, 0], 0, 1).reshape(CONV_WIDTH, R)
    wa_full = jnp.moveaxis(sm[2][:, 0], 0, 1).reshape(heads, hd, hd)
    ba_full = jnp.moveaxis(sm[3][:, 0], 0, 1).reshape(1, R)
    wx_full = jnp.moveaxis(sm[4][:, 0], 0, 1).reshape(heads, hd, hd)
    bx_full = jnp.moveaxis(sm[5][:, 0], 0, 1).reshape(1, R)

    lam_fn = functools.partial(_s5_discretise)
    (lr, li, bbr, bbi), disc_vjp = jax.vjp(lam_fn, s5_lambda_re[0], s5_lambda_im[0], s5_log_dt[0], s5_b_re[0], s5_b_im[0])
    lr_row, li_row = lr.reshape(1, W), li.reshape(1, W)
    wb = jnp.concatenate([_expand_diag(jnp.swapaxes(bbr, 1, 2), True),
                          _expand_diag(jnp.swapaxes(bbi, 1, 2), True)], axis=2)
    wc = jnp.concatenate([_expand_diag(jnp.swapaxes(s5_c_re[0], 1, 2), True),
                          _expand_diag(-jnp.swapaxes(s5_c_im[0], 1, 2), True)], axis=1)

    h0 = jnp.concatenate([meta_full, x[0], jnp.zeros((Tp - T, D), F32)], axis=0)
    tgt = jnp.concatenate([jnp.zeros((N_META, D), F32), loss_target[0], jnp.zeros((Tp - T, D), F32)], axis=0)

    h1, ffn1_saved, full["ffn1_w_up"], full["ffn1_w_down"], (full["w_in"],) = _ffn_fwd(
        "ffn1", h0, ffn1_norm, full["ffn1_w_gate"], shard16["ffn1_w_up"], shard16["ffn1_w_down"],
        [_ag_rider(shard16["w_in"])], Tp)

    tmd = _row_tile(Tp, D)
    (n2,) = _rowcall("mix_norm", lambda i, a, g: (_rmsnorm(a, g),), Tp, tmd, [(h1, 0, 0, D)], [mix_norm], [(D, BF16)])
    proj, (full["w_out"], full["s5_glu_w"]) = _mm_bcast(
        "w_in", n2, full["w_in"], NN, True, F32, riders=[_ag_rider(shard16["w_out"]), _ag_rider(shard16["s5_glu_w"])])

    tmr = _row_tile(Tp, R)
    tb = _pick(Tp, 768, SUBLANES)
    xc = _conv_fwd(proj, conv_w_full, rg_conv_b, R, tmr)
    a_t, b_t = _rowcall("rg_gates", lambda i, *a: _rg_gates(*a), Tp, tmr, [(xc, 0, 0, R)],
                        [wa_full, ba_full, wx_full, bx_full, rg_lambda], [(R, F32), (R, F32)])
    h_rg, hprev = _rg_scan_fwd(a_t, b_t, tb, _pick(R, 512, LANES))
    (yn_rg,) = _rowcall("rg_out", lambda i, *a: (_rg_out(*a),), Tp, tmr, [(h_rg, 0, 0, R), (proj, 0, 1, R)],
                        [rg_out_norm], [(R, BF16)])

    u_off = 2 * R // LANES
    bu = _mm_bdiag("s5_bu", [(proj, wb, u_off)], NN, LANES, 2 * NB, F32)
    xs, xps, (full["ffn2_w_gate"],) = _s5_scan_fwd(
        bu, _scan_tables(lr_row, li_row, False), tb, NB, riders=[_ag_rider(shard16["ffn2_w_gate"])])
    y_s5 = _mm_bdiag("s5_y", [(xs, wc, 0)], NN, 2 * NB, LANES, F32)
    tms = _row_tile(Tp, S)
    s_col = 2 * R // S
    (z,) = _rowcall("s5_pre", lambda i, *a: (_s5_pre(*a),), Tp, tms, [(y_s5, 0, 0, S), (proj, 0, s_col, S)], [s5_d],
                    [(S, F32)])
    gw = full["s5_glu_w"]
    zz = _mm_red("s5_glu", [(z, gw)], NN, F32, jb=N_DEV)
    (yn_s5,) = _rowcall("s5_out", lambda i, *a: (_s5_out(*a),), Tp, tms, [(z, 0, 0, S), (zz, 0, 0, S)],
                        [s5_glu_b, s5_out_norm], [(S, BF16)])
    yn = jnp.concatenate([yn_rg, yn_s5], axis=1)
    h2 = _mm_red("w_out", [(yn, full["w_out"])], NN, F32, res=h1, jb=4)

    h3, ffn2_saved, full["ffn2_w_up"], full["ffn2_w_down"], _ = _ffn_fwd(
        "ffn2", h2, ffn2_norm, full["ffn2_w_gate"], shard16["ffn2_w_up"], shard16["ffn2_w_down"], [], Tp)

    def final(i, hh, tt, g):
        out, vjp = jax.vjp(_rmsnorm, hh, g)
        row = i * tmd + lax.broadcasted_iota(jnp.int32, (tmd, 1), 0)
        valid = jnp.logical_and(row >= N_META, row < T)
        err = jnp.where(valid, out - tt, 0.0)
        part = 0.5 * jnp.sum(jnp.mean(err * err, axis=-1, keepdims=True))
        dx, dg = vjp(err * (1.0 / D))
        return dx, dx, dg, jnp.full((SUBLANES, LANES), part, F32)

    dh3, dh3_16, d_final_norm, loss_part = _rowcall(
        "final", final, Tp, tmd, [(h3, 0, 0, D), (tgt, 0, 0, D)], [final_norm.reshape(1, D)],
        [(D, F32), (D, BF16)], [(1, D), (SUBLANES, LANES)])
    loss = lax.psum(loss_part[0, 0], ("x", "y", "c"))

    grads = {}
    red = _GradReducer(c_idx)

    def carried(fn, *args, ici=1, d2d=1, extra=(), **kw):
        riders, absorb = red.take(ici, d2d)
        if not riders and not extra:
            return fn(*args, **kw), []
        out, routs = fn(*args, riders=riders + list(extra), **kw)
        absorb(routs[:len(riders)])
        return out, routs[len(riders):]

    dh2, dh2_16, grads["ffn2_norm"] = _ffn_bwd(
        "ffn2", dh3, dh3_16, h2, ffn2_norm, full["ffn2_w_gate"], full["ffn2_w_up"], full["ffn2_w_down"], ffn2_saved,
        Tp, red)

    Kb_out = full["w_out"].shape[1]
    dyn = _mm_bcast("w_out_dx", dh2_16, full["w_out"], NT, True, F32)
    red.push("w_out", carried(_mm_tn, "w_out_dw", yn, dh2_16, N_DEV, Kb_out, D, BF16)[0])

    def s5_out_bwd(i, zv, zzv, d, gb, gn):
        _, vjp = jax.vjp(_s5_out, zv, zzv, gb, gn)
        return vjp(d)

    dz_a, dzz, grads["s5_glu_b"], grads["s5_out_norm"] = _rowcall(
        "s5_out_bwd", s5_out_bwd, Tp, tms, [(z, 0, 0, S), (zz, 0, 0, S), (dyn, 0, R // S, S)],
        [s5_glu_b, s5_out_norm], [(S, F32), (S, BF16)], [(1, S), (1, S)])
    Kb_glu = gw.shape[1]
    dz = _mm_bcast("s5_glu_dx", dzz, gw, NT, True, F32)
    red.push("s5_glu_w", carried(_mm_tn, "s5_glu_dw", z, dzz, N_DEV, Kb_glu, S, BF16, ici=0)[0])

    def s5_pre_bwd(i, yv, uv, d1, d2, dd):
        _, vjp = jax.vjp(_s5_pre, yv, uv, dd)
        return vjp(d1 + d2)

    dy, du_a, grads["s5_d"] = _rowcall(
        "s5_pre_bwd", s5_pre_bwd, Tp, tms, [(y_s5, 0, 0, S), (proj, 0, s_col, S), (dz_a, 0, 0, S), (dz, 0, 0, S)],
        [s5_d], [(S, F32), (S, F32)], [(1, S)])
    gxs = _mm_bdiag("s5_dx", [(dy, wc, 0)], NT, LANES, 2 * NB, F32)
    d_wc = _mm_tn("s5_dc", xs, dy, Q, 2 * NB, LANES, F32)
    d_wc_r, d_wc_i = d_wc[:, :NB], d_wc[:, NB:]
    scan_riders, scan_absorb = red.take()
    adj, dlr8, dli8, scan_routs = _s5_scan_bwd(
        gxs, xps, _scan_tables(lr_row, -li_row, True), tb, NB, riders=scan_riders)
    scan_absorb(scan_routs)
    du_s5 = _mm_bdiag("s5_du", [(adj, wb, 0)], NT, 2 * NB, LANES, BF16, res=(du_a, 0))
    d_wb = carried(_mm_tn, "s5_db", proj, adj, Q, LANES, 2 * NB, F32, a_off=u_off, d2d=0)[0]
    d_bbr = jnp.swapaxes(_extract_diag(d_wb[:, :, :NB], C, N), 1, 2)
    d_bbi = jnp.swapaxes(_extract_diag(d_wb[:, :, NB:], C, N), 1, 2)
    d_lr = jnp.sum(dlr8, axis=0).reshape(G, N)
    d_li = jnp.sum(dli8, axis=0).reshape(G, N)
    d_lre, d_lim, d_logdt, d_bre, d_bim = disc_vjp((d_lr, d_li, d_bbr, d_bbi))
    grads["s5_lambda_re"], grads["s5_lambda_im"], grads["s5_log_dt"] = d_lre[None], d_lim[None], d_logdt[None]
    grads["s5_b_re"], grads["s5_b_im"] = d_bre[None], d_bim[None]
    grads["s5_c_re"] = jnp.swapaxes(_extract_diag(d_wc_r, N, C), 1, 2)[None]
    grads["s5_c_im"] = -jnp.swapaxes(_extract_diag(d_wc_i, N, C), 1, 2)[None]

    def rg_out_bwd(i, hv, gv, d, gn):
        _, vjp = jax.vjp(_rg_out, hv, gv, gn)
        return vjp(d)

    dh_scan, dg_rg, grads["rg_out_norm"] = _rowcall(
        "rg_out_bwd", rg_out_bwd, Tp, tmr, [(h_rg, 0, 0, R), (proj, 0, 1, R), (dyn, 0, 0, R)], [rg_out_norm],
        [(R, F32), (R, BF16)], [(1, R)])
    db_t, da_t = _rg_scan_bwd(dh_scan, a_t, hprev, tb, _pick(R, 512, LANES))

    def rg_gates_bwd(i, xv, da, db, wa, ba, wx, bx, lam):
        _, vjp = jax.vjp(_rg_gates, xv, wa, ba, wx, bx, lam)
        return vjp((da, db))

    dxc, d_wa, d_ba, d_wx, d_bx, grads["rg_lambda"] = _rowcall(
        "rg_gates_bwd", rg_gates_bwd, Tp, _row_tile(Tp, R, 4, 1024 * 1024),
        [(xc, 0, 0, R), (da_t, 0, 0, R), (db_t, 0, 0, R)], [wa_full, ba_full, wx_full, bx_full, rg_lambda],
        [(R, F32)], [(heads, hd, hd), (1, R), (heads, hd, hd), (1, R), (1, R)])
    du_rg, d_conv_w, grads["rg_conv_b"] = _conv_bwd(dxc, proj, conv_w_full, R, tmr)

    grads["final_norm"] = d_final_norm.reshape(D)
    late = ["ffn1_norm", "mix_norm"]
    replicated = ["ffn1_norm", "mix_norm", "rg_conv_b", "rg_lambda", "s5_lambda_re", "s5_lambda_im", "s5_log_dt",
                  "s5_b_re", "s5_b_im", "s5_c_re", "s5_c_im", "s5_d", "s5_glu_b", "rg_out_norm", "s5_out_norm",
                  "ffn2_norm", "final_norm"]
    early = [k for k in replicated if k not in late]
    early_full = [grads[k].reshape(weights[k].shape) for k in early] + [d_conv_w, d_wa, d_ba, d_wx, d_bx]

    dproj = jnp.concatenate([du_rg.astype(BF16), dg_rg, du_s5], axis=1)
    win = full["w_in"]
    Nb_in = win.shape[2]
    dn2, (early_gathered,) = carried(_mm_red, "w_in_dx", [(dproj, win)], NT, F32, jb=4,
                                     extra=[_ag_rider(_pack(early_full))])
    red.push("w_in", carried(_mm_tn, "w_in_dw", n2, dproj, N_DEV, D, Nb_in, BF16)[0])
    dh1, dh1_16, grads["mix_norm"] = _norm_bwd("mix_norm_bwd", h1, mix_norm, [dn2], dh2, Tp)

    dh0, _, grads["ffn1_norm"] = _ffn_bwd(
        "ffn1", dh1, dh1_16, h0, ffn1_norm, full["ffn1_w_gate"], full["ffn1_w_up"], full["ffn1_w_down"], ffn1_saved,
        Tp, red)
    grad_x = dh0[N_META:T][None]

    reduced = red.flush()
    outs = {}
    for k in big:
        r2 = reduced[k]
        _, Rk, Ck = r2.shape
        w2, m2, v2 = (as_held(k, a) for a in (weights[k], moments_m[k], moments_v[k]))

        def big_update(i, wv, mv, vv, p0, p1, p2, p3):
            gsum = (p0.astype(F32) + p1.astype(F32)) + (p2.astype(F32) + p3.astype(F32))
            return (gsum,) + _adamw(wv, gsum, mv, vv)

        res = _rowcall(f"adam_{k}", big_update, Rk, _row_tile(Rk, Ck, 4, 1024 * 1024),
                       [(w2, 0, 0, Ck), (m2, 0, 0, Ck), (v2, 0, 0, Ck)] + [(r2, kk, 0, Ck) for kk in range(N_CHIP)],
                       [], [(Ck, F32)] * 4)
        outs[k] = [as_held(k, o[None]) for o in res]

    late_full = [grads[k].reshape(weights[k].shape) for k in late] + [dh0[:N_META]]
    late_gathered = _all_gather("ag_late_grads", _pack(late_full))

    def sum8(i, *parts):
        s = parts[0]
        for q in parts[1:]:
            s = s + q
        return (s,)

    def sum_devices(name, gathered, shapes):
        n_rows = gathered.shape[1]
        (summed,) = _rowcall(name, sum8, n_rows, _pick(n_rows, 4096, SUBLANES),
                             [(gathered, d, 0, LANES) for d in range(N_DEV)], [], [(LANES, F32)])
        return _unpack(summed, shapes)

    early_sum = sum_devices("small_sum_early", early_gathered, [a.shape for a in early_full])
    late_sum = sum_devices("small_sum_late", late_gathered, [a.shape for a in late_full])
    g_small = dict(zip(early, early_sum[:len(early)]))
    g_small.update(zip(late, late_sum[:len(late)]))
    d_cw, d_wa_s, d_ba_s, d_wx_s, d_bx_s = early_sum[len(early):]
    d_meta = late_sum[len(late)]

    def shard_of(a, axis):
        n = a.shape[axis] // N_DEV
        return lax.dynamic_slice_in_dim(a, me * n, n, axis)

    g_small["meta_tokens"] = shard_of(d_meta, 1)
    g_small["rg_conv_w"] = shard_of(d_cw, 1)[None]
    g_small["rg_w_a"] = shard_of(d_wa_s, 1)[None]
    g_small["rg_b_a"] = shard_of(d_ba_s.reshape(heads, hd), 1)[None]
    g_small["rg_w_x"] = shard_of(d_wx_s, 1)[None]
    g_small["rg_b_x"] = shard_of(d_bx_s.reshape(heads, hd), 1)[None]
    small = replicated + sharded_small
    shapes = [weights[k].shape for k in small]
    gp = _pack([g_small[k] for k in small])
    n_rows = gp.shape[0]

    def small_update(i, wv, gv, mv, vv):
        return _adamw(wv, gv, mv, vv)

    res = _rowcall("adam_small", small_update, n_rows, _pick(n_rows, 512, SUBLANES),
                   [(_pack([weights[k] for k in small]), 0, 0, LANES), (gp, 0, 0, LANES),
                    (_pack([moments_m[k] for k in small]), 0, 0, LANES), (_pack([moments_v[k] for k in small]), 0, 0, LANES)],
                   [], [(LANES, F32)] * 3)
    unpacked = [_unpack(r, shapes) for r in res]
    for idx, k in enumerate(small):
        outs[k] = [g_small[k].reshape(weights[k].shape)] + [u[idx] for u in unpacked]

    return (loss, grad_x, *[outs[k][0] for k in order], *[outs[k][1] for k in order],
            *[outs[k][2] for k in order], *[outs[k][3] for k in order])
```

```python
import functools
import math

import jax
import jax.numpy as jnp
from jax import lax
from jax.experimental import pallas as pl
from jax.experimental.pallas import tpu as pltpu

F32, BF16 = jnp.float32, jnp.bfloat16
SDS = jax.ShapeDtypeStruct
MESH_ID = pl.DeviceIdType.MESH
N_DEV = 8
N_CHIP = 4
LANES = 128
SUBLANES = 8
VMEM_LIMIT = 56 * 1024 * 1024

EPS = 1e-6
RG_C = 8.0
N_META = 16
CONV_WIDTH = 4
S5_GROUP = 16
S5_STATE = 64
GROUPS_PER_BLOCK = LANES // S5_GROUP
ADAM_LR, ADAM_B1, ADAM_B2, ADAM_EPS, ADAM_WD, ADAM_STEP = 0.001, 0.9, 0.999, 1e-08, 0.01, 10

NN = (((1,), (0,)), ((), ()))
NT = (((1,), (1,)), ((), ()))
TN = (((0,), (0,)), ((), ()))


def _pick(n, target, mult=16):
    if n <= target:
        return n
    best = None
    for d in range(mult, target + 1, mult):
        if n % d == 0:
            best = d
    assert best is not None, (n, target, mult)
    return best


def _row_tile(nrows, ncols, itembytes=4, budget=2 * 1024 * 1024):
    return _pick(nrows, max(16, budget // (ncols * itembytes)))


def _params(sem):
    return pltpu.CompilerParams(dimension_semantics=sem, vmem_limit_bytes=VMEM_LIMIT)


def _rowcall(name, fn, nrows, tm, rows, fulls, row_outs, acc_outs=()):
    n_in = len(rows) + len(fulls)
    in_specs = []
    for arr, lead, cb, C in rows:
        if arr.ndim == 3:
            in_specs.append(pl.BlockSpec((None, tm, C), lambda i, lead=lead, cb=cb: (lead, i, cb)))
        else:
            in_specs.append(pl.BlockSpec((tm, C), lambda i, cb=cb: (i, cb)))
    for f in fulls:
        in_specs.append(pl.BlockSpec(f.shape, lambda i, nd=f.ndim: (0,) * nd))
    out_specs = [pl.BlockSpec((tm, C), lambda i: (i, 0)) for C, _ in row_outs]
    out_shape = [SDS((nrows, C), dt) for C, dt in row_outs]
    for shp in acc_outs:
        out_specs.append(pl.BlockSpec(shp, lambda i, nd=len(shp): (0,) * nd))
        out_shape.append(SDS(shp, F32))
    n_row_out = len(row_outs)

    def body(*refs):
        i = pl.program_id(0)
        res = fn(i, *[r[...] for r in refs[:n_in]])
        outs = refs[n_in:]
        for k in range(n_row_out):
            outs[k][...] = res[k].astype(outs[k].dtype)
        if acc_outs:
            @pl.when(i == 0)
            def _():
                for o in outs[n_row_out:]:
                    o[...] = jnp.zeros_like(o)
            for k in range(n_row_out, len(outs)):
                outs[k][...] += res[k].astype(F32)

    return pl.pallas_call(
        body, grid=(nrows // tm,), in_specs=in_specs, out_specs=out_specs, out_shape=out_shape,
        name=name, compiler_params=_params(("arbitrary",)))(*[r[0] for r in rows], *fulls)


class _Rider:
    def __init__(self, ins, outs, sems, steps):
        self.ins, self.outs, self.sems, self.steps = list(ins), list(outs), list(sems), steps


def _rider_counts(riders):
    return (sum(len(r.ins) for r in riders), sum(len(r.outs) for r in riders), sum(len(r.sems) for r in riders))


def _rider_hooks(riders, in_refs, out_refs, sem_refs):
    hooks, i, o, s = [], 0, 0, 0
    for r in riders:
        hooks.append(r.steps(*in_refs[i:i + len(r.ins)], *out_refs[o:o + len(r.outs)], *sem_refs[s:s + len(r.sems)]))
        i, o, s = i + len(r.ins), o + len(r.outs), s + len(r.sems)
    return hooks


def _linear_step(grid):
    step = pl.program_id(0)
    for ax in range(1, len(grid)):
        step = step * grid[ax] + pl.program_id(ax)
    return step


def _ride_begin(step, hooks):
    if hooks:
        @pl.when(step == 0)
        def _():
            for start, _, _ in hooks:
                start()


def _ride_end(step, nsteps, hooks):
    if hooks:
        mids = sorted(((min(int(frac * nsteps), nsteps - 1), k, fn) for k, (_, mid, _) in enumerate(hooks)
                       for frac, fn in (mid or ())), key=lambda m: m[:2])
        for at in sorted({m[0] for m in mids}):
            @pl.when(step == at)
            def _(at=at):
                for s, _, fn in mids:
                    if s == at:
                        fn()

        @pl.when(step == nsteps - 1)
        def _():
            for _, _, finish in hooks:
                finish()


def _mm(name, terms, dims, grid, nk, out_shape, out_spec, acc_shape, scale=1.0, res=None, riders=()):
    n_t = len(terms)
    kax = len(grid) - 1
    n_in = 2 * n_t + (1 if res is not None else 0)
    r_in, r_out, _ = _rider_counts(riders)
    nsteps = math.prod(grid)
    n_acc = 1 if nk > 1 else 0

    def body(*refs):
        ins, rin = refs[:n_in], refs[n_in:n_in + r_in]
        o_ref = refs[n_in + r_in]
        rout = refs[n_in + r_in + 1:n_in + r_in + 1 + r_out]
        scratch = refs[n_in + r_in + 1 + r_out:]
        step = _linear_step(grid)
        k = pl.program_id(kax)
        hooks = _rider_hooks(riders, rin, rout, scratch[n_acc:])
        _ride_begin(step, hooks)

        def product():
            r = None
            for t in range(n_t):
                d = lax.dot_general(ins[2 * t][...].astype(BF16), ins[2 * t + 1][...].astype(BF16), dims,
                                    preferred_element_type=F32)
                r = d if r is None else r + d
            return r

        def emit(r):
            r = r * scale
            if res is not None:
                r = r + ins[2 * n_t][...].astype(F32)
            o_ref[...] = r.astype(o_ref.dtype)

        if nk == 1:
            emit(product())
        else:
            acc = scratch[0]

            @pl.when(k == 0)
            def _():
                acc[...] = jnp.zeros_like(acc)

            acc[...] += product()

            @pl.when(k == nk - 1)
            def _():
                emit(acc[...])

        _ride_end(step, nsteps, hooks)

    ops, specs = [], []
    for a, a_spec, b, b_spec in terms:
        ops += [a, b]
        specs += [a_spec, b_spec]
    if res is not None:
        ops.append(res[0])
        specs.append(res[1])
    out_shapes, out_specs, scratch = [out_shape], [out_spec], []
    if n_acc:
        scratch.append(pltpu.VMEM(acc_shape, F32))
    for r in riders:
        ops += r.ins
        specs += [_ANY] * len(r.ins)
        out_shapes += r.outs
        out_specs += [_ANY] * len(r.outs)
        scratch += r.sems
    sem = ("arbitrary",) * len(grid)
    outs = pl.pallas_call(
        body, grid=grid, in_specs=specs, out_specs=out_specs, out_shape=out_shapes,
        scratch_shapes=scratch, name=name, compiler_params=_params(sem))(*ops)
    return (outs[0], list(outs[1:])) if riders else outs[0]


def _mm_bcast(name, a, w, dims, std_out, out_dtype, tm_target=384, riders=()):
    M, K = a.shape
    J = w.shape[0]
    Nb = w.shape[2] if dims == NN else w.shape[1]
    tm = _pick(M, tm_target)
    a_spec = pl.BlockSpec((tm, K), lambda j, i, k: (i, 0))
    w_spec = pl.BlockSpec((None,) + w.shape[1:], lambda j, i, k: (j, 0, 0))
    if std_out:
        out_shape, out_spec = SDS((M, J * Nb), out_dtype), pl.BlockSpec((tm, Nb), lambda j, i, k: (i, j))
    else:
        out_shape, out_spec = SDS((J, M, Nb), out_dtype), pl.BlockSpec((None, tm, Nb), lambda j, i, k: (j, i, 0))
    return _mm(name, [(a, a_spec, w, w_spec)], dims, (J, M // tm, 1), 1, out_shape, out_spec, (tm, Nb), riders=riders)


def _a_blk(a, tm, Kb, off, jb, s):
    if a.ndim == 3:
        return pl.BlockSpec((None, tm, Kb), lambda i, n, j: (j * jb + s, i, 0))
    return pl.BlockSpec((tm, Kb), lambda i, n, j: (i, j * jb + s + off))


def _mm_red(name, pairs, dims, out_dtype, scale=1.0, res=None, tm_target=704, tn_target=1024, n_off=0, n_len=None,
            jb=1, riders=()):
    a0, w0 = pairs[0][0], pairs[0][1]
    J = w0.shape[0]
    M = a0.shape[-2]
    N = w0.shape[2] if dims == NN else w0.shape[1]
    Kb = w0.shape[1] if dims == NN else w0.shape[2]
    if n_len is not None:
        N = n_len
    tm, tn = _pick(M, tm_target), _pick(N, tn_target, LANES)
    nb0 = n_off // tn
    assert J % jb == 0
    terms = []
    for a, w, *rest in pairs:
        off = rest[0] if rest else 0
        for s in range(jb):
            if dims == NN:
                w_spec = pl.BlockSpec((None, Kb, tn), lambda i, n, j, s=s: (j * jb + s, 0, n + nb0))
            else:
                w_spec = pl.BlockSpec((None, tn, Kb), lambda i, n, j, s=s: (j * jb + s, n + nb0, 0))
            terms.append((a, _a_blk(a, tm, Kb, off, jb, s), w, w_spec))
    out_spec = pl.BlockSpec((tm, tn), lambda i, n, j: (i, n))
    r = None if res is None else (res, out_spec)
    return _mm(name, terms, dims, (M // tm, N // tn, J // jb), J // jb, SDS((M, N), out_dtype), out_spec, (tm, tn),
               scale, r, riders)


def _mm_tn(name, a, b, J, Ka, Nb, out_dtype, a_off=0, b_off=0, scale=1.0, tw_target=1024, tk_target=1408, riders=()):
    M = a.shape[-2]
    tk = _pick(M, tk_target)
    tka = _pick(Ka, tw_target, LANES) if Ka % LANES == 0 else Ka
    tnb = _pick(Nb, tw_target, LANES) if Nb % LANES == 0 else Nb

    def spec(arr, width, tw, is_a, off):
        nb = width // tw

        def wi(m, n):
            return m if is_a else n
        if arr.ndim == 3:
            return pl.BlockSpec((None, tk, tw), lambda j, m, n, k: (j, k, wi(m, n)))
        if arr.shape[1] == width:
            return pl.BlockSpec((tk, tw), lambda j, m, n, k: (k, wi(m, n)))
        return pl.BlockSpec((tk, tw), lambda j, m, n, k: (k, (j + off) * nb + wi(m, n)))

    a_spec = spec(a, Ka, tka, True, a_off)
    b_spec = spec(b, Nb, tnb, False, b_off)
    out_spec = pl.BlockSpec((None, tka, tnb), lambda j, m, n, k: (j, m, n))
    return _mm(name, [(a, a_spec, b, b_spec)], TN, (J, Ka // tka, Nb // tnb, M // tk), M // tk,
               SDS((J, Ka, Nb), out_dtype), out_spec, (tka, tnb), scale, riders=riders)


def _mm_bdiag(name, pairs, dims, Kb, Nb, out_dtype, res=None, tm_target=704, riders=()):
    a0, w0 = pairs[0][0], pairs[0][1]
    M, J = a0.shape[0], w0.shape[0]
    tm = _pick(M, tm_target)
    terms = []
    for a, w, off in pairs:
        a_spec = pl.BlockSpec((tm, Kb), lambda j, i, k, off=off: (i, j + off))
        w_spec = pl.BlockSpec((None,) + w.shape[1:], lambda j, i, k: (j, 0, 0))
        terms.append((a, a_spec, w, w_spec))
    out_spec = pl.BlockSpec((tm, Nb), lambda j, i, k: (i, j))
    r = None if res is None else (res[0], pl.BlockSpec((tm, Nb), lambda j, i, k, off=res[1]: (i, j + off)))
    return _mm(name, terms, dims, (J, M // tm, 1), 1, SDS((M, J * Nb), out_dtype), out_spec, (tm, Nb), 1.0, r, riders)


def _rmsnorm(x, g):
    x = x.astype(F32)
    return x * lax.rsqrt(jnp.mean(x * x, axis=-1, keepdims=True) + EPS) * g


def _swiglu_act(gate, up):
    gate, up = gate.astype(F32), up.astype(F32)
    return gate * jax.nn.sigmoid(gate) * up


def _neg_expm1(x):
    series = -x * (1.0 + x * (1.0 / 2 + x * (1.0 / 6 + x * (1.0 / 24 + x * (1.0 / 120 + x * (1.0 / 720))))))
    return jnp.where(x > -0.25, series, 1.0 - jnp.exp(x))


@jax.custom_vjp
def _dot16(x, w):
    return jnp.dot(x.astype(BF16), w.astype(BF16), preferred_element_type=F32)


def _dot16_fwd(x, w):
    return _dot16(x, w), (x, w)


def _dot16_bwd(saved, ct):
    x, w = saved
    ct16 = ct.astype(BF16)
    dx = lax.dot_general(ct16, w.astype(BF16), NT, preferred_element_type=F32)
    dw = lax.dot_general(x.astype(BF16), ct16, TN, preferred_element_type=F32)
    return dx.astype(x.dtype), dw.astype(w.dtype)


_dot16.defvjp(_dot16_fwd, _dot16_bwd)


def _rg_gates(xc, wa, ba, wx, bx, lam):
    heads, hd = wa.shape[0], wa.shape[1]
    rs, ig = [], []
    for h in range(heads):
        xh = xc[:, h * hd:(h + 1) * hd]
        rs.append(_dot16(xh, wa[h]))
        ig.append(_dot16(xh, wx[h]))
    r = jax.nn.sigmoid(jnp.concatenate(rs, axis=1) + ba)
    ii = jax.nn.sigmoid(jnp.concatenate(ig, axis=1) + bx)
    log_a = -RG_C * r * jax.nn.softplus(-lam)
    a = jnp.exp(log_a)
    mult = jnp.sqrt(_neg_expm1(2.0 * log_a))
    return a, mult * ii * xc


def _rg_out(h, g, gain):
    return _rmsnorm(h * jax.nn.gelu(g), gain)


def _s5_pre(y, u, d):
    return jax.nn.gelu(y + d * u)


def _s5_out(z, zz, glu_b, gain):
    return _rmsnorm(z * jax.nn.sigmoid(zz + glu_b), gain)


def _adamw(w, g, m, v):
    m = ADAM_B1 * m + (1.0 - ADAM_B1) * g
    v = ADAM_B2 * v + (1.0 - ADAM_B2) * jnp.square(g)
    m_hat = m / (1.0 - ADAM_B1 ** ADAM_STEP)
    v_hat = v / (1.0 - ADAM_B2 ** ADAM_STEP)
    delta = -ADAM_LR * (m_hat / (jnp.sqrt(v_hat) + ADAM_EPS) + ADAM_WD * w)
    return delta, m, v


def _conv_fwd(proj, conv_w, conv_b, R, tm):
    T = proj.shape[0]
    tpb = tm // SUBLANES

    def body(cur_ref, prev_ref, w_ref, b_ref, o_ref):
        i = pl.program_id(0)
        cur = cur_ref[...]
        prev = jnp.where(i > 0, prev_ref[...], 0.0)
        ext = jnp.concatenate([prev, cur], axis=0)
        acc = b_ref[...] + w_ref[CONV_WIDTH - 1:CONV_WIDTH, :] * cur
        for s in range(1, CONV_WIDTH):
            acc = acc + w_ref[CONV_WIDTH - 1 - s:CONV_WIDTH - s, :] * pltpu.roll(ext, s, 0)[SUBLANES:, :]
        o_ref[...] = acc

    return pl.pallas_call(
        body, grid=(T // tm,),
        in_specs=[pl.BlockSpec((tm, R), lambda i: (i, 0)),
                  pl.BlockSpec((SUBLANES, R), lambda i: (jnp.maximum(i * tpb - 1, 0), 0)),
                  pl.BlockSpec((CONV_WIDTH, R), lambda i: (0, 0)), pl.BlockSpec((1, R), lambda i: (0, 0))],
        out_specs=pl.BlockSpec((tm, R), lambda i: (i, 0)), out_shape=SDS((T, R), F32),
        name="rg_conv_fwd", compiler_params=_params(("arbitrary",)))(proj, proj, conv_w, conv_b)


def _conv_bwd(dxc, proj, conv_w, R, tm):
    T = dxc.shape[0]
    tpb = tm // SUBLANES
    nt = T // tm
    n_ext = tm + SUBLANES

    def body(d_ref, dnext_ref, u_ref, uprev_ref, w_ref, du_ref, dw_ref, db_ref):
        i = pl.program_id(0)
        d = d_ref[...]
        dnext = jnp.where(i < nt - 1, dnext_ref[...], 0.0)
        dext = jnp.concatenate([d, dnext], axis=0)
        u = u_ref[...]
        uprev = jnp.where(i > 0, uprev_ref[...], 0.0)
        uext = jnp.concatenate([uprev, u], axis=0)
        du = w_ref[CONV_WIDTH - 1:CONV_WIDTH, :] * d
        dws = [jnp.sum(d * u, axis=0, keepdims=True)]
        for s in range(1, CONV_WIDTH):
            du = du + w_ref[CONV_WIDTH - 1 - s:CONV_WIDTH - s, :] * pltpu.roll(dext, n_ext - s, 0)[:tm, :]
            dws.append(jnp.sum(d * pltpu.roll(uext, s, 0)[SUBLANES:, :], axis=0, keepdims=True))
        du_ref[...] = du

        @pl.when(i == 0)
        def _():
            dw_ref[...] = jnp.zeros_like(dw_ref)
            db_ref[...] = jnp.zeros_like(db_ref)

        dw_ref[...] += jnp.concatenate(dws[::-1], axis=0)
        db_ref[...] += jnp.sum(d, axis=0, keepdims=True)

    row = pl.BlockSpec((tm, R), lambda i: (i, 0))
    return pl.pallas_call(
        body, grid=(nt,),
        in_specs=[row, pl.BlockSpec((SUBLANES, R), lambda i: (jnp.minimum((i + 1) * tpb, T // SUBLANES - 1), 0)),
                  row, pl.BlockSpec((SUBLANES, R), lambda i: (jnp.maximum(i * tpb - 1, 0), 0)),
                  pl.BlockSpec((CONV_WIDTH, R), lambda i: (0, 0))],
        out_specs=[row, pl.BlockSpec((CONV_WIDTH, R), lambda i: (0, 0)), pl.BlockSpec((1, R), lambda i: (0, 0))],
        out_shape=[SDS((T, R), F32), SDS((CONV_WIDTH, R), F32), SDS((1, R), F32)],
        name="rg_conv_bwd", compiler_params=_params(("arbitrary",)))(dxc, dxc, proj, proj, conv_w)


def _scan_tiles(tb):
    return tb // SUBLANES


def _rg_scan_fwd(a, b, tb, cb):
    T, W = a.shape

    def body(a_ref, b_ref, h_ref, p_ref, carry):
        @pl.when(pl.program_id(1) == 0)
        def _():
            carry[...] = jnp.zeros_like(carry)

        def tile(t, h):
            ds = pl.ds(pl.multiple_of(t * SUBLANES, SUBLANES), SUBLANES)
            a8, b8 = a_ref[ds, :], b_ref[ds, :]
            hs, ps = [], []
            for j in range(SUBLANES):
                ps.append(h)
                h = a8[j:j + 1, :] * h + b8[j:j + 1, :]
                hs.append(h)
            h_ref[ds, :] = jnp.concatenate(hs, axis=0)
            p_ref[ds, :] = jnp.concatenate(ps, axis=0)
            return h

        carry[0:1, :] = lax.fori_loop(0, _scan_tiles(tb), tile, carry[0:1, :])

    blk = pl.BlockSpec((tb, cb), lambda c, i: (i, c))
    return pl.pallas_call(
        body, grid=(W // cb, T // tb), in_specs=[blk, blk], out_specs=[blk, blk],
        out_shape=[SDS((T, W), F32)] * 2, scratch_shapes=[pltpu.VMEM((SUBLANES, cb), F32)],
        name="rg_scan_fwd", compiler_params=_params(("arbitrary", "arbitrary")))(a, b)


def _rg_scan_bwd(dh, a, hprev, tb, cb):
    T, W = a.shape
    nt = T // tb

    def body(g_ref, a_ref, p_ref, db_ref, da_ref, carry):
        @pl.when(pl.program_id(1) == 0)
        def _():
            carry[...] = jnp.zeros_like(carry)

        def tile(tt, c):
            t = _scan_tiles(tb) - 1 - tt
            ds = pl.ds(pl.multiple_of(t * SUBLANES, SUBLANES), SUBLANES)
            g8, a8 = g_ref[ds, :], a_ref[ds, :]
            adjs = [None] * SUBLANES
            for j in range(SUBLANES - 1, -1, -1):
                adj = g8[j:j + 1, :] + c
                adjs[j] = adj
                c = a8[j:j + 1, :] * adj
            adj8 = jnp.concatenate(adjs, axis=0)
            db_ref[ds, :] = adj8
            da_ref[ds, :] = adj8 * p_ref[ds, :]
            return c

        carry[0:1, :] = lax.fori_loop(0, _scan_tiles(tb), tile, carry[0:1, :])

    blk = pl.BlockSpec((tb, cb), lambda c, i: (nt - 1 - i, c))
    return pl.pallas_call(
        body, grid=(W // cb, nt), in_specs=[blk, blk, blk], out_specs=[blk, blk],
        out_shape=[SDS((T, W), F32)] * 2, scratch_shapes=[pltpu.VMEM((SUBLANES, cb), F32)],
        name="rg_scan_bwd", compiler_params=_params(("arbitrary", "arbitrary")))(dh, a, hprev)


SCAN_LEVELS = (1, 2, 4)


def _scan_tables(lr, li, reverse):
    def cmul(a, b):
        return a[0] * b[0] - a[1] * b[1], a[0] * b[1] + a[1] * b[0]

    powers = [(lr, li)]
    for _ in range(SUBLANES - 1):
        powers.append(cmul(powers[-1], (lr, li)))
    row = jnp.arange(SUBLANES)[:, None]
    rows = []
    for k in SCAN_LEVELS:
        has = (row <= SUBLANES - 1 - k) if reverse else (row >= k)
        rows += [jnp.where(has, powers[k - 1][0], 0.0), jnp.where(has, powers[k - 1][1], 0.0)]
    order = list(range(SUBLANES - 1, -1, -1)) if reverse else list(range(SUBLANES))
    rows += [jnp.concatenate([powers[j][0] for j in order], axis=0), jnp.concatenate([powers[j][1] for j in order], axis=0)]
    return jnp.concatenate(rows, axis=0)


def _scan_tile(vr, vi, carry, tab_ref, reverse):
    n_rows = SUBLANES
    for lvl, k in enumerate(SCAN_LEVELS):
        mr = tab_ref[2 * lvl * n_rows:(2 * lvl + 1) * n_rows, :]
        mi = tab_ref[(2 * lvl + 1) * n_rows:(2 * lvl + 2) * n_rows, :]
        shift = n_rows - k if reverse else k
        sr, si = pltpu.roll(vr, shift, 0), pltpu.roll(vi, shift, 0)
        vr, vi = vr + mr * sr - mi * si, vi + mr * si + mi * sr
    base = 2 * len(SCAN_LEVELS) * n_rows
    pr, pi = tab_ref[base:base + n_rows, :], tab_ref[base + n_rows:base + 2 * n_rows, :]
    cr, ci = carry
    return vr + pr * cr - pi * ci, vi + pr * ci + pi * cr


def _s5_scan_fwd(bu, tab, tb, cb, riders=()):
    T, W = bu.shape[0], bu.shape[1] // 2
    grid = (W // cb, T // tb)
    r_in, r_out, _ = _rider_counts(riders)
    re, im = slice(0, cb), slice(cb, 2 * cb)

    def body(*refs):
        b_ref, tab_ref = refs[:2]
        x_ref, p_ref = refs[2 + r_in:4 + r_in]
        cr, ci = refs[4 + r_in + r_out:6 + r_in + r_out]
        step = _linear_step(grid)
        hooks = _rider_hooks(riders, refs[2:2 + r_in], refs[4 + r_in:4 + r_in + r_out], refs[6 + r_in + r_out:])
        _ride_begin(step, hooks)

        @pl.when(pl.program_id(1) == 0)
        def _():
            cr[...] = jnp.zeros_like(cr)
            ci[...] = jnp.zeros_like(ci)

        first_row = lax.broadcasted_iota(jnp.int32, (SUBLANES, cb), 0) == 0

        def tile(t, carry):
            ds = pl.ds(pl.multiple_of(t * SUBLANES, SUBLANES), SUBLANES)
            xr8, xi8 = _scan_tile(b_ref[ds, re], b_ref[ds, im], carry, tab_ref, reverse=False)
            x_ref[ds, re] = xr8
            x_ref[ds, im] = xi8
            p_ref[ds, re] = jnp.where(first_row, carry[0], pltpu.roll(xr8, 1, 0))
            p_ref[ds, im] = jnp.where(first_row, carry[1], pltpu.roll(xi8, 1, 0))
            return xr8[SUBLANES - 1:SUBLANES, :], xi8[SUBLANES - 1:SUBLANES, :]

        xr, xi = lax.fori_loop(0, _scan_tiles(tb), tile, (cr[0:1, :], ci[0:1, :]))
        cr[0:1, :] = xr
        ci[0:1, :] = xi
        _ride_end(step, math.prod(grid), hooks)

    blk = pl.BlockSpec((tb, 2 * cb), lambda c, i: (i, c))
    tabs = pl.BlockSpec((tab.shape[0], cb), lambda c, i: (0, c))
    outs = pl.pallas_call(
        body, grid=grid, in_specs=[blk, tabs] + [_ANY] * r_in, out_specs=[blk] * 2 + [_ANY] * r_out,
        out_shape=[SDS((T, 2 * W), F32)] * 2 + [o for r in riders for o in r.outs],
        scratch_shapes=[pltpu.VMEM((SUBLANES, cb), F32), pltpu.VMEM((SUBLANES, cb), F32)] + [s for r in riders for s in r.sems],
        name="s5_scan_fwd", compiler_params=_params(("arbitrary", "arbitrary")))(
            bu, tab, *[a for r in riders for a in r.ins])
    return tuple(outs[:2]) + (list(outs[2:]),)


def _s5_scan_bwd(g, xp, tab, tb, cb, riders=()):
    T, W = g.shape[0], g.shape[1] // 2
    nt = T // tb
    grid = (W // cb, nt)
    r_in, r_out, _ = _rider_counts(riders)
    re, im = slice(0, cb), slice(cb, 2 * cb)

    def body(*refs):
        g_ref, p_ref, tab_ref = refs[:3]
        a_ref, dlr_ref, dli_ref = refs[3 + r_in:6 + r_in]
        cr, ci = refs[6 + r_in + r_out:8 + r_in + r_out]
        step = _linear_step(grid)
        hooks = _rider_hooks(riders, refs[3:3 + r_in], refs[6 + r_in:6 + r_in + r_out], refs[8 + r_in + r_out:])
        _ride_begin(step, hooks)

        @pl.when(pl.program_id(1) == 0)
        def _():
            cr[...] = jnp.zeros_like(cr)
            ci[...] = jnp.zeros_like(ci)
            dlr_ref[...] = jnp.zeros_like(dlr_ref)
            dli_ref[...] = jnp.zeros_like(dli_ref)

        def tile(tt, carry):
            t = _scan_tiles(tb) - 1 - tt
            ds = pl.ds(pl.multiple_of(t * SUBLANES, SUBLANES), SUBLANES)
            ar8, ai8 = _scan_tile(g_ref[ds, re], g_ref[ds, im], carry, tab_ref, reverse=True)
            a_ref[ds, re] = ar8
            a_ref[ds, im] = ai8
            pr8, pi8 = p_ref[ds, re], p_ref[ds, im]
            dlr_ref[...] += ar8 * pr8 + ai8 * pi8
            dli_ref[...] += ai8 * pr8 - ar8 * pi8
            return ar8[0:1, :], ai8[0:1, :]

        ar, ai = lax.fori_loop(0, _scan_tiles(tb), tile, (cr[0:1, :], ci[0:1, :]))
        cr[0:1, :] = ar
        ci[0:1, :] = ai
        _ride_end(step, math.prod(grid), hooks)

    blk = pl.BlockSpec((tb, 2 * cb), lambda c, i: (nt - 1 - i, c))
    tabs = pl.BlockSpec((tab.shape[0], cb), lambda c, i: (0, c))
    acc = pl.BlockSpec((SUBLANES, cb), lambda c, i: (0, c))
    outs = pl.pallas_call(
        body, grid=grid, in_specs=[blk, blk, tabs] + [_ANY] * r_in, out_specs=[blk, acc, acc] + [_ANY] * r_out,
        out_shape=[SDS((T, 2 * W), F32)] + [SDS((SUBLANES, W), F32)] * 2 + [o for r in riders for o in r.outs],
        scratch_shapes=[pltpu.VMEM((SUBLANES, cb), F32), pltpu.VMEM((SUBLANES, cb), F32)] + [s for r in riders for s in r.sems],
        name="s5_scan_bwd", compiler_params=_params(("arbitrary", "arbitrary")))(
            g, xp, tab, *[a for r in riders for a in r.ins])
    return tuple(outs[:3]) + (list(outs[3:]),)


_ANY = pl.BlockSpec(memory_space=pl.ANY)


def _ag_steps(x_ref, out_ref, send_sems, recv_sems, local_sem):
    x, y, c = lax.axis_index("x"), lax.axis_index("y"), lax.axis_index("c")
    me, sibling = (x, y, c), (x, y, 1 - c)
    nbr_a = ((x + c) % 2, (y + 1 - c) % 2)
    nbr_b = ((x + 1 - c) % 2, (y + c) % 2)
    diag = (1 - x, 1 - y)

    def slot(px, py, pc):
        return out_ref.at[4 * px + 2 * py + pc]

    def copy(k, block, to, src=None):
        return pltpu.make_async_remote_copy(
            src_ref=slot(*block) if src is None else src, dst_ref=slot(*block),
            send_sem=send_sems.at[k], recv_sem=recv_sems.at[k], device_id=to, device_id_type=MESH_ID)

    mine = pltpu.make_async_copy(x_ref, slot(*me), local_sem)
    first = [copy(0, me, sibling, src=x_ref), copy(1, me, (*nbr_a, c), src=x_ref), copy(2, me, (*nbr_b, c), src=x_ref)]
    relay = copy(3, (*nbr_a, c), (*nbr_b, c))
    passed = [copy(4, (*nbr_a, c), sibling), copy(5, (*nbr_b, c), sibling), copy(6, (*diag, c), sibling)]

    def start():
        mine.start()
        for cp in first:
            cp.start()

    def mid():
        copy(1, (*nbr_a, c), me).wait_recv()
        relay.start()
        passed[0].start()
        copy(2, (*nbr_b, c), me).wait_recv()
        passed[1].start()

    def late():
        copy(3, (*diag, c), me).wait_recv()
        passed[2].start()

    def finish():
        copy(0, sibling, me).wait_recv()
        copy(4, (*nbr_b, 1 - c), me).wait_recv()
        copy(5, (*nbr_a, 1 - c), me).wait_recv()
        copy(6, (*diag, 1 - c), me).wait_recv()
        for cp in first + [relay] + passed:
            cp.wait_send()
        mine.wait()

    return start, [(0.6, mid), (0.92, late)], finish


def _rs1_steps(g_ref, r_ref, send_sems, recv_sems):
    x, y, c = lax.axis_index("x"), lax.axis_index("y"), lax.axis_index("c")
    copies = [pltpu.make_async_remote_copy(
        src_ref=g_ref.at[2 * k + (1 - c)], dst_ref=r_ref.at[k], send_sem=send_sems.at[k], recv_sem=recv_sems.at[k],
        device_id=(x, y, 1 - c), device_id_type=MESH_ID) for k in range(N_CHIP)]

    def start():
        for cp in copies:
            cp.start()

    def finish():
        for cp in copies:
            cp.wait_recv()
        for cp in copies:
            cp.wait_send()

    return start, None, finish


def _rs2_steps(p_ref, r_ref, send_sems, recv_sems, local_sem):
    x, y, c = lax.axis_index("x"), lax.axis_index("y"), lax.axis_index("c")
    myk = 2 * x + y
    chips = [(1 - x, y), (x, 1 - y), (1 - x, 1 - y)]
    mine = pltpu.make_async_copy(p_ref.at[myk], r_ref.at[myk], local_sem)
    sends = [pltpu.make_async_remote_copy(
        src_ref=p_ref.at[2 * px + py], dst_ref=r_ref.at[myk], send_sem=send_sems.at[j], recv_sem=recv_sems.at[j],
        device_id=(px, py, c), device_id_type=MESH_ID) for j, (px, py) in enumerate(chips)]

    def start():
        mine.start()
        for cp in sends:
            cp.start()

    def finish():
        for j, (px, py) in enumerate(chips):
            pltpu.make_async_remote_copy(
                src_ref=p_ref.at[myk], dst_ref=r_ref.at[2 * px + py], send_sem=send_sems.at[j], recv_sem=recv_sems.at[j],
                device_id=(px, py, c), device_id_type=MESH_ID).wait_recv()
        for cp in sends:
            cp.wait_send()
        mine.wait()

    return start, None, finish


def _dma_sems(*counts):
    return [pltpu.SemaphoreType.DMA((n,)) if n else pltpu.SemaphoreType.DMA for n in counts]


def _ag_rider(shard):
    return _Rider([shard], [SDS((N_DEV,) + shard.shape, shard.dtype)], _dma_sems(7, 7, 0), _ag_steps)


def _rs1_rider(g):
    return _Rider([g], [SDS((N_CHIP,) + g.shape[1:], g.dtype)], _dma_sems(N_CHIP, N_CHIP), _rs1_steps)


def _rs2_rider(p):
    return _Rider([p], [SDS(p.shape, p.dtype)], _dma_sems(3, 3, 0), _rs2_steps)


def _comm_call(name, riders):
    r_in, r_out, _ = _rider_counts(riders)

    def body(*refs):
        hooks = _rider_hooks(riders, refs[:r_in], refs[r_in:r_in + r_out], refs[r_in + r_out:])
        for start, _, _ in hooks:
            start()
        for _, mid, _ in hooks:
            for _, fn in (mid or ()):
                fn()
        for _, _, finish in hooks:
            finish()

    ops = [a for r in riders for a in r.ins]
    outs = pl.pallas_call(
        body, out_shape=[o for r in riders for o in r.outs], in_specs=[_ANY] * r_in, out_specs=[_ANY] * r_out,
        scratch_shapes=[s for r in riders for s in r.sems], name=name)(*ops)
    return list(outs)


def _all_gather(name, shard):
    return _comm_call(name, [_ag_rider(shard)])[0]


def _pair_add(name, g, r1, c_idx):
    _, R, C = g.shape
    tr = _row_tile(R, C, 2, 6 * 1024 * 1024)

    def body(c_ref, g_ref, r_ref, o_ref):
        o_ref[...] = (g_ref[...].astype(F32) + r_ref[...].astype(F32)).astype(o_ref.dtype)

    grid_spec = pltpu.PrefetchScalarGridSpec(
        num_scalar_prefetch=1, grid=(N_CHIP, R // tr),
        in_specs=[pl.BlockSpec((None, tr, C), lambda k, i, c_ref: (2 * k + c_ref[0], i, 0)),
                  pl.BlockSpec((None, tr, C), lambda k, i, c_ref: (k, i, 0))],
        out_specs=pl.BlockSpec((None, tr, C), lambda k, i, c_ref: (k, i, 0)))
    return pl.pallas_call(body, grid_spec=grid_spec, out_shape=SDS((N_CHIP, R, C), g.dtype), name=name,
                          compiler_params=_params(("arbitrary", "arbitrary")))(c_idx, g, r1)


class _GradReducer:
    def __init__(self, c_idx):
        self.c_idx, self.wait_d2d, self.wait_ici, self.done = c_idx, [], [], {}

    def push(self, name, g):
        self.wait_d2d.append((name, g))

    def take(self, ici=1, d2d=1):
        jobs = [("ici",) + self.wait_ici.pop(0) for _ in range(min(ici, len(self.wait_ici)))]
        jobs += [("d2d",) + self.wait_d2d.pop(0) for _ in range(min(d2d, len(self.wait_d2d)))]
        riders = [_rs2_rider(a) if kind == "ici" else _rs1_rider(a) for kind, _, a in jobs]

        def absorb(outs):
            for (kind, name, a), out in zip(jobs, outs, strict=True):
                if kind == "ici":
                    self.done[name] = out
                else:
                    self.wait_ici.append((name, _pair_add(f"add_{name}", a, out, self.c_idx)))

        return riders, absorb

    def flush(self):
        n_calls = 0
        while self.wait_d2d or self.wait_ici:
            riders, absorb = self.take(ici=len(self.wait_ici), d2d=len(self.wait_d2d))
            absorb(_comm_call(f"rs_tail_{n_calls}", riders))
            n_calls += 1
        return self.done


PACK_ROWS = 64


def _pack(arrs):
    flat = jnp.concatenate([a.reshape(-1).astype(F32) for a in arrs])
    n = flat.shape[0]
    unit = PACK_ROWS * LANES
    padded = -(-n // unit) * unit
    return jnp.pad(flat, (0, padded - n)).reshape(padded // LANES, LANES)


def _unpack(buf, shapes):
    flat = buf.reshape(-1)
    outs, off = [], 0
    for shp in shapes:
        n = math.prod(shp)
        outs.append(flat[off:off + n].reshape(shp))
        off += n
    return outs


def _s5_discretise(lre, lim, log_dt, bre, bim):
    dt = jnp.exp(log_dt)[:, None]
    e_m1 = jnp.expm1(lre * dt)
    th = lim * dt
    lr = (e_m1 + 1.0) * jnp.cos(th)
    li = (e_m1 + 1.0) * jnp.sin(th)
    lr_m1 = e_m1 * jnp.cos(th) - 2.0 * jnp.square(jnp.sin(0.5 * th))
    den = lre * lre + lim * lim
    cr = (lr_m1 * lre + li * lim) / den
    ci = (li * lre - lr_m1 * lim) / den
    bbr = cr[..., None] * bre - ci[..., None] * bim
    bbi = cr[..., None] * bim + ci[..., None] * bre
    return lr, li, bbr, bbi


def _expand_diag(m, rows_first):
    G, A, B = m.shape
    q = G // GROUPS_PER_BLOCK
    eye = jnp.eye(GROUPS_PER_BLOCK, dtype=m.dtype)
    m5 = m.reshape(q, GROUPS_PER_BLOCK, A, 1, B) * eye[None, :, None, :, None]
    return m5.reshape(q, GROUPS_PER_BLOCK * A, GROUPS_PER_BLOCK * B)


def _extract_diag(m, A, B):
    q = m.shape[0]
    m5 = m.reshape(q, GROUPS_PER_BLOCK, A, GROUPS_PER_BLOCK, B)
    d = jnp.stack([m5[:, g, :, g, :] for g in range(GROUPS_PER_BLOCK)], axis=1)
    return d.reshape(q * GROUPS_PER_BLOCK, A, B)


def _ffn_fwd(tag, h, gain, wg, wu_shard, wd_shard, tail_riders, Tp):
    D = h.shape[1]
    J, Fb, _ = wg.shape
    tmn = _row_tile(Tp, D)
    (n,) = _rowcall(f"{tag}_norm", lambda i, x, g: (_rmsnorm(x, g),), Tp, tmn, [(h, 0, 0, D)], [gain], [(D, BF16)])
    gate, (wu,) = _mm_bcast(f"{tag}_gate", n, wg, NT, False, BF16, riders=[_ag_rider(wu_shard)])
    up, (wd,) = _mm_bcast(f"{tag}_up", n, wu, NT, False, BF16, riders=[_ag_rider(wd_shard)])
    rows = J * Tp
    tma = _row_tile(rows, Fb, 2, 3 * 512 * 1024)
    g2, u2 = gate.reshape(rows, Fb), up.reshape(rows, Fb)
    (act,) = _rowcall(f"{tag}_act", lambda i, g, u: (_swiglu_act(g, u),), rows, tma,
                      [(g2, 0, 0, Fb), (u2, 0, 0, Fb)], [], [(Fb, BF16)])
    act = act.reshape(J, Tp, Fb)
    h_out = _mm_red(f"{tag}_down", [(act, wd)], NN, F32, scale=0.5, res=h, jb=2, riders=tail_riders)
    tail = []
    if tail_riders:
        h_out, tail = h_out
    return h_out, (n, gate, up, act), wu, wd, tail


def _ffn_bwd(tag, dh, dh16, h, gain, wg, wu, wd, saved, Tp, red):
    n, gate, up, act = saved
    D = h.shape[1]
    J, Fb, _ = wg.shape

    def carried(fn, *args, ici=1, d2d=1, **kw):
        riders, absorb = red.take(ici, d2d)
        if not riders:
            return fn(*args, **kw)
        out, routs = fn(*args, riders=riders, **kw)
        absorb(routs)
        return out

    dact = carried(_mm_bcast, f"{tag}_dact", dh16, wd, NT, False, BF16)
    tk_w = Tp // 2 if Tp % 32 == 0 else Tp
    red.push(f"{tag}_w_down", carried(_mm_tn, f"{tag}_dwd", act, dh16, J, Fb, D, BF16, scale=0.5, tk_target=tk_w))
    rows = J * Tp
    tma = _row_tile(rows, Fb, 2, 3 * 512 * 1024)

    def act_bwd(i, g, u, d):
        _, vjp = jax.vjp(_swiglu_act, g, u)
        return vjp(0.5 * d.astype(F32))

    dgate, dup = _rowcall(f"{tag}_act_bwd", act_bwd, rows, tma,
                          [(gate.reshape(rows, Fb), 0, 0, Fb), (up.reshape(rows, Fb), 0, 0, Fb),
                           (dact.reshape(rows, Fb), 0, 0, Fb)], [], [(Fb, BF16), (Fb, BF16)])
    dgate, dup = dgate.reshape(J, Tp, Fb), dup.reshape(J, Tp, Fb)
    red.push(f"{tag}_w_gate", carried(_mm_tn, f"{tag}_dwg", dgate, n, J, Fb, D, BF16, tk_target=tk_w))
    red.push(f"{tag}_w_up", carried(_mm_tn, f"{tag}_dwu", dup, n, J, Fb, D, BF16, tk_target=tk_w))
    half = D // 2
    dn = [carried(_mm_red, f"{tag}_dn{part}", [(dgate, wg), (dup, wu)], NN, F32, n_off=part * half, n_len=half)
          for part in range(2)]
    return _norm_bwd(f"{tag}_norm_bwd", h, gain, dn, dh, Tp)


def _norm_bwd(name, h, gain, dn_parts, dres, Tp):
    D = h.shape[1]
    n_parts = len(dn_parts)

    def fn(i, x, r, *rest):
        d = jnp.concatenate([p.astype(F32) for p in rest[:n_parts]], axis=1) if n_parts > 1 else rest[0].astype(F32)
        _, vjp = jax.vjp(_rmsnorm, x, rest[n_parts])
        dx, dg = vjp(d)
        return r + dx, r + dx, dg

    rows = [(h, 0, 0, D), (dres, 0, 0, D)] + [(p, 0, 0, p.shape[1]) for p in dn_parts]
    return _rowcall(name, fn, Tp, _row_tile(Tp, D), rows, [gain], [(D, F32), (D, BF16)], [(1, D)])


def kernel(x, meta_tokens, ffn1_norm, ffn1_w_gate, ffn1_w_up, ffn1_w_down, mix_norm, w_in, rg_conv_w, rg_conv_b, rg_w_a, rg_b_a, rg_w_x, rg_b_x, rg_lambda, s5_lambda_re, s5_lambda_im, s5_log_dt, s5_b_re, s5_b_im, s5_c_re, s5_c_im, s5_d, s5_glu_w, s5_glu_b, rg_out_norm, s5_out_norm, w_out, ffn2_norm, ffn2_w_gate, ffn2_w_up, ffn2_w_down, final_norm, loss_target, m_meta_tokens, m_ffn1_norm, m_ffn1_w_gate, m_ffn1_w_up, m_ffn1_w_down, m_mix_norm, m_w_in, m_rg_conv_w, m_rg_conv_b, m_rg_w_a, m_rg_b_a, m_rg_w_x, m_rg_b_x, m_rg_lambda, m_s5_lambda_re, m_s5_lambda_im, m_s5_log_dt, m_s5_b_re, m_s5_b_im, m_s5_c_re, m_s5_c_im, m_s5_d, m_s5_glu_w, m_s5_glu_b, m_rg_out_norm, m_s5_out_norm, m_w_out, m_ffn2_norm, m_ffn2_w_gate, m_ffn2_w_up, m_ffn2_w_down, m_final_norm, v_meta_tokens, v_ffn1_norm, v_ffn1_w_gate, v_ffn1_w_up, v_ffn1_w_down, v_mix_norm, v_w_in, v_rg_conv_w, v_rg_conv_b, v_rg_w_a, v_rg_b_a, v_rg_w_x, v_rg_b_x, v_rg_lambda, v_s5_lambda_re, v_s5_lambda_im, v_s5_log_dt, v_s5_b_re, v_s5_b_im, v_s5_c_re, v_s5_c_im, v_s5_d, v_s5_glu_w, v_s5_glu_b, v_rg_out_norm, v_s5_out_norm, v_w_out, v_ffn2_norm, v_ffn2_w_gate, v_ffn2_w_up, v_ffn2_w_down, v_final_norm):
    weights = dict(
        meta_tokens=meta_tokens, ffn1_norm=ffn1_norm, ffn1_w_gate=ffn1_w_gate, ffn1_w_up=ffn1_w_up, ffn1_w_down=ffn1_w_down,
        mix_norm=mix_norm, w_in=w_in, rg_conv_w=rg_conv_w, rg_conv_b=rg_conv_b, rg_w_a=rg_w_a, rg_b_a=rg_b_a, rg_w_x=rg_w_x,
        rg_b_x=rg_b_x, rg_lambda=rg_lambda, s5_lambda_re=s5_lambda_re, s5_lambda_im=s5_lambda_im, s5_log_dt=s5_log_dt,
        s5_b_re=s5_b_re, s5_b_im=s5_b_im, s5_c_re=s5_c_re, s5_c_im=s5_c_im, s5_d=s5_d, s5_glu_w=s5_glu_w, s5_glu_b=s5_glu_b,
        rg_out_norm=rg_out_norm, s5_out_norm=s5_out_norm, w_out=w_out, ffn2_norm=ffn2_norm, ffn2_w_gate=ffn2_w_gate,
        ffn2_w_up=ffn2_w_up, ffn2_w_down=ffn2_w_down, final_norm=final_norm)
    moments_m = dict(
        meta_tokens=m_meta_tokens, ffn1_norm=m_ffn1_norm, ffn1_w_gate=m_ffn1_w_gate, ffn1_w_up=m_ffn1_w_up,
        ffn1_w_down=m_ffn1_w_down, mix_norm=m_mix_norm, w_in=m_w_in, rg_conv_w=m_rg_conv_w, rg_conv_b=m_rg_conv_b,
        rg_w_a=m_rg_w_a, rg_b_a=m_rg_b_a, rg_w_x=m_rg_w_x, rg_b_x=m_rg_b_x, rg_lambda=m_rg_lambda,
        s5_lambda_re=m_s5_lambda_re, s5_lambda_im=m_s5_lambda_im, s5_log_dt=m_s5_log_dt, s5_b_re=m_s5_b_re,
        s5_b_im=m_s5_b_im, s5_c_re=m_s5_c_re, s5_c_im=m_s5_c_im, s5_d=m_s5_d, s5_glu_w=m_s5_glu_w, s5_glu_b=m_s5_glu_b,
        rg_out_norm=m_rg_out_norm, s5_out_norm=m_s5_out_norm, w_out=m_w_out, ffn2_norm=m_ffn2_norm,
        ffn2_w_gate=m_ffn2_w_gate, ffn2_w_up=m_ffn2_w_up, ffn2_w_down=m_ffn2_w_down, final_norm=m_final_norm)
    moments_v = dict(
        meta_tokens=v_meta_tokens, ffn1_norm=v_ffn1_norm, ffn1_w_gate=v_ffn1_w_gate, ffn1_w_up=v_ffn1_w_up,
        ffn1_w_down=v_ffn1_w_down, mix_norm=v_mix_norm, w_in=v_w_in, rg_conv_w=v_rg_conv_w, rg_conv_b=v_rg_conv_b,
        rg_w_a=v_rg_w_a, rg_b_a=v_rg_b_a, rg_w_x=v_rg_w_x, rg_b_x=v_rg_b_x, rg_lambda=v_rg_lambda,
        s5_lambda_re=v_s5_lambda_re, s5_lambda_im=v_s5_lambda_im, s5_log_dt=v_s5_log_dt, s5_b_re=v_s5_b_re,
        s5_b_im=v_s5_b_im, s5_c_re=v_s5_c_re, s5_c_im=v_s5_c_im, s5_d=v_s5_d, s5_glu_w=v_s5_glu_w, s5_glu_b=v_s5_glu_b,
        rg_out_norm=v_rg_out_norm, s5_out_norm=v_s5_out_norm, w_out=v_w_out, ffn2_norm=v_ffn2_norm,
        ffn2_w_gate=v_ffn2_w_gate, ffn2_w_up=v_ffn2_w_up, ffn2_w_down=v_ffn2_w_down, final_norm=v_final_norm)
    order = list(weights)

    seq, D = x.shape[1], x.shape[2]
    R = rg_conv_b.shape[1]
    S = s5_d.shape[1]
    G, N, C = s5_b_re.shape[1:]
    heads, hd = rg_w_a.shape[1], rg_w_a.shape[3]
    Q = G // GROUPS_PER_BLOCK
    W = G * N
    NB = GROUPS_PER_BLOCK * N
    T = N_META + seq
    Tp = -(-T // LANES) * LANES
    me = 4 * lax.axis_index("x") + 2 * lax.axis_index("y") + lax.axis_index("c")
    c_idx = lax.axis_index("c").astype(jnp.int32).reshape(1)

    big = ["ffn1_w_gate", "ffn1_w_up", "ffn1_w_down", "w_in", "s5_glu_w", "w_out", "ffn2_w_gate", "ffn2_w_up", "ffn2_w_down"]
    transposed = ["ffn1_w_gate", "ffn1_w_up", "ffn2_w_gate", "ffn2_w_up"]

    def as_held(k, a):
        return jnp.swapaxes(a, 1, 2) if k in transposed else a

    shard16 = {k: as_held(k, weights[k])[0].astype(BF16) for k in big}
    full = {}
    sharded_small = ["meta_tokens", "rg_conv_w", "rg_w_a", "rg_b_a", "rg_w_x", "rg_b_x"]
    sm, full["ffn1_w_gate"] = _comm_call(
        "ag_first", [_ag_rider(_pack([weights[k] for k in sharded_small])), _ag_rider(shard16["ffn1_w_gate"])])
    sm = [jnp.stack(p) for p in zip(*[_unpack(sm[d], [weights[k].shape for k in sharded_small]) for d in range(N_DEV)])]
    meta_full = jnp.moveaxis(sm[0], 0, 1).reshape(N_META, D)
    conv_w_full = jnp.moveaxis(sm[1][:, 0], 0, 1).reshape(CONV_WIDTH, R)
    wa_full = jnp.moveaxis(sm[2][:, 0], 0, 1).reshape(heads, hd, hd)
    ba_full = jnp.moveaxis(sm[3][:, 0], 0, 1).reshape(1, R)
    wx_full = jnp.moveaxis(sm[4][:, 0], 0, 1).reshape(heads, hd, hd)
    bx_full = jnp.moveaxis(sm[5][:, 0], 0, 1).reshape(1, R)

    lam_fn = functools.partial(_s5_discretise)
    (lr, li, bbr, bbi), disc_vjp = jax.vjp(lam_fn, s5_lambda_re[0], s5_lambda_im[0], s5_log_dt[0], s5_b_re[0], s5_b_im[0])
    lr_row, li_row = lr.reshape(1, W), li.reshape(1, W)
    wb = jnp.concatenate([_expand_diag(jnp.swapaxes(bbr, 1, 2), True),
                          _expand_diag(jnp.swapaxes(bbi, 1, 2), True)], axis=2)
    wc = jnp.concatenate([_expand_diag(jnp.swapaxes(s5_c_re[0], 1, 2), True),
                          _expand_diag(-jnp.swapaxes(s5_c_im[0], 1, 2), True)], axis=1)

    h0 = jnp.concatenate([meta_full, x[0], jnp.zeros((Tp - T, D), F32)], axis=0)
    tgt = jnp.concatenate([jnp.zeros((N_META, D), F32), loss_target[0], jnp.zeros((Tp - T, D), F32)], axis=0)

    h1, ffn1_saved, full["ffn1_w_up"], full["ffn1_w_down"], (full["w_in"],) = _ffn_fwd(
        "ffn1", h0, ffn1_norm, full["ffn1_w_gate"], shard16["ffn1_w_up"], shard16["ffn1_w_down"],
        [_ag_rider(shard16["w_in"])], Tp)

    tmd = _row_tile(Tp, D)
    (n2,) = _rowcall("mix_norm", lambda i, a, g: (_rmsnorm(a, g),), Tp, tmd, [(h1, 0, 0, D)], [mix_norm], [(D, BF16)])
    proj, (full["w_out"], full["s5_glu_w"]) = _mm_bcast(
        "w_in", n2, full["w_in"], NN, True, F32, riders=[_ag_rider(shard16["w_out"]), _ag_rider(shard16["s5_glu_w"])])

    tmr = _row_tile(Tp, R)
    tb = _pick(Tp, 768, SUBLANES)
    xc = _conv_fwd(proj, conv_w_full, rg_conv_b, R, tmr)
    a_t, b_t = _rowcall("rg_gates", lambda i, *a: _rg_gates(*a), Tp, tmr, [(xc, 0, 0, R)],
                        [wa_full, ba_full, wx_full, bx_full, rg_lambda], [(R, F32), (R, F32)])
    h_rg, hprev = _rg_scan_fwd(a_t, b_t, tb, _pick(R, 512, LANES))
    (yn_rg,) = _rowcall("rg_out", lambda i, *a: (_rg_out(*a),), Tp, tmr, [(h_rg, 0, 0, R), (proj, 0, 1, R)],
                        [rg_out_norm], [(R, BF16)])

    u_off = 2 * R // LANES
    bu = _mm_bdiag("s5_bu", [(proj, wb, u_off)], NN, LANES, 2 * NB, F32)
    xs, xps, (full["ffn2_w_gate"],) = _s5_scan_fwd(
        bu, _scan_tables(lr_row, li_row, False), tb, NB, riders=[_ag_rider(shard16["ffn2_w_gate"])])
    y_s5 = _mm_bdiag("s5_y", [(xs, wc, 0)], NN, 2 * NB, LANES, F32)
    tms = _row_tile(Tp, S)
    s_col = 2 * R // S
    (z,) = _rowcall("s5_pre", lambda i, *a: (_s5_pre(*a),), Tp, tms, [(y_s5, 0, 0, S), (proj, 0, s_col, S)], [s5_d],
                    [(S, F32)])
    gw = full["s5_glu_w"]
    zz = _mm_red("s5_glu", [(z, gw)], NN, F32, jb=N_DEV)
    (yn_s5,) = _rowcall("s5_out", lambda i, *a: (_s5_out(*a),), Tp, tms, [(z, 0, 0, S), (zz, 0, 0, S)],
                        [s5_glu_b, s5_out_norm], [(S, BF16)])
    yn = jnp.concatenate([yn_rg, yn_s5], axis=1)
    h2 = _mm_red("w_out", [(yn, full["w_out"])], NN, F32, res=h1, jb=4)

    h3, ffn2_saved, full["ffn2_w_up"], full["ffn2_w_down"], _ = _ffn_fwd(
        "ffn2", h2, ffn2_norm, full["ffn2_w_gate"], shard16["ffn2_w_up"], shard16["ffn2_w_down"], [], Tp)

    def final(i, hh, tt, g):
        out, vjp = jax.vjp(_rmsnorm, hh, g)
        row = i * tmd + lax.broadcasted_iota(jnp.int32, (tmd, 1), 0)
        valid = jnp.logical_and(row >= N_META, row < T)
        err = jnp.where(valid, out - tt, 0.0)
        part = 0.5 * jnp.sum(jnp.mean(err * err, axis=-1, keepdims=True))
        dx, dg = vjp(err * (1.0 / D))
        return dx, dx, dg, jnp.full((SUBLANES, LANES), part, F32)

    dh3, dh3_16, d_final_norm, loss_part = _rowcall(
        "final", final, Tp, tmd, [(h3, 0, 0, D), (tgt, 0, 0, D)], [final_norm.reshape(1, D)],
        [(D, F32), (D, BF16)], [(1, D), (SUBLANES, LANES)])
    loss = lax.psum(loss_part[0, 0], ("x", "y", "c"))

    grads = {}
    red = _GradReducer(c_idx)

    def carried(fn, *args, ici=1, d2d=1, extra=(), **kw):
        riders, absorb = red.take(ici, d2d)
        if not riders and not extra:
            return fn(*args, **kw), []
        out, routs = fn(*args, riders=riders + list(extra), **kw)
        absorb(routs[:len(riders)])
        return out, routs[len(riders):]

    dh2, dh2_16, grads["ffn2_norm"] = _ffn_bwd(
        "ffn2", dh3, dh3_16, h2, ffn2_norm, full["ffn2_w_gate"], full["ffn2_w_up"], full["ffn2_w_down"], ffn2_saved,
        Tp, red)

    Kb_out = full["w_out"].shape[1]
    dyn = _mm_bcast("w_out_dx", dh2_16, full["w_out"], NT, True, F32)
    tk_w = Tp // 2 if Tp % 32 == 0 else Tp
    red.push("w_out", carried(_mm_tn, "w_out_dw", yn, dh2_16, N_DEV, Kb_out, D, BF16, tk_target=tk_w)[0])

    def s5_out_bwd(i, zv, zzv, d, gb, gn):
        _, vjp = jax.vjp(_s5_out, zv, zzv, gb, gn)
        return vjp(d)

    dz_a, dzz, grads["s5_glu_b"], grads["s5_out_norm"] = _rowcall(
        "s5_out_bwd", s5_out_bwd, Tp, tms, [(z, 0, 0, S), (zz, 0, 0, S), (dyn, 0, R // S, S)],
        [s5_glu_b, s5_out_norm], [(S, F32), (S, BF16)], [(1, S), (1, S)])
    Kb_glu = gw.shape[1]
    dz = _mm_bcast("s5_glu_dx", dzz, gw, NT, True, F32)
    red.push("s5_glu_w", carried(_mm_tn, "s5_glu_dw", z, dzz, N_DEV, Kb_glu, S, BF16, ici=0)[0])

    def s5_pre_bwd(i, yv, uv, d1, d2, dd):
        _, vjp = jax.vjp(_s5_pre, yv, uv, dd)
        return vjp(d1 + d2)

    dy, du_a, grads["s5_d"] = _rowcall(
        "s5_pre_bwd", s5_pre_bwd, Tp, tms, [(y_s5, 0, 0, S), (proj, 0, s_col, S), (dz_a, 0, 0, S), (dz, 0, 0, S)],
        [s5_d], [(S, F32), (S, F32)], [(1, S)])
    gxs = _mm_bdiag("s5_dx", [(dy, wc, 0)], NT, LANES, 2 * NB, F32)
    d_wc = _mm_tn("s5_dc", xs, dy, Q, 2 * NB, LANES, F32)
    d_wc_r, d_wc_i = d_wc[:, :NB], d_wc[:, NB:]
    scan_riders, scan_absorb = red.take()
    adj, dlr8, dli8, scan_routs = _s5_scan_bwd(
        gxs, xps, _scan_tables(lr_row, -li_row, True), tb, NB, riders=scan_riders)
    scan_absorb(scan_routs)
    du_s5 = _mm_bdiag("s5_du", [(adj, wb, 0)], NT, 2 * NB, LANES, BF16, res=(du_a, 0))
    d_wb = carried(_mm_tn, "s5_db", proj, adj, Q, LANES, 2 * NB, F32, a_off=u_off, d2d=0)[0]
    d_bbr = jnp.swapaxes(_extract_diag(d_wb[:, :, :NB], C, N), 1, 2)
    d_bbi = jnp.swapaxes(_extract_diag(d_wb[:, :, NB:], C, N), 1, 2)
    d_lr = jnp.sum(dlr8, axis=0).reshape(G, N)
    d_li = jnp.sum(dli8, axis=0).reshape(G, N)
    d_lre, d_lim, d_logdt, d_bre, d_bim = disc_vjp((d_lr, d_li, d_bbr, d_bbi))
    grads["s5_lambda_re"], grads["s5_lambda_im"], grads["s5_log_dt"] = d_lre[None], d_lim[None], d_logdt[None]
    grads["s5_b_re"], grads["s5_b_im"] = d_bre[None], d_bim[None]
    grads["s5_c_re"] = jnp.swapaxes(_extract_diag(d_wc_r, N, C), 1, 2)[None]
    grads["s5_c_im"] = -jnp.swapaxes(_extract_diag(d_wc_i, N, C), 1, 2)[None]

    def rg_out_bwd(i, hv, gv, d, gn):
        _, vjp = jax.vjp(_rg_out, hv, gv, gn)
        return vjp(d)

    dh_scan, dg_rg, grads["rg_out_norm"] = _rowcall(
        "rg_out_bwd", rg_out_bwd, Tp, tmr, [(h_rg, 0, 0, R), (proj, 0, 1, R), (dyn, 0, 0, R)], [rg_out_norm],
        [(R, F32), (R, BF16)], [(1, R)])
    db_t, da_t = _rg_scan_bwd(dh_scan, a_t, hprev, tb, _pick(R, 512, LANES))

    def rg_gates_bwd(i, xv, da, db, wa, ba, wx, bx, lam):
        _, vjp = jax.vjp(_rg_gates, xv, wa, ba, wx, bx, lam)
        return vjp((da, db))

    dxc, d_wa, d_ba, d_wx, d_bx, grads["rg_lambda"] = _rowcall(
        "rg_gates_bwd", rg_gates_bwd, Tp, _row_tile(Tp, R, 4, 1024 * 1024),
        [(xc, 0, 0, R), (da_t, 0, 0, R), (db_t, 0, 0, R)], [wa_full, ba_full, wx_full, bx_full, rg_lambda],
        [(R, F32)], [(heads, hd, hd), (1, R), (heads, hd, hd), (1, R), (1, R)])
    du_rg, d_conv_w, grads["rg_conv_b"] = _conv_bwd(dxc, proj, conv_w_full, R, tmr)

    grads["final_norm"] = d_final_norm.reshape(D)
    late = ["ffn1_norm", "mix_norm"]
    replicated = ["ffn1_norm", "mix_norm", "rg_conv_b", "rg_lambda", "s5_lambda_re", "s5_lambda_im", "s5_log_dt",
                  "s5_b_re", "s5_b_im", "s5_c_re", "s5_c_im", "s5_d", "s5_glu_b", "rg_out_norm", "s5_out_norm",
                  "ffn2_norm", "final_norm"]
    early = [k for k in replicated if k not in late]
    early_full = [grads[k].reshape(weights[k].shape) for k in early] + [d_conv_w, d_wa, d_ba, d_wx, d_bx]

    dproj = jnp.concatenate([du_rg.astype(BF16), dg_rg, du_s5], axis=1)
    win = full["w_in"]
    Nb_in = win.shape[2]
    dn2, (early_gathered,) = carried(_mm_red, "w_in_dx", [(dproj, win)], NT, F32, jb=4,
                                     extra=[_ag_rider(_pack(early_full))])
    red.push("w_in", carried(_mm_tn, "w_in_dw", n2, dproj, N_DEV, D, Nb_in, BF16, tk_target=tk_w)[0])
    dh1, dh1_16, grads["mix_norm"] = _norm_bwd("mix_norm_bwd", h1, mix_norm, [dn2], dh2, Tp)

    dh0, _, grads["ffn1_norm"] = _ffn_bwd(
        "ffn1", dh1, dh1_16, h0, ffn1_norm, full["ffn1_w_gate"], full["ffn1_w_up"], full["ffn1_w_down"], ffn1_saved,
        Tp, red)
    grad_x = dh0[N_META:T][None]

    reduced = red.flush()
    outs = {}
    for k in big:
        r2 = reduced[k]
        _, Rk, Ck = r2.shape
        w2, m2, v2 = (as_held(k, a) for a in (weights[k], moments_m[k], moments_v[k]))

        def big_update(i, wv, mv, vv, p0, p1, p2, p3):
            gsum = (p0.astype(F32) + p1.astype(F32)) + (p2.astype(F32) + p3.astype(F32))
            return (gsum,) + _adamw(wv, gsum, mv, vv)

        res = _rowcall(f"adam_{k}", big_update, Rk, _row_tile(Rk, Ck, 4, 1024 * 1024),
                       [(w2, 0, 0, Ck), (m2, 0, 0, Ck), (v2, 0, 0, Ck)] + [(r2, kk, 0, Ck) for kk in range(N_CHIP)],
                       [], [(Ck, F32)] * 4)
        outs[k] = [as_held(k, o[None]) for o in res]

    late_full = [grads[k].reshape(weights[k].shape) for k in late] + [dh0[:N_META]]
    late_gathered = _all_gather("ag_late_grads", _pack(late_full))

    def sum8(i, *parts):
        s = parts[0]
        for q in parts[1:]:
            s = s + q
        return (s,)

    def sum_devices(name, gathered, shapes):
        n_rows = gathered.shape[1]
        (summed,) = _rowcall(name, sum8, n_rows, _pick(n_rows, 4096, SUBLANES),
                             [(gathered, d, 0, LANES) for d in range(N_DEV)], [], [(LANES, F32)])
        return _unpack(summed, shapes)

    early_sum = sum_devices("small_sum_early", early_gathered, [a.shape for a in early_full])
    late_sum = sum_devices("small_sum_late", late_gathered, [a.shape for a in late_full])
    g_small = dict(zip(early, early_sum[:len(early)]))
    g_small.update(zip(late, late_sum[:len(late)]))
    d_cw, d_wa_s, d_ba_s, d_wx_s, d_bx_s = early_sum[len(early):]
    d_meta = late_sum[len(late)]

    def shard_of(a, axis):
        n = a.shape[axis] // N_DEV
        return lax.dynamic_slice_in_dim(a, me * n, n, axis)

    g_small["meta_tokens"] = shard_of(d_meta, 1)
    g_small["rg_conv_w"] = shard_of(d_cw, 1)[None]
    g_small["rg_w_a"] = shard_of(d_wa_s, 1)[None]
    g_small["rg_b_a"] = shard_of(d_ba_s.reshape(heads, hd), 1)[None]
    g_small["rg_w_x"] = shard_of(d_wx_s, 1)[None]
    g_small["rg_b_x"] = shard_of(d_bx_s.reshape(heads, hd), 1)[None]
    small = replicated + sharded_small
    shapes = [weights[k].shape for k in small]
    gp = _pack([g_small[k] for k in small])
    n_rows = gp.shape[0]

    def small_update(i, wv, gv, mv, vv):
        return _adamw(wv, gv, mv, vv)

    res = _rowcall("adam_small", small_update, n_rows, _pick(n_rows, 512, SUBLANES),
                   [(_pack([weights[k] for k in small]), 0, 0, LANES), (gp, 0, 0, LANES),
                    (_pack([moments_m[k] for k in small]), 0, 0, LANES), (_pack([moments_v[k] for k in small]), 0, 0, LANES)],
                   [], [(LANES, F32)] * 3)
    unpacked = [_unpack(r, shapes) for r in res]
    for idx, k in enumerate(small):
        outs[k] = [g_small[k].reshape(weights[k].shape)] + [u[idx] for u in unpacked]

    return (loss, grad_x, *[outs[k][0] for k in order], *[outs[k][1] for k in order],
            *[outs[k][2] for k in order], *[outs[k][3] for k in order])
```

```python
import functools
import math

import jax
import jax.numpy as jnp
from jax import lax
from jax.experimental import pallas as pl
from jax.experimental.pallas import tpu as pltpu

F32, BF16 = jnp.float32, jnp.bfloat16
SDS = jax.ShapeDtypeStruct
MESH_ID = pl.DeviceIdType.MESH
N_DEV = 8
N_CHIP = 4
LANES = 128
SUBLANES = 8
VMEM_LIMIT = 56 * 1024 * 1024

EPS = 1e-6
RG_C = 8.0
N_META = 16
CONV_WIDTH = 4
S5_GROUP = 16
S5_STATE = 64
GROUPS_PER_BLOCK = LANES // S5_GROUP
ADAM_LR, ADAM_B1, ADAM_B2, ADAM_EPS, ADAM_WD, ADAM_STEP = 0.001, 0.9, 0.999, 1e-08, 0.01, 10

NN = (((1,), (0,)), ((), ()))
NT = (((1,), (1,)), ((), ()))
TN = (((0,), (0,)), ((), ()))


def _pick(n, target, mult=16):
    if n <= target:
        return n
    best = None
    for d in range(mult, target + 1, mult):
        if n % d == 0:
            best = d
    assert best is not None, (n, target, mult)
    return best


def _row_tile(nrows, ncols, itembytes=4, budget=2 * 1024 * 1024):
    return _pick(nrows, max(16, budget // (ncols * itembytes)))


def _params(sem):
    return pltpu.CompilerParams(dimension_semantics=sem, vmem_limit_bytes=VMEM_LIMIT)


def _rowcall(name, fn, nrows, tm, rows, fulls, row_outs, acc_outs=()):
    n_in = len(rows) + len(fulls)
    in_specs = []
    for arr, lead, cb, C in rows:
        if arr.ndim == 3:
            in_specs.append(pl.BlockSpec((None, tm, C), lambda i, lead=lead, cb=cb: (lead, i, cb)))
        else:
            in_specs.append(pl.BlockSpec((tm, C), lambda i, cb=cb: (i, cb)))
    for f in fulls:
        in_specs.append(pl.BlockSpec(f.shape, lambda i, nd=f.ndim: (0,) * nd))
    out_specs = [pl.BlockSpec((tm, C), lambda i: (i, 0)) for C, _ in row_outs]
    out_shape = [SDS((nrows, C), dt) for C, dt in row_outs]
    for shp in acc_outs:
        out_specs.append(pl.BlockSpec(shp, lambda i, nd=len(shp): (0,) * nd))
        out_shape.append(SDS(shp, F32))
    n_row_out = len(row_outs)

    def body(*refs):
        i = pl.program_id(0)
        res = fn(i, *[r[...] for r in refs[:n_in]])
        outs = refs[n_in:]
        for k in range(n_row_out):
            outs[k][...] = res[k].astype(outs[k].dtype)
        if acc_outs:
            @pl.when(i == 0)
            def _():
                for o in outs[n_row_out:]:
                    o[...] = jnp.zeros_like(o)
            for k in range(n_row_out, len(outs)):
                outs[k][...] += res[k].astype(F32)

    return pl.pallas_call(
        body, grid=(nrows // tm,), in_specs=in_specs, out_specs=out_specs, out_shape=out_shape,
        name=name, compiler_params=_params(("arbitrary",)))(*[r[0] for r in rows], *fulls)


class _Rider:
    def __init__(self, ins, outs, sems, steps):
        self.ins, self.outs, self.sems, self.steps = list(ins), list(outs), list(sems), steps


def _rider_counts(riders):
    return (sum(len(r.ins) for r in riders), sum(len(r.outs) for r in riders), sum(len(r.sems) for r in riders))


def _rider_hooks(riders, in_refs, out_refs, sem_refs):
    hooks, i, o, s = [], 0, 0, 0
    for r in riders:
        hooks.append(r.steps(*in_refs[i:i + len(r.ins)], *out_refs[o:o + len(r.outs)], *sem_refs[s:s + len(r.sems)]))
        i, o, s = i + len(r.ins), o + len(r.outs), s + len(r.sems)
    return hooks


def _linear_step(grid):
    step = pl.program_id(0)
    for ax in range(1, len(grid)):
        step = step * grid[ax] + pl.program_id(ax)
    return step


def _ride_begin(step, hooks):
    if hooks:
        @pl.when(step == 0)
        def _():
            for start, _, _ in hooks:
                start()


def _ride_end(step, nsteps, hooks):
    if hooks:
        mids = sorted(((min(int(frac * nsteps), nsteps - 1), k, fn) for k, (_, mid, _) in enumerate(hooks)
                       for frac, fn in (mid or ())), key=lambda m: m[:2])
        for at in sorted({m[0] for m in mids}):
            @pl.when(step == at)
            def _(at=at):
                for s, _, fn in mids:
                    if s == at:
                        fn()

        @pl.when(step == nsteps - 1)
        def _():
            for _, _, finish in hooks:
                finish()


def _mm(name, terms, dims, grid, nk, out_shape, out_spec, acc_shape, scale=1.0, res=None, riders=(), epi=None):
    n_t = len(terms)
    kax = len(grid) - 1
    n_res = 1 if res is not None else 0
    n_in = 2 * n_t + n_res + (1 if epi is not None else 0)
    n_out = 2 if epi is not None else 1
    r_in, r_out, _ = _rider_counts(riders)
    nsteps = math.prod(grid)
    n_acc = 1 if nk > 1 else 0

    def body(*refs):
        ins, rin = refs[:n_in], refs[n_in:n_in + r_in]
        o_ref = refs[n_in + r_in]
        rout = refs[n_in + r_in + n_out:n_in + r_in + n_out + r_out]
        scratch = refs[n_in + r_in + n_out + r_out:]
        step = _linear_step(grid)
        k = pl.program_id(kax)
        hooks = _rider_hooks(riders, rin, rout, scratch[n_acc:])
        _ride_begin(step, hooks)

        def product():
            r = None
            for t in range(n_t):
                d = lax.dot_general(ins[2 * t][...].astype(BF16), ins[2 * t + 1][...].astype(BF16), dims,
                                    preferred_element_type=F32)
                r = d if r is None else r + d
            return r

        def emit(r):
            r = r * scale
            if res is not None:
                r = r + ins[2 * n_t][...].astype(F32)
            stored = r.astype(o_ref.dtype)
            o_ref[...] = stored
            if epi is not None:
                o2_ref = refs[n_in + r_in + 1]
                o2_ref[...] = epi[0](stored, ins[2 * n_t + n_res][...]).astype(o2_ref.dtype)

        if nk == 1:
            emit(product())
        else:
            acc = scratch[0]

            @pl.when(k == 0)
            def _():
                acc[...] = jnp.zeros_like(acc)

            acc[...] += product()

            @pl.when(k == nk - 1)
            def _():
                emit(acc[...])

        _ride_end(step, nsteps, hooks)

    ops, specs = [], []
    for a, a_spec, b, b_spec in terms:
        ops += [a, b]
        specs += [a_spec, b_spec]
    if res is not None:
        ops.append(res[0])
        specs.append(res[1])
    out_shapes, out_specs, scratch = [out_shape], [out_spec], []
    if epi is not None:
        ops.append(epi[1])
        specs.append(epi[2])
        out_shapes.append(out_shape)
        out_specs.append(out_spec)
    if n_acc:
        scratch.append(pltpu.VMEM(acc_shape, F32))
    for r in riders:
        ops += r.ins
        specs += [_ANY] * len(r.ins)
        out_shapes += r.outs
        out_specs += [_ANY] * len(r.outs)
        scratch += r.sems
    sem = ("arbitrary",) * len(grid)
    outs = pl.pallas_call(
        body, grid=grid, in_specs=specs, out_specs=out_specs, out_shape=out_shapes,
        scratch_shapes=scratch, name=name, compiler_params=_params(sem))(*ops)
    prod = outs[0] if epi is None else (outs[0], outs[1])
    return (prod, list(outs[n_out:])) if riders else prod


def _mm_bcast(name, a, w, dims, std_out, out_dtype, tm_target=384, riders=(), epi=None):
    M, K = a.shape
    J = w.shape[0]
    Nb = w.shape[2] if dims == NN else w.shape[1]
    tm = _pick(M, tm_target)
    a_spec = pl.BlockSpec((tm, K), lambda j, i, k: (i, 0))
    w_spec = pl.BlockSpec((None,) + w.shape[1:], lambda j, i, k: (j, 0, 0))
    if std_out:
        out_shape, out_spec = SDS((M, J * Nb), out_dtype), pl.BlockSpec((tm, Nb), lambda j, i, k: (i, j))
    else:
        out_shape, out_spec = SDS((J, M, Nb), out_dtype), pl.BlockSpec((None, tm, Nb), lambda j, i, k: (j, i, 0))
    if epi is not None:
        epi = (epi[0], epi[1], out_spec)
    return _mm(name, [(a, a_spec, w, w_spec)], dims, (J, M // tm, 1), 1, out_shape, out_spec, (tm, Nb), riders=riders,
               epi=epi)


def _a_blk(a, tm, Kb, off, jb, s):
    if a.ndim == 3:
        return pl.BlockSpec((None, tm, Kb), lambda i, n, j: (j * jb + s, i, 0))
    return pl.BlockSpec((tm, Kb), lambda i, n, j: (i, j * jb + s + off))


def _mm_red(name, pairs, dims, out_dtype, scale=1.0, res=None, tm_target=704, tn_target=1024, n_off=0, n_len=None,
            jb=1, riders=()):
    a0, w0 = pairs[0][0], pairs[0][1]
    J = w0.shape[0]
    M = a0.shape[-2]
    N = w0.shape[2] if dims == NN else w0.shape[1]
    Kb = w0.shape[1] if dims == NN else w0.shape[2]
    if n_len is not None:
        N = n_len
    tm, tn = _pick(M, tm_target), _pick(N, tn_target, LANES)
    nb0 = n_off // tn
    assert J % jb == 0
    terms = []
    for a, w, *rest in pairs:
        off = rest[0] if rest else 0
        for s in range(jb):
            if dims == NN:
                w_spec = pl.BlockSpec((None, Kb, tn), lambda i, n, j, s=s: (j * jb + s, 0, n + nb0))
            else:
                w_spec = pl.BlockSpec((None, tn, Kb), lambda i, n, j, s=s: (j * jb + s, n + nb0, 0))
            terms.append((a, _a_blk(a, tm, Kb, off, jb, s), w, w_spec))
    out_spec = pl.BlockSpec((tm, tn), lambda i, n, j: (i, n))
    r = None if res is None else (res, out_spec)
    return _mm(name, terms, dims, (M // tm, N // tn, J // jb), J // jb, SDS((M, N), out_dtype), out_spec, (tm, tn),
               scale, r, riders)


def _mm_tn(name, a, b, J, Ka, Nb, out_dtype, a_off=0, b_off=0, scale=1.0, tw_target=1024, tk_target=1408, riders=()):
    M = a.shape[-2]
    tk = _pick(M, tk_target)
    tka = _pick(Ka, tw_target, LANES) if Ka % LANES == 0 else Ka
    tnb = _pick(Nb, tw_target, LANES) if Nb % LANES == 0 else Nb

    def spec(arr, width, tw, is_a, off):
        nb = width // tw

        def wi(m, n):
            return m if is_a else n
        if arr.ndim == 3:
            return pl.BlockSpec((None, tk, tw), lambda j, m, n, k: (j, k, wi(m, n)))
        if arr.shape[1] == width:
            return pl.BlockSpec((tk, tw), lambda j, m, n, k: (k, wi(m, n)))
        return pl.BlockSpec((tk, tw), lambda j, m, n, k: (k, (j + off) * nb + wi(m, n)))

    a_spec = spec(a, Ka, tka, True, a_off)
    b_spec = spec(b, Nb, tnb, False, b_off)
    out_spec = pl.BlockSpec((None, tka, tnb), lambda j, m, n, k: (j, m, n))
    return _mm(name, [(a, a_spec, b, b_spec)], TN, (J, Ka // tka, Nb // tnb, M // tk), M // tk,
               SDS((J, Ka, Nb), out_dtype), out_spec, (tka, tnb), scale, riders=riders)


def _mm_bdiag(name, pairs, dims, Kb, Nb, out_dtype, res=None, tm_target=704, riders=()):
    a0, w0 = pairs[0][0], pairs[0][1]
    M, J = a0.shape[0], w0.shape[0]
    tm = _pick(M, tm_target)
    terms = []
    for a, w, off in pairs:
        a_spec = pl.BlockSpec((tm, Kb), lambda j, i, k, off=off: (i, j + off))
        w_spec = pl.BlockSpec((None,) + w.shape[1:], lambda j, i, k: (j, 0, 0))
        terms.append((a, a_spec, w, w_spec))
    out_spec = pl.BlockSpec((tm, Nb), lambda j, i, k: (i, j))
    r = None if res is None else (res[0], pl.BlockSpec((tm, Nb), lambda j, i, k, off=res[1]: (i, j + off)))
    return _mm(name, terms, dims, (J, M // tm, 1), 1, SDS((M, J * Nb), out_dtype), out_spec, (tm, Nb), 1.0, r, riders)


def _rmsnorm(x, g):
    x = x.astype(F32)
    return x * lax.rsqrt(jnp.mean(x * x, axis=-1, keepdims=True) + EPS) * g


def _swiglu_act(gate, up):
    gate, up = gate.astype(F32), up.astype(F32)
    return gate * jax.nn.sigmoid(gate) * up


def _neg_expm1(x):
    series = -x * (1.0 + x * (1.0 / 2 + x * (1.0 / 6 + x * (1.0 / 24 + x * (1.0 / 120 + x * (1.0 / 720))))))
    return jnp.where(x > -0.25, series, 1.0 - jnp.exp(x))


@jax.custom_vjp
def _dot16(x, w):
    return jnp.dot(x.astype(BF16), w.astype(BF16), preferred_element_type=F32)


def _dot16_fwd(x, w):
    return _dot16(x, w), (x, w)


def _dot16_bwd(saved, ct):
    x, w = saved
    ct16 = ct.astype(BF16)
    dx = lax.dot_general(ct16, w.astype(BF16), NT, preferred_element_type=F32)
    dw = lax.dot_general(x.astype(BF16), ct16, TN, preferred_element_type=F32)
    return dx.astype(x.dtype), dw.astype(w.dtype)


_dot16.defvjp(_dot16_fwd, _dot16_bwd)


def _rg_gates(xc, wa, ba, wx, bx, lam):
    heads, hd = wa.shape[0], wa.shape[1]
    rs, ig = [], []
    for h in range(heads):
        xh = xc[:, h * hd:(h + 1) * hd]
        rs.append(_dot16(xh, wa[h]))
        ig.append(_dot16(xh, wx[h]))
    r = jax.nn.sigmoid(jnp.concatenate(rs, axis=1) + ba)
    ii = jax.nn.sigmoid(jnp.concatenate(ig, axis=1) + bx)
    log_a = -RG_C * r * jax.nn.softplus(-lam)
    a = jnp.exp(log_a)
    mult = jnp.sqrt(_neg_expm1(2.0 * log_a))
    return a, mult * ii * xc


def _rg_out(h, g, gain):
    return _rmsnorm(h * jax.nn.gelu(g), gain)


def _s5_pre(y, u, d):
    return jax.nn.gelu(y + d * u)


def _s5_out(z, zz, glu_b, gain):
    return _rmsnorm(z * jax.nn.sigmoid(zz + glu_b), gain)


def _adamw(w, g, m, v):
    m = ADAM_B1 * m + (1.0 - ADAM_B1) * g
    v = ADAM_B2 * v + (1.0 - ADAM_B2) * jnp.square(g)
    m_hat = m / (1.0 - ADAM_B1 ** ADAM_STEP)
    v_hat = v / (1.0 - ADAM_B2 ** ADAM_STEP)
    delta = -ADAM_LR * (m_hat / (jnp.sqrt(v_hat) + ADAM_EPS) + ADAM_WD * w)
    return delta, m, v


def _conv_fwd(proj, conv_w, conv_b, R, tm):
    T = proj.shape[0]
    tpb = tm // SUBLANES

    def body(cur_ref, prev_ref, w_ref, b_ref, o_ref):
        i = pl.program_id(0)
        cur = cur_ref[...]
        prev = jnp.where(i > 0, prev_ref[...], 0.0)
        ext = jnp.concatenate([prev, cur], axis=0)
        acc = b_ref[...] + w_ref[CONV_WIDTH - 1:CONV_WIDTH, :] * cur
        for s in range(1, CONV_WIDTH):
            acc = acc + w_ref[CONV_WIDTH - 1 - s:CONV_WIDTH - s, :] * pltpu.roll(ext, s, 0)[SUBLANES:, :]
        o_ref[...] = acc

    return pl.pallas_call(
        body, grid=(T // tm,),
        in_specs=[pl.BlockSpec((tm, R), lambda i: (i, 0)),
                  pl.BlockSpec((SUBLANES, R), lambda i: (jnp.maximum(i * tpb - 1, 0), 0)),
                  pl.BlockSpec((CONV_WIDTH, R), lambda i: (0, 0)), pl.BlockSpec((1, R), lambda i: (0, 0))],
        out_specs=pl.BlockSpec((tm, R), lambda i: (i, 0)), out_shape=SDS((T, R), F32),
        name="rg_conv_fwd", compiler_params=_params(("arbitrary",)))(proj, proj, conv_w, conv_b)


def _conv_bwd(dxc, proj, conv_w, R, tm):
    T = dxc.shape[0]
    tpb = tm // SUBLANES
    nt = T // tm
    n_ext = tm + SUBLANES

    def body(d_ref, dnext_ref, u_ref, uprev_ref, w_ref, du_ref, dw_ref, db_ref):
        i = pl.program_id(0)
        d = d_ref[...]
        dnext = jnp.where(i < nt - 1, dnext_ref[...], 0.0)
        dext = jnp.concatenate([d, dnext], axis=0)
        u = u_ref[...]
        uprev = jnp.where(i > 0, uprev_ref[...], 0.0)
        uext = jnp.concatenate([uprev, u], axis=0)
        du = w_ref[CONV_WIDTH - 1:CONV_WIDTH, :] * d
        dws = [jnp.sum(d * u, axis=0, keepdims=True)]
        for s in range(1, CONV_WIDTH):
            du = du + w_ref[CONV_WIDTH - 1 - s:CONV_WIDTH - s, :] * pltpu.roll(dext, n_ext - s, 0)[:tm, :]
            dws.append(jnp.sum(d * pltpu.roll(uext, s, 0)[SUBLANES:, :], axis=0, keepdims=True))
        du_ref[...] = du

        @pl.when(i == 0)
        def _():
            dw_ref[...] = jnp.zeros_like(dw_ref)
            db_ref[...] = jnp.zeros_like(db_ref)

        dw_ref[...] += jnp.concatenate(dws[::-1], axis=0)
        db_ref[...] += jnp.sum(d, axis=0, keepdims=True)

    row = pl.BlockSpec((tm, R), lambda i: (i, 0))
    return pl.pallas_call(
        body, grid=(nt,),
        in_specs=[row, pl.BlockSpec((SUBLANES, R), lambda i: (jnp.minimum((i + 1) * tpb, T // SUBLANES - 1), 0)),
                  row, pl.BlockSpec((SUBLANES, R), lambda i: (jnp.maximum(i * tpb - 1, 0), 0)),
                  pl.BlockSpec((CONV_WIDTH, R), lambda i: (0, 0))],
        out_specs=[row, pl.BlockSpec((CONV_WIDTH, R), lambda i: (0, 0)), pl.BlockSpec((1, R), lambda i: (0, 0))],
        out_shape=[SDS((T, R), F32), SDS((CONV_WIDTH, R), F32), SDS((1, R), F32)],
        name="rg_conv_bwd", compiler_params=_params(("arbitrary",)))(dxc, dxc, proj, proj, conv_w)


def _scan_tiles(tb):
    return tb // SUBLANES


def _rg_scan_fwd(a, b, tb, cb):
    T, W = a.shape

    def body(a_ref, b_ref, h_ref, p_ref, carry):
        @pl.when(pl.program_id(1) == 0)
        def _():
            carry[...] = jnp.zeros_like(carry)

        def tile(t, h):
            ds = pl.ds(pl.multiple_of(t * SUBLANES, SUBLANES), SUBLANES)
            a8, b8 = a_ref[ds, :], b_ref[ds, :]
            hs, ps = [], []
            for j in range(SUBLANES):
                ps.append(h)
                h = a8[j:j + 1, :] * h + b8[j:j + 1, :]
                hs.append(h)
            h_ref[ds, :] = jnp.concatenate(hs, axis=0)
            p_ref[ds, :] = jnp.concatenate(ps, axis=0)
            return h

        carry[0:1, :] = lax.fori_loop(0, _scan_tiles(tb), tile, carry[0:1, :])

    blk = pl.BlockSpec((tb, cb), lambda c, i: (i, c))
    return pl.pallas_call(
        body, grid=(W // cb, T // tb), in_specs=[blk, blk], out_specs=[blk, blk],
        out_shape=[SDS((T, W), F32)] * 2, scratch_shapes=[pltpu.VMEM((SUBLANES, cb), F32)],
        name="rg_scan_fwd", compiler_params=_params(("arbitrary", "arbitrary")))(a, b)


def _rg_scan_bwd(dh, a, hprev, tb, cb):
    T, W = a.shape
    nt = T // tb

    def body(g_ref, a_ref, p_ref, db_ref, da_ref, carry):
        @pl.when(pl.program_id(1) == 0)
        def _():
            carry[...] = jnp.zeros_like(carry)

        def tile(tt, c):
            t = _scan_tiles(tb) - 1 - tt
            ds = pl.ds(pl.multiple_of(t * SUBLANES, SUBLANES), SUBLANES)
            g8, a8 = g_ref[ds, :], a_ref[ds, :]
            adjs = [None] * SUBLANES
            for j in range(SUBLANES - 1, -1, -1):
                adj = g8[j:j + 1, :] + c
                adjs[j] = adj
                c = a8[j:j + 1, :] * adj
            adj8 = jnp.concatenate(adjs, axis=0)
            db_ref[ds, :] = adj8
            da_ref[ds, :] = adj8 * p_ref[ds, :]
            return c

        carry[0:1, :] = lax.fori_loop(0, _scan_tiles(tb), tile, carry[0:1, :])

    blk = pl.BlockSpec((tb, cb), lambda c, i: (nt - 1 - i, c))
    return pl.pallas_call(
        body, grid=(W // cb, nt), in_specs=[blk, blk, blk], out_specs=[blk, blk],
        out_shape=[SDS((T, W), F32)] * 2, scratch_shapes=[pltpu.VMEM((SUBLANES, cb), F32)],
        name="rg_scan_bwd", compiler_params=_params(("arbitrary", "arbitrary")))(dh, a, hprev)


SCAN_LEVELS = (1, 2, 4)


def _scan_tables(lr, li, reverse):
    def cmul(a, b):
        return a[0] * b[0] - a[1] * b[1], a[0] * b[1] + a[1] * b[0]

    powers = [(lr, li)]
    for _ in range(SUBLANES - 1):
        powers.append(cmul(powers[-1], (lr, li)))
    row = jnp.arange(SUBLANES)[:, None]
    rows = []
    for k in SCAN_LEVELS:
        has = (row <= SUBLANES - 1 - k) if reverse else (row >= k)
        rows += [jnp.where(has, powers[k - 1][0], 0.0), jnp.where(has, powers[k - 1][1], 0.0)]
    order = list(range(SUBLANES - 1, -1, -1)) if reverse else list(range(SUBLANES))
    rows += [jnp.concatenate([powers[j][0] for j in order], axis=0), jnp.concatenate([powers[j][1] for j in order], axis=0)]
    return jnp.concatenate(rows, axis=0)


def _scan_tile(vr, vi, carry, tab_ref, reverse):
    n_rows = SUBLANES
    for lvl, k in enumerate(SCAN_LEVELS):
        mr = tab_ref[2 * lvl * n_rows:(2 * lvl + 1) * n_rows, :]
        mi = tab_ref[(2 * lvl + 1) * n_rows:(2 * lvl + 2) * n_rows, :]
        shift = n_rows - k if reverse else k
        sr, si = pltpu.roll(vr, shift, 0), pltpu.roll(vi, shift, 0)
        vr, vi = vr + mr * sr - mi * si, vi + mr * si + mi * sr
    base = 2 * len(SCAN_LEVELS) * n_rows
    pr, pi = tab_ref[base:base + n_rows, :], tab_ref[base + n_rows:base + 2 * n_rows, :]
    cr, ci = carry
    return vr + pr * cr - pi * ci, vi + pr * ci + pi * cr


def _s5_scan_fwd(bu, tab, tb, cb, riders=()):
    T, W = bu.shape[0], bu.shape[1] // 2
    grid = (W // cb, T // tb)
    r_in, r_out, _ = _rider_counts(riders)
    re, im = slice(0, cb), slice(cb, 2 * cb)

    def body(*refs):
        b_ref, tab_ref = refs[:2]
        x_ref, p_ref = refs[2 + r_in:4 + r_in]
        cr, ci = refs[4 + r_in + r_out:6 + r_in + r_out]
        step = _linear_step(grid)
        hooks = _rider_hooks(riders, refs[2:2 + r_in], refs[4 + r_in:4 + r_in + r_out], refs[6 + r_in + r_out:])
        _ride_begin(step, hooks)

        @pl.when(pl.program_id(1) == 0)
        def _():
            cr[...] = jnp.zeros_like(cr)
            ci[...] = jnp.zeros_like(ci)

        first_row = lax.broadcasted_iota(jnp.int32, (SUBLANES, cb), 0) == 0

        def tile(t, carry):
            ds = pl.ds(pl.multiple_of(t * SUBLANES, SUBLANES), SUBLANES)
            xr8, xi8 = _scan_tile(b_ref[ds, re], b_ref[ds, im], carry, tab_ref, reverse=False)
            x_ref[ds, re] = xr8
            x_ref[ds, im] = xi8
            p_ref[ds, re] = jnp.where(first_row, carry[0], pltpu.roll(xr8, 1, 0))
            p_ref[ds, im] = jnp.where(first_row, carry[1], pltpu.roll(xi8, 1, 0))
            return xr8[SUBLANES - 1:SUBLANES, :], xi8[SUBLANES - 1:SUBLANES, :]

        xr, xi = lax.fori_loop(0, _scan_tiles(tb), tile, (cr[0:1, :], ci[0:1, :]))
        cr[0:1, :] = xr
        ci[0:1, :] = xi
        _ride_end(step, math.prod(grid), hooks)

    blk = pl.BlockSpec((tb, 2 * cb), lambda c, i: (i, c))
    tabs = pl.BlockSpec((tab.shape[0], cb), lambda c, i: (0, c))
    outs = pl.pallas_call(
        body, grid=grid, in_specs=[blk, tabs] + [_ANY] * r_in, out_specs=[blk] * 2 + [_ANY] * r_out,
        out_shape=[SDS((T, 2 * W), F32)] * 2 + [o for r in riders for o in r.outs],
        scratch_shapes=[pltpu.VMEM((SUBLANES, cb), F32), pltpu.VMEM((SUBLANES, cb), F32)] + [s for r in riders for s in r.sems],
        name="s5_scan_fwd", compiler_params=_params(("arbitrary", "arbitrary")))(
            bu, tab, *[a for r in riders for a in r.ins])
    return tuple(outs[:2]) + (list(outs[2:]),)


def _s5_scan_bwd(g, xp, tab, tb, cb, riders=()):
    T, W = g.shape[0], g.shape[1] // 2
    nt = T // tb
    grid = (W // cb, nt)
    r_in, r_out, _ = _rider_counts(riders)
    re, im = slice(0, cb), slice(cb, 2 * cb)

    def body(*refs):
        g_ref, p_ref, tab_ref = refs[:3]
        a_ref, dlr_ref, dli_ref = refs[3 + r_in:6 + r_in]
        cr, ci = refs[6 + r_in + r_out:8 + r_in + r_out]
        step = _linear_step(grid)
        hooks = _rider_hooks(riders, refs[3:3 + r_in], refs[6 + r_in:6 + r_in + r_out], refs[8 + r_in + r_out:])
        _ride_begin(step, hooks)

        @pl.when(pl.program_id(1) == 0)
        def _():
            cr[...] = jnp.zeros_like(cr)
            ci[...] = jnp.zeros_like(ci)
            dlr_ref[...] = jnp.zeros_like(dlr_ref)
            dli_ref[...] = jnp.zeros_like(dli_ref)

        def tile(tt, carry):
            t = _scan_tiles(tb) - 1 - tt
            ds = pl.ds(pl.multiple_of(t * SUBLANES, SUBLANES), SUBLANES)
            ar8, ai8 = _scan_tile(g_ref[ds, re], g_ref[ds, im], carry, tab_ref, reverse=True)
            a_ref[ds, re] = ar8
            a_ref[ds, im] = ai8
            pr8, pi8 = p_ref[ds, re], p_ref[ds, im]
            dlr_ref[...] += ar8 * pr8 + ai8 * pi8
            dli_ref[...] += ai8 * pr8 - ar8 * pi8
            return ar8[0:1, :], ai8[0:1, :]

        ar, ai = lax.fori_loop(0, _scan_tiles(tb), tile, (cr[0:1, :], ci[0:1, :]))
        cr[0:1, :] = ar
        ci[0:1, :] = ai
        _ride_end(step, math.prod(grid), hooks)

    blk = pl.BlockSpec((tb, 2 * cb), lambda c, i: (nt - 1 - i, c))
    tabs = pl.BlockSpec((tab.shape[0], cb), lambda c, i: (0, c))
    acc = pl.BlockSpec((SUBLANES, cb), lambda c, i: (0, c))
    outs = pl.pallas_call(
        body, grid=grid, in_specs=[blk, blk, tabs] + [_ANY] * r_in, out_specs=[blk, acc, acc] + [_ANY] * r_out,
        out_shape=[SDS((T, 2 * W), F32)] + [SDS((SUBLANES, W), F32)] * 2 + [o for r in riders for o in r.outs],
        scratch_shapes=[pltpu.VMEM((SUBLANES, cb), F32), pltpu.VMEM((SUBLANES, cb), F32)] + [s for r in riders for s in r.sems],
        name="s5_scan_bwd", compiler_params=_params(("arbitrary", "arbitrary")))(
            g, xp, tab, *[a for r in riders for a in r.ins])
    return tuple(outs[:3]) + (list(outs[3:]),)


_ANY = pl.BlockSpec(memory_space=pl.ANY)


def _ag_steps(x_ref, out_ref, send_sems, recv_sems, local_sem):
    x, y, c = lax.axis_index("x"), lax.axis_index("y"), lax.axis_index("c")
    me, sibling = (x, y, c), (x, y, 1 - c)
    nbr_a = ((x + c) % 2, (y + 1 - c) % 2)
    nbr_b = ((x + 1 - c) % 2, (y + c) % 2)
    diag = (1 - x, 1 - y)

    def slot(px, py, pc):
        return out_ref.at[4 * px + 2 * py + pc]

    def copy(k, block, to, src=None):
        return pltpu.make_async_remote_copy(
            src_ref=slot(*block) if src is None else src, dst_ref=slot(*block),
            send_sem=send_sems.at[k], recv_sem=recv_sems.at[k], device_id=to, device_id_type=MESH_ID)

    mine = pltpu.make_async_copy(x_ref, slot(*me), local_sem)
    first = [copy(0, me, sibling, src=x_ref), copy(1, me, (*nbr_a, c), src=x_ref), copy(2, me, (*nbr_b, c), src=x_ref)]
    relay = copy(3, (*nbr_a, c), (*nbr_b, c))
    passed = [copy(4, (*nbr_a, c), sibling), copy(5, (*nbr_b, c), sibling), copy(6, (*diag, c), sibling)]

    def start():
        mine.start()
        for cp in first:
            cp.start()

    def mid():
        copy(1, (*nbr_a, c), me).wait_recv()
        relay.start()
        passed[0].start()
        copy(2, (*nbr_b, c), me).wait_recv()
        passed[1].start()

    def late():
        copy(3, (*diag, c), me).wait_recv()
        passed[2].start()

    def finish():
        copy(0, sibling, me).wait_recv()
        copy(4, (*nbr_b, 1 - c), me).wait_recv()
        copy(5, (*nbr_a, 1 - c), me).wait_recv()
        copy(6, (*diag, 1 - c), me).wait_recv()
        for cp in first + [relay] + passed:
            cp.wait_send()
        mine.wait()

    return start, [(0.6, mid), (0.92, late)], finish


def _rs1_steps(g_ref, r_ref, send_sems, recv_sems):
    x, y, c = lax.axis_index("x"), lax.axis_index("y"), lax.axis_index("c")
    copies = [pltpu.make_async_remote_copy(
        src_ref=g_ref.at[2 * k + (1 - c)], dst_ref=r_ref.at[k], send_sem=send_sems.at[k], recv_sem=recv_sems.at[k],
        device_id=(x, y, 1 - c), device_id_type=MESH_ID) for k in range(N_CHIP)]

    def start():
        for cp in copies:
            cp.start()

    def finish():
        for cp in copies:
            cp.wait_recv()
        for cp in copies:
            cp.wait_send()

    return start, None, finish


def _rs2_steps(p_ref, r_ref, send_sems, recv_sems, local_sem):
    x, y, c = lax.axis_index("x"), lax.axis_index("y"), lax.axis_index("c")
    myk = 2 * x + y
    chips = [(1 - x, y), (x, 1 - y), (1 - x, 1 - y)]
    mine = pltpu.make_async_copy(p_ref.at[myk], r_ref.at[myk], local_sem)
    sends = [pltpu.make_async_remote_copy(
        src_ref=p_ref.at[2 * px + py], dst_ref=r_ref.at[myk], send_sem=send_sems.at[j], recv_sem=recv_sems.at[j],
        device_id=(px, py, c), device_id_type=MESH_ID) for j, (px, py) in enumerate(chips)]

    def start():
        mine.start()
        for cp in sends:
            cp.start()

    def finish():
        for j, (px, py) in enumerate(chips):
            pltpu.make_async_remote_copy(
                src_ref=p_ref.at[myk], dst_ref=r_ref.at[2 * px + py], send_sem=send_sems.at[j], recv_sem=recv_sems.at[j],
                device_id=(px, py, c), device_id_type=MESH_ID).wait_recv()
        for cp in sends:
            cp.wait_send()
        mine.wait()

    return start, None, finish


def _dma_sems(*counts):
    return [pltpu.SemaphoreType.DMA((n,)) if n else pltpu.SemaphoreType.DMA for n in counts]


def _ag_rider(shard):
    return _Rider([shard], [SDS((N_DEV,) + shard.shape, shard.dtype)], _dma_sems(7, 7, 0), _ag_steps)


def _rs1_rider(g):
    return _Rider([g], [SDS((N_CHIP,) + g.shape[1:], g.dtype)], _dma_sems(N_CHIP, N_CHIP), _rs1_steps)


def _rs2_rider(p):
    return _Rider([p], [SDS(p.shape, p.dtype)], _dma_sems(3, 3, 0), _rs2_steps)


def _comm_call(name, riders):
    r_in, r_out, _ = _rider_counts(riders)

    def body(*refs):
        hooks = _rider_hooks(riders, refs[:r_in], refs[r_in:r_in + r_out], refs[r_in + r_out:])
        for start, _, _ in hooks:
            start()
        for _, mid, _ in hooks:
            for _, fn in (mid or ()):
                fn()
        for _, _, finish in hooks:
            finish()

    ops = [a for r in riders for a in r.ins]
    outs = pl.pallas_call(
        body, out_shape=[o for r in riders for o in r.outs], in_specs=[_ANY] * r_in, out_specs=[_ANY] * r_out,
        scratch_shapes=[s for r in riders for s in r.sems], name=name)(*ops)
    return list(outs)


def _all_gather(name, shard):
    return _comm_call(name, [_ag_rider(shard)])[0]


def _pair_add(name, g, r1, c_idx):
    _, R, C = g.shape
    tr = _row_tile(R, C, 2, 6 * 1024 * 1024)

    def body(c_ref, g_ref, r_ref, o_ref):
        o_ref[...] = (g_ref[...].astype(F32) + r_ref[...].astype(F32)).astype(o_ref.dtype)

    grid_spec = pltpu.PrefetchScalarGridSpec(
        num_scalar_prefetch=1, grid=(N_CHIP, R // tr),
        in_specs=[pl.BlockSpec((None, tr, C), lambda k, i, c_ref: (2 * k + c_ref[0], i, 0)),
                  pl.BlockSpec((None, tr, C), lambda k, i, c_ref: (k, i, 0))],
        out_specs=pl.BlockSpec((None, tr, C), lambda k, i, c_ref: (k, i, 0)))
    return pl.pallas_call(body, grid_spec=grid_spec, out_shape=SDS((N_CHIP, R, C), g.dtype), name=name,
                          compiler_params=_params(("arbitrary", "arbitrary")))(c_idx, g, r1)


class _GradReducer:
    def __init__(self, c_idx):
        self.c_idx, self.wait_d2d, self.wait_ici, self.done = c_idx, [], [], {}

    def push(self, name, g):
        self.wait_d2d.append((name, g))

    def take(self, ici=1, d2d=1):
        jobs = [("ici",) + self.wait_ici.pop(0) for _ in range(min(ici, len(self.wait_ici)))]
        jobs += [("d2d",) + self.wait_d2d.pop(0) for _ in range(min(d2d, len(self.wait_d2d)))]
        riders = [_rs2_rider(a) if kind == "ici" else _rs1_rider(a) for kind, _, a in jobs]

        def absorb(outs):
            for (kind, name, a), out in zip(jobs, outs, strict=True):
                if kind == "ici":
                    self.done[name] = out
                else:
                    self.wait_ici.append((name, _pair_add(f"add_{name}", a, out, self.c_idx)))

        return riders, absorb

    def flush(self):
        n_calls = 0
        while self.wait_d2d or self.wait_ici:
            riders, absorb = self.take(ici=len(self.wait_ici), d2d=len(self.wait_d2d))
            absorb(_comm_call(f"rs_tail_{n_calls}", riders))
            n_calls += 1
        return self.done


PACK_ROWS = 64


def _pack(arrs):
    flat = jnp.concatenate([a.reshape(-1).astype(F32) for a in arrs])
    n = flat.shape[0]
    unit = PACK_ROWS * LANES
    padded = -(-n // unit) * unit
    return jnp.pad(flat, (0, padded - n)).reshape(padded // LANES, LANES)


def _unpack(buf, shapes):
    flat = buf.reshape(-1)
    outs, off = [], 0
    for shp in shapes:
        n = math.prod(shp)
        outs.append(flat[off:off + n].reshape(shp))
        off += n
    return outs


def _s5_discretise(lre, lim, log_dt, bre, bim):
    dt = jnp.exp(log_dt)[:, None]
    e_m1 = jnp.expm1(lre * dt)
    th = lim * dt
    lr = (e_m1 + 1.0) * jnp.cos(th)
    li = (e_m1 + 1.0) * jnp.sin(th)
    lr_m1 = e_m1 * jnp.cos(th) - 2.0 * jnp.square(jnp.sin(0.5 * th))
    den = lre * lre + lim * lim
    cr = (lr_m1 * lre + li * lim) / den
    ci = (li * lre - lr_m1 * lim) / den
    bbr = cr[..., None] * bre - ci[..., None] * bim
    bbi = cr[..., None] * bim + ci[..., None] * bre
    return lr, li, bbr, bbi


def _expand_diag(m, rows_first):
    G, A, B = m.shape
    q = G // GROUPS_PER_BLOCK
    eye = jnp.eye(GROUPS_PER_BLOCK, dtype=m.dtype)
    m5 = m.reshape(q, GROUPS_PER_BLOCK, A, 1, B) * eye[None, :, None, :, None]
    return m5.reshape(q, GROUPS_PER_BLOCK * A, GROUPS_PER_BLOCK * B)


def _extract_diag(m, A, B):
    q = m.shape[0]
    m5 = m.reshape(q, GROUPS_PER_BLOCK, A, GROUPS_PER_BLOCK, B)
    d = jnp.stack([m5[:, g, :, g, :] for g in range(GROUPS_PER_BLOCK)], axis=1)
    return d.reshape(q * GROUPS_PER_BLOCK, A, B)


def _ffn_fwd(tag, h, gain, wg, wu_shard, wd_shard, tail_riders, Tp):
    D = h.shape[1]
    J, Fb, _ = wg.shape
    tmn = _row_tile(Tp, D)
    (n,) = _rowcall(f"{tag}_norm", lambda i, x, g: (_rmsnorm(x, g),), Tp, tmn, [(h, 0, 0, D)], [gain], [(D, BF16)])
    gate, (wu,) = _mm_bcast(f"{tag}_gate", n, wg, NT, False, BF16, riders=[_ag_rider(wu_shard)])
    (up, act), (wd,) = _mm_bcast(f"{tag}_up", n, wu, NT, False, BF16, riders=[_ag_rider(wd_shard)],
                                 epi=(lambda u, g: _swiglu_act(g, u), gate))
    h_out = _mm_red(f"{tag}_down", [(act, wd)], NN, F32, scale=0.5, res=h, jb=2, riders=tail_riders)
    tail = []
    if tail_riders:
        h_out, tail = h_out
    return h_out, (n, gate, up, act), wu, wd, tail


def _ffn_bwd(tag, dh, dh16, h, gain, wg, wu, wd, saved, Tp, red):
    n, gate, up, act = saved
    D = h.shape[1]
    J, Fb, _ = wg.shape

    def carried(fn, *args, ici=1, d2d=1, **kw):
        riders, absorb = red.take(ici, d2d)
        if not riders:
            return fn(*args, **kw)
        out, routs = fn(*args, riders=riders, **kw)
        absorb(routs)
        return out

    dact = carried(_mm_bcast, f"{tag}_dact", dh16, wd, NT, False, BF16)
    tk_w = Tp // 2 if Tp % 32 == 0 else Tp
    red.push(f"{tag}_w_down", carried(_mm_tn, f"{tag}_dwd", act, dh16, J, Fb, D, BF16, scale=0.5, tk_target=tk_w))
    rows = J * Tp
    tma = _row_tile(rows, Fb, 2, 3 * 512 * 1024)

    def act_bwd(i, g, u, d):
        _, vjp = jax.vjp(_swiglu_act, g, u)
        return vjp(0.5 * d.astype(F32))

    dgate, dup = _rowcall(f"{tag}_act_bwd", act_bwd, rows, tma,
                          [(gate.reshape(rows, Fb), 0, 0, Fb), (up.reshape(rows, Fb), 0, 0, Fb),
                           (dact.reshape(rows, Fb), 0, 0, Fb)], [], [(Fb, BF16), (Fb, BF16)])
    dgate, dup = dgate.reshape(J, Tp, Fb), dup.reshape(J, Tp, Fb)
    red.push(f"{tag}_w_gate", carried(_mm_tn, f"{tag}_dwg", dgate, n, J, Fb, D, BF16, tk_target=tk_w))
    red.push(f"{tag}_w_up", carried(_mm_tn, f"{tag}_dwu", dup, n, J, Fb, D, BF16, tk_target=tk_w))
    half = D // 2
    dn = [carried(_mm_red, f"{tag}_dn{part}", [(dgate, wg), (dup, wu)], NN, F32, n_off=part * half, n_len=half)
          for part in range(2)]
    return _norm_bwd(f"{tag}_norm_bwd", h, gain, dn, dh, Tp)


def _norm_bwd(name, h, gain, dn_parts, dres, Tp):
    D = h.shape[1]
    n_parts = len(dn_parts)

    def fn(i, x, r, *rest):
        d = jnp.concatenate([p.astype(F32) for p in rest[:n_parts]], axis=1) if n_parts > 1 else rest[0].astype(F32)
        _, vjp = jax.vjp(_rmsnorm, x, rest[n_parts])
        dx, dg = vjp(d)
        return r + dx, r + dx, dg

    rows = [(h, 0, 0, D), (dres, 0, 0, D)] + [(p, 0, 0, p.shape[1]) for p in dn_parts]
    return _rowcall(name, fn, Tp, _row_tile(Tp, D), rows, [gain], [(D, F32), (D, BF16)], [(1, D)])


def kernel(x, meta_tokens, ffn1_norm, ffn1_w_gate, ffn1_w_up, ffn1_w_down, mix_norm, w_in, rg_conv_w, rg_conv_b, rg_w_a, rg_b_a, rg_w_x, rg_b_x, rg_lambda, s5_lambda_re, s5_lambda_im, s5_log_dt, s5_b_re, s5_b_im, s5_c_re, s5_c_im, s5_d, s5_glu_w, s5_glu_b, rg_out_norm, s5_out_norm, w_out, ffn2_norm, ffn2_w_gate, ffn2_w_up, ffn2_w_down, final_norm, loss_target, m_meta_tokens, m_ffn1_norm, m_ffn1_w_gate, m_ffn1_w_up, m_ffn1_w_down, m_mix_norm, m_w_in, m_rg_conv_w, m_rg_conv_b, m_rg_w_a, m_rg_b_a, m_rg_w_x, m_rg_b_x, m_rg_lambda, m_s5_lambda_re, m_s5_lambda_im, m_s5_log_dt, m_s5_b_re, m_s5_b_im, m_s5_c_re, m_s5_c_im, m_s5_d, m_s5_glu_w, m_s5_glu_b, m_rg_out_norm, m_s5_out_norm, m_w_out, m_ffn2_norm, m_ffn2_w_gate, m_ffn2_w_up, m_ffn2_w_down, m_final_norm, v_meta_tokens, v_ffn1_norm, v_ffn1_w_gate, v_ffn1_w_up, v_ffn1_w_down, v_mix_norm, v_w_in, v_rg_conv_w, v_rg_conv_b, v_rg_w_a, v_rg_b_a, v_rg_w_x, v_rg_b_x, v_rg_lambda, v_s5_lambda_re, v_s5_lambda_im, v_s5_log_dt, v_s5_b_re, v_s5_b_im, v_s5_c_re, v_s5_c_im, v_s5_d, v_s5_glu_w, v_s5_glu_b, v_rg_out_norm, v_s5_out_norm, v_w_out, v_ffn2_norm, v_ffn2_w_gate, v_ffn2_w_up, v_ffn2_w_down, v_final_norm):
    weights = dict(
        meta_tokens=meta_tokens, ffn1_norm=ffn1_norm, ffn1_w_gate=ffn1_w_gate, ffn1_w_up=ffn1_w_up, ffn1_w_down=ffn1_w_down,
        mix_norm=mix_norm, w_in=w_in, rg_conv_w=rg_conv_w, rg_conv_b=rg_conv_b, rg_w_a=rg_w_a, rg_b_a=rg_b_a, rg_w_x=rg_w_x,
        rg_b_x=rg_b_x, rg_lambda=rg_lambda, s5_lambda_re=s5_lambda_re, s5_lambda_im=s5_lambda_im, s5_log_dt=s5_log_dt,
        s5_b_re=s5_b_re, s5_b_im=s5_b_im, s5_c_re=s5_c_re, s5_c_im=s5_c_im, s5_d=s5_d, s5_glu_w=s5_glu_w, s5_glu_b=s5_glu_b,
        rg_out_norm=rg_out_norm, s5_out_norm=s5_out_norm, w_out=w_out, ffn2_norm=ffn2_norm, ffn2_w_gate=ffn2_w_gate,
        ffn2_w_up=ffn2_w_up, ffn2_w_down=ffn2_w_down, final_norm=final_norm)
    moments_m = dict(
        meta_tokens=m_meta_tokens, ffn1_norm=m_ffn1_norm, ffn1_w_gate=m_ffn1_w_gate, ffn1_w_up=m_ffn1_w_up,
        ffn1_w_down=m_ffn1_w_down, mix_norm=m_mix_norm, w_in=m_w_in, rg_conv_w=m_rg_conv_w, rg_conv_b=m_rg_conv_b,
        rg_w_a=m_rg_w_a, rg_b_a=m_rg_b_a, rg_w_x=m_rg_w_x, rg_b_x=m_rg_b_x, rg_lambda=m_rg_lambda,
        s5_lambda_re=m_s5_lambda_re, s5_lambda_im=m_s5_lambda_im, s5_log_dt=m_s5_log_dt, s5_b_re=m_s5_b_re,
        s5_b_im=m_s5_b_im, s5_c_re=m_s5_c_re, s5_c_im=m_s5_c_im, s5_d=m_s5_d, s5_glu_w=m_s5_glu_w, s5_glu_b=m_s5_glu_b,
        rg_out_norm=m_rg_out_norm, s5_out_norm=m_s5_out_norm, w_out=m_w_out, ffn2_norm=m_ffn2_norm,
        ffn2_w_gate=m_ffn2_w_gate, ffn2_w_up=m_ffn2_w_up, ffn2_w_down=m_ffn2_w_down, final_norm=m_final_norm)
    moments_v = dict(
        meta_tokens=v_meta_tokens, ffn1_norm=v_ffn1_norm, ffn1_w_gate=v_ffn1_w_gate, ffn1_w_up=v_ffn1_w_up,
        ffn1_w_down=v_ffn1_w_down, mix_norm=v_mix_norm, w_in=v_w_in, rg_conv_w=v_rg_conv_w, rg_conv_b=v_rg_conv_b,
        rg_w_a=v_rg_w_a, rg_b_a=v_rg_b_a, rg_w_x=v_rg_w_x, rg_b_x=v_rg_b_x, rg_lambda=v_rg_lambda,
        s5_lambda_re=v_s5_lambda_re, s5_lambda_im=v_s5_lambda_im, s5_log_dt=v_s5_log_dt, s5_b_re=v_s5_b_re,
        s5_b_im=v_s5_b_im, s5_c_re=v_s5_c_re, s5_c_im=v_s5_c_im, s5_d=v_s5_d, s5_glu_w=v_s5_glu_w, s5_glu_b=v_s5_glu_b,
        rg_out_norm=v_rg_out_norm, s5_out_norm=v_s5_out_norm, w_out=v_w_out, ffn2_norm=v_ffn2_norm,
        ffn2_w_gate=v_ffn2_w_gate, ffn2_w_up=v_ffn2_w_up, ffn2_w_down=v_ffn2_w_down, final_norm=v_final_norm)
    order = list(weights)

    seq, D = x.shape[1], x.shape[2]
    R = rg_conv_b.shape[1]
    S = s5_d.shape[1]
    G, N, C = s5_b_re.shape[1:]
    heads, hd = rg_w_a.shape[1], rg_w_a.shape[3]
    Q = G // GROUPS_PER_BLOCK
    W = G * N
    NB = GROUPS_PER_BLOCK * N
    T = N_META + seq
    Tp = -(-T // LANES) * LANES
    me = 4 * lax.axis_index("x") + 2 * lax.axis_index("y") + lax.axis_index("c")
    c_idx = lax.axis_index("c").astype(jnp.int32).reshape(1)

    big = ["ffn1_w_gate", "ffn1_w_up", "ffn1_w_down", "w_in", "s5_glu_w", "w_out", "ffn2_w_gate", "ffn2_w_up", "ffn2_w_down"]
    transposed = ["ffn1_w_gate", "ffn1_w_up", "ffn2_w_gate", "ffn2_w_up"]

    def as_held(k, a):
        return jnp.swapaxes(a, 1, 2) if k in transposed else a

    shard16 = {k: as_held(k, weights[k])[0].astype(BF16) for k in big}
    full = {}
    sharded_small = ["meta_tokens", "rg_conv_w", "rg_w_a", "rg_b_a", "rg_w_x", "rg_b_x"]
    sm, full["ffn1_w_gate"] = _comm_call(
        "ag_first", [_ag_rider(_pack([weights[k] for k in sharded_small])), _ag_rider(shard16["ffn1_w_gate"])])
    sm = [jnp.stack(p) for p in zip(*[_unpack(sm[d], [weights[k].shape for k in sharded_small]) for d in range(N_DEV)])]
    meta_full = jnp.moveaxis(sm[0], 0, 1).reshape(N_META, D)
    conv_w_full = jnp.moveaxis(sm[1][:, 0], 0, 1).reshape(CONV_WIDTH, R)
    wa_full = jnp.moveaxis(sm[2][:, 0], 0, 1).reshape(heads, hd, hd)
    ba_full = jnp.moveaxis(sm[3][:, 0], 0, 1).reshape(1, R)
    wx_full = jnp.moveaxis(sm[4][:, 0], 0, 1).reshape(heads, hd, hd)
    bx_full = jnp.moveaxis(sm[5][:, 0], 0, 1).reshape(1, R)

    lam_fn = functools.partial(_s5_discretise)
    (lr, li, bbr, bbi), disc_vjp = jax.vjp(lam_fn, s5_lambda_re[0], s5_lambda_im[0], s5_log_dt[0], s5_b_re[0], s5_b_im[0])
    lr_row, li_row = lr.reshape(1, W), li.reshape(1, W)
    wb = jnp.concatenate([_expand_diag(jnp.swapaxes(bbr, 1, 2), True),
                          _expand_diag(jnp.swapaxes(bbi, 1, 2), True)], axis=2)
    wc = jnp.concatenate([_expand_diag(jnp.swapaxes(s5_c_re[0], 1, 2), True),
                          _expand_diag(-jnp.swapaxes(s5_c_im[0], 1, 2), True)], axis=1)

    h0 = jnp.concatenate([meta_full, x[0], jnp.zeros((Tp - T, D), F32)], axis=0)
    tgt = jnp.concatenate([jnp.zeros((N_META, D), F32), loss_target[0], jnp.zeros((Tp - T, D), F32)], axis=0)

    h1, ffn1_saved, full["ffn1_w_up"], full["ffn1_w_down"], (full["w_in"],) = _ffn_fwd(
        "ffn1", h0, ffn1_norm, full["ffn1_w_gate"], shard16["ffn1_w_up"], shard16["ffn1_w_down"],
        [_ag_rider(shard16["w_in"])], Tp)

    tmd = _row_tile(Tp, D)
    (n2,) = _rowcall("mix_norm", lambda i, a, g: (_rmsnorm(a, g),), Tp, tmd, [(h1, 0, 0, D)], [mix_norm], [(D, BF16)])
    proj, (full["w_out"], full["s5_glu_w"]) = _mm_bcast(
        "w_in", n2, full["w_in"], NN, True, F32, riders=[_ag_rider(shard16["w_out"]), _ag_rider(shard16["s5_glu_w"])])

    tmr = _row_tile(Tp, R)
    tb = _pick(Tp, 768, SUBLANES)
    xc = _conv_fwd(proj, conv_w_full, rg_conv_b, R, tmr)
    a_t, b_t = _rowcall("rg_gates", lambda i, *a: _rg_gates(*a), Tp, tmr, [(xc, 0, 0, R)],
                        [wa_full, ba_full, wx_full, bx_full, rg_lambda], [(R, F32), (R, F32)])
    h_rg, hprev = _rg_scan_fwd(a_t, b_t, tb, _pick(R, 512, LANES))
    (yn_rg,) = _rowcall("rg_out", lambda i, *a: (_rg_out(*a),), Tp, tmr, [(h_rg, 0, 0, R), (proj, 0, 1, R)],
                        [rg_out_norm], [(R, BF16)])

    u_off = 2 * R // LANES
    bu = _mm_bdiag("s5_bu", [(proj, wb, u_off)], NN, LANES, 2 * NB, F32)
    xs, xps, (full["ffn2_w_gate"],) = _s5_scan_fwd(
        bu, _scan_tables(lr_row, li_row, False), tb, NB, riders=[_ag_rider(shard16["ffn2_w_gate"])])
    y_s5 = _mm_bdiag("s5_y", [(xs, wc, 0)], NN, 2 * NB, LANES, F32)
    tms = _row_tile(Tp, S)
    s_col = 2 * R // S
    (z,) = _rowcall("s5_pre", lambda i, *a: (_s5_pre(*a),), Tp, tms, [(y_s5, 0, 0, S), (proj, 0, s_col, S)], [s5_d],
                    [(S, F32)])
    gw = full["s5_glu_w"]
    zz = _mm_red("s5_glu", [(z, gw)], NN, F32, jb=N_DEV)
    (yn_s5,) = _rowcall("s5_out", lambda i, *a: (_s5_out(*a),), Tp, tms, [(z, 0, 0, S), (zz, 0, 0, S)],
                        [s5_glu_b, s5_out_norm], [(S, BF16)])
    yn = jnp.concatenate([yn_rg, yn_s5], axis=1)
    h2 = _mm_red("w_out", [(yn, full["w_out"])], NN, F32, res=h1, jb=4)

    h3, ffn2_saved, full["ffn2_w_up"], full["ffn2_w_down"], _ = _ffn_fwd(
        "ffn2", h2, ffn2_norm, full["ffn2_w_gate"], shard16["ffn2_w_up"], shard16["ffn2_w_down"], [], Tp)

    def final(i, hh, tt, g):
        out, vjp = jax.vjp(_rmsnorm, hh, g)
        row = i * tmd + lax.broadcasted_iota(jnp.int32, (tmd, 1), 0)
        valid = jnp.logical_and(row >= N_META, row < T)
        err = jnp.where(valid, out - tt, 0.0)
        part = 0.5 * jnp.sum(jnp.mean(err * err, axis=-1, keepdims=True))
        dx, dg = vjp(err * (1.0 / D))
        return dx, dx, dg, jnp.full((SUBLANES, LANES), part, F32)

    dh3, dh3_16, d_final_norm, loss_part = _rowcall(
        "final", final, Tp, tmd, [(h3, 0, 0, D), (tgt, 0, 0, D)], [final_norm.reshape(1, D)],
        [(D, F32), (D, BF16)], [(1, D), (SUBLANES, LANES)])
    loss = lax.psum(loss_part[0, 0], ("x", "y", "c"))

    grads = {}
    red = _GradReducer(c_idx)

    def carried(fn, *args, ici=1, d2d=1, extra=(), **kw):
        riders, absorb = red.take(ici, d2d)
        if not riders and not extra:
            return fn(*args, **kw), []
        out, routs = fn(*args, riders=riders + list(extra), **kw)
        absorb(routs[:len(riders)])
        return out, routs[len(riders):]

    dh2, dh2_16, grads["ffn2_norm"] = _ffn_bwd(
        "ffn2", dh3, dh3_16, h2, ffn2_norm, full["ffn2_w_gate"], full["ffn2_w_up"], full["ffn2_w_down"], ffn2_saved,
        Tp, red)

    Kb_out = full["w_out"].shape[1]
    dyn = _mm_bcast("w_out_dx", dh2_16, full["w_out"], NT, True, F32)
    tk_w = Tp // 2 if Tp % 32 == 0 else Tp
    red.push("w_out", carried(_mm_tn, "w_out_dw", yn, dh2_16, N_DEV, Kb_out, D, BF16, tk_target=tk_w)[0])

    def s5_out_bwd(i, zv, zzv, d, gb, gn):
        _, vjp = jax.vjp(_s5_out, zv, zzv, gb, gn)
        return vjp(d)

    dz_a, dzz, grads["s5_glu_b"], grads["s5_out_norm"] = _rowcall(
        "s5_out_bwd", s5_out_bwd, Tp, tms, [(z, 0, 0, S), (zz, 0, 0, S), (dyn, 0, R // S, S)],
        [s5_glu_b, s5_out_norm], [(S, F32), (S, BF16)], [(1, S), (1, S)])
    Kb_glu = gw.shape[1]
    dz = _mm_bcast("s5_glu_dx", dzz, gw, NT, True, F32)
    red.push("s5_glu_w", carried(_mm_tn, "s5_glu_dw", z, dzz, N_DEV, Kb_glu, S, BF16, ici=0)[0])

    def s5_pre_bwd(i, yv, uv, d1, d2, dd):
        _, vjp = jax.vjp(_s5_pre, yv, uv, dd)
        return vjp(d1 + d2)

    dy, du_a, grads["s5_d"] = _rowcall(
        "s5_pre_bwd", s5_pre_bwd, Tp, tms, [(y_s5, 0, 0, S), (proj, 0, s_col, S), (dz_a, 0, 0, S), (dz, 0, 0, S)],
        [s5_d], [(S, F32), (S, F32)], [(1, S)])
    gxs = _mm_bdiag("s5_dx", [(dy, wc, 0)], NT, LANES, 2 * NB, F32)
    d_wc = _mm_tn("s5_dc", xs, dy, Q, 2 * NB, LANES, F32)
    d_wc_r, d_wc_i = d_wc[:, :NB], d_wc[:, NB:]
    scan_riders, scan_absorb = red.take()
    adj, dlr8, dli8, scan_routs = _s5_scan_bwd(
        gxs, xps, _scan_tables(lr_row, -li_row, True), tb, NB, riders=scan_riders)
    scan_absorb(scan_routs)
    du_s5 = _mm_bdiag("s5_du", [(adj, wb, 0)], NT, 2 * NB, LANES, BF16, res=(du_a, 0))
    d_wb = carried(_mm_tn, "s5_db", proj, adj, Q, LANES, 2 * NB, F32, a_off=u_off, d2d=0)[0]
    d_bbr = jnp.swapaxes(_extract_diag(d_wb[:, :, :NB], C, N), 1, 2)
    d_bbi = jnp.swapaxes(_extract_diag(d_wb[:, :, NB:], C, N), 1, 2)
    d_lr = jnp.sum(dlr8, axis=0).reshape(G, N)
    d_li = jnp.sum(dli8, axis=0).reshape(G, N)
    d_lre, d_lim, d_logdt, d_bre, d_bim = disc_vjp((d_lr, d_li, d_bbr, d_bbi))
    grads["s5_lambda_re"], grads["s5_lambda_im"], grads["s5_log_dt"] = d_lre[None], d_lim[None], d_logdt[None]
    grads["s5_b_re"], grads["s5_b_im"] = d_bre[None], d_bim[None]
    grads["s5_c_re"] = jnp.swapaxes(_extract_diag(d_wc_r, N, C), 1, 2)[None]
    grads["s5_c_im"] = -jnp.swapaxes(_extract_diag(d_wc_i, N, C), 1, 2)[None]

    def rg_out_bwd(i, hv, gv, d, gn):
        _, vjp = jax.vjp(_rg_out, hv, gv, gn)
        return vjp(d)

    dh_scan, dg_rg, grads["rg_out_norm"] = _rowcall(
        "rg_out_bwd", rg_out_bwd, Tp, tmr, [(h_rg, 0, 0, R), (proj, 0, 1, R), (dyn, 0, 0, R)], [rg_out_norm],
        [(R, F32), (R, BF16)], [(1, R)])
    db_t, da_t = _rg_scan_bwd(dh_scan, a_t, hprev, tb, _pick(R, 512, LANES))

    def rg_gates_bwd(i, xv, da, db, wa, ba, wx, bx, lam):
        _, vjp = jax.vjp(_rg_gates, xv, wa, ba, wx, bx, lam)
        return vjp((da, db))

    dxc, d_wa, d_ba, d_wx, d_bx, grads["rg_lambda"] = _rowcall(
        "rg_gates_bwd", rg_gates_bwd, Tp, _row_tile(Tp, R, 4, 1024 * 1024),
        [(xc, 0, 0, R), (da_t, 0, 0, R), (db_t, 0, 0, R)], [wa_full, ba_full, wx_full, bx_full, rg_lambda],
        [(R, F32)], [(heads, hd, hd), (1, R), (heads, hd, hd), (1, R), (1, R)])
    du_rg, d_conv_w, grads["rg_conv_b"] = _conv_bwd(dxc, proj, conv_w_full, R, tmr)

    grads["final_norm"] = d_final_norm.reshape(D)
    late = ["ffn1_norm", "mix_norm"]
    replicated = ["ffn1_norm", "mix_norm", "rg_conv_b", "rg_lambda", "s5_lambda_re", "s5_lambda_im", "s5_log_dt",
                  "s5_b_re", "s5_b_im", "s5_c_re", "s5_c_im", "s5_d", "s5_glu_b", "rg_out_norm", "s5_out_norm",
                  "ffn2_norm", "final_norm"]
    early = [k for k in replicated if k not in late]
    early_full = [grads[k].reshape(weights[k].shape) for k in early] + [d_conv_w, d_wa, d_ba, d_wx, d_bx]

    dproj = jnp.concatenate([du_rg.astype(BF16), dg_rg, du_s5], axis=1)
    win = full["w_in"]
    Nb_in = win.shape[2]
    dn2, (early_gathered,) = carried(_mm_red, "w_in_dx", [(dproj, win)], NT, F32, jb=4,
                                     extra=[_ag_rider(_pack(early_full))])
    red.push("w_in", carried(_mm_tn, "w_in_dw", n2, dproj, N_DEV, D, Nb_in, BF16, tk_target=tk_w)[0])
    dh1, dh1_16, grads["mix_norm"] = _norm_bwd("mix_norm_bwd", h1, mix_norm, [dn2], dh2, Tp)

    dh0, _, grads["ffn1_norm"] = _ffn_bwd(
        "ffn1", dh1, dh1_16, h0, ffn1_norm, full["ffn1_w_gate"], full["ffn1_w_up"], full["ffn1_w_down"], ffn1_saved,
        Tp, red)
    grad_x = dh0[N_META:T][None]

    reduced = red.flush()
    outs = {}
    for k in big:
        r2 = reduced[k]
        _, Rk, Ck = r2.shape
        w2, m2, v2 = (as_held(k, a) for a in (weights[k], moments_m[k], moments_v[k]))

        def big_update(i, wv, mv, vv, p0, p1, p2, p3):
            gsum = (p0.astype(F32) + p1.astype(F32)) + (p2.astype(F32) + p3.astype(F32))
            return (gsum,) + _adamw(wv, gsum, mv, vv)

        res = _rowcall(f"adam_{k}", big_update, Rk, _row_tile(Rk, Ck, 4, 1024 * 1024),
                       [(w2, 0, 0, Ck), (m2, 0, 0, Ck), (v2, 0, 0, Ck)] + [(r2, kk, 0, Ck) for kk in range(N_CHIP)],
                       [], [(Ck, F32)] * 4)
        outs[k] = [as_held(k, o[None]) for o in res]

    late_full = [grads[k].reshape(weights[k].shape) for k in late] + [dh0[:N_META]]
    late_gathered = _all_gather("ag_late_grads", _pack(late_full))

    def sum8(i, *parts):
        s = parts[0]
        for q in parts[1:]:
            s = s + q
        return (s,)

    def sum_devices(name, gathered, shapes):
        n_rows = gathered.shape[1]
        (summed,) = _rowcall(name, sum8, n_rows, _pick(n_rows, 4096, SUBLANES),
                             [(gathered, d, 0, LANES) for d in range(N_DEV)], [], [(LANES, F32)])
        return _unpack(summed, shapes)

    early_sum = sum_devices("small_sum_early", early_gathered, [a.shape for a in early_full])
    late_sum = sum_devices("small_sum_late", late_gathered, [a.shape for a in late_full])
    g_small = dict(zip(early, early_sum[:len(early)]))
    g_small.update(zip(late, late_sum[:len(late)]))
    d_cw, d_wa_s, d_ba_s, d_wx_s, d_bx_s = early_sum[len(early):]
    d_meta = late_sum[len(late)]

    def shard_of(a, axis):
        n = a.shape[axis] // N_DEV
        return lax.dynamic_slice_in_dim(a, me * n, n, axis)

    g_small["meta_tokens"] = shard_of(d_meta, 1)
    g_small["rg_conv_w"] = shard_of(d_cw, 1)[None]
    g_small["rg_w_a"] = shard_of(d_wa_s, 1)[None]
    g_small["rg_b_a"] = shard_of(d_ba_s.reshape(heads, hd), 1)[None]
    g_small["rg_w_x"] = shard_of(d_wx_s, 1)[None]
    g_small["rg_b_x"] = shard_of(d_bx_s.reshape(heads, hd), 1)[None]
    small = replicated + sharded_small
    shapes = [weights[k].shape for k in small]
    gp = _pack([g_small[k] for k in small])
    n_rows = gp.shape[0]

    def small_update(i, wv, gv, mv, vv):
        return _adamw(wv, gv, mv, vv)

    res = _rowcall("adam_small", small_update, n_rows, _pick(n_rows, 512, SUBLANES),
                   [(_pack([weights[k] for k in small]), 0, 0, LANES), (gp, 0, 0, LANES),
                    (_pack([moments_m[k] for k in small]), 0, 0, LANES), (_pack([moments_v[k] for k in small]), 0, 0, LANES)],
                   [], [(LANES, F32)] * 3)
    unpacked = [_unpack(r, shapes) for r in res]
    for idx, k in enumerate(small):
        outs[k] = [g_small[k].reshape(weights[k].shape)] + [u[idx] for u in unpacked]

    return (loss, grad_x, *[outs[k][0] for k in order], *[outs[k][1] for k in order],
            *[outs[k][2] for k in order], *[outs[k][3] for k in order])
```
